```python
import jax
import jax.numpy as jnp
from jax import lax
import numpy as np

D_MODEL = 1024
BATCH = 16
SEQ = 4096
DEPTH = 2

GRID_W = 64
CTX_LEN = 256
BRANCH_W = 512
N_BRANCH = 3
GLA_H = 4
GLA_DK = 64
GLA_DV = 128
GLA_LR = 16
GLA_TAU = 16.0
GLA_CHUNK = 64
ATT_H = 4
ATT_KV = 2
ATT_G = ATT_H // ATT_KV
ATT_HD = 128
Q_BLOCK = 128
ROPE_THETA = 10000.0
RWKV_H = 8
RWKV_HD = 64
RWKV_W_LR = 64
RWKV_A_LR = 64
RWKV_DECAY_SCALE = 0.6065306597
NORM_EPS = 1e-6
GN_EPS = 64e-5
L2_EPS = 1e-12
F32 = jnp.float32

GLA_COLS = (GLA_H * GLA_DK, GLA_H * GLA_DK, GLA_H * GLA_DV, BRANCH_W, GLA_LR, GLA_LR)
ATT_COLS = (ATT_H * ATT_HD, ATT_KV * ATT_HD, ATT_KV * ATT_HD, BRANCH_W)
RWKV_COLS = (BRANCH_W, BRANCH_W, BRANCH_W, BRANCH_W, RWKV_W_LR, RWKV_W_LR, RWKV_A_LR, RWKV_A_LR)
MERGE_COLS = (D_MODEL, D_MODEL, D_MODEL)
GROUP_COLS = (sum(GLA_COLS), sum(ATT_COLS), sum(RWKV_COLS), sum(MERGE_COLS))
D_IN = sum(GROUP_COLS)

kernel_name = 'hybrid_gla_gqa_rwkv7_prefix_dit'


def _offsets(sizes):
    out, acc = [], 0
    for s in sizes:
        out.append((acc, acc + s))
        acc += s
    return out


def _split(p, sizes):
    return [p[..., a:b] for a, b in _offsets(sizes)]


def _project(h, w, sizes):
    return [h @ w[:, a:b] for a, b in _offsets(sizes)]


def rms_norm(x, g):
    xf = x.astype(F32)
    y = xf * lax.rsqrt(jnp.mean(xf * xf, -1, keepdims=True) + NORM_EPS)
    return (y * g.astype(F32)).astype(x.dtype)


def head_layer_norm(y, g, b):
    mu = jnp.mean(y, -1, keepdims=True)
    d = y - mu
    var = jnp.mean(d * d, -1, keepdims=True)
    return d * lax.rsqrt(var + GN_EPS) * g + b


def modulation(cond, w_mod, b_mod):
    m = jax.nn.silu(cond) @ w_mod + b_mod
    return jnp.split(m, 3, axis=-1)


def centred_shift(p, mu):
    pad = jnp.pad(p, ((0, 0), (1, 1), (0, 0)))
    nb = 0.5 * (pad[:, :-2] + pad[:, 2:])
    return p + mu * (nb - p)


def axial_rope(t, row_pos, col_pos):
    half = ATT_HD // 2
    quarter = half // 2
    inv = ROPE_THETA ** (-jnp.arange(quarter, dtype=F32) / quarter)
    tf = t.astype(F32)

    def rot(u, pos):
        ang = pos[:, None] * inv[None, :]
        cos = jnp.cos(ang)[None, :, None, :]
        sin = jnp.sin(ang)[None, :, None, :]
        u1, u2 = u[..., :quarter], u[..., quarter:]
        return jnp.concatenate([u1 * cos - u2 * sin, u2 * cos + u1 * sin], -1)

    return jnp.concatenate([rot(tf[..., :half], row_pos), rot(tf[..., half:], col_pos)], -1).astype(t.dtype)


def gla_chunked(q, k, v, g, s0):
    b, h, t, dk = q.shape
    dv = v.shape[-1]
    n = t // GLA_CHUNK
    q = q.reshape(b, h, n, GLA_CHUNK, dk)
    k = k.reshape(b, h, n, GLA_CHUNK, dk)
    v = v.reshape(b, h, n, GLA_CHUNK, dv)
    g_cum = jnp.cumsum(g.reshape(b, h, n, GLA_CHUNK, dk), axis=3)
    g_last = g_cum[:, :, :, -1:, :]
    q_dec = q * jnp.exp(g_cum)
    k_inv = k * jnp.exp(-g_cum)
    k_tail = k * jnp.exp(g_last - g_cum)
    lower = jnp.tril(jnp.ones((GLA_CHUNK, GLA_CHUNK), bool))
    a = jnp.where(lower, jnp.einsum('bhncd,bhnsd->bhncs', q_dec, k_inv), 0.0)
    o = jnp.einsum('bhncs,bhnse->bhnce', a, v)
    u = jnp.einsum('bhncd,bhnce->bhnde', k_tail, v)
    decay = jnp.exp(g_last[:, :, :, 0, :])

    def step(s, inp):
        d_n, u_n = inp
        return d_n[..., None] * s + u_n, s

    s_fin, s_in = lax.scan(step, s0, (jnp.moveaxis(decay, 2, 0), jnp.moveaxis(u, 2, 0)))
    o = o + jnp.einsum('bhncd,nbhde->bhnce', q_dec, s_in)
    return o.reshape(b, h, t, dv), s_fin


def gla_prep(p, wup_f, b_f, wup_b, b_b):
    q, k, v, gate, wd_f, wd_b = _split(p.astype(F32), GLA_COLS)
    b, t, _ = q.shape

    def heads(z, d):
        return z.reshape(b, t, GLA_H, d).transpose(0, 2, 1, 3)

    g_f = jax.nn.log_sigmoid(wd_f @ wup_f + b_f) / GLA_TAU
    g_b = jax.nn.log_sigmoid(wd_b @ wup_b + b_b) / GLA_TAU
    return (heads(q * GLA_DK ** -0.5, GLA_DK), heads(k, GLA_DK), heads(v, GLA_DV),
            heads(g_f, GLA_DK), heads(g_b, GLA_DK), gate)


def gla_out(o, gate, g_norm, dtype):
    o = o.transpose(0, 2, 1, 3)
    b, t = o.shape[:2]
    o = rms_norm(o, g_norm).reshape(b, t, GLA_H * GLA_DV)
    return (o * jax.nn.silu(gate)).astype(dtype)


def gla_branch(p_lat, p_ctx, wup_f, b_f, wup_b, b_b, g_norm, need_ctx):
    ql, kl, vl, gfl, gbl, gate_l = gla_prep(p_lat, wup_f, b_f, wup_b, b_b)
    qc, kc, vc, gfc, gbc, gate_c = gla_prep(p_ctx, wup_f, b_f, wup_b, b_b)
    s0 = jnp.zeros((ql.shape[0], GLA_H, GLA_DK, GLA_DV), F32)

    def fl(z):
        return z[:, :, ::-1]

    oc_f, sc_f = gla_chunked(qc, kc, vc, gfc, s0)
    oc_b, sc_b = gla_chunked(fl(qc), fl(kc), fl(vc), fl(gbc), s0)
    ol_f, _ = gla_chunked(ql, kl, vl, gfl, sc_f)
    ol_b, _ = gla_chunked(fl(ql), fl(kl), fl(vl), fl(gbl), sc_b)
    y_lat = gla_out(ol_f + fl(ol_b), gate_l, g_norm, p_lat.dtype)
    y_ctx = gla_out(oc_f + fl(oc_b), gate_c, g_norm, p_ctx.dtype) if need_ctx else None
    return y_lat, y_ctx


def attn_prep(p, qn_g, kn_g):
    q, k, v, gate = _split(p, ATT_COLS)
    b, t, _ = q.shape
    q = rms_norm(q.reshape(b, t, ATT_H, ATT_HD), qn_g)
    k = rms_norm(k.reshape(b, t, ATT_KV, ATT_HD), kn_g)
    return q, k, v.reshape(b, t, ATT_KV, ATT_HD), gate


def attend(qg, k, v):
    s = jnp.einsum('bqkgd,bskd->bkgqs', qg, k) * (ATT_HD ** -0.5)
    pr = jax.nn.softmax(s.astype(F32), axis=-1).astype(v.dtype)
    return jnp.einsum('bkgqs,bskd->bqkgd', pr, v)


def attn_branch(p_lat, p_ctx, qn_g, kn_g, row_pos, col_pos, need_ctx):
    ql, kl, vl, gate_l = attn_prep(p_lat, qn_g, kn_g)
    qc, kc, vc, gate_c = attn_prep(p_ctx, qn_g, kn_g)
    ql = axial_rope(ql, row_pos, col_pos)
    kl = axial_rope(kl, row_pos, col_pos)
    b, t = ql.shape[:2]
    k_all = jnp.concatenate([kl, kc], axis=1)
    v_all = jnp.concatenate([vl, vc], axis=1)
    qg = ql.reshape(b, t // Q_BLOCK, Q_BLOCK, ATT_KV, ATT_G, ATT_HD).swapaxes(0, 1)
    o = lax.map(lambda qb: attend(qb, k_all, v_all), qg)
    o = o.swapaxes(0, 1).reshape(b, t, ATT_H * ATT_HD)
    y_lat = o * jax.nn.silu(gate_l)
    y_ctx = None
    if need_ctx:
        lc = qc.shape[1]
        oc = attend(qc.reshape(b, lc, ATT_KV, ATT_G, ATT_HD), kc, vc).reshape(b, lc, ATT_H * ATT_HD)
        y_ctx = oc * jax.nn.silu(gate_c)
    return y_lat, y_ctx


def rwkv_scan(r, w, k, v, kk, a, s0, reverse):
    xs = tuple(jnp.moveaxis(z, 1, 0) for z in (r, w, k, v, kk, a))

    def step(s, inp):
        r_t, w_t, k_t, v_t, kk_t, a_t = inp
        sa = jnp.einsum('bhvk,bhk->bhv', s, kk_t)
        s = (s * w_t[:, :, None, :] - sa[..., None] * (kk_t * a_t)[:, :, None, :]
             + v_t[..., None] * k_t[:, :, None, :])
        return s, jnp.einsum('bhvk,bhk->bhv', s, r_t)

    s_fin, y = lax.scan(step, s0, xs, reverse=reverse)
    return jnp.moveaxis(y, 0, 1), s_fin


def rwkv_prep(p, mu, w0_f, wup_f, w0_b, wup_b, a0_f, aup_f, a0_b, aup_b, k_k, k_a):
    p = centred_shift(p, mu).astype(F32)
    r, k, v, gate, wd_f, wd_b, ad_f, ad_b = _split(p, RWKV_COLS)
    b, t, _ = r.shape

    def heads(z):
        return z.reshape(b, t, RWKV_H, RWKV_HD)

    def direction(w0, wup, a0, aup, wd, ad):
        w = jnp.exp(-RWKV_DECAY_SCALE * jax.nn.sigmoid(w0 + jnp.tanh(wd) @ wup))
        a = jax.nn.sigmoid(a0 + ad @ aup)
        kd = k * (1.0 + (a - 1.0) * k_a)
        return heads(w), heads(kd), heads(a)

    kk = heads(k * k_k)
    kk = kk * lax.rsqrt(jnp.sum(kk * kk, -1, keepdims=True) + L2_EPS)
    return (heads(r), heads(v), kk, direction(w0_f, wup_f, a0_f, aup_f, wd_f, ad_f),
            direction(w0_b, wup_b, a0_b, aup_b, wd_b, ad_b), gate)


def rwkv_branch(p_lat, p_ctx, mu, w0_f, wup_f, w0_b, wup_b, a0_f, aup_f, a0_b, aup_b,
                k_k, k_a, r_k, ln_g, ln_b, need_ctx):
    prm = (mu, w0_f, wup_f, w0_b, wup_b, a0_f, aup_f, a0_b, aup_b, k_k, k_a)
    rl, vl, kkl, (wfl, kfl, afl), (wbl, kbl, abl), gate_l = rwkv_prep(p_lat, *prm)
    rc, vc, kkc, (wfc, kfc, afc), (wbc, kbc, abc), gate_c = rwkv_prep(p_ctx, *prm)
    s0 = jnp.zeros((rl.shape[0], RWKV_H, RWKV_HD, RWKV_HD), F32)
    yc_f, sc_f = rwkv_scan(rc, wfc, kfc, vc, kkc, afc, s0, False)
    yc_b, sc_b = rwkv_scan(rc, wbc, kbc, vc, kkc, abc, s0, True)
    yl_f, _ = rwkv_scan(rl, wfl, kfl, vl, kkl, afl, sc_f, False)
    yl_b, _ = rwkv_scan(rl, wbl, kbl, vl, kkl, abl, sc_b, True)

    def out(y, r, kf, kb, v, gate, dtype):
        y = head_layer_norm(y, ln_g, ln_b) + jnp.sum(r * (kf + kb) * r_k, -1, keepdims=True) * v
        b, t = y.shape[:2]
        return (y.reshape(b, t, BRANCH_W) * jax.nn.silu(gate)).astype(dtype)

    y_lat = out(yl_f + yl_b, rl, kfl, kbl, vl, gate_l, p_lat.dtype)
    y_ctx = out(yc_f + yc_b, rc, kfc, kbc, vc, gate_c, p_ctx.dtype) if need_ctx else None
    return y_lat, y_ctx


def layer(x, xc, c, c_ctx, row_pos, col_pos, need_ctx, w_mod, b_mod, g_pre, w_in,
          gla_wup_f, gla_b_f, gla_wup_b, gla_b_b, gla_norm, att_qnorm, att_knorm,
          rwkv_mu, rwkv_w0_f, rwkv_wup_f, rwkv_w0_b, rwkv_wup_b, rwkv_a0_f, rwkv_aup_f,
          rwkv_a0_b, rwkv_aup_b, rwkv_kk, rwkv_ka, rwkv_rk, rwkv_ln_g, rwkv_ln_b,
          w_o_gla, w_o_att, w_o_rwkv, w_out, g_post):
    shift, scale, gate = modulation(c, w_mod, b_mod)
    shift_c, scale_c, gate_c = modulation(c_ctx, w_mod, b_mod)
    h = rms_norm(x, g_pre) * (1.0 + scale[:, None]) + shift[:, None]
    hc = rms_norm(xc, g_pre) * (1.0 + scale_c) + shift_c
    gla_l, att_l, rwkv_l, mg_l = _project(h, w_in, GROUP_COLS)
    ctx_parts = _project(hc, w_in, GROUP_COLS if need_ctx else GROUP_COLS[:3])
    gla_c, att_c, rwkv_c = ctx_parts[0], ctx_parts[1], ctx_parts[2]

    y1, y1c = gla_branch(gla_l, gla_c, gla_wup_f, gla_b_f, gla_wup_b, gla_b_b, gla_norm, need_ctx)
    y2, y2c = attn_branch(att_l, att_c, att_qnorm, att_knorm, row_pos, col_pos, need_ctx)
    y3, y3c = rwkv_branch(rwkv_l, rwkv_c, rwkv_mu, rwkv_w0_f, rwkv_wup_f, rwkv_w0_b, rwkv_wup_b,
                          rwkv_a0_f, rwkv_aup_f, rwkv_a0_b, rwkv_aup_b, rwkv_kk, rwkv_ka,
                          rwkv_rk, rwkv_ln_g, rwkv_ln_b, need_ctx)

    def merge(ya, yb, yc, mg):
        g1, g2, g3 = _split(mg, MERGE_COLS)
        m = (jax.nn.sigmoid(g1) * (ya @ w_o_gla) + jax.nn.sigmoid(g2) * (yb @ w_o_att)
             + jax.nn.sigmoid(g3) * (yc @ w_o_rwkv))
        return rms_norm(m @ w_out, g_post)

    x = x + gate[:, None] * merge(y1, y2, y3, mg_l)
    xc_new = xc + gate_c * merge(y1c, y2c, y3c, ctx_parts[3]) if need_ctx else None
    return x, xc_new


def setup_inputs(seed: int = 0) -> dict:
    key = jax.random.key(seed)
    ks = iter(jax.random.split(key, 48))
    L, D = DEPTH, D_MODEL

    def nrm(shape, s):
        return s * jax.random.normal(next(ks), shape, F32)

    return {
        'x': nrm((BATCH, SEQ, D), 1.0),
        'c': nrm((BATCH, D), 1.0),
        'ctx': nrm((BATCH, CTX_LEN, D), 1.0),
        'c_ctx': nrm((D,), 1.0),
        'w_mod': nrm((L, D, 3 * D), 0.5 * D ** -0.5),
        'b_mod': nrm((L, 3 * D), 0.02),
        'g_pre': 1.0 + nrm((L, D), 0.05),
        'w_in': nrm((L, D, D_IN), D ** -0.5),
        'gla_wup_f': nrm((L, GLA_LR, GLA_H * GLA_DK), GLA_LR ** -0.5),
        'gla_b_f': nrm((L, GLA_H * GLA_DK), 0.1),
        'gla_wup_b': nrm((L, GLA_LR, GLA_H * GLA_DK), GLA_LR ** -0.5),
        'gla_b_b': nrm((L, GLA_H * GLA_DK), 0.1),
        'gla_norm': 1.0 + nrm((L, GLA_H, GLA_DV), 0.05),
        'att_qnorm': 1.0 + nrm((L, ATT_HD), 0.05),
        'att_knorm': 1.0 + nrm((L, ATT_HD), 0.05),
        'rwkv_mu': jax.random.uniform(next(ks), (L, GROUP_COLS[2]), F32),
        'rwkv_w0_f': nrm((L, BRANCH_W), 0.5),
        'rwkv_wup_f': nrm((L, RWKV_W_LR, BRANCH_W), 0.5 * RWKV_W_LR ** -0.5),
        'rwkv_w0_b': nrm((L, BRANCH_W), 0.5),
        'rwkv_wup_b': nrm((L, RWKV_W_LR, BRANCH_W), 0.5 * RWKV_W_LR ** -0.5),
        'rwkv_a0_f': nrm((L, BRANCH_W), 0.1),
        'rwkv_aup_f': nrm((L, RWKV_A_LR, BRANCH_W), 0.5 * RWKV_A_LR ** -0.5),
        'rwkv_a0_b': nrm((L, BRANCH_W), 0.1),
        'rwkv_aup_b': nrm((L, RWKV_A_LR, BRANCH_W), 0.5 * RWKV_A_LR ** -0.5),
        'rwkv_kk': 0.85 + nrm((L, BRANCH_W), 0.05),
        'rwkv_ka': 1.0 + nrm((L, BRANCH_W), 0.05),
        'rwkv_rk': nrm((L, RWKV_H, RWKV_HD), 0.1),
        'rwkv_ln_g': 1.0 + nrm((L, RWKV_H, RWKV_HD), 0.05),
        'rwkv_ln_b': nrm((L, RWKV_H, RWKV_HD), 0.02),
        'w_o_gla': nrm((L, BRANCH_W, D), BRANCH_W ** -0.5),
        'w_o_att': nrm((L, BRANCH_W, D), BRANCH_W ** -0.5),
        'w_o_rwkv': nrm((L, BRANCH_W, D), BRANCH_W ** -0.5),
        'w_out': nrm((L, D, D), D ** -0.5),
        'g_post': 1.0 + nrm((L, D), 0.05),
    }


def reference(x, c, ctx, c_ctx, w_mod, b_mod, g_pre, w_in, gla_wup_f, gla_b_f, gla_wup_b,
              gla_b_b, gla_norm, att_qnorm, att_knorm, rwkv_mu, rwkv_w0_f, rwkv_wup_f,
              rwkv_w0_b, rwkv_wup_b, rwkv_a0_f, rwkv_aup_f, rwkv_a0_b, rwkv_aup_b, rwkv_kk,
              rwkv_ka, rwkv_rk, rwkv_ln_g, rwkv_ln_b, w_o_gla, w_o_att, w_o_rwkv, w_out, g_post):
    n_lat = x.shape[1]
    ROWS = n_lat // GRID_W
    row_pos = jnp.repeat(jnp.arange(ROWS, dtype=F32), GRID_W)
    col_pos = jnp.tile(jnp.arange(GRID_W, dtype=F32), ROWS)
    xc = ctx
    for l in range(DEPTH):
        x, xc = layer(x, xc, c, c_ctx, row_pos, col_pos, l < DEPTH - 1,
                      w_mod[l], b_mod[l], g_pre[l], w_in[l],
                      gla_wup_f[l], gla_b_f[l], gla_wup_b[l], gla_b_b[l], gla_norm[l],
                      att_qnorm[l], att_knorm[l],
                      rwkv_mu[l], rwkv_w0_f[l], rwkv_wup_f[l], rwkv_w0_b[l], rwkv_wup_b[l],
                      rwkv_a0_f[l], rwkv_aup_f[l], rwkv_a0_b[l], rwkv_aup_b[l], rwkv_kk[l],
                      rwkv_ka[l], rwkv_rk[l], rwkv_ln_g[l], rwkv_ln_b[l],
                      w_o_gla[l], w_o_att[l], w_o_rwkv[l], w_out[l], g_post[l])
    return x
```

```python
import functools

import jax
import jax.numpy as jnp
from jax import lax
from jax.experimental import pallas as pl
from jax.experimental.pallas import tpu as pltpu

F32 = jnp.float32
BF16 = jnp.bfloat16

GRID_W = 64
BRANCH_W = 512
GLA_H, GLA_DK, GLA_DV, GLA_LR = 4, 64, 128, 16
GLA_TAU = 16.0
ATT_H, ATT_KV, ATT_HD = 4, 2, 128
ROPE_THETA = 10000.0
RWKV_H, RWKV_HD, RWKV_LR = 8, 64, 64
RWKV_DECAY_SCALE = 0.6065306597
NORM_EPS = 1e-6
GN_EPS = 64e-5
L2_EPS = 1e-12

CHUNK = 64
LANES = 128
VMEM_LIMIT = 56 * 1024 * 1024

CB_MG = 0
CB_GLA_QK = 24
CB_GLA_V = 28
CB_GLA_GATE = 32
CB_GLA_WD = 36
CB_ATT_Q = 37
CB_ATT_V = 43
CB_ATT_GATE = 45
CB_RW_R = 49
CB_RW_K = 53
CB_RW_V = 57
CB_RW_GATE = 61
CB_RW_WD = 65
CB_RW_AD = 66
NP_BLOCKS = 68
NP = NP_BLOCKS * LANES
TN_PROJ = NP // 4


def _cparams(sem):
    return pltpu.CompilerParams(dimension_semantics=sem, vmem_limit_bytes=VMEM_LIMIT)


def _dot(a, b):
    return jnp.dot(a, b, preferred_element_type=F32)


def _dot_nt(a, b):
    return lax.dot_general(a, b, (((1,), (1,)), ((), ())), preferred_element_type=F32)


def _dot_tn(a, b):
    return lax.dot_general(a, b, (((0,), (0,)), ((), ())), preferred_element_type=F32)


def _silu(x):
    return x / (1.0 + jnp.exp(-x))


def _sigmoid(x):
    return 1.0 / (1.0 + jnp.exp(-x))


def _split3(x):
    hi = x.astype(BF16)
    r1 = x - hi.astype(F32)
    mid = r1.astype(BF16)
    lo = (r1 - mid.astype(F32)).astype(BF16)
    return hi, mid, lo


def _tri_cumsum(tri, x):
    hi, mid, lo = _split3(x)
    return _dot(tri, hi) + _dot(tri, mid) + _dot(tri, lo)


def _chunk_masks(reverse):
    t = lax.broadcasted_iota(jnp.int32, (CHUNK, CHUNK), 0)
    s = lax.broadcasted_iota(jnp.int32, (CHUNK, CHUNK), 1)
    incl = (s >= t) if reverse else (s <= t)
    strict = (s > t) if reverse else (s < t)
    return incl.astype(BF16), incl, strict


def _mod_kernel(c_ref, w_ref, b_ref, o_ref):
    c = c_ref[...]
    s = _silu(c)
    o_ref[...] = jnp.dot(s, w_ref[...], preferred_element_type=F32,
                         precision=lax.Precision.HIGHEST) + b_ref[...]


def _modulation(cond, w_mod, b_mod):
    nb, d = cond.shape
    n = w_mod.shape[1]
    tn = 768
    return pl.pallas_call(
        _mod_kernel,
        grid=(n // tn,),
        in_specs=[pl.BlockSpec((nb, d), lambda j: (0, 0)),
                  pl.BlockSpec((d, tn), lambda j: (0, j)),
                  pl.BlockSpec((1, tn), lambda j: (0, j))],
        out_specs=pl.BlockSpec((nb, tn), lambda j: (0, j)),
        out_shape=jax.ShapeDtypeStruct((nb, n), F32),
        compiler_params=_cparams(("parallel",)),
        name="modulation",
    )(cond, w_mod, b_mod.reshape(1, n))


def _inproj_kernel(x_ref, mod_ref, g_ref, w_ref, o_ref, h_ref):
    @pl.when(pl.program_id(1) == 0)
    def _():
        x = x_ref[...]
        ms = jnp.mean(x * x, axis=-1, keepdims=True)
        y = x * lax.rsqrt(ms + NORM_EPS) * g_ref[...]
        mod = mod_ref[0]
        h_ref[...] = (y * (1.0 + mod[1:2]) + mod[0:1]).astype(BF16)

    o_ref[...] = _dot(h_ref[...], w_ref[...]).astype(o_ref.dtype)


def _inproj(x2d, mod, mod_index, g_pre, w_cat, tm):
    r, d = x2d.shape
    return pl.pallas_call(
        _inproj_kernel,
        grid=(r // tm, NP // TN_PROJ),
        in_specs=[pl.BlockSpec((tm, d), lambda i, j: (i, 0)),
                  pl.BlockSpec((1, 3, d), lambda i, j: (mod_index(i), 0, 0)),
                  pl.BlockSpec((1, d), lambda i, j: (0, 0)),
                  pl.BlockSpec((d, TN_PROJ), lambda i, j: (0, j))],
        out_specs=pl.BlockSpec((tm, TN_PROJ), lambda i, j: (i, j)),
        out_shape=jax.ShapeDtypeStruct((r, NP), BF16),
        scratch_shapes=[pltpu.VMEM((tm, d), BF16)],
        compiler_params=_cparams(("parallel", "arbitrary")),
        name="inproj",
    )(x2d, mod, g_pre.reshape(1, d), w_cat)


def _qkprep_kernel(*refs, rope):
    if rope:
        p_ref, g_ref, cos_ref, sa_ref, sb_ref, o_ref = refs
    else:
        p_ref, g_ref, o_ref = refs
    x = p_ref[...].astype(F32)
    ms = jnp.mean(x * x, axis=-1, keepdims=True)
    y = x * lax.rsqrt(ms + NORM_EPS) * g_ref[0]
    if rope:
        y = (y * cos_ref[...] + pltpu.roll(y, 96, 1) * sa_ref[...]
             + pltpu.roll(y, 32, 1) * sb_ref[...])
    o_ref[...] = y.astype(o_ref.dtype)


def _qkprep(p, gains, tables, seq, tm):
    r = p.shape[0]
    nheads = ATT_H + ATT_KV
    in_specs = [pl.BlockSpec((tm, LANES), lambda i, h: (i, CB_ATT_Q + h)),
                pl.BlockSpec((1, 1, LANES), lambda i, h: (h, 0, 0))]
    args = [p, gains]
    if tables is not None:
        nt = seq // tm
        for tab in tables:
            in_specs.append(pl.BlockSpec((tm, LANES), lambda i, h: (i % nt, 0)))
            args.append(tab)
    return pl.pallas_call(
        functools.partial(_qkprep_kernel, rope=tables is not None),
        grid=(r // tm, nheads),
        in_specs=in_specs,
        out_specs=pl.BlockSpec((tm, LANES), lambda i, h: (i, h)),
        out_shape=jax.ShapeDtypeStruct((r, nheads * LANES), BF16),
        compiler_params=_cparams(("parallel", "parallel")),
        name="qkprep_rope" if tables is not None else "qkprep",
    )(*args)


def _flash_kernel(*refs, n_lat_blocks, tk, has_ctx_kv):
    if n_lat_blocks and has_ctx_kv:
        q0_ref, q1_ref, g0_ref, g1_ref, kl_ref, vl_ref, kc_ref, vc_ref, o_ref = refs
    elif n_lat_blocks:
        q0_ref, q1_ref, g0_ref, g1_ref, kl_ref, vl_ref, o_ref = refs
    else:
        q0_ref, q1_ref, g0_ref, g1_ref, kc_ref, vc_ref, o_ref = refs
    tq = q0_ref.shape[0]
    q = jnp.concatenate([q0_ref[...], q1_ref[...]], axis=0)

    def block(k, v, carry):
        m, l, acc = carry
        s = _dot_nt(q, k)
        m_new = jnp.maximum(m, jnp.max(s, axis=-1, keepdims=True))
        alpha = jnp.exp(m - m_new)
        p = jnp.exp(s - m_new)
        l = alpha * l + jnp.sum(p, axis=-1, keepdims=True)
        acc = alpha * acc + _dot(p.astype(BF16), v)
        return m_new, l, acc

    carry = (jnp.full((2 * tq, 1), -jnp.inf, F32), jnp.zeros((2 * tq, 1), F32),
             jnp.zeros((2 * tq, LANES), F32))
    if n_lat_blocks:
        def body(i, c):
            rows = pl.ds(pl.multiple_of(i * tk, tk), tk)
            return block(kl_ref[rows, :], vl_ref[rows, :], c)
        carry = lax.fori_loop(0, n_lat_blocks, body, carry)
    if has_ctx_kv:
        carry = block(kc_ref[...], vc_ref[...], carry)
    _, l, acc = carry
    o = acc / l
    o_ref[:, :LANES] = (o[:tq] * _silu(g0_ref[...].astype(F32))).astype(o_ref.dtype)
    o_ref[:, LANES:] = (o[tq:] * _silu(g1_ref[...].astype(F32))).astype(o_ref.dtype)


def _flash(qk_q, p_q, seq_q, lat_kv, ctx_kv, batch, tq, tk):
    nq = seq_q // tq
    in_specs = [
        pl.BlockSpec((tq, LANES), lambda b, j, i: (b * nq + i, 2 * j)),
        pl.BlockSpec((tq, LANES), lambda b, j, i: (b * nq + i, 2 * j + 1)),
        pl.BlockSpec((tq, LANES), lambda b, j, i: (b * nq + i, CB_ATT_GATE + 2 * j)),
        pl.BlockSpec((tq, LANES), lambda b, j, i: (b * nq + i, CB_ATT_GATE + 2 * j + 1)),
    ]
    args = [qk_q, qk_q, p_q, p_q]
    n_lat_blocks = 0
    for seg in (lat_kv, ctx_kv):
        if seg is None:
            continue
        qk_s, p_s, seq_s = seg
        in_specs.append(pl.BlockSpec((seq_s, LANES), lambda b, j, i: (b, ATT_H + j)))
        in_specs.append(pl.BlockSpec((seq_s, LANES), lambda b, j, i: (b, CB_ATT_V + j)))
        args += [qk_s, p_s]
    if lat_kv is not None:
        n_lat_blocks = lat_kv[2] // tk
    return pl.pallas_call(
        functools.partial(_flash_kernel, n_lat_blocks=n_lat_blocks, tk=tk,
                          has_ctx_kv=ctx_kv is not None),
        grid=(batch, ATT_KV, nq),
        in_specs=in_specs,
        out_specs=pl.BlockSpec((tq, 2 * LANES), lambda b, j, i: (b * nq + i, j)),
        out_shape=jax.ShapeDtypeStruct((batch * seq_q, BRANCH_W), BF16),
        compiler_params=_cparams(("parallel", "parallel", "arbitrary")),
        name="flash_lat" if lat_kv is not None else "flash_ctx",
    )(*args)


def _gla_chunk(qk_ref, v_ref, wd_ref, acc_ref, c, st, wup, bias, masks):
    tri, incl, _ = masks
    rows = pl.ds(pl.multiple_of(c * CHUNK, CHUNK), CHUNK)
    qk = qk_ref[rows, :].astype(F32)
    q = qk[:, :GLA_DK] * (GLA_DK ** -0.5)
    k = qk[:, GLA_DK:]
    v = v_ref[rows, :]
    z = _dot(wd_ref[rows, :], wup) + bias
    g = (jnp.minimum(z, 0.0) - jnp.log1p(jnp.exp(-jnp.abs(z)))) * (1.0 / GLA_TAU)
    gc = _tri_cumsum(tri, g)
    tot = jnp.sum(g, axis=0, keepdims=True)
    q_dec = (q * jnp.exp(gc)).astype(BF16)
    k_inv = (k * jnp.exp(-gc)).astype(BF16)
    k_tail = (k * jnp.exp(tot - gc)).astype(BF16)
    a = jnp.where(incl, _dot_nt(q_dec, k_inv), 0.0)
    o = _dot(a.astype(BF16), v) + _dot_nt(q_dec, st.astype(BF16))
    acc_ref[rows, :] += o
    return st * jnp.exp(tot) + _dot_tn(v, k_tail)


def _gla_kernel(*refs, need_ctx):
    (qk_l, v_l, gate_l, wd_l, qk_c, v_c, gate_c, wd_c,
     wupf_ref, wupb_ref, bf_ref, bb_ref, gn_ref) = refs[:13]
    if need_ctx:
        y_l, y_c, acc_l, acc_c = refs[13:]
    else:
        y_l, acc_l, acc_c = refs[13:]
    nl = qk_l.shape[0] // CHUNK
    nc = qk_c.shape[0] // CHUNK
    wupf, wupb = wupf_ref[0], wupb_ref[0]
    bias_f, bias_b = bf_ref[0], bb_ref[0]
    masks_f, masks_b = _chunk_masks(False), _chunk_masks(True)
    acc_l[...] = jnp.zeros_like(acc_l)
    acc_c[...] = jnp.zeros_like(acc_c)

    def make_body(qk, v, wd, acc, n):
        def body(i, carry):
            st_f, st_b = carry
            st_f = _gla_chunk(qk, v, wd, acc, i, st_f, wupf, bias_f, masks_f)
            st_b = _gla_chunk(qk, v, wd, acc, n - 1 - i, st_b, wupb, bias_b, masks_b)
            return st_f, st_b
        return body

    zero = jnp.zeros((GLA_DV, GLA_DK), F32)
    carry = lax.fori_loop(0, nc, make_body(qk_c, v_c, wd_c, acc_c, nc), (zero, zero))
    lax.fori_loop(0, nl, make_body(qk_l, v_l, wd_l, acc_l, nl), carry)

    gn = gn_ref[0]

    def finish(acc, gate, y, n):
        def body(i, _):
            rows = pl.ds(pl.multiple_of(i * CHUNK, CHUNK), CHUNK)
            o = acc[rows, :]
            ms = jnp.mean(o * o, axis=-1, keepdims=True)
            o = o * lax.rsqrt(ms + NORM_EPS) * gn
            y[rows, :] = (o * _silu(gate[rows, :].astype(F32))).astype(y.dtype)
            return 0
        lax.fori_loop(0, n, body, 0)

    finish(acc_l, gate_l, y_l, nl)
    if need_ctx:
        finish(acc_c, gate_c, y_c, nc)


def _gla(p_lat, p_ctx, batch, seq, ctx_len, wupf, wupb, bf, bb, gnorm, need_ctx):
    def seg_specs(n):
        return [pl.BlockSpec((n, LANES), lambda b, h: (b, CB_GLA_QK + h)),
                pl.BlockSpec((n, LANES), lambda b, h: (b, CB_GLA_V + h)),
                pl.BlockSpec((n, LANES), lambda b, h: (b, CB_GLA_GATE + h)),
                pl.BlockSpec((n, LANES), lambda b, h: (b, CB_GLA_WD))]
    w_spec = pl.BlockSpec((1, LANES, GLA_DK), lambda b, h: (h, 0, 0))
    b_spec = pl.BlockSpec((1, 1, GLA_DK), lambda b, h: (h, 0, 0))
    g_spec = pl.BlockSpec((1, 1, GLA_DV), lambda b, h: (h, 0, 0))
    out_specs = [pl.BlockSpec((seq, LANES), lambda b, h: (b, h))]
    out_shape = [jax.ShapeDtypeStruct((batch * seq, BRANCH_W), BF16)]
    if need_ctx:
        out_specs.append(pl.BlockSpec((ctx_len, LANES), lambda b, h: (b, h)))
        out_shape.append(jax.ShapeDtypeStruct((batch * ctx_len, BRANCH_W), BF16))
    outs = pl.pallas_call(
        functools.partial(_gla_kernel, need_ctx=need_ctx),
        grid=(batch, GLA_H),
        in_specs=seg_specs(seq) + seg_specs(ctx_len) + [w_spec, w_spec, b_spec, b_spec, g_spec],
        out_specs=out_specs,
        out_shape=out_shape,
        scratch_shapes=[pltpu.VMEM((seq, LANES), F32), pltpu.VMEM((ctx_len, LANES), F32)],
        compiler_params=_cparams(("parallel", "parallel")),
        name="gla",
    )(p_lat, p_lat, p_lat, p_lat, p_ctx, p_ctx, p_ctx, p_ctx, wupf, wupb, bf, bb, gnorm)
    return (outs[0], outs[1]) if need_ctx else (outs[0], None)


def _seg_sum(x, lo):
    s0 = jnp.sum(jnp.where(lo, x, 0.0), axis=-1, keepdims=True)
    s1 = jnp.sum(jnp.where(lo, 0.0, x), axis=-1, keepdims=True)
    return jnp.where(lo, s0, s1)


def _unit_tri_inverse_offdiag(a, blockdiag):
    def mm(x, y):
        return _dot(x.astype(BF16), y.astype(BF16))

    n = -a
    nd = jnp.where(blockdiag, n, 0.0)
    no = n - nd
    n2 = mm(nd, nd)
    n4 = mm(n2, n2)
    n8 = mm(n4, n4)
    x = nd + n2 + mm(nd, n2)
    x = x + n4 + mm(x, n4)
    xd = x + n8 + mm(x, n8)
    m = no + mm(xd, no)
    m2 = mm(m, m)
    xq = m + m2 + mm(m, m2)
    return xq + xd + mm(xq, xd)


def _shifted(ref, c, n_chunks, mu, row):
    base = pl.multiple_of(c * CHUNK, CHUNK)
    main = ref[pl.ds(base, CHUNK), :].astype(F32)
    pstart = pl.multiple_of(jnp.maximum(base - 16, 0), 16)
    nstart = pl.multiple_of(jnp.minimum(base + CHUNK, (n_chunks - 1) * CHUNK + CHUNK - 16), 16)
    prev_row = ref[pl.ds(pstart, 16), :].astype(F32)[15:16, :]
    next_row = ref[pl.ds(nstart, 16), :].astype(F32)[0:1, :]
    prev_row = jnp.where(c > 0, prev_row, 0.0)
    next_row = jnp.where(c < n_chunks - 1, next_row, 0.0)
    up = jnp.where(row == 0, prev_row, pltpu.roll(main, 1, 0))
    dn = jnp.where(row == CHUNK - 1, next_row, pltpu.roll(main, CHUNK - 1, 0))
    return main + mu * (0.5 * (up + dn) - main)


def _rwkv_chunk(seg, c, reverse, s2, prm, consts, write_gate):
    (r_ref, k_ref, v_ref, g_ref, wd_ref, ad_ref, yacc, bacc, gsil, n_chunks) = seg
    (mu_r, mu_k, mu_v, mu_g, mu_wd, mu_ad, w0, a0, kkw, kaw, rkw, wup, aup) = prm
    (row, lo, blockdiag16, headdiag, masks, strict_hi, incl_wide) = consts
    tri, _, strict = masks
    rows = pl.ds(pl.multiple_of(c * CHUNK, CHUNK), CHUNK)

    r = _shifted(r_ref, c, n_chunks, mu_r, row)
    k = _shifted(k_ref, c, n_chunks, mu_k, row)
    v = _shifted(v_ref, c, n_chunks, mu_v, row)
    wd = _shifted(wd_ref, c, n_chunks, mu_wd, row)
    ad = _shifted(ad_ref, c, n_chunks, mu_ad, row)
    if write_gate:
        gsil[rows, :] = _silu(_shifted(g_ref, c, n_chunks, mu_g, row))

    lw = -RWKV_DECAY_SCALE * _sigmoid(w0 + _dot(jnp.tanh(wd).astype(BF16), wup))
    a = _sigmoid(a0 + _dot(ad.astype(BF16), aup))
    kd = k * (1.0 + (a - 1.0) * kaw)
    kk = k * kkw
    kk = kk * lax.rsqrt(_seg_sum(kk * kk, lo) + L2_EPS)
    kka = kk * a

    cw = _tri_cumsum(tri, lw)
    tot = jnp.sum(lw, axis=0, keepdims=True)
    e_neg = jnp.exp(-cw)
    e_tail = jnp.exp(tot - cw)
    kap = kk * jnp.exp(cw - lw)
    rt = r * jnp.exp(cw)
    bt = (kka * e_neg).astype(BF16)
    kt = (kd * e_neg).astype(BF16)
    v16 = v.astype(BF16)

    yt = jnp.concatenate([bt, kt], axis=0)
    ks_rs = _dot_nt(jnp.concatenate([kap, rt], axis=0).astype(BF16), s2.astype(BF16))
    ks, rs = ks_rs[:CHUNK], ks_rs[CHUNK:]

    vv = jnp.concatenate([v16, v16], axis=0)
    e_heads, mats = [], []
    for head_lo in (True, False):
        hm = lo if head_lo else jnp.logical_not(lo)
        xh = jnp.concatenate([jnp.where(hm, kap, 0.0), jnp.where(hm, rt, 0.0)], axis=0)
        g = _dot_nt(xh.astype(BF16), yt)
        a_b = jnp.where(strict, g[:CHUNK, :CHUNK], 0.0)
        a_k_wide = jnp.where(strict_hi, g[:CHUNK], 0.0)
        x_inv = _unit_tri_inverse_offdiag(a_b, blockdiag16)
        rhs = ks + _dot(a_k_wide.astype(BF16), vv)
        e_heads.append(rhs + _dot(x_inv.astype(BF16), rhs.astype(BF16)))
        mats.append(jnp.where(incl_wide, g[CHUNK:], 0.0).astype(BF16))
    e = jnp.where(lo, e_heads[0], e_heads[1])
    ev = jnp.concatenate([-e, v], axis=0).astype(BF16)
    y = rs + jnp.where(lo, _dot(mats[0], ev), _dot(mats[1], ev))
    yacc[rows, :] += y
    bacc[rows, :] += _seg_sum(r * kd * rkw, lo) * v

    tails = jnp.concatenate([kka * e_tail, kd * e_tail], axis=0).astype(BF16)
    upd = _dot_tn(ev, tails)
    return s2 * jnp.exp(tot) + jnp.where(headdiag, upd, 0.0)


def _rwkv_kernel(*refs, need_ctx):
    lat = refs[0:6]
    ctx = refs[6:12]
    (mu_r, mu_k, mu_v, mu_g, mu_wd, mu_ad, w0f, w0b, a0f, a0b, kkw, kaw, rkw, lng, lnb,
     wupf, wupb, aupf, aupb) = [x[0] for x in refs[12:31]]
    rest = refs[31:]
    if need_ctx:
        y_l, y_c = rest[:2]
        scr = rest[2:]
    else:
        y_l, y_c = rest[0], None
        scr = rest[1:]
    yacc_l, bacc_l, gs_l, yacc_c, bacc_c, gs_c = scr
    nl = lat[0].shape[0] // CHUNK
    nc = ctx[0].shape[0] // CHUNK

    row = lax.broadcasted_iota(jnp.int32, (CHUNK, LANES), 0)
    lo = lax.broadcasted_iota(jnp.int32, (1, LANES), 1) < RWKV_HD
    t64 = lax.broadcasted_iota(jnp.int32, (CHUNK, CHUNK), 0)
    s64 = lax.broadcasted_iota(jnp.int32, (CHUNK, CHUNK), 1)
    blockdiag16 = (t64 // 16) == (s64 // 16)
    v128 = lax.broadcasted_iota(jnp.int32, (LANES, LANES), 0)
    k128 = lax.broadcasted_iota(jnp.int32, (LANES, LANES), 1)
    headdiag = (v128 < RWKV_HD) == (k128 < RWKV_HD)
    tw = lax.broadcasted_iota(jnp.int32, (CHUNK, LANES), 0)
    lw_ = lax.broadcasted_iota(jnp.int32, (CHUNK, LANES), 1)
    sw = lw_ % CHUNK

    def wide_masks(reverse):
        strict_hi = (lw_ >= CHUNK) & ((sw > tw) if reverse else (sw < tw))
        incl_wide = (sw >= tw) if reverse else (sw <= tw)
        return strict_hi, incl_wide

    consts_f = (row, lo, blockdiag16, headdiag, _chunk_masks(False)) + wide_masks(False)
    consts_b = (row, lo, blockdiag16, headdiag, _chunk_masks(True)) + wide_masks(True)
    shared = (mu_r, mu_k, mu_v, mu_g, mu_wd, mu_ad)
    prm_f = shared + (w0f, a0f, kkw, kaw, rkw, wupf, aupf)
    prm_b = shared + (w0b, a0b, kkw, kaw, rkw, wupb, aupb)

    for acc in (yacc_l, bacc_l, yacc_c, bacc_c):
        acc[...] = jnp.zeros_like(acc)

    def make_body(seg, n):
        def body(i, carry):
            s_f, s_b = carry
            s_f = _rwkv_chunk(seg, i, False, s_f, prm_f, consts_f, True)
            s_b = _rwkv_chunk(seg, n - 1 - i, True, s_b, prm_b, consts_b, False)
            return s_f, s_b
        return body

    seg_l = tuple(lat) + (yacc_l, bacc_l, gs_l, nl)
    seg_c = tuple(ctx) + (yacc_c, bacc_c, gs_c, nc)
    zero = jnp.zeros((LANES, LANES), F32)
    carry = lax.fori_loop(0, nc, make_body(seg_c, nc), (zero, zero))
    lax.fori_loop(0, nl, make_body(seg_l, nl), carry)

    def finish(yacc, bacc, gs, y_out, n):
        def body(i, _):
            rows = pl.ds(pl.multiple_of(i * CHUNK, CHUNK), CHUNK)
            y = yacc[rows, :]
            mean = _seg_sum(y, lo) * (1.0 / RWKV_HD)
            d = y - mean
            var = _seg_sum(d * d, lo) * (1.0 / RWKV_HD)
            yn = d * lax.rsqrt(var + GN_EPS) * lng + lnb
            y_out[rows, :] = ((yn + bacc[rows, :]) * gs[rows, :]).astype(y_out.dtype)
            return 0
        lax.fori_loop(0, n, body, 0)

    finish(yacc_l, bacc_l, gs_l, y_l, nl)
    if need_ctx:
        finish(yacc_c, bacc_c, gs_c, y_c, nc)


def _rwkv(p_lat, p_ctx, batch, seq, ctx_len, prm, need_ctx):
    def seg_specs(n):
        return [pl.BlockSpec((n, LANES), lambda b, h: (b, CB_RW_R + h)),
                pl.BlockSpec((n, LANES), lambda b, h: (b, CB_RW_K + h)),
                pl.BlockSpec((n, LANES), lambda b, h: (b, CB_RW_V + h)),
                pl.BlockSpec((n, LANES), lambda b, h: (b, CB_RW_GATE + h)),
                pl.BlockSpec((n, LANES), lambda b, h: (b, CB_RW_WD)),
                pl.BlockSpec((n, LANES), lambda b, h: (b, CB_RW_AD))]
    vec_h = pl.BlockSpec((1, 1, LANES), lambda b, h: (h, 0, 0))
    vec_s = pl.BlockSpec((1, 1, LANES), lambda b, h: (0, 0, 0))
    mat_h = pl.BlockSpec((1, LANES, LANES), lambda b, h: (h, 0, 0))
    prm_specs = [vec_h, vec_h, vec_h, vec_h, vec_s, vec_s] + [vec_h] * 9 + [mat_h] * 4
    out_specs = [pl.BlockSpec((seq, LANES), lambda b, h: (b, h))]
    out_shape = [jax.ShapeDtypeStruct((batch * seq, BRANCH_W), BF16)]
    if need_ctx:
        out_specs.append(pl.BlockSpec((ctx_len, LANES), lambda b, h: (b, h)))
        out_shape.append(jax.ShapeDtypeStruct((batch * ctx_len, BRANCH_W), BF16))
    scratch = [pltpu.VMEM((seq, LANES), F32)] * 3 + [pltpu.VMEM((ctx_len, LANES), F32)] * 3
    outs = pl.pallas_call(
        functools.partial(_rwkv_kernel, need_ctx=need_ctx),
        grid=(batch, RWKV_H // 2),
        in_specs=seg_specs(seq) + seg_specs(ctx_len) + prm_specs,
        out_specs=out_specs,
        out_shape=out_shape,
        scratch_shapes=scratch,
        compiler_params=_cparams(("parallel", "parallel")),
        name="rwkv",
    )(*([p_lat] * 6 + [p_ctx] * 6 + list(prm)))
    return (outs[0], outs[1]) if need_ctx else (outs[0], None)


def _merge_kernel(x_ref, y1_ref, y2_ref, y3_ref, g1_ref, g2_ref, g3_ref, mod_ref,
                  wo1_ref, wo2_ref, wo3_ref, wout_ref, gp_ref, o_ref):
    m = (_sigmoid(g1_ref[...].astype(F32)) * _dot(y1_ref[...], wo1_ref[...])
         + _sigmoid(g2_ref[...].astype(F32)) * _dot(y2_ref[...], wo2_ref[...])
         + _sigmoid(g3_ref[...].astype(F32)) * _dot(y3_ref[...], wo3_ref[...]))
    mo = _dot(m.astype(BF16), wout_ref[...])
    ms = jnp.mean(mo * mo, axis=-1, keepdims=True)
    out = mo * lax.rsqrt(ms + NORM_EPS) * gp_ref[...]
    o_ref[...] = x_ref[...] + mod_ref[0][2:3] * out


def _merge(x2d, y1, y2, y3, p, mod, mod_index, wo1, wo2, wo3, wout, g_post, tm):
    r, d = x2d.shape
    row = lambda i: (i, 0)
    const = lambda i: (0, 0)
    return pl.pallas_call(
        _merge_kernel,
        grid=(r // tm,),
        in_specs=[pl.BlockSpec((tm, d), row),
                  pl.BlockSpec((tm, BRANCH_W), row),
                  pl.BlockSpec((tm, BRANCH_W), row),
                  pl.BlockSpec((tm, BRANCH_W), row),
                  pl.BlockSpec((tm, d), lambda i: (i, 0)),
                  pl.BlockSpec((tm, d), lambda i: (i, 1)),
                  pl.BlockSpec((tm, d), lambda i: (i, 2)),
                  pl.BlockSpec((1, 3, d), lambda i: (mod_index(i), 0, 0)),
                  pl.BlockSpec((BRANCH_W, d), const),
                  pl.BlockSpec((BRANCH_W, d), const),
                  pl.BlockSpec((BRANCH_W, d), const),
                  pl.BlockSpec((d, d), const),
                  pl.BlockSpec((1, d), const)],
        out_specs=pl.BlockSpec((tm, d), row),
        out_shape=jax.ShapeDtypeStruct((r, d), F32),
        compiler_params=_cparams(("parallel",)),
        name="merge",
    )(x2d, y1, y2, y3, p, p, p, mod, wo1, wo2, wo3, wout, g_post.reshape(1, d))


def _pack_w_in(w_in):
    d = w_in.shape[0]
    o = 0
    gq = w_in[:, o:o + 256]; o += 256
    gk = w_in[:, o:o + 256]; o += 256
    gv = w_in[:, o:o + 512]; o += 512
    gg = w_in[:, o:o + 512]; o += 512
    gwd = w_in[:, o:o + 32]; o += 32
    att = w_in[:, o:o + 1536]; o += 1536
    rw = w_in[:, o:o + 2304]; o += 2304
    mg = w_in[:, o:o + 3072]
    qk = jnp.concatenate([gq.reshape(d, GLA_H, GLA_DK), gk.reshape(d, GLA_H, GLA_DK)],
                         axis=-1).reshape(d, GLA_H * 2 * GLA_DK)
    pad_wd = jnp.zeros((d, LANES - 32), w_in.dtype)
    pad_end = jnp.zeros((d, LANES), w_in.dtype)
    return jnp.concatenate([mg, qk, gv, gg, gwd, pad_wd, att, rw, pad_end], axis=1).astype(BF16)


def _rope_tables(seq):
    quarter = ATT_HD // 4
    inv = ROPE_THETA ** (-jnp.arange(quarter, dtype=F32) / quarter)
    t = jnp.arange(seq)
    row_pos = (t // GRID_W).astype(F32)
    col_pos = (t % GRID_W).astype(F32)
    ar = row_pos[:, None] * inv[None, :]
    ac = col_pos[:, None] * inv[None, :]
    z = jnp.zeros_like(ar)
    cos = jnp.concatenate([jnp.cos(ar), jnp.cos(ar), jnp.cos(ac), jnp.cos(ac)], axis=1)
    sa = jnp.concatenate([-jnp.sin(ar), z, -jnp.sin(ac), z], axis=1)
    sb = jnp.concatenate([z, jnp.sin(ar), z, jnp.sin(ac)], axis=1)
    return cos, sa, sb


def _pad_rows(w, lo_half):
    z = jnp.zeros((LANES - w.shape[0], w.shape[1]), w.dtype)
    if lo_half:
        return jnp.concatenate([w, z], axis=0)
    half = LANES // 2
    return jnp.concatenate([z[:half], w, z[half:]], axis=0)


def _pick(n, target):
    t = min(n, target)
    while n % t:
        t //= 2
    return t


def kernel(x, c, ctx, c_ctx, w_mod, b_mod, g_pre, w_in, gla_wup_f, gla_b_f, gla_wup_b, gla_b_b, gla_norm, att_qnorm, att_knorm, rwkv_mu, rwkv_w0_f, rwkv_wup_f, rwkv_w0_b, rwkv_wup_b, rwkv_a0_f, rwkv_aup_f, rwkv_a0_b, rwkv_aup_b, rwkv_kk, rwkv_ka, rwkv_rk, rwkv_ln_g, rwkv_ln_b, w_o_gla, w_o_att, w_o_rwkv, w_out, g_post):
    batch, seq, d = x.shape
    ctx_len = ctx.shape[1]
    depth = w_in.shape[0]
    assert seq % CHUNK == 0 and ctx_len % CHUNK == 0 and seq % GRID_W == 0

    nb = -(-(batch + 1) // 8) * 8
    cond = jnp.concatenate([c, c_ctx[None, :], jnp.zeros((nb - batch - 1, d), F32)], axis=0)
    tables = _rope_tables(seq)

    tm_lat = _pick(seq, 1024)
    tm_ctx = _pick(batch * ctx_len, 1024)

    def lat_mod(tm):
        return lambda i: i // (seq // tm)

    def ctx_mod(tm):
        return lambda i: batch
    tq_lat, tk = _pick(seq, 256), _pick(seq, 512)
    tq_ctx = _pick(ctx_len, 256)
    tm_prep = _pick(seq, 512)
    tm_prep_c = _pick(ctx_len, 512)
    tm_merge = _pick(seq, 512)
    tm_merge_c = _pick(batch * ctx_len, 512)

    x2 = x.reshape(batch * seq, d)
    xc2 = ctx.reshape(batch * ctx_len, d)

    def hp(v):
        return v.reshape(RWKV_H // 2, 1, LANES)

    for l in range(depth):
        need_ctx = l < depth - 1
        mod = _modulation(cond, w_mod[l], b_mod[l]).reshape(nb, 3, d)
        w_cat = _pack_w_in(w_in[l])
        p_lat = _inproj(x2, mod, lat_mod(tm_lat), g_pre[l], w_cat, tm_lat)
        p_ctx = _inproj(xc2, mod, ctx_mod(tm_ctx), g_pre[l], w_cat, tm_ctx)

        scale = ATT_HD ** -0.5
        gains = jnp.concatenate([jnp.tile(att_qnorm[l][None, :] * scale, (ATT_H, 1)),
                                 jnp.tile(att_knorm[l][None, :], (ATT_KV, 1))], axis=0)
        gains = gains.reshape(ATT_H + ATT_KV, 1, LANES)
        qk_lat = _qkprep(p_lat, gains, tables, seq, tm_prep)
        qk_ctx = _qkprep(p_ctx, gains, None, ctx_len, tm_prep_c)
        y2 = _flash(qk_lat, p_lat, seq, (qk_lat, p_lat, seq), (qk_ctx, p_ctx, ctx_len),
                    batch, tq_lat, tk)
        y2c = None
        if need_ctx:
            y2c = _flash(qk_ctx, p_ctx, ctx_len, None, (qk_ctx, p_ctx, ctx_len),
                         batch, tq_ctx, tk)

        def gla_w(w, lo_rows):
            w = w.reshape(GLA_LR, GLA_H, GLA_DK).transpose(1, 0, 2)
            z = jnp.zeros((GLA_H, LANES, GLA_DK), w.dtype)
            start = 0 if lo_rows else GLA_LR
            return lax.dynamic_update_slice(z, w, (0, start, 0)).astype(BF16)
        y1, y1c = _gla(p_lat, p_ctx, batch, seq, ctx_len,
                       gla_w(gla_wup_f[l], True), gla_w(gla_wup_b[l], False),
                       gla_b_f[l].reshape(GLA_H, 1, GLA_DK), gla_b_b[l].reshape(GLA_H, 1, GLA_DK),
                       gla_norm[l].reshape(GLA_H, 1, GLA_DV), need_ctx)

        mu = rwkv_mu[l]

        def rw_w(w, lo_rows):
            w = w.reshape(RWKV_LR, RWKV_H // 2, LANES).transpose(1, 0, 2)
            return jnp.stack([_pad_rows(w[i], lo_rows) for i in range(RWKV_H // 2)]).astype(BF16)
        prm = (hp(mu[0:512]), hp(mu[512:1024]), hp(mu[1024:1536]), hp(mu[1536:2048]),
               mu[2048:2176].reshape(1, 1, LANES), mu[2176:2304].reshape(1, 1, LANES),
               hp(rwkv_w0_f[l]), hp(rwkv_w0_b[l]), hp(rwkv_a0_f[l]), hp(rwkv_a0_b[l]),
               hp(rwkv_kk[l]), hp(rwkv_ka[l]), hp(rwkv_rk[l]), hp(rwkv_ln_g[l]), hp(rwkv_ln_b[l]),
               rw_w(rwkv_wup_f[l], True), rw_w(rwkv_wup_b[l], False),
               rw_w(rwkv_aup_f[l], True), rw_w(rwkv_aup_b[l], False))
        y3, y3c = _rwkv(p_lat, p_ctx, batch, seq, ctx_len, prm, need_ctx)

        wo1, wo2, wo3 = (w_o_gla[l].astype(BF16), w_o_att[l].astype(BF16),
                         w_o_rwkv[l].astype(BF16))
        wout = w_out[l].astype(BF16)
        x2 = _merge(x2, y1, y2, y3, p_lat, mod, lat_mod(tm_merge), wo1, wo2, wo3, wout, g_post[l], tm_merge)
        if need_ctx:
            xc2 = _merge(xc2, y1c, y2c, y3c, p_ctx, mod, ctx_mod(tm_merge_c), wo1, wo2, wo3, wout,
                         g_post[l], tm_merge_c)
    return x2.reshape(batch, seq, d)
```

```python
import functools

import jax
import jax.numpy as jnp
from jax import lax
from jax.experimental import pallas as pl
from jax.experimental.pallas import tpu as pltpu

F32 = jnp.float32
BF16 = jnp.bfloat16

GRID_W = 64
BRANCH_W = 512
GLA_H, GLA_DK, GLA_DV, GLA_LR = 4, 64, 128, 16
GLA_TAU = 16.0
ATT_H, ATT_KV, ATT_HD = 4, 2, 128
ROPE_THETA = 10000.0
RWKV_H, RWKV_HD, RWKV_LR = 8, 64, 64
RWKV_DECAY_SCALE = 0.6065306597
NORM_EPS = 1e-6
GN_EPS = 64e-5
L2_EPS = 1e-12

CHUNK = 64
LANES = 128
VMEM_LIMIT = 56 * 1024 * 1024

CB_MG = 0
CB_GLA_QK = 24
CB_GLA_V = 28
CB_GLA_GATE = 32
CB_GLA_WD = 36
CB_ATT_Q = 37
CB_ATT_V = 43
CB_ATT_GATE = 45
CB_RW_R = 49
CB_RW_K = 53
CB_RW_V = 57
CB_RW_GATE = 61
CB_RW_WD = 65
CB_RW_AD = 66
NP_BLOCKS = 68
NP = NP_BLOCKS * LANES
TN_PROJ = NP // 4


def _cparams(sem):
    return pltpu.CompilerParams(dimension_semantics=sem, vmem_limit_bytes=VMEM_LIMIT)


def _dot(a, b):
    return jnp.dot(a, b, preferred_element_type=F32)


def _dot_nt(a, b):
    return lax.dot_general(a, b, (((1,), (1,)), ((), ())), preferred_element_type=F32)


def _dot_tn(a, b):
    return lax.dot_general(a, b, (((0,), (0,)), ((), ())), preferred_element_type=F32)


def _silu(x):
    return x / (1.0 + jnp.exp(-x))


def _sigmoid(x):
    return 1.0 / (1.0 + jnp.exp(-x))


def _split3(x):
    hi = x.astype(BF16)
    r1 = x - hi.astype(F32)
    mid = r1.astype(BF16)
    lo = (r1 - mid.astype(F32)).astype(BF16)
    return hi, mid, lo


def _tri_cumsum(tri, x):
    hi, mid, lo = _split3(x)
    return _dot(tri, hi) + _dot(tri, mid) + _dot(tri, lo)


GLA_UNROLL = 8
RWKV_UNROLL = 4
FINISH_UNROLL = 4


def _unroll_factor(n_chunks, target):
    u = target
    while n_chunks % u:
        u //= 2
    return u


def _chunk_masks(reverse):
    t = lax.broadcasted_iota(jnp.int32, (CHUNK, CHUNK), 0)
    s = lax.broadcasted_iota(jnp.int32, (CHUNK, CHUNK), 1)
    incl = (s >= t) if reverse else (s <= t)
    strict = (s > t) if reverse else (s < t)
    return incl.astype(BF16), incl, strict


def _mod_kernel(c_ref, w_ref, b_ref, o_ref):
    c = c_ref[...]
    s = _silu(c)
    o_ref[...] = jnp.dot(s, w_ref[...], preferred_element_type=F32,
                         precision=lax.Precision.HIGHEST) + b_ref[...]


def _modulation(cond, w_mod, b_mod):
    nb, d = cond.shape
    n = w_mod.shape[1]
    tn = 768
    return pl.pallas_call(
        _mod_kernel,
        grid=(n // tn,),
        in_specs=[pl.BlockSpec((nb, d), lambda j: (0, 0)),
                  pl.BlockSpec((d, tn), lambda j: (0, j)),
                  pl.BlockSpec((1, tn), lambda j: (0, j))],
        out_specs=pl.BlockSpec((nb, tn), lambda j: (0, j)),
        out_shape=jax.ShapeDtypeStruct((nb, n), F32),
        compiler_params=_cparams(("parallel",)),
        name="modulation",
    )(cond, w_mod, b_mod.reshape(1, n))


def _inproj_kernel(x_ref, mod_ref, g_ref, w_ref, o_ref, h_ref):
    @pl.when(pl.program_id(1) == 0)
    def _():
        x = x_ref[...]
        ms = jnp.mean(x * x, axis=-1, keepdims=True)
        y = x * lax.rsqrt(ms + NORM_EPS) * g_ref[...]
        mod = mod_ref[0]
        h_ref[...] = (y * (1.0 + mod[1:2]) + mod[0:1]).astype(BF16)

    o_ref[...] = _dot(h_ref[...], w_ref[...]).astype(o_ref.dtype)


def _inproj(x2d, mod, mod_index, g_pre, w_cat, tm):
    r, d = x2d.shape
    return pl.pallas_call(
        _inproj_kernel,
        grid=(r // tm, NP // TN_PROJ),
        in_specs=[pl.BlockSpec((tm, d), lambda i, j: (i, 0)),
                  pl.BlockSpec((1, 3, d), lambda i, j: (mod_index(i), 0, 0)),
                  pl.BlockSpec((1, d), lambda i, j: (0, 0)),
                  pl.BlockSpec((d, TN_PROJ), lambda i, j: (0, j))],
        out_specs=pl.BlockSpec((tm, TN_PROJ), lambda i, j: (i, j)),
        out_shape=jax.ShapeDtypeStruct((r, NP), BF16),
        scratch_shapes=[pltpu.VMEM((tm, d), BF16)],
        compiler_params=_cparams(("parallel", "arbitrary")),
        name="inproj",
    )(x2d, mod, g_pre.reshape(1, d), w_cat)


def _qkprep_kernel(*refs, rope):
    if rope:
        p_ref, g_ref, cos_ref, sa_ref, sb_ref, o_ref = refs
    else:
        p_ref, g_ref, o_ref = refs
    x = p_ref[...].astype(F32)
    ms = jnp.mean(x * x, axis=-1, keepdims=True)
    y = x * lax.rsqrt(ms + NORM_EPS) * g_ref[0]
    if rope:
        y = (y * cos_ref[...] + pltpu.roll(y, 96, 1) * sa_ref[...]
             + pltpu.roll(y, 32, 1) * sb_ref[...])
    o_ref[...] = y.astype(o_ref.dtype)


def _qkprep(p, gains, tables, seq, tm):
    r = p.shape[0]
    nheads = ATT_H + ATT_KV
    in_specs = [pl.BlockSpec((tm, LANES), lambda i, h: (i, CB_ATT_Q + h)),
                pl.BlockSpec((1, 1, LANES), lambda i, h: (h, 0, 0))]
    args = [p, gains]
    if tables is not None:
        nt = seq // tm
        for tab in tables:
            in_specs.append(pl.BlockSpec((tm, LANES), lambda i, h: (i % nt, 0)))
            args.append(tab)
    return pl.pallas_call(
        functools.partial(_qkprep_kernel, rope=tables is not None),
        grid=(r // tm, nheads),
        in_specs=in_specs,
        out_specs=pl.BlockSpec((tm, LANES), lambda i, h: (i, h)),
        out_shape=jax.ShapeDtypeStruct((r, nheads * LANES), BF16),
        compiler_params=_cparams(("parallel", "parallel")),
        name="qkprep_rope" if tables is not None else "qkprep",
    )(*args)


def _flash_kernel(*refs, n_lat_blocks, tk, has_ctx_kv):
    if n_lat_blocks and has_ctx_kv:
        q0_ref, q1_ref, g0_ref, g1_ref, kl_ref, vl_ref, kc_ref, vc_ref, o_ref = refs
    elif n_lat_blocks:
        q0_ref, q1_ref, g0_ref, g1_ref, kl_ref, vl_ref, o_ref = refs
    else:
        q0_ref, q1_ref, g0_ref, g1_ref, kc_ref, vc_ref, o_ref = refs
    tq = q0_ref.shape[0]
    q = jnp.concatenate([q0_ref[...], q1_ref[...]], axis=0)

    def block(k, v, carry):
        m, l, acc = carry
        s = _dot_nt(q, k)
        m_new = jnp.maximum(m, jnp.max(s, axis=-1, keepdims=True))
        alpha = jnp.exp(m - m_new)
        p = jnp.exp(s - m_new)
        l = alpha * l + jnp.sum(p, axis=-1, keepdims=True)
        acc = alpha * acc + _dot(p.astype(BF16), v)
        return m_new, l, acc

    carry = (jnp.full((2 * tq, 1), -jnp.inf, F32), jnp.zeros((2 * tq, 1), F32),
             jnp.zeros((2 * tq, LANES), F32))
    if n_lat_blocks:
        def body(i, c):
            rows = pl.ds(pl.multiple_of(i * tk, tk), tk)
            return block(kl_ref[rows, :], vl_ref[rows, :], c)
        carry = lax.fori_loop(0, n_lat_blocks, body, carry)
    if has_ctx_kv:
        carry = block(kc_ref[...], vc_ref[...], carry)
    _, l, acc = carry
    o = acc / l
    o_ref[:, :LANES] = (o[:tq] * _silu(g0_ref[...].astype(F32))).astype(o_ref.dtype)
    o_ref[:, LANES:] = (o[tq:] * _silu(g1_ref[...].astype(F32))).astype(o_ref.dtype)


def _flash(qk_q, p_q, seq_q, lat_kv, ctx_kv, batch, tq, tk):
    nq = seq_q // tq
    in_specs = [
        pl.BlockSpec((tq, LANES), lambda b, j, i: (b * nq + i, 2 * j)),
        pl.BlockSpec((tq, LANES), lambda b, j, i: (b * nq + i, 2 * j + 1)),
        pl.BlockSpec((tq, LANES), lambda b, j, i: (b * nq + i, CB_ATT_GATE + 2 * j)),
        pl.BlockSpec((tq, LANES), lambda b, j, i: (b * nq + i, CB_ATT_GATE + 2 * j + 1)),
    ]
    args = [qk_q, qk_q, p_q, p_q]
    n_lat_blocks = 0
    for seg in (lat_kv, ctx_kv):
        if seg is None:
            continue
        qk_s, p_s, seq_s = seg
        in_specs.append(pl.BlockSpec((seq_s, LANES), lambda b, j, i: (b, ATT_H + j)))
        in_specs.append(pl.BlockSpec((seq_s, LANES), lambda b, j, i: (b, CB_ATT_V + j)))
        args += [qk_s, p_s]
    if lat_kv is not None:
        n_lat_blocks = lat_kv[2] // tk
    return pl.pallas_call(
        functools.partial(_flash_kernel, n_lat_blocks=n_lat_blocks, tk=tk,
                          has_ctx_kv=ctx_kv is not None),
        grid=(batch, ATT_KV, nq),
        in_specs=in_specs,
        out_specs=pl.BlockSpec((tq, 2 * LANES), lambda b, j, i: (b * nq + i, j)),
        out_shape=jax.ShapeDtypeStruct((batch * seq_q, BRANCH_W), BF16),
        compiler_params=_cparams(("parallel", "parallel", "arbitrary")),
        name="flash_lat" if lat_kv is not None else "flash_ctx",
    )(*args)


def _chunk_rows(c):
    return pl.ds(pl.multiple_of(c * CHUNK, CHUNK), CHUNK)


def _gla_chunk(qk_ref, v_ref, wd_ref, c, wup, bias, masks, recv_state, send_state):
    tri, incl, _ = masks
    rows = _chunk_rows(c)
    qk = qk_ref[rows, :].astype(F32)
    q = qk[:, :GLA_DK] * (GLA_DK ** -0.5)
    k = qk[:, GLA_DK:]
    v = v_ref[rows, :]
    z = _dot(wd_ref[rows, :], wup) + bias
    yield
    g = (jnp.minimum(z, 0.0) - jnp.log1p(jnp.exp(-jnp.abs(z)))) * (1.0 / GLA_TAU)
    gc = _tri_cumsum(tri, g)
    yield
    tot = jnp.sum(g, axis=0, keepdims=True)
    q_dec = (q * jnp.exp(gc)).astype(BF16)
    k_inv = (k * jnp.exp(-gc)).astype(BF16)
    k_tail = (k * jnp.exp(tot - gc)).astype(BF16)
    a = jnp.where(incl, _dot_nt(q_dec, k_inv), 0.0)
    upd = _dot_tn(v, k_tail)
    yield
    st = recv_state()
    while st is None:
        yield
        st = recv_state()
    send_state(st * jnp.exp(tot) + upd)
    return _dot(a.astype(BF16), v) + _dot_nt(q_dec, st.astype(BF16))


def _gla_kernel(*refs, need_ctx):
    (qk_l, v_l, gate_l, wd_l, qk_c, v_c, gate_c, wd_c,
     wupf_ref, wupb_ref, bf_ref, bb_ref, gn_ref) = refs[:13]
    if need_ctx:
        y_l, y_c, of_l, ob_l, of_c, ob_c = refs[13:]
    else:
        y_l, of_l, ob_l, of_c, ob_c = refs[13:]
    nl = qk_l.shape[0] // CHUNK
    nc = qk_c.shape[0] // CHUNK
    wupf, wupb = wupf_ref[0], wupb_ref[0]
    bias_f, bias_b = bf_ref[0], bb_ref[0]
    masks_f, masks_b = _chunk_masks(False), _chunk_masks(True)

    def scan(qk, v, wd, o_f, o_b, n, carry):
        unroll = _unroll_factor(n, GLA_UNROLL)

        def body(i, carry):
            states = {("f", -1): carry[0], ("b", -1): carry[1]}
            gens = []
            for u in range(unroll):
                c = i * unroll + u
                for d, cc, wup, bias, masks in (("f", c, wupf, bias_f, masks_f),
                                                ("b", n - 1 - c, wupb, bias_b, masks_b)):
                    gens.append(_gla_chunk(
                        qk, v, wd, cc, wup, bias, masks,
                        functools.partial(states.get, (d, u - 1)),
                        functools.partial(states.__setitem__, (d, u))))
            outs = _run_interleaved(gens)
            for u in range(unroll):
                c = i * unroll + u
                o_f[_chunk_rows(c), :] = outs[2 * u]
                o_b[_chunk_rows(n - 1 - c), :] = outs[2 * u + 1]
            return states[("f", unroll - 1)], states[("b", unroll - 1)]
        return lax.fori_loop(0, n // unroll, body, carry)

    zero = jnp.zeros((GLA_DV, GLA_DK), F32)
    carry = scan(qk_c, v_c, wd_c, of_c, ob_c, nc, (zero, zero))
    scan(qk_l, v_l, wd_l, of_l, ob_l, nl, carry)

    gn = gn_ref[0]

    def finish(o_f, o_b, gate, y, n):
        def body(i, _):
            rows = _chunk_rows(i)
            o = o_f[rows, :] + o_b[rows, :]
            ms = jnp.mean(o * o, axis=-1, keepdims=True)
            o = o * lax.rsqrt(ms + NORM_EPS) * gn
            y[rows, :] = (o * _silu(gate[rows, :].astype(F32))).astype(y.dtype)
            return 0
        lax.fori_loop(0, n, body, 0, unroll=_unroll_factor(n, FINISH_UNROLL))

    finish(of_l, ob_l, gate_l, y_l, nl)
    if need_ctx:
        finish(of_c, ob_c, gate_c, y_c, nc)


def _gla(p_lat, p_ctx, batch, seq, ctx_len, wupf, wupb, bf, bb, gnorm, need_ctx):
    def seg_specs(n):
        return [pl.BlockSpec((n, LANES), lambda b, h: (b, CB_GLA_QK + h)),
                pl.BlockSpec((n, LANES), lambda b, h: (b, CB_GLA_V + h)),
                pl.BlockSpec((n, LANES), lambda b, h: (b, CB_GLA_GATE + h)),
                pl.BlockSpec((n, LANES), lambda b, h: (b, CB_GLA_WD))]
    w_spec = pl.BlockSpec((1, LANES, GLA_DK), lambda b, h: (h, 0, 0))
    b_spec = pl.BlockSpec((1, 1, GLA_DK), lambda b, h: (h, 0, 0))
    g_spec = pl.BlockSpec((1, 1, GLA_DV), lambda b, h: (h, 0, 0))
    out_specs = [pl.BlockSpec((seq, LANES), lambda b, h: (b, h))]
    out_shape = [jax.ShapeDtypeStruct((batch * seq, BRANCH_W), BF16)]
    if need_ctx:
        out_specs.append(pl.BlockSpec((ctx_len, LANES), lambda b, h: (b, h)))
        out_shape.append(jax.ShapeDtypeStruct((batch * ctx_len, BRANCH_W), BF16))
    outs = pl.pallas_call(
        functools.partial(_gla_kernel, need_ctx=need_ctx),
        grid=(batch, GLA_H),
        in_specs=seg_specs(seq) + seg_specs(ctx_len) + [w_spec, w_spec, b_spec, b_spec, g_spec],
        out_specs=out_specs,
        out_shape=out_shape,
        scratch_shapes=[pltpu.VMEM((seq, LANES), F32)] * 2 + [pltpu.VMEM((ctx_len, LANES), F32)] * 2,
        compiler_params=_cparams(("parallel", "parallel")),
        name="gla",
    )(p_lat, p_lat, p_lat, p_lat, p_ctx, p_ctx, p_ctx, p_ctx, wupf, wupb, bf, bb, gnorm)
    return (outs[0], outs[1]) if need_ctx else (outs[0], None)


def _seg_sum(x, lo):
    s0 = jnp.sum(jnp.where(lo, x, 0.0), axis=-1, keepdims=True)
    s1 = jnp.sum(jnp.where(lo, 0.0, x), axis=-1, keepdims=True)
    return jnp.where(lo, s0, s1)


def _run_interleaved(gens):
    gens = list(gens)
    out = [None] * len(gens)
    live = list(range(len(gens)))
    while live:
        still = []
        for j in live:
            try:
                next(gens[j])
                still.append(j)
            except StopIteration as stop:
                out[j] = stop.value
        live = still
    return out


def _unit_tri_inverse_offdiag(mats, blockdiag):
    def mm(x, y):
        return _dot(x.astype(BF16), y.astype(BF16))

    nd = [jnp.where(blockdiag, -a, 0.0) for a in mats]
    no = [-a - d for a, d in zip(mats, nd)]
    n2 = [mm(d, d) for d in nd]
    yield
    n4 = [mm(s, s) for s in n2]
    x = [d + s + mm(d, s) for d, s in zip(nd, n2)]
    yield
    n8 = [mm(s, s) for s in n4]
    x = [xi + s + mm(xi, s) for xi, s in zip(x, n4)]
    yield
    xd = [xi + s + mm(xi, s) for xi, s in zip(x, n8)]
    yield
    m = [o + mm(xi, o) for xi, o in zip(xd, no)]
    yield
    m2 = [mm(mi, mi) for mi in m]
    yield
    xq = [mi + s + mm(mi, s) for mi, s in zip(m, m2)]
    yield
    return [q + d + mm(q, d) for q, d in zip(xq, xd)]


def _shifted(ref, c, n_chunks, mu, row):
    base = pl.multiple_of(c * CHUNK, CHUNK)
    main = ref[pl.ds(base, CHUNK), :].astype(F32)
    pstart = pl.multiple_of(jnp.maximum(base - 16, 0), 16)
    nstart = pl.multiple_of(jnp.minimum(base + CHUNK, (n_chunks - 1) * CHUNK + CHUNK - 16), 16)
    prev_row = ref[pl.ds(pstart, 16), :].astype(F32)[15:16, :]
    next_row = ref[pl.ds(nstart, 16), :].astype(F32)[0:1, :]
    prev_row = jnp.where(c > 0, prev_row, 0.0)
    next_row = jnp.where(c < n_chunks - 1, next_row, 0.0)
    up = jnp.where(row == 0, prev_row, pltpu.roll(main, 1, 0))
    dn = jnp.where(row == CHUNK - 1, next_row, pltpu.roll(main, CHUNK - 1, 0))
    return main + mu * (0.5 * (up + dn) - main)


def _rwkv_chunk(seg, c, prm, consts, with_gate, recv_state, send_state):
    (r_ref, k_ref, v_ref, g_ref, wd_ref, ad_ref, n_chunks) = seg
    (mu_r, mu_k, mu_v, mu_g, mu_wd, mu_ad, w0, a0, kkw, kaw, rkw, wup, aup) = prm
    (row, lo, blockdiag16, headdiag, masks, strict_hi, incl_wide) = consts
    tri, _, strict = masks

    r = _shifted(r_ref, c, n_chunks, mu_r, row)
    k = _shifted(k_ref, c, n_chunks, mu_k, row)
    v = _shifted(v_ref, c, n_chunks, mu_v, row)
    wd = _shifted(wd_ref, c, n_chunks, mu_wd, row)
    ad = _shifted(ad_ref, c, n_chunks, mu_ad, row)
    gate = _silu(_shifted(g_ref, c, n_chunks, mu_g, row)) if with_gate else None

    z_w = _dot(jnp.tanh(wd).astype(BF16), wup)
    z_a = _dot(ad.astype(BF16), aup)
    yield
    lw = -RWKV_DECAY_SCALE * _sigmoid(w0 + z_w)
    a = _sigmoid(a0 + z_a)
    kd = k * (1.0 + (a - 1.0) * kaw)
    kk = k * kkw
    kk = kk * lax.rsqrt(_seg_sum(kk * kk, lo) + L2_EPS)
    kka = kk * a
    bonus = _seg_sum(r * kd * rkw, lo) * v

    cw = _tri_cumsum(tri, lw)
    yield
    tot = jnp.sum(lw, axis=0, keepdims=True)
    e_neg = jnp.exp(-cw)
    e_tail = jnp.exp(tot - cw)
    kap = kk * jnp.exp(cw - lw)
    rt = r * jnp.exp(cw)
    bt = (kka * e_neg).astype(BF16)
    kt = (kd * e_neg).astype(BF16)
    v16 = v.astype(BF16)
    tails = jnp.concatenate([kka * e_tail, kd * e_tail], axis=0).astype(BF16)
    decay = jnp.exp(tot)
    kap_rt = jnp.concatenate([kap, rt], axis=0).astype(BF16)

    yt = jnp.concatenate([bt, kt], axis=0)
    grams = []
    for hm in (lo, jnp.logical_not(lo)):
        xh = jnp.concatenate([jnp.where(hm, kap, 0.0), jnp.where(hm, rt, 0.0)], axis=0)
        grams.append(_dot_nt(xh.astype(BF16), yt))
    yield
    vv = jnp.concatenate([v16, v16], axis=0)
    a_b = [jnp.where(strict, g[:CHUNK, :CHUNK], 0.0) for g in grams]
    akv = [_dot(jnp.where(strict_hi, g[:CHUNK], 0.0).astype(BF16), vv) for g in grams]
    mats = [jnp.where(incl_wide, g[CHUNK:], 0.0).astype(BF16) for g in grams]
    x_inv = yield from _unit_tri_inverse_offdiag(a_b, blockdiag16)
    x_inv = [x.astype(BF16) for x in x_inv]
    yield

    s2 = recv_state()
    while s2 is None:
        yield
        s2 = recv_state()
    ks_rs = _dot_nt(kap_rt, s2.astype(BF16))
    ks, rs = ks_rs[:CHUNK], ks_rs[CHUNK:]
    yield
    e_heads = []
    for x, kv in zip(x_inv, akv):
        rhs = ks + kv
        e_heads.append(rhs + _dot(x, rhs.astype(BF16)))
    yield
    e = jnp.where(lo, e_heads[0], e_heads[1])
    ev = jnp.concatenate([-e, v], axis=0).astype(BF16)
    upd = _dot_tn(ev, tails)
    send_state(s2 * decay + jnp.where(headdiag, upd, 0.0))
    y = rs + jnp.where(lo, _dot(mats[0], ev), _dot(mats[1], ev))
    return y, bonus, gate


def _rwkv_kernel(*refs, need_ctx):
    lat = refs[0:6]
    ctx = refs[6:12]
    (mu_r, mu_k, mu_v, mu_g, mu_wd, mu_ad, w0f, w0b, a0f, a0b, kkw, kaw, rkw, lng, lnb,
     wupf, wupb, aupf, aupb) = [x[0] for x in refs[12:31]]
    rest = refs[31:]
    if need_ctx:
        y_l, y_c = rest[:2]
        scr = rest[2:]
    else:
        y_l, y_c = rest[0], None
        scr = rest[1:]
    scr_l, scr_c = scr[:5], scr[5:]
    nl = lat[0].shape[0] // CHUNK
    nc = ctx[0].shape[0] // CHUNK

    row = lax.broadcasted_iota(jnp.int32, (CHUNK, LANES), 0)
    lo = lax.broadcasted_iota(jnp.int32, (1, LANES), 1) < RWKV_HD
    t64 = lax.broadcasted_iota(jnp.int32, (CHUNK, CHUNK), 0)
    s64 = lax.broadcasted_iota(jnp.int32, (CHUNK, CHUNK), 1)
    blockdiag16 = (t64 // 16) == (s64 // 16)
    v128 = lax.broadcasted_iota(jnp.int32, (LANES, LANES), 0)
    k128 = lax.broadcasted_iota(jnp.int32, (LANES, LANES), 1)
    headdiag = (v128 < RWKV_HD) == (k128 < RWKV_HD)
    tw = lax.broadcasted_iota(jnp.int32, (CHUNK, LANES), 0)
    lw_ = lax.broadcasted_iota(jnp.int32, (CHUNK, LANES), 1)
    sw = lw_ % CHUNK

    def wide_masks(reverse):
        strict_hi = (lw_ >= CHUNK) & ((sw > tw) if reverse else (sw < tw))
        incl_wide = (sw >= tw) if reverse else (sw <= tw)
        return strict_hi, incl_wide

    consts_f = (row, lo, blockdiag16, headdiag, _chunk_masks(False)) + wide_masks(False)
    consts_b = (row, lo, blockdiag16, headdiag, _chunk_masks(True)) + wide_masks(True)
    shared = (mu_r, mu_k, mu_v, mu_g, mu_wd, mu_ad)
    prm_f = shared + (w0f, a0f, kkw, kaw, rkw, wupf, aupf)
    prm_b = shared + (w0b, a0b, kkw, kaw, rkw, wupb, aupb)

    def scan(seg, scr, n, carry):
        unroll = _unroll_factor(n, RWKV_UNROLL)
        yf, yb, bf, bb, gs = scr

        def body(i, carry):
            states = {("f", -1): carry[0], ("b", -1): carry[1]}
            gens = []
            for u in range(unroll):
                c = i * unroll + u
                for d, cc, prm, consts in (("f", c, prm_f, consts_f), ("b", n - 1 - c, prm_b, consts_b)):
                    gens.append(_rwkv_chunk(
                        seg, cc, prm, consts, d == "f",
                        functools.partial(states.get, (d, u - 1)),
                        functools.partial(states.__setitem__, (d, u))))
            outs = _run_interleaved(gens)
            for u in range(unroll):
                c = i * unroll + u
                (y_f, bon_f, gate), (y_b, bon_b, _) = outs[2 * u], outs[2 * u + 1]
                rows_f, rows_b = _chunk_rows(c), _chunk_rows(n - 1 - c)
                yf[rows_f, :] = y_f
                bf[rows_f, :] = bon_f
                gs[rows_f, :] = gate
                yb[rows_b, :] = y_b
                bb[rows_b, :] = bon_b
            return states[("f", unroll - 1)], states[("b", unroll - 1)]
        return lax.fori_loop(0, n // unroll, body, carry)

    zero = jnp.zeros((LANES, LANES), F32)
    carry = scan(tuple(ctx) + (nc,), scr_c, nc, (zero, zero))
    scan(tuple(lat) + (nl,), scr_l, nl, carry)

    def finish(scr, y_out, n):
        yf, yb, bf, bb, gs = scr

        def body(i, _):
            rows = _chunk_rows(i)
            y = yf[rows, :] + yb[rows, :]
            mean = _seg_sum(y, lo) * (1.0 / RWKV_HD)
            d = y - mean
            var = _seg_sum(d * d, lo) * (1.0 / RWKV_HD)
            yn = d * lax.rsqrt(var + GN_EPS) * lng + lnb
            bonus = bf[rows, :] + bb[rows, :]
            y_out[rows, :] = ((yn + bonus) * gs[rows, :]).astype(y_out.dtype)
            return 0
        lax.fori_loop(0, n, body, 0, unroll=_unroll_factor(n, FINISH_UNROLL))

    finish(scr_l, y_l, nl)
    if need_ctx:
        finish(scr_c, y_c, nc)


def _rwkv(p_lat, p_ctx, batch, seq, ctx_len, prm, need_ctx):
    def seg_specs(n):
        return [pl.BlockSpec((n, LANES), lambda b, h: (b, CB_RW_R + h)),
                pl.BlockSpec((n, LANES), lambda b, h: (b, CB_RW_K + h)),
                pl.BlockSpec((n, LANES), lambda b, h: (b, CB_RW_V + h)),
                pl.BlockSpec((n, LANES), lambda b, h: (b, CB_RW_GATE + h)),
                pl.BlockSpec((n, LANES), lambda b, h: (b, CB_RW_WD)),
                pl.BlockSpec((n, LANES), lambda b, h: (b, CB_RW_AD))]
    vec_h = pl.BlockSpec((1, 1, LANES), lambda b, h: (h, 0, 0))
    vec_s = pl.BlockSpec((1, 1, LANES), lambda b, h: (0, 0, 0))
    mat_h = pl.BlockSpec((1, LANES, LANES), lambda b, h: (h, 0, 0))
    prm_specs = [vec_h, vec_h, vec_h, vec_h, vec_s, vec_s] + [vec_h] * 9 + [mat_h] * 4
    out_specs = [pl.BlockSpec((seq, LANES), lambda b, h: (b, h))]
    out_shape = [jax.ShapeDtypeStruct((batch * seq, BRANCH_W), BF16)]
    if need_ctx:
        out_specs.append(pl.BlockSpec((ctx_len, LANES), lambda b, h: (b, h)))
        out_shape.append(jax.ShapeDtypeStruct((batch * ctx_len, BRANCH_W), BF16))
    scratch = [pltpu.VMEM((seq, LANES), F32)] * 5 + [pltpu.VMEM((ctx_len, LANES), F32)] * 5
    outs = pl.pallas_call(
        functools.partial(_rwkv_kernel, need_ctx=need_ctx),
        grid=(batch, RWKV_H // 2),
        in_specs=seg_specs(seq) + seg_specs(ctx_len) + prm_specs,
        out_specs=out_specs,
        out_shape=out_shape,
        scratch_shapes=scratch,
        compiler_params=_cparams(("parallel", "parallel")),
        name="rwkv",
    )(*([p_lat] * 6 + [p_ctx] * 6 + list(prm)))
    return (outs[0], outs[1]) if need_ctx else (outs[0], None)


def _merge_kernel(x_ref, y1_ref, y2_ref, y3_ref, g1_ref, g2_ref, g3_ref, mod_ref,
                  wo1_ref, wo2_ref, wo3_ref, wout_ref, gp_ref, o_ref):
    m = (_sigmoid(g1_ref[...].astype(F32)) * _dot(y1_ref[...], wo1_ref[...])
         + _sigmoid(g2_ref[...].astype(F32)) * _dot(y2_ref[...], wo2_ref[...])
         + _sigmoid(g3_ref[...].astype(F32)) * _dot(y3_ref[...], wo3_ref[...]))
    mo = _dot(m.astype(BF16), wout_ref[...])
    ms = jnp.mean(mo * mo, axis=-1, keepdims=True)
    out = mo * lax.rsqrt(ms + NORM_EPS) * gp_ref[...]
    o_ref[...] = x_ref[...] + mod_ref[0][2:3] * out


def _merge(x2d, y1, y2, y3, p, mod, mod_index, wo1, wo2, wo3, wout, g_post, tm):
    r, d = x2d.shape
    row = lambda i: (i, 0)
    const = lambda i: (0, 0)
    return pl.pallas_call(
        _merge_kernel,
        grid=(r // tm,),
        in_specs=[pl.BlockSpec((tm, d), row),
                  pl.BlockSpec((tm, BRANCH_W), row),
                  pl.BlockSpec((tm, BRANCH_W), row),
                  pl.BlockSpec((tm, BRANCH_W), row),
                  pl.BlockSpec((tm, d), lambda i: (i, 0)),
                  pl.BlockSpec((tm, d), lambda i: (i, 1)),
                  pl.BlockSpec((tm, d), lambda i: (i, 2)),
                  pl.BlockSpec((1, 3, d), lambda i: (mod_index(i), 0, 0)),
                  pl.BlockSpec((BRANCH_W, d), const),
                  pl.BlockSpec((BRANCH_W, d), const),
                  pl.BlockSpec((BRANCH_W, d), const),
                  pl.BlockSpec((d, d), const),
                  pl.BlockSpec((1, d), const)],
        out_specs=pl.BlockSpec((tm, d), row),
        out_shape=jax.ShapeDtypeStruct((r, d), F32),
        compiler_params=_cparams(("parallel",)),
        name="merge",
    )(x2d, y1, y2, y3, p, p, p, mod, wo1, wo2, wo3, wout, g_post.reshape(1, d))


def _pack_w_in(w_in):
    d = w_in.shape[0]
    o = 0
    gq = w_in[:, o:o + 256]; o += 256
    gk = w_in[:, o:o + 256]; o += 256
    gv = w_in[:, o:o + 512]; o += 512
    gg = w_in[:, o:o + 512]; o += 512
    gwd = w_in[:, o:o + 32]; o += 32
    att = w_in[:, o:o + 1536]; o += 1536
    rw = w_in[:, o:o + 2304]; o += 2304
    mg = w_in[:, o:o + 3072]
    qk = jnp.concatenate([gq.reshape(d, GLA_H, GLA_DK), gk.reshape(d, GLA_H, GLA_DK)],
                         axis=-1).reshape(d, GLA_H * 2 * GLA_DK)
    pad_wd = jnp.zeros((d, LANES - 32), w_in.dtype)
    pad_end = jnp.zeros((d, LANES), w_in.dtype)
    return jnp.concatenate([mg, qk, gv, gg, gwd, pad_wd, att, rw, pad_end], axis=1).astype(BF16)


def _rope_tables(seq):
    quarter = ATT_HD // 4
    inv = ROPE_THETA ** (-jnp.arange(quarter, dtype=F32) / quarter)
    t = jnp.arange(seq)
    row_pos = (t // GRID_W).astype(F32)
    col_pos = (t % GRID_W).astype(F32)
    ar = row_pos[:, None] * inv[None, :]
    ac = col_pos[:, None] * inv[None, :]
    z = jnp.zeros_like(ar)
    cos = jnp.concatenate([jnp.cos(ar), jnp.cos(ar), jnp.cos(ac), jnp.cos(ac)], axis=1)
    sa = jnp.concatenate([-jnp.sin(ar), z, -jnp.sin(ac), z], axis=1)
    sb = jnp.concatenate([z, jnp.sin(ar), z, jnp.sin(ac)], axis=1)
    return cos, sa, sb


def _pad_rows(w, lo_half):
    z = jnp.zeros((LANES - w.shape[0], w.shape[1]), w.dtype)
    if lo_half:
        return jnp.concatenate([w, z], axis=0)
    half = LANES // 2
    return jnp.concatenate([z[:half], w, z[half:]], axis=0)


def _pick(n, target):
    t = min(n, target)
    while n % t:
        t //= 2
    return t


def kernel(x, c, ctx, c_ctx, w_mod, b_mod, g_pre, w_in, gla_wup_f, gla_b_f, gla_wup_b, gla_b_b, gla_norm, att_qnorm, att_knorm, rwkv_mu, rwkv_w0_f, rwkv_wup_f, rwkv_w0_b, rwkv_wup_b, rwkv_a0_f, rwkv_aup_f, rwkv_a0_b, rwkv_aup_b, rwkv_kk, rwkv_ka, rwkv_rk, rwkv_ln_g, rwkv_ln_b, w_o_gla, w_o_att, w_o_rwkv, w_out, g_post):
    batch, seq, d = x.shape
    ctx_len = ctx.shape[1]
    depth = w_in.shape[0]
    assert seq % CHUNK == 0 and ctx_len % CHUNK == 0 and seq % GRID_W == 0

    nb = -(-(batch + 1) // 8) * 8
    cond = jnp.concatenate([c, c_ctx[None, :], jnp.zeros((nb - batch - 1, d), F32)], axis=0)
    tables = _rope_tables(seq)

    tm_lat = _pick(seq, 1024)
    tm_ctx = _pick(batch * ctx_len, 1024)

    def lat_mod(tm):
        return lambda i: i // (seq // tm)

    def ctx_mod(tm):
        return lambda i: batch
    tq_lat, tk = _pick(seq, 256), _pick(seq, 512)
    tq_ctx = _pick(ctx_len, 256)
    tm_prep = _pick(seq, 512)
    tm_prep_c = _pick(ctx_len, 512)
    tm_merge = _pick(seq, 512)
    tm_merge_c = _pick(batch * ctx_len, 512)

    x2 = x.reshape(batch * seq, d)
    xc2 = ctx.reshape(batch * ctx_len, d)

    def hp(v):
        return v.reshape(RWKV_H // 2, 1, LANES)

    for l in range(depth):
        need_ctx = l < depth - 1
        mod = _modulation(cond, w_mod[l], b_mod[l]).reshape(nb, 3, d)
        w_cat = _pack_w_in(w_in[l])
        p_lat = _inproj(x2, mod, lat_mod(tm_lat), g_pre[l], w_cat, tm_lat)
        p_ctx = _inproj(xc2, mod, ctx_mod(tm_ctx), g_pre[l], w_cat, tm_ctx)

        scale = ATT_HD ** -0.5
        gains = jnp.concatenate([jnp.tile(att_qnorm[l][None, :] * scale, (ATT_H, 1)),
                                 jnp.tile(att_knorm[l][None, :], (ATT_KV, 1))], axis=0)
        gains = gains.reshape(ATT_H + ATT_KV, 1, LANES)
        qk_lat = _qkprep(p_lat, gains, tables, seq, tm_prep)
        qk_ctx = _qkprep(p_ctx, gains, None, ctx_len, tm_prep_c)
        y2 = _flash(qk_lat, p_lat, seq, (qk_lat, p_lat, seq), (qk_ctx, p_ctx, ctx_len),
                    batch, tq_lat, tk)
        y2c = None
        if need_ctx:
            y2c = _flash(qk_ctx, p_ctx, ctx_len, None, (qk_ctx, p_ctx, ctx_len),
                         batch, tq_ctx, tk)

        def gla_w(w, lo_rows):
            w = w.reshape(GLA_LR, GLA_H, GLA_DK).transpose(1, 0, 2)
            z = jnp.zeros((GLA_H, LANES, GLA_DK), w.dtype)
            start = 0 if lo_rows else GLA_LR
            return lax.dynamic_update_slice(z, w, (0, start, 0)).astype(BF16)
        y1, y1c = _gla(p_lat, p_ctx, batch, seq, ctx_len,
                       gla_w(gla_wup_f[l], True), gla_w(gla_wup_b[l], False),
                       gla_b_f[l].reshape(GLA_H, 1, GLA_DK), gla_b_b[l].reshape(GLA_H, 1, GLA_DK),
                       gla_norm[l].reshape(GLA_H, 1, GLA_DV), need_ctx)

        mu = rwkv_mu[l]

        def rw_w(w, lo_rows):
            w = w.reshape(RWKV_LR, RWKV_H // 2, LANES).transpose(1, 0, 2)
            return jnp.stack([_pad_rows(w[i], lo_rows) for i in range(RWKV_H // 2)]).astype(BF16)
        prm = (hp(mu[0:512]), hp(mu[512:1024]), hp(mu[1024:1536]), hp(mu[1536:2048]),
               mu[2048:2176].reshape(1, 1, LANES), mu[2176:2304].reshape(1, 1, LANES),
               hp(rwkv_w0_f[l]), hp(rwkv_w0_b[l]), hp(rwkv_a0_f[l]), hp(rwkv_a0_b[l]),
               hp(rwkv_kk[l]), hp(rwkv_ka[l]), hp(rwkv_rk[l]), hp(rwkv_ln_g[l]), hp(rwkv_ln_b[l]),
               rw_w(rwkv_wup_f[l], True), rw_w(rwkv_wup_b[l], False),
               rw_w(rwkv_aup_f[l], True), rw_w(rwkv_aup_b[l], False))
        y3, y3c = _rwkv(p_lat, p_ctx, batch, seq, ctx_len, prm, need_ctx)

        wo1, wo2, wo3 = (w_o_gla[l].astype(BF16), w_o_att[l].astype(BF16),
                         w_o_rwkv[l].astype(BF16))
        wout = w_out[l].astype(BF16)
        x2 = _merge(x2, y1, y2, y3, p_lat, mod, lat_mod(tm_merge), wo1, wo2, wo3, wout, g_post[l], tm_merge)
        if need_ctx:
            xc2 = _merge(xc2, y1c, y2c, y3c, p_ctx, mod, ctx_mod(tm_merge_c), wo1, wo2, wo3, wout,
                         g_post[l], tm_merge_c)
    return x2.reshape(batch, seq, d)
```

```python
import functools

import jax
import jax.numpy as jnp
from jax import lax
from jax.experimental import pallas as pl
from jax.experimental.pallas import tpu as pltpu

F32 = jnp.float32
BF16 = jnp.bfloat16

GRID_W = 64
BRANCH_W = 512
GLA_H, GLA_DK, GLA_DV, GLA_LR = 4, 64, 128, 16
GLA_TAU = 16.0
ATT_H, ATT_KV, ATT_HD = 4, 2, 128
ROPE_THETA = 10000.0
RWKV_H, RWKV_HD, RWKV_LR = 8, 64, 64
RWKV_DECAY_SCALE = 0.6065306597
NORM_EPS = 1e-6
GN_EPS = 64e-5
L2_EPS = 1e-12
LOG2_E = 1.4426950408889634

CHUNK = 64
LANES = 128
VMEM_LIMIT = 56 * 1024 * 1024

CB_MG = 0
CB_GLA_QK = 24
CB_GLA_V = 28
CB_GLA_GATE = 32
CB_GLA_WD = 36
CB_ATT_Q = 37
CB_ATT_V = 43
CB_ATT_GATE = 45
CB_RW_R = 49
CB_RW_K = 53
CB_RW_V = 57
CB_RW_GATE = 61
CB_RW_WD = 65
CB_RW_AD = 66
NP_BLOCKS = 68
NP = NP_BLOCKS * LANES
TN_PROJ = NP // 4


def _cparams(sem):
    return pltpu.CompilerParams(dimension_semantics=sem, vmem_limit_bytes=VMEM_LIMIT)


def _dot(a, b):
    return jnp.dot(a, b, preferred_element_type=F32)


def _dot_nt(a, b):
    return lax.dot_general(a, b, (((1,), (1,)), ((), ())), preferred_element_type=F32)


def _dot_tn(a, b):
    return lax.dot_general(a, b, (((0,), (0,)), ((), ())), preferred_element_type=F32)


def _silu(x):
    return x / (1.0 + jnp.exp(-x))


def _sigmoid(x):
    return 1.0 / (1.0 + jnp.exp(-x))


def _split3(x):
    hi = x.astype(BF16)
    r1 = x - hi.astype(F32)
    mid = r1.astype(BF16)
    lo = (r1 - mid.astype(F32)).astype(BF16)
    return hi, mid, lo


def _tri_cumsum(tri, x):
    hi, mid, lo = _split3(x)
    return _dot(tri, hi) + _dot(tri, mid) + _dot(tri, lo)


GLA_UNROLL = 8
RWKV_UNROLL = 4
FINISH_UNROLL = 4


def _unroll_factor(n_chunks, target):
    u = target
    while n_chunks % u:
        u //= 2
    return u


def _chunk_masks(reverse):
    t = lax.broadcasted_iota(jnp.int32, (CHUNK, CHUNK), 0)
    s = lax.broadcasted_iota(jnp.int32, (CHUNK, CHUNK), 1)
    incl = (s >= t) if reverse else (s <= t)
    strict = (s > t) if reverse else (s < t)
    return incl.astype(BF16), incl, strict


def _mod_kernel(c_ref, w_ref, b_ref, o_ref):
    c = c_ref[...]
    s = _silu(c)
    o_ref[...] = jnp.dot(s, w_ref[...], preferred_element_type=F32,
                         precision=lax.Precision.HIGHEST) + b_ref[...]


def _modulation(cond, w_mod, b_mod):
    nb, d = cond.shape
    n = w_mod.shape[1]
    tn = 768
    return pl.pallas_call(
        _mod_kernel,
        grid=(n // tn,),
        in_specs=[pl.BlockSpec((nb, d), lambda j: (0, 0)),
                  pl.BlockSpec((d, tn), lambda j: (0, j)),
                  pl.BlockSpec((1, tn), lambda j: (0, j))],
        out_specs=pl.BlockSpec((nb, tn), lambda j: (0, j)),
        out_shape=jax.ShapeDtypeStruct((nb, n), F32),
        compiler_params=_cparams(("parallel",)),
        name="modulation",
    )(cond, w_mod, b_mod.reshape(1, n))


def _inproj_kernel(x_ref, mod_ref, g_ref, w_ref, o_ref, h_ref):
    @pl.when(pl.program_id(1) == 0)
    def _():
        x = x_ref[...]
        ms = jnp.mean(x * x, axis=-1, keepdims=True)
        y = x * lax.rsqrt(ms + NORM_EPS) * g_ref[...]
        mod = mod_ref[0]
        h_ref[...] = (y * (1.0 + mod[1:2]) + mod[0:1]).astype(BF16)

    o_ref[...] = _dot(h_ref[...], w_ref[...]).astype(o_ref.dtype)


def _inproj(x2d, mod, mod_index, g_pre, w_cat, tm):
    r, d = x2d.shape
    return pl.pallas_call(
        _inproj_kernel,
        grid=(r // tm, NP // TN_PROJ),
        in_specs=[pl.BlockSpec((tm, d), lambda i, j: (i, 0)),
                  pl.BlockSpec((1, 3, d), lambda i, j: (mod_index(i), 0, 0)),
                  pl.BlockSpec((1, d), lambda i, j: (0, 0)),
                  pl.BlockSpec((d, TN_PROJ), lambda i, j: (0, j))],
        out_specs=pl.BlockSpec((tm, TN_PROJ), lambda i, j: (i, j)),
        out_shape=jax.ShapeDtypeStruct((r, NP), BF16),
        scratch_shapes=[pltpu.VMEM((tm, d), BF16)],
        compiler_params=_cparams(("parallel", "arbitrary")),
        name="inproj",
    )(x2d, mod, g_pre.reshape(1, d), w_cat)


def _norm_rope(x, gain, tables):
    x = x.astype(F32)
    ms = jnp.mean(x * x, axis=-1, keepdims=True)
    y = x * lax.rsqrt(ms + NORM_EPS) * gain
    if tables is not None:
        cos, sa, sb = tables
        y = y * cos + pltpu.roll(y, 96, 1) * sa + pltpu.roll(y, 32, 1) * sb
    return y.astype(BF16)


def _kvprep_kernel(*refs, rope):
    if rope:
        k_ref, v_ref, g_ref, cos_ref, sa_ref, sb_ref, ko_ref, vt_ref = refs
        tables = (cos_ref[...], sa_ref[...], sb_ref[...])
    else:
        k_ref, v_ref, g_ref, ko_ref, vt_ref = refs
        tables = None
    ko_ref[...] = _norm_rope(k_ref[...], g_ref[...], tables)
    vt_ref[:ATT_HD, :] = v_ref[...].astype(F32).T.astype(BF16)
    vt_ref[ATT_HD:, :] = jnp.ones((VT_ROWS - ATT_HD, vt_ref.shape[1]), BF16)


VT_ROWS = ATT_HD + 16


def _kvprep(p, k_gain, tables, batch, seq, tm):
    nt = seq // tm
    in_specs = [pl.BlockSpec((tm, LANES), lambda i, j: (i, CB_ATT_Q + ATT_H + j)),
                pl.BlockSpec((tm, LANES), lambda i, j: (i, CB_ATT_V + j)),
                pl.BlockSpec((1, LANES), lambda i, j: (0, 0))]
    args = [p, p, k_gain]
    if tables is not None:
        for tab in tables:
            in_specs.append(pl.BlockSpec((tm, LANES), lambda i, j: (i % nt, 0)))
            args.append(tab)
    return pl.pallas_call(
        functools.partial(_kvprep_kernel, rope=tables is not None),
        grid=(batch * nt, ATT_KV),
        in_specs=in_specs,
        out_specs=[pl.BlockSpec((tm, LANES), lambda i, j: (i, j)),
                   pl.BlockSpec((VT_ROWS, tm), lambda i, j: ((i // nt) * ATT_KV + j, i % nt))],
        out_shape=[jax.ShapeDtypeStruct((batch * seq, ATT_KV * LANES), BF16),
                   jax.ShapeDtypeStruct((batch * ATT_KV * VT_ROWS, seq), BF16)],
        compiler_params=_cparams(("parallel", "parallel")),
        name="kvprep_rope" if tables is not None else "kvprep",
    )(*args)


FLASH_KV_INTERLEAVE = 8


def _flash_kernel(*refs, n_lat_blocks, tk, has_ctx_kv, rope):
    refs = list(refs)
    q0_ref, q1_ref, g0_ref, g1_ref, qg_ref = refs[:5]
    pos = 5
    tables = None
    if rope:
        tables = tuple(r[...] for r in refs[pos:pos + 3])
        pos += 3
    if n_lat_blocks:
        kl_ref, vtl_ref = refs[pos:pos + 2]
        pos += 2
    if has_ctx_kv:
        kc_ref, vtc_ref = refs[pos:pos + 2]
        pos += 2
    o_ref = refs[pos]
    tq = q0_ref.shape[0]
    gain = qg_ref[...]
    q = jnp.concatenate([_norm_rope(q0_ref[...], gain, tables),
                         _norm_rope(q1_ref[...], gain, tables)], axis=0)

    def blocks(kvs, carry):
        scores = [_dot_nt(k, q) for k, _ in kvs]
        m, acc = carry
        for s, (_, vt) in zip(scores, kvs):
            m_new = jnp.maximum(m, jnp.max(s, axis=0, keepdims=True))
            p = jnp.exp2(s - m_new)
            acc = jnp.exp2(m - m_new) * acc + _dot(vt, p.astype(BF16))
            m = m_new
        return m, acc

    carry = (jnp.full((1, 2 * tq), -jnp.inf, F32), jnp.zeros((VT_ROWS, 2 * tq), F32))
    if n_lat_blocks:
        nb = _unroll_factor(n_lat_blocks, FLASH_KV_INTERLEAVE)

        def body(i, c):
            kvs = []
            for u in range(nb):
                start = pl.multiple_of((i * nb + u) * tk, tk)
                kvs.append((kl_ref[pl.ds(start, tk), :], vtl_ref[:, pl.ds(start, tk)]))
            return blocks(kvs, c)
        carry = lax.fori_loop(0, n_lat_blocks // nb, body, carry)
    if has_ctx_kv:
        carry = blocks([(kc_ref[...], vtc_ref[...])], carry)
    _, acc = carry
    o = (acc[:ATT_HD] / acc[ATT_HD:ATT_HD + 1]).T
    o_ref[:, :LANES] = (o[:tq] * _silu(g0_ref[...].astype(F32))).astype(o_ref.dtype)
    o_ref[:, LANES:] = (o[tq:] * _silu(g1_ref[...].astype(F32))).astype(o_ref.dtype)


def _flash(p_q, seq_q, q_gain, tables, lat_kv, ctx_kv, batch, tq, tk):
    nq = seq_q // tq
    in_specs = [
        pl.BlockSpec((tq, LANES), lambda b, j, i: (b * nq + i, CB_ATT_Q + 2 * j)),
        pl.BlockSpec((tq, LANES), lambda b, j, i: (b * nq + i, CB_ATT_Q + 2 * j + 1)),
        pl.BlockSpec((tq, LANES), lambda b, j, i: (b * nq + i, CB_ATT_GATE + 2 * j)),
        pl.BlockSpec((tq, LANES), lambda b, j, i: (b * nq + i, CB_ATT_GATE + 2 * j + 1)),
        pl.BlockSpec((1, LANES), lambda b, j, i: (0, 0)),
    ]
    args = [p_q, p_q, p_q, p_q, q_gain]
    if tables is not None:
        for tab in tables:
            in_specs.append(pl.BlockSpec((tq, LANES), lambda b, j, i: (i, 0)))
            args.append(tab)
    n_lat_blocks = 0
    for seg in (lat_kv, ctx_kv):
        if seg is None:
            continue
        k_s, vt_s, seq_s = seg
        in_specs.append(pl.BlockSpec((seq_s, LANES), lambda b, j, i: (b, j)))
        in_specs.append(pl.BlockSpec((VT_ROWS, seq_s), lambda b, j, i: (b * ATT_KV + j, 0)))
        args += [k_s, vt_s]
    if lat_kv is not None:
        n_lat_blocks = lat_kv[2] // tk
    return pl.pallas_call(
        functools.partial(_flash_kernel, n_lat_blocks=n_lat_blocks, tk=tk,
                          has_ctx_kv=ctx_kv is not None, rope=tables is not None),
        grid=(batch, ATT_KV, nq),
        in_specs=in_specs,
        out_specs=pl.BlockSpec((tq, 2 * LANES), lambda b, j, i: (b * nq + i, j)),
        out_shape=jax.ShapeDtypeStruct((batch * seq_q, BRANCH_W), BF16),
        compiler_params=_cparams(("parallel", "parallel", "arbitrary")),
        name="flash_lat" if lat_kv is not None else "flash_ctx",
    )(*args)


def _chunk_rows(c):
    return pl.ds(pl.multiple_of(c * CHUNK, CHUNK), CHUNK)


def _gla_chunk(qk_ref, v_ref, wd_ref, c, wup, bias, masks, recv_state, send_state):
    tri, incl, _ = masks
    rows = _chunk_rows(c)
    qk = qk_ref[rows, :].astype(F32)
    q = qk[:, :GLA_DK] * (GLA_DK ** -0.5)
    k = qk[:, GLA_DK:]
    v = v_ref[rows, :]
    z = _dot(wd_ref[rows, :], wup) + bias
    yield
    g = (jnp.minimum(z, 0.0) - jnp.log1p(jnp.exp(-jnp.abs(z)))) * (1.0 / GLA_TAU)
    gc = _tri_cumsum(tri, g)
    yield
    tot = jnp.sum(g, axis=0, keepdims=True)
    q_dec = (q * jnp.exp(gc)).astype(BF16)
    k_inv = (k * jnp.exp(-gc)).astype(BF16)
    k_tail = (k * jnp.exp(tot - gc)).astype(BF16)
    a = jnp.where(incl, _dot_nt(q_dec, k_inv), 0.0)
    upd = _dot_tn(v, k_tail)
    yield
    st = recv_state()
    while st is None:
        yield
        st = recv_state()
    send_state(st * jnp.exp(tot) + upd)
    return _dot(a.astype(BF16), v) + _dot_nt(q_dec, st.astype(BF16))


def _gla_kernel(*refs, need_ctx):
    (qk_l, v_l, gate_l, wd_l, qk_c, v_c, gate_c, wd_c,
     wupf_ref, wupb_ref, bf_ref, bb_ref, gn_ref) = refs[:13]
    if need_ctx:
        y_l, y_c, of_l, ob_l, of_c, ob_c = refs[13:]
    else:
        y_l, of_l, ob_l, of_c, ob_c = refs[13:]
    nl = qk_l.shape[0] // CHUNK
    nc = qk_c.shape[0] // CHUNK
    wupf, wupb = wupf_ref[0], wupb_ref[0]
    bias_f, bias_b = bf_ref[0], bb_ref[0]
    masks_f, masks_b = _chunk_masks(False), _chunk_masks(True)

    def scan(qk, v, wd, o_f, o_b, n, carry):
        unroll = _unroll_factor(n, GLA_UNROLL)

        def body(i, carry):
            states = {("f", -1): carry[0], ("b", -1): carry[1]}
            gens = []
            for u in range(unroll):
                c = i * unroll + u
                for d, cc, wup, bias, masks in (("f", c, wupf, bias_f, masks_f),
                                                ("b", n - 1 - c, wupb, bias_b, masks_b)):
                    gens.append(_gla_chunk(
                        qk, v, wd, cc, wup, bias, masks,
                        functools.partial(states.get, (d, u - 1)),
                        functools.partial(states.__setitem__, (d, u))))
            outs = _run_interleaved(gens)
            for u in range(unroll):
                c = i * unroll + u
                o_f[_chunk_rows(c), :] = outs[2 * u]
                o_b[_chunk_rows(n - 1 - c), :] = outs[2 * u + 1]
            return states[("f", unroll - 1)], states[("b", unroll - 1)]
        return lax.fori_loop(0, n // unroll, body, carry)

    zero = jnp.zeros((GLA_DV, GLA_DK), F32)
    carry = scan(qk_c, v_c, wd_c, of_c, ob_c, nc, (zero, zero))
    scan(qk_l, v_l, wd_l, of_l, ob_l, nl, carry)

    gn = gn_ref[0]

    def finish(o_f, o_b, gate, y, n):
        def body(i, _):
            rows = _chunk_rows(i)
            o = o_f[rows, :] + o_b[rows, :]
            ms = jnp.mean(o * o, axis=-1, keepdims=True)
            o = o * lax.rsqrt(ms + NORM_EPS) * gn
            y[rows, :] = (o * _silu(gate[rows, :].astype(F32))).astype(y.dtype)
            return 0
        lax.fori_loop(0, n, body, 0, unroll=_unroll_factor(n, FINISH_UNROLL))

    finish(of_l, ob_l, gate_l, y_l, nl)
    if need_ctx:
        finish(of_c, ob_c, gate_c, y_c, nc)


def _gla(p_lat, p_ctx, batch, seq, ctx_len, wupf, wupb, bf, bb, gnorm, need_ctx):
    def seg_specs(n):
        return [pl.BlockSpec((n, LANES), lambda b, h: (b, CB_GLA_QK + h)),
                pl.BlockSpec((n, LANES), lambda b, h: (b, CB_GLA_V + h)),
                pl.BlockSpec((n, LANES), lambda b, h: (b, CB_GLA_GATE + h)),
                pl.BlockSpec((n, LANES), lambda b, h: (b, CB_GLA_WD))]
    w_spec = pl.BlockSpec((1, LANES, GLA_DK), lambda b, h: (h, 0, 0))
    b_spec = pl.BlockSpec((1, 1, GLA_DK), lambda b, h: (h, 0, 0))
    g_spec = pl.BlockSpec((1, 1, GLA_DV), lambda b, h: (h, 0, 0))
    out_specs = [pl.BlockSpec((seq, LANES), lambda b, h: (b, h))]
    out_shape = [jax.ShapeDtypeStruct((batch * seq, BRANCH_W), BF16)]
    if need_ctx:
        out_specs.append(pl.BlockSpec((ctx_len, LANES), lambda b, h: (b, h)))
        out_shape.append(jax.ShapeDtypeStruct((batch * ctx_len, BRANCH_W), BF16))
    outs = pl.pallas_call(
        functools.partial(_gla_kernel, need_ctx=need_ctx),
        grid=(batch, GLA_H),
        in_specs=seg_specs(seq) + seg_specs(ctx_len) + [w_spec, w_spec, b_spec, b_spec, g_spec],
        out_specs=out_specs,
        out_shape=out_shape,
        scratch_shapes=[pltpu.VMEM((seq, LANES), F32)] * 2 + [pltpu.VMEM((ctx_len, LANES), F32)] * 2,
        compiler_params=_cparams(("parallel", "parallel")),
        name="gla",
    )(p_lat, p_lat, p_lat, p_lat, p_ctx, p_ctx, p_ctx, p_ctx, wupf, wupb, bf, bb, gnorm)
    return (outs[0], outs[1]) if need_ctx else (outs[0], None)


def _seg_sum(x, lo):
    s0 = jnp.sum(jnp.where(lo, x, 0.0), axis=-1, keepdims=True)
    s1 = jnp.sum(jnp.where(lo, 0.0, x), axis=-1, keepdims=True)
    return jnp.where(lo, s0, s1)


def _run_interleaved(gens):
    gens = list(gens)
    out = [None] * len(gens)
    live = list(range(len(gens)))
    while live:
        still = []
        for j in live:
            try:
                next(gens[j])
                still.append(j)
            except StopIteration as stop:
                out[j] = stop.value
        live = still
    return out


def _unit_tri_inverse_offdiag(mats, blockdiag):
    def mm(x, y):
        return _dot(x.astype(BF16), y.astype(BF16))

    nd = [jnp.where(blockdiag, -a, 0.0) for a in mats]
    no = [-a - d for a, d in zip(mats, nd)]
    n2 = [mm(d, d) for d in nd]
    yield
    n4 = [mm(s, s) for s in n2]
    x = [d + s + mm(d, s) for d, s in zip(nd, n2)]
    yield
    n8 = [mm(s, s) for s in n4]
    x = [xi + s + mm(xi, s) for xi, s in zip(x, n4)]
    yield
    xd = [xi + s + mm(xi, s) for xi, s in zip(x, n8)]
    yield
    m = [o + mm(xi, o) for xi, o in zip(xd, no)]
    yield
    m2 = [mm(mi, mi) for mi in m]
    yield
    xq = [mi + s + mm(mi, s) for mi, s in zip(m, m2)]
    yield
    return [q + d + mm(q, d) for q, d in zip(xq, xd)]


def _shifted(ref, c, n_chunks, mu, row):
    base = pl.multiple_of(c * CHUNK, CHUNK)
    main = ref[pl.ds(base, CHUNK), :].astype(F32)
    pstart = pl.multiple_of(jnp.maximum(base - 16, 0), 16)
    nstart = pl.multiple_of(jnp.minimum(base + CHUNK, (n_chunks - 1) * CHUNK + CHUNK - 16), 16)
    prev_row = ref[pl.ds(pstart, 16), :].astype(F32)[15:16, :]
    next_row = ref[pl.ds(nstart, 16), :].astype(F32)[0:1, :]
    prev_row = jnp.where(c > 0, prev_row, 0.0)
    next_row = jnp.where(c < n_chunks - 1, next_row, 0.0)
    up = jnp.where(row == 0, prev_row, pltpu.roll(main, 1, 0))
    dn = jnp.where(row == CHUNK - 1, next_row, pltpu.roll(main, CHUNK - 1, 0))
    return main + mu * (0.5 * (up + dn) - main)


def _rwkv_chunk(seg, c, prm, consts, with_gate, recv_state, send_state):
    (r_ref, k_ref, v_ref, g_ref, wd_ref, ad_ref, n_chunks) = seg
    (mu_r, mu_k, mu_v, mu_g, mu_wd, mu_ad, w0, a0, kkw, kaw, rkw, wup, aup) = prm
    (row, lo, blockdiag16, headdiag, masks, strict_hi, incl_wide) = consts
    tri, _, strict = masks

    r = _shifted(r_ref, c, n_chunks, mu_r, row)
    k = _shifted(k_ref, c, n_chunks, mu_k, row)
    v = _shifted(v_ref, c, n_chunks, mu_v, row)
    wd = _shifted(wd_ref, c, n_chunks, mu_wd, row)
    ad = _shifted(ad_ref, c, n_chunks, mu_ad, row)
    gate = _silu(_shifted(g_ref, c, n_chunks, mu_g, row)) if with_gate else None

    z_w = _dot(jnp.tanh(wd).astype(BF16), wup)
    z_a = _dot(ad.astype(BF16), aup)
    yield
    lw = -RWKV_DECAY_SCALE * _sigmoid(w0 + z_w)
    a = _sigmoid(a0 + z_a)
    kd = k * (1.0 + (a - 1.0) * kaw)
    kk = k * kkw
    kk = kk * lax.rsqrt(_seg_sum(kk * kk, lo) + L2_EPS)
    kka = kk * a
    bonus = _seg_sum(r * kd * rkw, lo) * v

    cw = _tri_cumsum(tri, lw)
    yield
    tot = jnp.sum(lw, axis=0, keepdims=True)
    e_neg = jnp.exp(-cw)
    e_tail = jnp.exp(tot - cw)
    kap = kk * jnp.exp(cw - lw)
    rt = r * jnp.exp(cw)
    bt = (kka * e_neg).astype(BF16)
    kt = (kd * e_neg).astype(BF16)
    v16 = v.astype(BF16)
    tails = jnp.concatenate([kka * e_tail, kd * e_tail], axis=0).astype(BF16)
    decay = jnp.exp(tot)
    kap_rt = jnp.concatenate([kap, rt], axis=0).astype(BF16)

    yt = jnp.concatenate([bt, kt], axis=0)
    grams = []
    for hm in (lo, jnp.logical_not(lo)):
        xh = jnp.concatenate([jnp.where(hm, kap, 0.0), jnp.where(hm, rt, 0.0)], axis=0)
        grams.append(_dot_nt(xh.astype(BF16), yt))
    yield
    vv = jnp.concatenate([v16, v16], axis=0)
    a_b = [jnp.where(strict, g[:CHUNK, :CHUNK], 0.0) for g in grams]
    akv = [_dot(jnp.where(strict_hi, g[:CHUNK], 0.0).astype(BF16), vv) for g in grams]
    mats = [jnp.where(incl_wide, g[CHUNK:], 0.0).astype(BF16) for g in grams]
    x_inv = yield from _unit_tri_inverse_offdiag(a_b, blockdiag16)
    x_inv = [x.astype(BF16) for x in x_inv]
    yield

    s2 = recv_state()
    while s2 is None:
        yield
        s2 = recv_state()
    ks_rs = _dot_nt(kap_rt, s2.astype(BF16))
    ks, rs = ks_rs[:CHUNK], ks_rs[CHUNK:]
    yield
    e_heads = []
    for x, kv in zip(x_inv, akv):
        rhs = ks + kv
        e_heads.append(rhs + _dot(x, rhs.astype(BF16)))
    yield
    e = jnp.where(lo, e_heads[0], e_heads[1])
    ev = jnp.concatenate([-e, v], axis=0).astype(BF16)
    upd = _dot_tn(ev, tails)
    send_state(s2 * decay + jnp.where(headdiag, upd, 0.0))
    y = rs + jnp.where(lo, _dot(mats[0], ev), _dot(mats[1], ev))
    return y, bonus, gate


RWKV_PAIRS_PER_STEP = 2
_RWKV_IN_REFS = 31
_RWKV_SCRATCH = 5


def _rwkv_kernel(*refs, need_ctx):
    n_pairs = RWKV_PAIRS_PER_STEP
    n_in = n_pairs * _RWKV_IN_REFS
    rest = refs[n_in:]
    if need_ctx:
        y_l, y_c = rest[:2]
        scr = rest[2:]
    else:
        y_l, y_c = rest[0], None
        scr = rest[1:]
    nl = refs[0].shape[0] // CHUNK
    nc = refs[6].shape[0] // CHUNK

    row = lax.broadcasted_iota(jnp.int32, (CHUNK, LANES), 0)
    lo = lax.broadcasted_iota(jnp.int32, (1, LANES), 1) < RWKV_HD
    t64 = lax.broadcasted_iota(jnp.int32, (CHUNK, CHUNK), 0)
    s64 = lax.broadcasted_iota(jnp.int32, (CHUNK, CHUNK), 1)
    blockdiag16 = (t64 // 16) == (s64 // 16)
    v128 = lax.broadcasted_iota(jnp.int32, (LANES, LANES), 0)
    k128 = lax.broadcasted_iota(jnp.int32, (LANES, LANES), 1)
    headdiag = (v128 < RWKV_HD) == (k128 < RWKV_HD)
    tw = lax.broadcasted_iota(jnp.int32, (CHUNK, LANES), 0)
    lw_ = lax.broadcasted_iota(jnp.int32, (CHUNK, LANES), 1)
    sw = lw_ % CHUNK

    def wide_masks(reverse):
        strict_hi = (lw_ >= CHUNK) & ((sw > tw) if reverse else (sw < tw))
        incl_wide = (sw >= tw) if reverse else (sw <= tw)
        return strict_hi, incl_wide

    consts_f = (row, lo, blockdiag16, headdiag, _chunk_masks(False)) + wide_masks(False)
    consts_b = (row, lo, blockdiag16, headdiag, _chunk_masks(True)) + wide_masks(True)

    pairs = []
    for g in range(n_pairs):
        r = refs[g * _RWKV_IN_REFS:(g + 1) * _RWKV_IN_REFS]
        (mu_r, mu_k, mu_v, mu_g, mu_wd, mu_ad, w0f, w0b, a0f, a0b, kkw, kaw, rkw, lng, lnb,
         wupf, wupb, aupf, aupb) = [x[0] for x in r[12:]]
        shared = (mu_r, mu_k, mu_v, mu_g, mu_wd, mu_ad)
        s = scr[g * 2 * _RWKV_SCRATCH:(g + 1) * 2 * _RWKV_SCRATCH]
        pairs.append(dict(
            lat=tuple(r[0:6]) + (nl,), ctx=tuple(r[6:12]) + (nc,),
            prm_f=shared + (w0f, a0f, kkw, kaw, rkw, wupf, aupf),
            prm_b=shared + (w0b, a0b, kkw, kaw, rkw, wupb, aupb),
            scr_lat=s[:_RWKV_SCRATCH], scr_ctx=s[_RWKV_SCRATCH:], lng=lng, lnb=lnb))

    def scan(which, n, carry):
        unroll = _unroll_factor(n, RWKV_UNROLL)

        def body(i, carry):
            states = {}
            gens = []
            for u in range(unroll):
                c = i * unroll + u
                for g, pr in enumerate(pairs):
                    states[(g, "f", -1)], states[(g, "b", -1)] = carry[2 * g], carry[2 * g + 1]
                    for d, cc, prm, consts in (("f", c, pr["prm_f"], consts_f),
                                               ("b", n - 1 - c, pr["prm_b"], consts_b)):
                        gens.append(_rwkv_chunk(
                            pr[which], cc, prm, consts, d == "f",
                            functools.partial(states.get, (g, d, u - 1)),
                            functools.partial(states.__setitem__, (g, d, u))))
            outs = iter(_run_interleaved(gens))
            for u in range(unroll):
                c = i * unroll + u
                rows_f, rows_b = _chunk_rows(c), _chunk_rows(n - 1 - c)
                for pr in pairs:
                    yf, yb, bf, bb, gs = pr["scr_" + which]
                    (y_f, bon_f, gate), (y_b, bon_b, _) = next(outs), next(outs)
                    yf[rows_f, :] = y_f
                    bf[rows_f, :] = bon_f.astype(bf.dtype)
                    gs[rows_f, :] = gate.astype(gs.dtype)
                    yb[rows_b, :] = y_b
                    bb[rows_b, :] = bon_b.astype(bb.dtype)
            return tuple(states[(g, d, unroll - 1)] for g in range(n_pairs) for d in ("f", "b"))
        return lax.fori_loop(0, n // unroll, body, carry)

    zero = jnp.zeros((LANES, LANES), F32)
    carry = scan("ctx", nc, (zero,) * (2 * n_pairs))
    scan("lat", nl, carry)

    def finish(which, y_out, n):
        def body(i, _):
            rows = _chunk_rows(i)
            for g, pr in enumerate(pairs):
                yf, yb, bf, bb, gs = pr["scr_" + which]
                y = yf[rows, :] + yb[rows, :]
                mean = _seg_sum(y, lo) * (1.0 / RWKV_HD)
                d = y - mean
                var = _seg_sum(d * d, lo) * (1.0 / RWKV_HD)
                yn = d * lax.rsqrt(var + GN_EPS) * pr["lng"] + pr["lnb"]
                bonus = bf[rows, :].astype(F32) + bb[rows, :].astype(F32)
                y_out[rows, g * LANES:(g + 1) * LANES] = (
                    (yn + bonus) * gs[rows, :].astype(F32)).astype(y_out.dtype)
            return 0
        lax.fori_loop(0, n, body, 0, unroll=_unroll_factor(n, FINISH_UNROLL))

    finish("lat", y_l, nl)
    if need_ctx:
        finish("ctx", y_c, nc)


def _rwkv(p_lat, p_ctx, batch, seq, ctx_len, prm, need_ctx):
    n_pairs = RWKV_PAIRS_PER_STEP

    def pair_specs(g):
        hp = lambda s: n_pairs * s + g

        def seg_specs(n):
            one = pl.Buffered(1)
            return [pl.BlockSpec((n, LANES), lambda b, s: (b, CB_RW_R + hp(s)), pipeline_mode=one),
                    pl.BlockSpec((n, LANES), lambda b, s: (b, CB_RW_K + hp(s)), pipeline_mode=one),
                    pl.BlockSpec((n, LANES), lambda b, s: (b, CB_RW_V + hp(s)), pipeline_mode=one),
                    pl.BlockSpec((n, LANES), lambda b, s: (b, CB_RW_GATE + hp(s)), pipeline_mode=one),
                    pl.BlockSpec((n, LANES), lambda b, s: (b, CB_RW_WD), pipeline_mode=one),
                    pl.BlockSpec((n, LANES), lambda b, s: (b, CB_RW_AD), pipeline_mode=one)]
        vec_h = pl.BlockSpec((1, 1, LANES), lambda b, s: (hp(s), 0, 0))
        vec_s = pl.BlockSpec((1, 1, LANES), lambda b, s: (0, 0, 0))
        mat_h = pl.BlockSpec((1, LANES, LANES), lambda b, s: (hp(s), 0, 0))
        prm_specs = [vec_h, vec_h, vec_h, vec_h, vec_s, vec_s] + [vec_h] * 9 + [mat_h] * 4
        return seg_specs(seq) + seg_specs(ctx_len) + prm_specs

    in_specs, args = [], []
    for g in range(n_pairs):
        in_specs += pair_specs(g)
        args += [p_lat] * 6 + [p_ctx] * 6 + list(prm)
    width = n_pairs * LANES
    out_specs = [pl.BlockSpec((seq, width), lambda b, s: (b, s))]
    out_shape = [jax.ShapeDtypeStruct((batch * seq, BRANCH_W), BF16)]
    if need_ctx:
        out_specs.append(pl.BlockSpec((ctx_len, width), lambda b, s: (b, s)))
        out_shape.append(jax.ShapeDtypeStruct((batch * ctx_len, BRANCH_W), BF16))
    def seg_scratch(n):
        return [pltpu.VMEM((n, LANES), F32)] * 2 + [pltpu.VMEM((n, LANES), BF16)] * 3
    scratch = (seg_scratch(seq) + seg_scratch(ctx_len)) * n_pairs
    outs = pl.pallas_call(
        functools.partial(_rwkv_kernel, need_ctx=need_ctx),
        grid=(batch, RWKV_H // 2 // n_pairs),
        in_specs=in_specs,
        out_specs=out_specs,
        out_shape=out_shape,
        scratch_shapes=scratch,
        compiler_params=_cparams(("parallel", "parallel")),
        name="rwkv",
    )(*args)
    return (outs[0], outs[1]) if need_ctx else (outs[0], None)


def _merge_kernel(x_ref, y1_ref, y2_ref, y3_ref, g1_ref, g2_ref, g3_ref, mod_ref,
                  wo1_ref, wo2_ref, wo3_ref, wout_ref, gp_ref, o_ref):
    m = (_sigmoid(g1_ref[...].astype(F32)) * _dot(y1_ref[...], wo1_ref[...])
         + _sigmoid(g2_ref[...].astype(F32)) * _dot(y2_ref[...], wo2_ref[...])
         + _sigmoid(g3_ref[...].astype(F32)) * _dot(y3_ref[...], wo3_ref[...]))
    mo = _dot(m.astype(BF16), wout_ref[...])
    ms = jnp.mean(mo * mo, axis=-1, keepdims=True)
    out = mo * lax.rsqrt(ms + NORM_EPS) * gp_ref[...]
    o_ref[...] = x_ref[...] + mod_ref[0][2:3] * out


def _merge(x2d, y1, y2, y3, p, mod, mod_index, wo1, wo2, wo3, wout, g_post, tm):
    r, d = x2d.shape
    row = lambda i: (i, 0)
    const = lambda i: (0, 0)
    return pl.pallas_call(
        _merge_kernel,
        grid=(r // tm,),
        in_specs=[pl.BlockSpec((tm, d), row),
                  pl.BlockSpec((tm, BRANCH_W), row),
                  pl.BlockSpec((tm, BRANCH_W), row),
                  pl.BlockSpec((tm, BRANCH_W), row),
                  pl.BlockSpec((tm, d), lambda i: (i, 0)),
                  pl.BlockSpec((tm, d), lambda i: (i, 1)),
                  pl.BlockSpec((tm, d), lambda i: (i, 2)),
                  pl.BlockSpec((1, 3, d), lambda i: (mod_index(i), 0, 0)),
                  pl.BlockSpec((BRANCH_W, d), const),
                  pl.BlockSpec((BRANCH_W, d), const),
                  pl.BlockSpec((BRANCH_W, d), const),
                  pl.BlockSpec((d, d), const),
                  pl.BlockSpec((1, d), const)],
        out_specs=pl.BlockSpec((tm, d), row),
        out_shape=jax.ShapeDtypeStruct((r, d), F32),
        compiler_params=_cparams(("parallel",)),
        name="merge",
    )(x2d, y1, y2, y3, p, p, p, mod, wo1, wo2, wo3, wout, g_post.reshape(1, d))


def _pack_w_in(w_in):
    d = w_in.shape[0]
    o = 0
    gq = w_in[:, o:o + 256]; o += 256
    gk = w_in[:, o:o + 256]; o += 256
    gv = w_in[:, o:o + 512]; o += 512
    gg = w_in[:, o:o + 512]; o += 512
    gwd = w_in[:, o:o + 32]; o += 32
    att = w_in[:, o:o + 1536]; o += 1536
    rw = w_in[:, o:o + 2304]; o += 2304
    mg = w_in[:, o:o + 3072]
    qk = jnp.concatenate([gq.reshape(d, GLA_H, GLA_DK), gk.reshape(d, GLA_H, GLA_DK)],
                         axis=-1).reshape(d, GLA_H * 2 * GLA_DK)
    pad_wd = jnp.zeros((d, LANES - 32), w_in.dtype)
    pad_end = jnp.zeros((d, LANES), w_in.dtype)
    return jnp.concatenate([mg, qk, gv, gg, gwd, pad_wd, att, rw, pad_end], axis=1).astype(BF16)


def _rope_tables(seq):
    quarter = ATT_HD // 4
    inv = ROPE_THETA ** (-jnp.arange(quarter, dtype=F32) / quarter)
    t = jnp.arange(seq)
    row_pos = (t // GRID_W).astype(F32)
    col_pos = (t % GRID_W).astype(F32)
    ar = row_pos[:, None] * inv[None, :]
    ac = col_pos[:, None] * inv[None, :]
    z = jnp.zeros_like(ar)
    cos = jnp.concatenate([jnp.cos(ar), jnp.cos(ar), jnp.cos(ac), jnp.cos(ac)], axis=1)
    sa = jnp.concatenate([-jnp.sin(ar), z, -jnp.sin(ac), z], axis=1)
    sb = jnp.concatenate([z, jnp.sin(ar), z, jnp.sin(ac)], axis=1)
    return cos, sa, sb


def _pad_rows(w, lo_half):
    z = jnp.zeros((LANES - w.shape[0], w.shape[1]), w.dtype)
    if lo_half:
        return jnp.concatenate([w, z], axis=0)
    half = LANES // 2
    return jnp.concatenate([z[:half], w, z[half:]], axis=0)


def _pick(n, target):
    t = min(n, target)
    while n % t:
        t //= 2
    return t


def kernel(x, c, ctx, c_ctx, w_mod, b_mod, g_pre, w_in, gla_wup_f, gla_b_f, gla_wup_b, gla_b_b, gla_norm, att_qnorm, att_knorm, rwkv_mu, rwkv_w0_f, rwkv_wup_f, rwkv_w0_b, rwkv_wup_b, rwkv_a0_f, rwkv_aup_f, rwkv_a0_b, rwkv_aup_b, rwkv_kk, rwkv_ka, rwkv_rk, rwkv_ln_g, rwkv_ln_b, w_o_gla, w_o_att, w_o_rwkv, w_out, g_post):
    batch, seq, d = x.shape
    ctx_len = ctx.shape[1]
    depth = w_in.shape[0]
    assert seq % CHUNK == 0 and ctx_len % CHUNK == 0 and seq % GRID_W == 0

    nb = -(-(batch + 1) // 8) * 8
    cond = jnp.concatenate([c, c_ctx[None, :], jnp.zeros((nb - batch - 1, d), F32)], axis=0)
    tables = _rope_tables(seq)

    tm_lat = _pick(seq, 1024)
    tm_ctx = _pick(batch * ctx_len, 1024)

    def lat_mod(tm):
        return lambda i: i // (seq // tm)

    def ctx_mod(tm):
        return lambda i: batch
    tq_lat, tk = _pick(seq, 256), _pick(seq, 512)
    tq_ctx = _pick(ctx_len, 256)
    tm_prep = _pick(seq, 512)
    tm_prep_c = _pick(ctx_len, 512)
    tm_merge = _pick(seq, 512)
    tm_merge_c = _pick(batch * ctx_len, 512)

    x2 = x.reshape(batch * seq, d)
    xc2 = ctx.reshape(batch * ctx_len, d)

    def hp(v):
        return v.reshape(RWKV_H // 2, 1, LANES)

    for l in range(depth):
        need_ctx = l < depth - 1
        mod = _modulation(cond, w_mod[l], b_mod[l]).reshape(nb, 3, d)
        w_cat = _pack_w_in(w_in[l])
        p_lat = _inproj(x2, mod, lat_mod(tm_lat), g_pre[l], w_cat, tm_lat)
        p_ctx = _inproj(xc2, mod, ctx_mod(tm_ctx), g_pre[l], w_cat, tm_ctx)

        q_gain = (att_qnorm[l] * (ATT_HD ** -0.5 * LOG2_E)).reshape(1, LANES)
        k_gain = att_knorm[l].reshape(1, LANES)
        kv_lat = tuple(_kvprep(p_lat, k_gain, tables, batch, seq, tm_prep)) + (seq,)
        kv_ctx = tuple(_kvprep(p_ctx, k_gain, None, batch, ctx_len, tm_prep_c)) + (ctx_len,)
        y2 = _flash(p_lat, seq, q_gain, tables, kv_lat, kv_ctx, batch, tq_lat, tk)
        y2c = None
        if need_ctx:
            y2c = _flash(p_ctx, ctx_len, q_gain, None, None, kv_ctx, batch, tq_ctx, tk)

        def gla_w(w, lo_rows):
            w = w.reshape(GLA_LR, GLA_H, GLA_DK).transpose(1, 0, 2)
            z = jnp.zeros((GLA_H, LANES, GLA_DK), w.dtype)
            start = 0 if lo_rows else GLA_LR
            return lax.dynamic_update_slice(z, w, (0, start, 0)).astype(BF16)
        y1, y1c = _gla(p_lat, p_ctx, batch, seq, ctx_len,
                       gla_w(gla_wup_f[l], True), gla_w(gla_wup_b[l], False),
                       gla_b_f[l].reshape(GLA_H, 1, GLA_DK), gla_b_b[l].reshape(GLA_H, 1, GLA_DK),
                       gla_norm[l].reshape(GLA_H, 1, GLA_DV), need_ctx)

        mu = rwkv_mu[l]

        def rw_w(w, lo_rows):
            w = w.reshape(RWKV_LR, RWKV_H // 2, LANES).transpose(1, 0, 2)
            return jnp.stack([_pad_rows(w[i], lo_rows) for i in range(RWKV_H // 2)]).astype(BF16)
        prm = (hp(mu[0:512]), hp(mu[512:1024]), hp(mu[1024:1536]), hp(mu[1536:2048]),
               mu[2048:2176].reshape(1, 1, LANES), mu[2176:2304].reshape(1, 1, LANES),
               hp(rwkv_w0_f[l]), hp(rwkv_w0_b[l]), hp(rwkv_a0_f[l]), hp(rwkv_a0_b[l]),
               hp(rwkv_kk[l]), hp(rwkv_ka[l]), hp(rwkv_rk[l]), hp(rwkv_ln_g[l]), hp(rwkv_ln_b[l]),
               rw_w(rwkv_wup_f[l], True), rw_w(rwkv_wup_b[l], False),
               rw_w(rwkv_aup_f[l], True), rw_w(rwkv_aup_b[l], False))
        y3, y3c = _rwkv(p_lat, p_ctx, batch, seq, ctx_len, prm, need_ctx)

        wo1, wo2, wo3 = (w_o_gla[l].astype(BF16), w_o_att[l].astype(BF16),
                         w_o_rwkv[l].astype(BF16))
        wout = w_out[l].astype(BF16)
        x2 = _merge(x2, y1, y2, y3, p_lat, mod, lat_mod(tm_merge), wo1, wo2, wo3, wout, g_post[l], tm_merge)
        if need_ctx:
            xc2 = _merge(xc2, y1c, y2c, y3c, p_ctx, mod, ctx_mod(tm_merge_c), wo1, wo2, wo3, wout,
                         g_post[l], tm_merge_c)
    return x2.reshape(batch, seq, d)
```

```python
import functools

import jax
import jax.numpy as jnp
from jax import lax
from jax.experimental import pallas as pl
from jax.experimental.pallas import tpu as pltpu

F32 = jnp.float32
BF16 = jnp.bfloat16

GRID_W = 64
BRANCH_W = 512
GLA_H, GLA_DK, GLA_DV, GLA_LR = 4, 64, 128, 16
GLA_TAU = 16.0
ATT_H, ATT_KV, ATT_HD = 4, 2, 128
ROPE_THETA = 10000.0
RWKV_H, RWKV_HD, RWKV_LR = 8, 64, 64
RWKV_DECAY_SCALE = 0.6065306597
NORM_EPS = 1e-6
GN_EPS = 64e-5
L2_EPS = 1e-12
LOG2_E = 1.4426950408889634

CHUNK = 64
LANES = 128
VMEM_LIMIT = 56 * 1024 * 1024

CB_MG = 0
CB_ATT_Q = 24
CB_ATT_V = 30
CB_ATT_GATE = 32
CB_RW = 36
RW_BLOCKS = 18
CB_GLA_QK = 54
CB_GLA_V = 58
CB_GLA_GATE = 62
CB_GLA_WD = 66
NP_BLOCKS = 68
RW_R, RW_K, RW_V, RW_GATE, RW_WD, RW_AD = 0, 4, 8, 12, 16, 17
NP = NP_BLOCKS * LANES
TN_PROJ = NP // 4


def _cparams(sem):
    return pltpu.CompilerParams(dimension_semantics=sem, vmem_limit_bytes=VMEM_LIMIT)


def _dot(a, b):
    return jnp.dot(a, b, preferred_element_type=F32)


def _dot_nt(a, b):
    return lax.dot_general(a, b, (((1,), (1,)), ((), ())), preferred_element_type=F32)


def _dot_tn(a, b):
    return lax.dot_general(a, b, (((0,), (0,)), ((), ())), preferred_element_type=F32)


def _silu(x):
    return x / (1.0 + jnp.exp(-x))


def _sigmoid(x):
    return 1.0 / (1.0 + jnp.exp(-x))


def _split3(x):
    hi = x.astype(BF16)
    r1 = x - hi.astype(F32)
    mid = r1.astype(BF16)
    lo = (r1 - mid.astype(F32)).astype(BF16)
    return hi, mid, lo


def _tri_cumsum(tri, x):
    hi, mid, lo = _split3(x)
    return _dot(tri, hi) + _dot(tri, mid) + _dot(tri, lo)


GLA_UNROLL = 8
RWKV_UNROLL = 2
FINISH_UNROLL = 4


def _unroll_factor(n_chunks, target):
    u = target
    while n_chunks % u:
        u //= 2
    return u


def _chunk_masks(reverse):
    t = lax.broadcasted_iota(jnp.int32, (CHUNK, CHUNK), 0)
    s = lax.broadcasted_iota(jnp.int32, (CHUNK, CHUNK), 1)
    incl = (s >= t) if reverse else (s <= t)
    strict = (s > t) if reverse else (s < t)
    return incl.astype(BF16), incl, strict


def _mod_kernel(c_ref, w_ref, b_ref, o_ref):
    c = c_ref[...]
    s = _silu(c)
    o_ref[...] = jnp.dot(s, w_ref[...], preferred_element_type=F32,
                         precision=lax.Precision.HIGHEST) + b_ref[...]


def _modulation(cond, w_mod, b_mod):
    nb, d = cond.shape
    n = w_mod.shape[1]
    tn = 768
    return pl.pallas_call(
        _mod_kernel,
        grid=(n // tn,),
        in_specs=[pl.BlockSpec((nb, d), lambda j: (0, 0)),
                  pl.BlockSpec((d, tn), lambda j: (0, j)),
                  pl.BlockSpec((1, tn), lambda j: (0, j))],
        out_specs=pl.BlockSpec((nb, tn), lambda j: (0, j)),
        out_shape=jax.ShapeDtypeStruct((nb, n), F32),
        compiler_params=_cparams(("parallel",)),
        name="modulation",
    )(cond, w_mod, b_mod.reshape(1, n))


def _inproj_kernel(x_ref, mod_ref, g_ref, w_ref, o_ref, h_ref):
    @pl.when(pl.program_id(1) == 0)
    def _():
        x = x_ref[...]
        ms = jnp.mean(x * x, axis=-1, keepdims=True)
        y = x * lax.rsqrt(ms + NORM_EPS) * g_ref[...]
        mod = mod_ref[0]
        h_ref[...] = (y * (1.0 + mod[1:2]) + mod[0:1]).astype(BF16)

    o_ref[...] = _dot(h_ref[...], w_ref[...]).astype(o_ref.dtype)


def _inproj(x2d, mod, mod_index, g_pre, w_cat, tm):
    r, d = x2d.shape
    return pl.pallas_call(
        _inproj_kernel,
        grid=(r // tm, NP // TN_PROJ),
        in_specs=[pl.BlockSpec((tm, d), lambda i, j: (i, 0)),
                  pl.BlockSpec((1, 3, d), lambda i, j: (mod_index(i), 0, 0)),
                  pl.BlockSpec((1, d), lambda i, j: (0, 0)),
                  pl.BlockSpec((d, TN_PROJ), lambda i, j: (0, j))],
        out_specs=pl.BlockSpec((tm, TN_PROJ), lambda i, j: (i, j)),
        out_shape=jax.ShapeDtypeStruct((r, NP), BF16),
        scratch_shapes=[pltpu.VMEM((tm, d), BF16)],
        compiler_params=_cparams(("parallel", "arbitrary")),
        name="inproj",
    )(x2d, mod, g_pre.reshape(1, d), w_cat)


def _norm_rope(x, gain, tables):
    x = x.astype(F32)
    ms = jnp.mean(x * x, axis=-1, keepdims=True)
    y = x * lax.rsqrt(ms + NORM_EPS) * gain
    if tables is not None:
        cos, sa, sb = tables
        y = y * cos + pltpu.roll(y, 96, 1) * sa + pltpu.roll(y, 32, 1) * sb
    return y.astype(BF16)


def _kvprep_kernel(*refs, rope):
    if rope:
        k_ref, v_ref, g_ref, cos_ref, sa_ref, sb_ref, ko_ref, vt_ref = refs
        tables = (cos_ref[...], sa_ref[...], sb_ref[...])
    else:
        k_ref, v_ref, g_ref, ko_ref, vt_ref = refs
        tables = None
    ko_ref[...] = _norm_rope(k_ref[...], g_ref[...], tables)
    vt_ref[:ATT_HD, :] = v_ref[...].astype(F32).T.astype(BF16)
    vt_ref[ATT_HD:, :] = jnp.ones((VT_ROWS - ATT_HD, vt_ref.shape[1]), BF16)


VT_ROWS = ATT_HD + 16


def _kvprep(p, k_gain, tables, batch, seq, tm):
    nt = seq // tm
    in_specs = [pl.BlockSpec((tm, LANES), lambda i, j: (i, CB_ATT_Q + ATT_H + j)),
                pl.BlockSpec((tm, LANES), lambda i, j: (i, CB_ATT_V + j)),
                pl.BlockSpec((1, LANES), lambda i, j: (0, 0))]
    args = [p, p, k_gain]
    if tables is not None:
        for tab in tables:
            in_specs.append(pl.BlockSpec((tm, LANES), lambda i, j: (i % nt, 0)))
            args.append(tab)
    return pl.pallas_call(
        functools.partial(_kvprep_kernel, rope=tables is not None),
        grid=(batch * nt, ATT_KV),
        in_specs=in_specs,
        out_specs=[pl.BlockSpec((tm, LANES), lambda i, j: (i, j)),
                   pl.BlockSpec((VT_ROWS, tm), lambda i, j: ((i // nt) * ATT_KV + j, i % nt))],
        out_shape=[jax.ShapeDtypeStruct((batch * seq, ATT_KV * LANES), BF16),
                   jax.ShapeDtypeStruct((batch * ATT_KV * VT_ROWS, seq), BF16)],
        compiler_params=_cparams(("parallel", "parallel")),
        name="kvprep_rope" if tables is not None else "kvprep",
    )(*args)


FLASH_KV_INTERLEAVE = 8


def _flash_kernel(*refs, n_lat_blocks, tk, has_ctx_kv, rope):
    refs = list(refs)
    q0_ref, q1_ref, g0_ref, g1_ref, qg_ref = refs[:5]
    pos = 5
    tables = None
    if rope:
        tables = tuple(r[...] for r in refs[pos:pos + 3])
        pos += 3
    if n_lat_blocks:
        kl_ref, vtl_ref = refs[pos:pos + 2]
        pos += 2
    if has_ctx_kv:
        kc_ref, vtc_ref = refs[pos:pos + 2]
        pos += 2
    o_ref = refs[pos]
    tq = q0_ref.shape[0]
    gain = qg_ref[...]
    q = jnp.concatenate([_norm_rope(q0_ref[...], gain, tables),
                         _norm_rope(q1_ref[...], gain, tables)], axis=0)

    def blocks(kvs, carry):
        scores = [_dot_nt(k, q) for k, _ in kvs]
        m, acc = carry
        for s, (_, vt) in zip(scores, kvs):
            m_new = jnp.maximum(m, jnp.max(s, axis=0, keepdims=True))
            p = jnp.exp2(s - m_new)
            acc = jnp.exp2(m - m_new) * acc + _dot(vt, p.astype(BF16))
            m = m_new
        return m, acc

    carry = (jnp.full((1, 2 * tq), -jnp.inf, F32), jnp.zeros((VT_ROWS, 2 * tq), F32))
    if n_lat_blocks:
        nb = _unroll_factor(n_lat_blocks, FLASH_KV_INTERLEAVE)

        def body(i, c):
            kvs = []
            for u in range(nb):
                start = pl.multiple_of((i * nb + u) * tk, tk)
                kvs.append((kl_ref[pl.ds(start, tk), :], vtl_ref[:, pl.ds(start, tk)]))
            return blocks(kvs, c)
        carry = lax.fori_loop(0, n_lat_blocks // nb, body, carry)
    if has_ctx_kv:
        carry = blocks([(kc_ref[...], vtc_ref[...])], carry)
    _, acc = carry
    o = (acc[:ATT_HD] / acc[ATT_HD:ATT_HD + 1]).T
    o_ref[:, :LANES] = (o[:tq] * _silu(g0_ref[...].astype(F32))).astype(o_ref.dtype)
    o_ref[:, LANES:] = (o[tq:] * _silu(g1_ref[...].astype(F32))).astype(o_ref.dtype)


def _flash(p_q, seq_q, q_gain, tables, lat_kv, ctx_kv, batch, tq, tk):
    nq = seq_q // tq
    in_specs = [
        pl.BlockSpec((tq, LANES), lambda b, j, i: (b * nq + i, CB_ATT_Q + 2 * j)),
        pl.BlockSpec((tq, LANES), lambda b, j, i: (b * nq + i, CB_ATT_Q + 2 * j + 1)),
        pl.BlockSpec((tq, LANES), lambda b, j, i: (b * nq + i, CB_ATT_GATE + 2 * j)),
        pl.BlockSpec((tq, LANES), lambda b, j, i: (b * nq + i, CB_ATT_GATE + 2 * j + 1)),
        pl.BlockSpec((1, LANES), lambda b, j, i: (0, 0)),
    ]
    args = [p_q, p_q, p_q, p_q, q_gain]
    if tables is not None:
        for tab in tables:
            in_specs.append(pl.BlockSpec((tq, LANES), lambda b, j, i: (i, 0)))
            args.append(tab)
    n_lat_blocks = 0
    for seg in (lat_kv, ctx_kv):
        if seg is None:
            continue
        k_s, vt_s, seq_s = seg
        in_specs.append(pl.BlockSpec((seq_s, LANES), lambda b, j, i: (b, j)))
        in_specs.append(pl.BlockSpec((VT_ROWS, seq_s), lambda b, j, i: (b * ATT_KV + j, 0)))
        args += [k_s, vt_s]
    if lat_kv is not None:
        n_lat_blocks = lat_kv[2] // tk
    return pl.pallas_call(
        functools.partial(_flash_kernel, n_lat_blocks=n_lat_blocks, tk=tk,
                          has_ctx_kv=ctx_kv is not None, rope=tables is not None),
        grid=(batch, ATT_KV, nq),
        in_specs=in_specs,
        out_specs=pl.BlockSpec((tq, 2 * LANES), lambda b, j, i: (b * nq + i, j)),
        out_shape=jax.ShapeDtypeStruct((batch * seq_q, BRANCH_W), BF16),
        compiler_params=_cparams(("parallel", "parallel", "arbitrary")),
        name="flash_lat" if lat_kv is not None else "flash_ctx",
    )(*args)


def _chunk_rows(c):
    return pl.ds(pl.multiple_of(c * CHUNK, CHUNK), CHUNK)


def _gla_chunk(qk_ref, v_ref, wd_ref, c, wup, bias, masks, recv_state, send_state):
    tri, incl, _ = masks
    rows = _chunk_rows(c)
    qk = qk_ref[rows, :].astype(F32)
    q = qk[:, :GLA_DK] * (GLA_DK ** -0.5)
    k = qk[:, GLA_DK:]
    v = v_ref[rows, :]
    z = _dot(wd_ref[rows, :], wup) + bias
    yield
    g = (jnp.minimum(z, 0.0) - jnp.log1p(jnp.exp(-jnp.abs(z)))) * (1.0 / GLA_TAU)
    gc = _tri_cumsum(tri, g)
    yield
    tot = jnp.sum(g, axis=0, keepdims=True)
    q_dec = (q * jnp.exp(gc)).astype(BF16)
    k_inv = (k * jnp.exp(-gc)).astype(BF16)
    k_tail = (k * jnp.exp(tot - gc)).astype(BF16)
    a = jnp.where(incl, _dot_nt(q_dec, k_inv), 0.0)
    upd = _dot_tn(v, k_tail)
    yield
    st = recv_state()
    while st is None:
        yield
        st = recv_state()
    send_state(st * jnp.exp(tot) + upd)
    return _dot(a.astype(BF16), v) + _dot_nt(q_dec, st.astype(BF16))


def _gla_kernel(*refs, need_ctx):
    (qk_l, v_l, gate_l, wd_l, qk_c, v_c, gate_c, wd_c,
     wupf_ref, wupb_ref, bf_ref, bb_ref, gn_ref) = refs[:13]
    if need_ctx:
        y_l, y_c, of_l, ob_l, of_c, ob_c = refs[13:]
    else:
        y_l, of_l, ob_l, of_c, ob_c = refs[13:]
    nl = qk_l.shape[0] // CHUNK
    nc = qk_c.shape[0] // CHUNK
    wupf, wupb = wupf_ref[0], wupb_ref[0]
    bias_f, bias_b = bf_ref[0], bb_ref[0]
    masks_f, masks_b = _chunk_masks(False), _chunk_masks(True)

    def scan(qk, v, wd, o_f, o_b, n, carry):
        unroll = _unroll_factor(n, GLA_UNROLL)

        def body(i, carry):
            states = {("f", -1): carry[0], ("b", -1): carry[1]}
            gens = []
            for u in range(unroll):
                c = i * unroll + u
                for d, cc, wup, bias, masks in (("f", c, wupf, bias_f, masks_f),
                                                ("b", n - 1 - c, wupb, bias_b, masks_b)):
                    gens.append(_gla_chunk(
                        qk, v, wd, cc, wup, bias, masks,
                        functools.partial(states.get, (d, u - 1)),
                        functools.partial(states.__setitem__, (d, u))))
            outs = _run_interleaved(gens)
            for u in range(unroll):
                c = i * unroll + u
                o_f[_chunk_rows(c), :] = outs[2 * u]
                o_b[_chunk_rows(n - 1 - c), :] = outs[2 * u + 1]
            return states[("f", unroll - 1)], states[("b", unroll - 1)]
        return lax.fori_loop(0, n // unroll, body, carry)

    zero = jnp.zeros((GLA_DV, GLA_DK), F32)
    carry = scan(qk_c, v_c, wd_c, of_c, ob_c, nc, (zero, zero))
    scan(qk_l, v_l, wd_l, of_l, ob_l, nl, carry)

    gn = gn_ref[0]

    def finish(o_f, o_b, gate, y, n):
        def body(i, _):
            rows = _chunk_rows(i)
            o = o_f[rows, :] + o_b[rows, :]
            ms = jnp.mean(o * o, axis=-1, keepdims=True)
            o = o * lax.rsqrt(ms + NORM_EPS) * gn
            y[rows, :] = (o * _silu(gate[rows, :].astype(F32))).astype(y.dtype)
            return 0
        lax.fori_loop(0, n, body, 0, unroll=_unroll_factor(n, FINISH_UNROLL))

    finish(of_l, ob_l, gate_l, y_l, nl)
    if need_ctx:
        finish(of_c, ob_c, gate_c, y_c, nc)


def _gla(p_lat, p_ctx, batch, seq, ctx_len, wupf, wupb, bf, bb, gnorm, need_ctx):
    def seg_specs(n):
        return [pl.BlockSpec((n, LANES), lambda b, h: (b, CB_GLA_QK + h)),
                pl.BlockSpec((n, LANES), lambda b, h: (b, CB_GLA_V + h)),
                pl.BlockSpec((n, LANES), lambda b, h: (b, CB_GLA_GATE + h)),
                pl.BlockSpec((n, LANES), lambda b, h: (b, CB_GLA_WD))]
    w_spec = pl.BlockSpec((1, LANES, GLA_DK), lambda b, h: (h, 0, 0))
    b_spec = pl.BlockSpec((1, 1, GLA_DK), lambda b, h: (h, 0, 0))
    g_spec = pl.BlockSpec((1, 1, GLA_DV), lambda b, h: (h, 0, 0))
    out_specs = [pl.BlockSpec((seq, LANES), lambda b, h: (b, h))]
    out_shape = [jax.ShapeDtypeStruct((batch * seq, BRANCH_W), BF16)]
    if need_ctx:
        out_specs.append(pl.BlockSpec((ctx_len, LANES), lambda b, h: (b, h)))
        out_shape.append(jax.ShapeDtypeStruct((batch * ctx_len, BRANCH_W), BF16))
    outs = pl.pallas_call(
        functools.partial(_gla_kernel, need_ctx=need_ctx),
        grid=(batch, GLA_H),
        in_specs=seg_specs(seq) + seg_specs(ctx_len) + [w_spec, w_spec, b_spec, b_spec, g_spec],
        out_specs=out_specs,
        out_shape=out_shape,
        scratch_shapes=[pltpu.VMEM((seq, LANES), F32)] * 2 + [pltpu.VMEM((ctx_len, LANES), F32)] * 2,
        compiler_params=_cparams(("parallel", "parallel")),
        name="gla",
    )(p_lat, p_lat, p_lat, p_lat, p_ctx, p_ctx, p_ctx, p_ctx, wupf, wupb, bf, bb, gnorm)
    return (outs[0], outs[1]) if need_ctx else (outs[0], None)


def _seg_sum(x, lo):
    s0 = jnp.sum(jnp.where(lo, x, 0.0), axis=-1, keepdims=True)
    s1 = jnp.sum(jnp.where(lo, 0.0, x), axis=-1, keepdims=True)
    return jnp.where(lo, s0, s1)


def _run_interleaved(gens):
    gens = list(gens)
    out = [None] * len(gens)
    live = list(range(len(gens)))
    while live:
        still = []
        for j in live:
            try:
                next(gens[j])
                still.append(j)
            except StopIteration as stop:
                out[j] = stop.value
        live = still
    return out


def _unit_tri_inverse_offdiag(mats, blockdiag):
    def mm(x, y):
        return _dot(x.astype(BF16), y.astype(BF16))

    nd = [-(a * blockdiag) for a in mats]
    no = [-a - d for a, d in zip(mats, nd)]
    n2 = [mm(d, d) for d in nd]
    yield
    n4 = [mm(s, s) for s in n2]
    x = [d + s + mm(d, s) for d, s in zip(nd, n2)]
    yield
    n8 = [mm(s, s) for s in n4]
    x = [xi + s + mm(xi, s) for xi, s in zip(x, n4)]
    yield
    xd = [xi + s + mm(xi, s) for xi, s in zip(x, n8)]
    yield
    m = [o + mm(xi, o) for xi, o in zip(xd, no)]
    yield
    m2 = [mm(mi, mi) for mi in m]
    yield
    xq = [mi + s + mm(mi, s) for mi, s in zip(m, m2)]
    yield
    return [q + d + mm(q, d) for q, d in zip(xq, xd)]


SHIFT_COLS = RW_BLOCKS * LANES // 2


def _rwkv_shift_kernel(cur_ref, prev_ref, next_ref, mu_ref, o_ref, *, tiles_per_seq):
    t = pl.program_id(0) % tiles_per_seq
    x = cur_ref[...].astype(F32)
    tm = x.shape[0]
    prev_row = jnp.where(t > 0, prev_ref[...].astype(F32)[15:16, :], 0.0)
    next_row = jnp.where(t < tiles_per_seq - 1, next_ref[...].astype(F32)[0:1, :], 0.0)
    row = lax.broadcasted_iota(jnp.int32, x.shape, 0)
    up = jnp.where(row == 0, prev_row, pltpu.roll(x, 1, 0))
    dn = jnp.where(row == tm - 1, next_row, pltpu.roll(x, tm - 1, 0))
    y = x + mu_ref[...] * (0.5 * (up + dn) - x)
    o_ref[...] = y.astype(o_ref.dtype)
    half = pl.program_id(1)

    @pl.when(half == 1)
    def _():
        g0 = (RW_GATE - RW_BLOCKS // 2) * LANES
        w0 = (RW_WD - RW_BLOCKS // 2) * LANES
        o_ref[:, g0:g0 + 4 * LANES] = _silu(y[:, g0:g0 + 4 * LANES]).astype(o_ref.dtype)
        o_ref[:, w0:w0 + LANES] = jnp.tanh(y[:, w0:w0 + LANES]).astype(o_ref.dtype)


def _rwkv_shift(p, mu, seq, tm):
    r = p.shape[0]
    tiles_per_seq = seq // tm
    sub = tm // 16
    cb = CB_RW * LANES // SHIFT_COLS
    return pl.pallas_call(
        functools.partial(_rwkv_shift_kernel, tiles_per_seq=tiles_per_seq),
        grid=(r // tm, 2),
        in_specs=[pl.BlockSpec((tm, SHIFT_COLS), lambda i, j: (i, cb + j)),
                  pl.BlockSpec((16, SHIFT_COLS), lambda i, j: (jnp.maximum(i * sub - 1, 0), cb + j)),
                  pl.BlockSpec((16, SHIFT_COLS),
                               lambda i, j: (jnp.minimum((i + 1) * sub, r // 16 - 1), cb + j)),
                  pl.BlockSpec((1, SHIFT_COLS), lambda i, j: (0, j))],
        out_specs=pl.BlockSpec((tm, SHIFT_COLS), lambda i, j: (i, j)),
        out_shape=jax.ShapeDtypeStruct((r, RW_BLOCKS * LANES), BF16),
        compiler_params=_cparams(("parallel", "parallel")),
        name="rwkv_shift",
    )(p, p, p, mu.reshape(1, RW_BLOCKS * LANES))


def _rwkv_chunk(seg, c, prm, consts, recv_state, send_state):
    (r_ref, k_ref, v_ref, _, wd_ref, ad_ref) = seg
    (w0, a0, kkw, kaw, rkw, wup, aup) = prm
    (lo_f, hi_f, lo, blockdiag16, headdiag, tri, strict, strict_hi, incl_wide) = consts
    rows = _chunk_rows(c)

    r = r_ref[rows, :].astype(F32)
    k = k_ref[rows, :].astype(F32)
    v16 = v_ref[rows, :]
    v = v16.astype(F32)
    z_w = _dot(wd_ref[rows, :], wup)
    z_a = _dot(ad_ref[rows, :], aup)
    yield
    lw = -RWKV_DECAY_SCALE * _sigmoid(w0 + z_w)
    a = _sigmoid(a0 + z_a)
    kd = k * (1.0 + (a - 1.0) * kaw)
    kk = k * kkw
    kk = kk * lax.rsqrt(_seg_sum(kk * kk, lo) + L2_EPS)
    kka = kk * a
    bonus = _seg_sum(r * kd * rkw, lo) * v

    cw = _tri_cumsum(tri, lw)
    yield
    tot = jnp.sum(lw, axis=0, keepdims=True)
    e_neg = jnp.exp(-cw)
    e_tail = jnp.exp(tot - cw)
    kap = kk * jnp.exp(cw - lw)
    rt = r * jnp.exp(cw)
    bt = (kka * e_neg).astype(BF16)
    kt = (kd * e_neg).astype(BF16)
    tails = jnp.concatenate([kka * e_tail, kd * e_tail], axis=0).astype(BF16)
    decay = jnp.exp(tot)
    kap_rt = jnp.concatenate([kap, rt], axis=0)

    yt = jnp.concatenate([bt, kt], axis=0)
    grams = []
    for hm in (lo_f, hi_f):
        grams.append(_dot_nt((kap_rt * hm).astype(BF16), yt))
    kap_rt = kap_rt.astype(BF16)
    yield
    vv = jnp.concatenate([v16, v16], axis=0)
    a_b = [g[:CHUNK, :CHUNK] * strict for g in grams]
    akv = [_dot((g[:CHUNK] * strict_hi).astype(BF16), vv) for g in grams]
    mats = [(g[CHUNK:] * incl_wide).astype(BF16) for g in grams]
    x_inv = yield from _unit_tri_inverse_offdiag(a_b, blockdiag16)
    x_inv = [x.astype(BF16) for x in x_inv]
    yield

    s2 = recv_state()
    while s2 is None:
        yield
        s2 = recv_state()
    ks_rs = _dot_nt(kap_rt, s2.astype(BF16))
    ks, rs = ks_rs[:CHUNK], ks_rs[CHUNK:]
    yield
    e_heads = []
    for x, kv in zip(x_inv, akv):
        rhs = ks + kv
        e_heads.append(rhs + _dot(x, rhs.astype(BF16)))
    yield
    e = jnp.where(lo, e_heads[0], e_heads[1])
    ev = jnp.concatenate([-e, v], axis=0).astype(BF16)
    upd = _dot_tn(ev, tails)
    send_state(s2 * decay + upd * headdiag)
    y = rs + jnp.where(lo, _dot(mats[0], ev), _dot(mats[1], ev))
    return y, bonus


RWKV_PAIRS_PER_STEP = 2
_RWKV_IN_REFS = 25
_RWKV_SCRATCH = 4
_MT_STRICT, _MT_STRICT_HI, _MT_INCL_WIDE = 0, 2, 4
_MT_BLOCKDIAG, _MT_HEADDIAG, _MT_HALVES, _MT_BANDS = 6, 7, 9, 10


def _rwkv_mask_tables():
    import numpy as np
    t = np.arange(CHUNK)[:, None]
    s = np.arange(CHUNK)[None, :]
    lane = np.arange(LANES)[None, :]
    table = np.zeros((_MT_BANDS, CHUNK, LANES), np.float32)
    tri = np.zeros((2, CHUNK, CHUNK), np.float32)
    for d, (strict, incl) in enumerate(((s < t, s <= t), (s > t, s >= t))):
        table[_MT_STRICT + d, :, :CHUNK] = strict
        table[_MT_STRICT_HI + d, :, CHUNK:] = strict
        table[_MT_INCL_WIDE + d] = np.concatenate([incl, incl], axis=1)
        tri[d] = incl
    table[_MT_BLOCKDIAG, :, :CHUNK] = (t // 16) == (s // 16)
    v = np.arange(LANES)[:, None]
    table[_MT_HEADDIAG:_MT_HEADDIAG + 2] = ((v < RWKV_HD) == (lane < RWKV_HD)).reshape(2, CHUNK, LANES)
    table[_MT_HALVES, 0] = lane[0] < RWKV_HD
    table[_MT_HALVES, 1] = lane[0] >= RWKV_HD
    return table.reshape(_MT_BANDS * CHUNK, LANES), tri


def _rwkv_kernel(*refs, need_ctx):
    n_pairs = RWKV_PAIRS_PER_STEP
    mt_ref, tri_ref = refs[:2]
    refs = refs[2:]
    n_in = n_pairs * _RWKV_IN_REFS
    rest = refs[n_in:]
    if need_ctx:
        y_l, y_c = rest[:2]
        scr = rest[2:]
    else:
        y_l, y_c = rest[0], None
        scr = rest[1:]
    nl = refs[0].shape[0] // CHUNK
    nc = refs[6].shape[0] // CHUNK

    def band(j, rows=CHUNK, lanes=LANES):
        return mt_ref[j * CHUNK:j * CHUNK + rows, :lanes]

    lo = lax.broadcasted_iota(jnp.int32, (1, LANES), 1) < RWKV_HD
    lo_f, hi_f = band(_MT_HALVES, 1), mt_ref[_MT_HALVES * CHUNK + 1:_MT_HALVES * CHUNK + 2, :]
    blockdiag16 = band(_MT_BLOCKDIAG, lanes=CHUNK)
    headdiag = band(_MT_HEADDIAG, rows=LANES)

    def consts(d):
        return (lo_f, hi_f, lo, blockdiag16, headdiag, tri_ref[d], band(_MT_STRICT + d, lanes=CHUNK),
                band(_MT_STRICT_HI + d), band(_MT_INCL_WIDE + d))
    consts_f, consts_b = consts(0), consts(1)

    pairs = []
    for g in range(n_pairs):
        r = refs[g * _RWKV_IN_REFS:(g + 1) * _RWKV_IN_REFS]
        (w0f, w0b, a0f, a0b, kkw, kaw, rkw, lng, lnb, wupf, wupb, aupf, aupb) = [x[0] for x in r[12:]]
        s = scr[g * 2 * _RWKV_SCRATCH:(g + 1) * 2 * _RWKV_SCRATCH]
        pairs.append(dict(
            lat=tuple(r[0:6]), ctx=tuple(r[6:12]),
            prm_f=(w0f, a0f, kkw, kaw, rkw, wupf, aupf),
            prm_b=(w0b, a0b, kkw, kaw, rkw, wupb, aupb),
            scr_lat=s[:_RWKV_SCRATCH], scr_ctx=s[_RWKV_SCRATCH:], lng=lng, lnb=lnb))

    def scan(which, n, carry):
        unroll = _unroll_factor(n, RWKV_UNROLL)

        def body(i, carry):
            states = {}
            gens = []
            for u in range(unroll):
                c = i * unroll + u
                for g, pr in enumerate(pairs):
                    states[(g, "f", -1)], states[(g, "b", -1)] = carry[2 * g], carry[2 * g + 1]
                    for d, cc, prm, consts in (("f", c, pr["prm_f"], consts_f),
                                               ("b", n - 1 - c, pr["prm_b"], consts_b)):
                        gens.append(_rwkv_chunk(
                            pr[which], cc, prm, consts,
                            functools.partial(states.get, (g, d, u - 1)),
                            functools.partial(states.__setitem__, (g, d, u))))
            outs = iter(_run_interleaved(gens))
            for u in range(unroll):
                c = i * unroll + u
                rows_f, rows_b = _chunk_rows(c), _chunk_rows(n - 1 - c)
                for pr in pairs:
                    yf, yb, bf, bb = pr["scr_" + which]
                    (y_f, bon_f), (y_b, bon_b) = next(outs), next(outs)
                    yf[rows_f, :] = y_f
                    bf[rows_f, :] = bon_f.astype(bf.dtype)
                    yb[rows_b, :] = y_b
                    bb[rows_b, :] = bon_b.astype(bb.dtype)
            return tuple(states[(g, d, unroll - 1)] for g in range(n_pairs) for d in ("f", "b"))
        return lax.fori_loop(0, n // unroll, body, carry)

    zero = jnp.zeros((LANES, LANES), F32)
    carry = scan("ctx", nc, (zero,) * (2 * n_pairs))
    scan("lat", nl, carry)

    def finish(which, y_out, n):
        def body(i, _):
            rows = _chunk_rows(i)
            for g, pr in enumerate(pairs):
                yf, yb, bf, bb = pr["scr_" + which]
                gs = pr[which][3]
                y = yf[rows, :] + yb[rows, :]
                mean = _seg_sum(y, lo) * (1.0 / RWKV_HD)
                d = y - mean
                var = _seg_sum(d * d, lo) * (1.0 / RWKV_HD)
                yn = d * lax.rsqrt(var + GN_EPS) * pr["lng"] + pr["lnb"]
                bonus = bf[rows, :].astype(F32) + bb[rows, :].astype(F32)
                y_out[rows, g * LANES:(g + 1) * LANES] = (
                    (yn + bonus) * gs[rows, :].astype(F32)).astype(y_out.dtype)
            return 0
        lax.fori_loop(0, n, body, 0, unroll=_unroll_factor(n, FINISH_UNROLL))

    finish("lat", y_l, nl)
    if need_ctx:
        finish("ctx", y_c, nc)


def _rwkv(ps_lat, ps_ctx, batch, seq, ctx_len, prm, need_ctx):
    n_pairs = RWKV_PAIRS_PER_STEP

    def pair_specs(g):
        hp = lambda s: n_pairs * s + g

        def seg_specs(n):
            one = pl.Buffered(1)
            return [pl.BlockSpec((n, LANES), lambda b, s: (b, RW_R + hp(s)), pipeline_mode=one),
                    pl.BlockSpec((n, LANES), lambda b, s: (b, RW_K + hp(s)), pipeline_mode=one),
                    pl.BlockSpec((n, LANES), lambda b, s: (b, RW_V + hp(s)), pipeline_mode=one),
                    pl.BlockSpec((n, LANES), lambda b, s: (b, RW_GATE + hp(s)), pipeline_mode=one),
                    pl.BlockSpec((n, LANES), lambda b, s: (b, RW_WD), pipeline_mode=one),
                    pl.BlockSpec((n, LANES), lambda b, s: (b, RW_AD), pipeline_mode=one)]
        vec_h = pl.BlockSpec((1, 1, LANES), lambda b, s: (hp(s), 0, 0))
        mat_h = pl.BlockSpec((1, LANES, LANES), lambda b, s: (hp(s), 0, 0))
        return seg_specs(seq) + seg_specs(ctx_len) + [vec_h] * 9 + [mat_h] * 4

    table, tri = _rwkv_mask_tables()
    in_specs = [pl.BlockSpec(table.shape, lambda b, s: (0, 0)),
                pl.BlockSpec(tri.shape, lambda b, s: (0, 0, 0))]
    args = [jnp.asarray(table), jnp.asarray(tri, BF16)]
    for g in range(n_pairs):
        in_specs += pair_specs(g)
        args += [ps_lat] * 6 + [ps_ctx] * 6 + list(prm)
    width = n_pairs * LANES
    out_specs = [pl.BlockSpec((seq, width), lambda b, s: (b, s))]
    out_shape = [jax.ShapeDtypeStruct((batch * seq, BRANCH_W), BF16)]
    if need_ctx:
        out_specs.append(pl.BlockSpec((ctx_len, width), lambda b, s: (b, s)))
        out_shape.append(jax.ShapeDtypeStruct((batch * ctx_len, BRANCH_W), BF16))
    def seg_scratch(n):
        return [pltpu.VMEM((n, LANES), F32)] * 2 + [pltpu.VMEM((n, LANES), BF16)] * 2
    scratch = (seg_scratch(seq) + seg_scratch(ctx_len)) * n_pairs
    outs = pl.pallas_call(
        functools.partial(_rwkv_kernel, need_ctx=need_ctx),
        grid=(batch, RWKV_H // 2 // n_pairs),
        in_specs=in_specs,
        out_specs=out_specs,
        out_shape=out_shape,
        scratch_shapes=scratch,
        compiler_params=_cparams(("parallel", "parallel")),
        name="rwkv",
    )(*args)
    return (outs[0], outs[1]) if need_ctx else (outs[0], None)


def _merge_kernel(x_ref, y1_ref, y2_ref, y3_ref, g1_ref, g2_ref, g3_ref, mod_ref,
                  wo1_ref, wo2_ref, wo3_ref, wout_ref, gp_ref, o_ref):
    m = (_sigmoid(g1_ref[...].astype(F32)) * _dot(y1_ref[...], wo1_ref[...])
         + _sigmoid(g2_ref[...].astype(F32)) * _dot(y2_ref[...], wo2_ref[...])
         + _sigmoid(g3_ref[...].astype(F32)) * _dot(y3_ref[...], wo3_ref[...]))
    mo = _dot(m.astype(BF16), wout_ref[...])
    ms = jnp.mean(mo * mo, axis=-1, keepdims=True)
    out = mo * lax.rsqrt(ms + NORM_EPS) * gp_ref[...]
    o_ref[...] = x_ref[...] + mod_ref[0][2:3] * out


def _merge(x2d, y1, y2, y3, p, mod, mod_index, wo1, wo2, wo3, wout, g_post, tm):
    r, d = x2d.shape
    row = lambda i: (i, 0)
    const = lambda i: (0, 0)
    return pl.pallas_call(
        _merge_kernel,
        grid=(r // tm,),
        in_specs=[pl.BlockSpec((tm, d), row),
                  pl.BlockSpec((tm, BRANCH_W), row),
                  pl.BlockSpec((tm, BRANCH_W), row),
                  pl.BlockSpec((tm, BRANCH_W), row),
                  pl.BlockSpec((tm, d), lambda i: (i, 0)),
                  pl.BlockSpec((tm, d), lambda i: (i, 1)),
                  pl.BlockSpec((tm, d), lambda i: (i, 2)),
                  pl.BlockSpec((1, 3, d), lambda i: (mod_index(i), 0, 0)),
                  pl.BlockSpec((BRANCH_W, d), const),
                  pl.BlockSpec((BRANCH_W, d), const),
                  pl.BlockSpec((BRANCH_W, d), const),
                  pl.BlockSpec((d, d), const),
                  pl.BlockSpec((1, d), const)],
        out_specs=pl.BlockSpec((tm, d), row),
        out_shape=jax.ShapeDtypeStruct((r, d), F32),
        compiler_params=_cparams(("parallel",)),
        name="merge",
    )(x2d, y1, y2, y3, p, p, p, mod, wo1, wo2, wo3, wout, g_post.reshape(1, d))


def _pack_w_in(w_in):
    d = w_in.shape[0]
    o = 0
    gq = w_in[:, o:o + 256]; o += 256
    gk = w_in[:, o:o + 256]; o += 256
    gv = w_in[:, o:o + 512]; o += 512
    gg = w_in[:, o:o + 512]; o += 512
    gwd = w_in[:, o:o + 32]; o += 32
    att = w_in[:, o:o + 1536]; o += 1536
    rw = w_in[:, o:o + 2304]; o += 2304
    mg = w_in[:, o:o + 3072]
    qk = jnp.concatenate([gq.reshape(d, GLA_H, GLA_DK), gk.reshape(d, GLA_H, GLA_DK)],
                         axis=-1).reshape(d, GLA_H * 2 * GLA_DK)
    pad_wd = jnp.zeros((d, LANES - 32), w_in.dtype)
    pad_end = jnp.zeros((d, LANES), w_in.dtype)
    return jnp.concatenate([mg, att, rw, qk, gv, gg, gwd, pad_wd, pad_end], axis=1).astype(BF16)


def _rope_tables(seq):
    quarter = ATT_HD // 4
    inv = ROPE_THETA ** (-jnp.arange(quarter, dtype=F32) / quarter)
    t = jnp.arange(seq)
    row_pos = (t // GRID_W).astype(F32)
    col_pos = (t % GRID_W).astype(F32)
    ar = row_pos[:, None] * inv[None, :]
    ac = col_pos[:, None] * inv[None, :]
    z = jnp.zeros_like(ar)
    cos = jnp.concatenate([jnp.cos(ar), jnp.cos(ar), jnp.cos(ac), jnp.cos(ac)], axis=1)
    sa = jnp.concatenate([-jnp.sin(ar), z, -jnp.sin(ac), z], axis=1)
    sb = jnp.concatenate([z, jnp.sin(ar), z, jnp.sin(ac)], axis=1)
    return cos, sa, sb


def _pad_rows(w, lo_half):
    z = jnp.zeros((LANES - w.shape[0], w.shape[1]), w.dtype)
    if lo_half:
        return jnp.concatenate([w, z], axis=0)
    half = LANES // 2
    return jnp.concatenate([z[:half], w, z[half:]], axis=0)


def _pick(n, target):
    t = min(n, target)
    while n % t:
        t //= 2
    return t


def kernel(x, c, ctx, c_ctx, w_mod, b_mod, g_pre, w_in, gla_wup_f, gla_b_f, gla_wup_b, gla_b_b, gla_norm, att_qnorm, att_knorm, rwkv_mu, rwkv_w0_f, rwkv_wup_f, rwkv_w0_b, rwkv_wup_b, rwkv_a0_f, rwkv_aup_f, rwkv_a0_b, rwkv_aup_b, rwkv_kk, rwkv_ka, rwkv_rk, rwkv_ln_g, rwkv_ln_b, w_o_gla, w_o_att, w_o_rwkv, w_out, g_post):
    batch, seq, d = x.shape
    ctx_len = ctx.shape[1]
    depth = w_in.shape[0]
    assert seq % CHUNK == 0 and ctx_len % CHUNK == 0 and seq % GRID_W == 0

    nb = -(-(batch + 1) // 8) * 8
    cond = jnp.concatenate([c, c_ctx[None, :], jnp.zeros((nb - batch - 1, d), F32)], axis=0)
    tables = _rope_tables(seq)

    tm_lat = _pick(seq, 1024)
    tm_ctx = _pick(batch * ctx_len, 1024)

    def lat_mod(tm):
        return lambda i: i // (seq // tm)

    def ctx_mod(tm):
        return lambda i: batch
    tq_lat, tk = _pick(seq, 256), _pick(seq, 512)
    tq_ctx = _pick(ctx_len, 256)
    tm_prep = _pick(seq, 512)
    tm_prep_c = _pick(ctx_len, 512)
    tm_shift = _pick(seq, 512)
    tm_shift_c = _pick(ctx_len, 512)
    tm_merge = _pick(seq, 512)
    tm_merge_c = _pick(batch * ctx_len, 512)

    x2 = x.reshape(batch * seq, d)
    xc2 = ctx.reshape(batch * ctx_len, d)

    def hp(v):
        return v.reshape(RWKV_H // 2, 1, LANES)

    for l in range(depth):
        need_ctx = l < depth - 1
        mod = _modulation(cond, w_mod[l], b_mod[l]).reshape(nb, 3, d)
        w_cat = _pack_w_in(w_in[l])
        p_lat = _inproj(x2, mod, lat_mod(tm_lat), g_pre[l], w_cat, tm_lat)
        p_ctx = _inproj(xc2, mod, ctx_mod(tm_ctx), g_pre[l], w_cat, tm_ctx)

        q_gain = (att_qnorm[l] * (ATT_HD ** -0.5 * LOG2_E)).reshape(1, LANES)
        k_gain = att_knorm[l].reshape(1, LANES)
        kv_lat = tuple(_kvprep(p_lat, k_gain, tables, batch, seq, tm_prep)) + (seq,)
        kv_ctx = tuple(_kvprep(p_ctx, k_gain, None, batch, ctx_len, tm_prep_c)) + (ctx_len,)
        y2 = _flash(p_lat, seq, q_gain, tables, kv_lat, kv_ctx, batch, tq_lat, tk)
        y2c = None
        if need_ctx:
            y2c = _flash(p_ctx, ctx_len, q_gain, None, None, kv_ctx, batch, tq_ctx, tk)

        def gla_w(w, lo_rows):
            w = w.reshape(GLA_LR, GLA_H, GLA_DK).transpose(1, 0, 2)
            z = jnp.zeros((GLA_H, LANES, GLA_DK), w.dtype)
            start = 0 if lo_rows else GLA_LR
            return lax.dynamic_update_slice(z, w, (0, start, 0)).astype(BF16)
        y1, y1c = _gla(p_lat, p_ctx, batch, seq, ctx_len,
                       gla_w(gla_wup_f[l], True), gla_w(gla_wup_b[l], False),
                       gla_b_f[l].reshape(GLA_H, 1, GLA_DK), gla_b_b[l].reshape(GLA_H, 1, GLA_DK),
                       gla_norm[l].reshape(GLA_H, 1, GLA_DV), need_ctx)

        def rw_w(w, lo_rows):
            w = w.reshape(RWKV_LR, RWKV_H // 2, LANES).transpose(1, 0, 2)
            return jnp.stack([_pad_rows(w[i], lo_rows) for i in range(RWKV_H // 2)]).astype(BF16)
        prm = (hp(rwkv_w0_f[l]), hp(rwkv_w0_b[l]), hp(rwkv_a0_f[l]), hp(rwkv_a0_b[l]),
               hp(rwkv_kk[l]), hp(rwkv_ka[l]), hp(rwkv_rk[l]), hp(rwkv_ln_g[l]), hp(rwkv_ln_b[l]),
               rw_w(rwkv_wup_f[l], True), rw_w(rwkv_wup_b[l], False),
               rw_w(rwkv_aup_f[l], True), rw_w(rwkv_aup_b[l], False))
        ps_lat = _rwkv_shift(p_lat, rwkv_mu[l], seq, tm_shift)
        ps_ctx = _rwkv_shift(p_ctx, rwkv_mu[l], ctx_len, tm_shift_c)
        y3, y3c = _rwkv(ps_lat, ps_ctx, batch, seq, ctx_len, prm, need_ctx)

        wo1, wo2, wo3 = (w_o_gla[l].astype(BF16), w_o_att[l].astype(BF16),
                         w_o_rwkv[l].astype(BF16))
        wout = w_out[l].astype(BF16)
        x2 = _merge(x2, y1, y2, y3, p_lat, mod, lat_mod(tm_merge), wo1, wo2, wo3, wout, g_post[l], tm_merge)
        if need_ctx:
            xc2 = _merge(xc2, y1c, y2c, y3c, p_ctx, mod, ctx_mod(tm_merge_c), wo1, wo2, wo3, wout,
                         g_post[l], tm_merge_c)
    return x2.reshape(batch, seq, d)
```

```python
import functools

import jax
import jax.numpy as jnp
from jax import lax
from jax.experimental import pallas as pl
from jax.experimental.pallas import tpu as pltpu

F32 = jnp.float32
BF16 = jnp.bfloat16

GRID_W = 64
BRANCH_W = 512
GLA_H, GLA_DK, GLA_DV, GLA_LR = 4, 64, 128, 16
GLA_TAU = 16.0
ATT_H, ATT_KV, ATT_HD = 4, 2, 128
ROPE_THETA = 10000.0
RWKV_H, RWKV_HD, RWKV_LR = 8, 64, 64
RWKV_DECAY_SCALE = 0.6065306597
NORM_EPS = 1e-6
GN_EPS = 64e-5
L2_EPS = 1e-12
LOG2_E = 1.4426950408889634

CHUNK = 64
LANES = 128
VMEM_LIMIT = 56 * 1024 * 1024

CB_MG = 0
CB_ATT_Q = 24
CB_ATT_V = 30
CB_ATT_GATE = 32
CB_RW = 36
RW_BLOCKS = 18
CB_GLA_QK = 54
CB_GLA_V = 58
CB_GLA_GATE = 62
CB_GLA_WD = 66
NP_BLOCKS = 68
RW_R, RW_K, RW_V, RW_GATE, RW_WD, RW_AD = 0, 4, 8, 12, 16, 17
NP = NP_BLOCKS * LANES
TN_PROJ = NP // 4


def _cparams(sem):
    return pltpu.CompilerParams(dimension_semantics=sem, vmem_limit_bytes=VMEM_LIMIT)


def _dot(a, b):
    return jnp.dot(a, b, preferred_element_type=F32)


def _dot_nt(a, b):
    return lax.dot_general(a, b, (((1,), (1,)), ((), ())), preferred_element_type=F32)


def _dot_tn(a, b):
    return lax.dot_general(a, b, (((0,), (0,)), ((), ())), preferred_element_type=F32)


def _silu(x):
    return x / (1.0 + jnp.exp(-x))


def _sigmoid(x):
    return 1.0 / (1.0 + jnp.exp(-x))


def _tri_cumsum(tri, x):
    hi = x.astype(BF16)
    lo = (x - hi.astype(F32)).astype(BF16)
    n = x.shape[1]
    if n % LANES:
        return _dot(tri, hi) + _dot(tri, lo)
    r = _dot(tri, jnp.concatenate([hi, lo], axis=1))
    return r[:, :n] + r[:, n:]


GLA_UNROLL = 8
RWKV_UNROLL = 2
FINISH_UNROLL = 4


def _unroll_factor(n_chunks, target):
    u = target
    while n_chunks % u:
        u //= 2
    return u


def _chunk_masks(reverse):
    t = lax.broadcasted_iota(jnp.int32, (CHUNK, CHUNK), 0)
    s = lax.broadcasted_iota(jnp.int32, (CHUNK, CHUNK), 1)
    incl = (s >= t) if reverse else (s <= t)
    strict = (s > t) if reverse else (s < t)
    return incl.astype(BF16), incl, strict


def _mod_kernel(c_ref, w_ref, b_ref, o_ref):
    c = c_ref[...]
    s = _silu(c)
    o_ref[...] = jnp.dot(s, w_ref[...], preferred_element_type=F32,
                         precision=lax.Precision.HIGHEST) + b_ref[...]


def _modulation(cond, w_mod, b_mod):
    nb, d = cond.shape
    n = w_mod.shape[1]
    tn = 768
    return pl.pallas_call(
        _mod_kernel,
        grid=(n // tn,),
        in_specs=[pl.BlockSpec((nb, d), lambda j: (0, 0)),
                  pl.BlockSpec((d, tn), lambda j: (0, j)),
                  pl.BlockSpec((1, tn), lambda j: (0, j))],
        out_specs=pl.BlockSpec((nb, tn), lambda j: (0, j)),
        out_shape=jax.ShapeDtypeStruct((nb, n), F32),
        compiler_params=_cparams(("parallel",)),
        name="modulation",
    )(cond, w_mod, b_mod.reshape(1, n))


def _inproj_kernel(x_ref, mod_ref, g_ref, w_ref, o_ref, h_ref):
    @pl.when(pl.program_id(1) == 0)
    def _():
        x = x_ref[...]
        ms = jnp.mean(x * x, axis=-1, keepdims=True)
        y = x * lax.rsqrt(ms + NORM_EPS) * g_ref[...]
        mod = mod_ref[0]
        h_ref[...] = (y * (1.0 + mod[1:2]) + mod[0:1]).astype(BF16)

    o_ref[...] = _dot(h_ref[...], w_ref[...]).astype(o_ref.dtype)


def _inproj(x2d, mod, mod_index, g_pre, w_cat, tm):
    r, d = x2d.shape
    return pl.pallas_call(
        _inproj_kernel,
        grid=(r // tm, NP // TN_PROJ),
        in_specs=[pl.BlockSpec((tm, d), lambda i, j: (i, 0)),
                  pl.BlockSpec((1, 3, d), lambda i, j: (mod_index(i), 0, 0)),
                  pl.BlockSpec((1, d), lambda i, j: (0, 0)),
                  pl.BlockSpec((d, TN_PROJ), lambda i, j: (0, j))],
        out_specs=pl.BlockSpec((tm, TN_PROJ), lambda i, j: (i, j)),
        out_shape=jax.ShapeDtypeStruct((r, NP), BF16),
        scratch_shapes=[pltpu.VMEM((tm, d), BF16)],
        compiler_params=_cparams(("parallel", "arbitrary")),
        name="inproj",
    )(x2d, mod, g_pre.reshape(1, d), w_cat)


def _norm_rope(x, gain, tables):
    x = x.astype(F32)
    ms = jnp.mean(x * x, axis=-1, keepdims=True)
    y = x * lax.rsqrt(ms + NORM_EPS) * gain
    if tables is not None:
        cos, sa, sb = tables
        y = y * cos + pltpu.roll(y, 96, 1) * sa + pltpu.roll(y, 32, 1) * sb
    return y.astype(BF16)


def _kvprep_kernel(*refs, rope):
    if rope:
        k_ref, v_ref, g_ref, cos_ref, sa_ref, sb_ref, ko_ref, vt_ref = refs
        tables = (cos_ref[...], sa_ref[...], sb_ref[...])
    else:
        k_ref, v_ref, g_ref, ko_ref, vt_ref = refs
        tables = None
    ko_ref[...] = _norm_rope(k_ref[...], g_ref[...], tables)
    vt_ref[:ATT_HD, :] = v_ref[...].astype(F32).T.astype(BF16)
    vt_ref[ATT_HD:, :] = jnp.ones((VT_ROWS - ATT_HD, vt_ref.shape[1]), BF16)


VT_ROWS = ATT_HD + 16


def _kvprep(p, k_gain, tables, batch, seq, tm):
    nt = seq // tm
    in_specs = [pl.BlockSpec((tm, LANES), lambda i, j: (i, CB_ATT_Q + ATT_H + j)),
                pl.BlockSpec((tm, LANES), lambda i, j: (i, CB_ATT_V + j)),
                pl.BlockSpec((1, LANES), lambda i, j: (0, 0))]
    args = [p, p, k_gain]
    if tables is not None:
        for tab in tables:
            in_specs.append(pl.BlockSpec((tm, LANES), lambda i, j: (i % nt, 0)))
            args.append(tab)
    return pl.pallas_call(
        functools.partial(_kvprep_kernel, rope=tables is not None),
        grid=(batch * nt, ATT_KV),
        in_specs=in_specs,
        out_specs=[pl.BlockSpec((tm, LANES), lambda i, j: (i, j)),
                   pl.BlockSpec((VT_ROWS, tm), lambda i, j: ((i // nt) * ATT_KV + j, i % nt))],
        out_shape=[jax.ShapeDtypeStruct((batch * seq, ATT_KV * LANES), BF16),
                   jax.ShapeDtypeStruct((batch * ATT_KV * VT_ROWS, seq), BF16)],
        compiler_params=_cparams(("parallel", "parallel")),
        name="kvprep_rope" if tables is not None else "kvprep",
    )(*args)


FLASH_KV_INTERLEAVE = 8


def _flash_kernel(*refs, n_lat_blocks, tk, has_ctx_kv, rope):
    refs = list(refs)
    q0_ref, q1_ref, g0_ref, g1_ref, qg_ref = refs[:5]
    pos = 5
    tables = None
    if rope:
        tables = tuple(r[...] for r in refs[pos:pos + 3])
        pos += 3
    if n_lat_blocks:
        kl_ref, vtl_ref = refs[pos:pos + 2]
        pos += 2
    if has_ctx_kv:
        kc_ref, vtc_ref = refs[pos:pos + 2]
        pos += 2
    o_ref = refs[pos]
    tq = q0_ref.shape[0]
    gain = qg_ref[...]
    q = jnp.concatenate([_norm_rope(q0_ref[...], gain, tables),
                         _norm_rope(q1_ref[...], gain, tables)], axis=0)

    def blocks(kvs, carry):
        scores = [_dot_nt(k, q) for k, _ in kvs]
        m, acc = carry
        for s, (_, vt) in zip(scores, kvs):
            m_new = jnp.maximum(m, jnp.max(s, axis=0, keepdims=True))
            p = jnp.exp2(s - m_new)
            acc = jnp.exp2(m - m_new) * acc + _dot(vt, p.astype(BF16))
            m = m_new
        return m, acc

    carry = (jnp.full((1, 2 * tq), -jnp.inf, F32), jnp.zeros((VT_ROWS, 2 * tq), F32))
    if n_lat_blocks:
        nb = _unroll_factor(n_lat_blocks, FLASH_KV_INTERLEAVE)

        def body(i, c):
            kvs = []
            for u in range(nb):
                start = pl.multiple_of((i * nb + u) * tk, tk)
                kvs.append((kl_ref[pl.ds(start, tk), :], vtl_ref[:, pl.ds(start, tk)]))
            return blocks(kvs, c)
        carry = lax.fori_loop(0, n_lat_blocks // nb, body, carry)
    if has_ctx_kv:
        carry = blocks([(kc_ref[...], vtc_ref[...])], carry)
    _, acc = carry
    o = (acc[:ATT_HD] / acc[ATT_HD:ATT_HD + 1]).T
    o_ref[:, :LANES] = (o[:tq] * _silu(g0_ref[...].astype(F32))).astype(o_ref.dtype)
    o_ref[:, LANES:] = (o[tq:] * _silu(g1_ref[...].astype(F32))).astype(o_ref.dtype)


def _flash(p_q, seq_q, q_gain, tables, lat_kv, ctx_kv, batch, tq, tk):
    nq = seq_q // tq
    in_specs = [
        pl.BlockSpec((tq, LANES), lambda b, j, i: (b * nq + i, CB_ATT_Q + 2 * j)),
        pl.BlockSpec((tq, LANES), lambda b, j, i: (b * nq + i, CB_ATT_Q + 2 * j + 1)),
        pl.BlockSpec((tq, LANES), lambda b, j, i: (b * nq + i, CB_ATT_GATE + 2 * j)),
        pl.BlockSpec((tq, LANES), lambda b, j, i: (b * nq + i, CB_ATT_GATE + 2 * j + 1)),
        pl.BlockSpec((1, LANES), lambda b, j, i: (0, 0)),
    ]
    args = [p_q, p_q, p_q, p_q, q_gain]
    if tables is not None:
        for tab in tables:
            in_specs.append(pl.BlockSpec((tq, LANES), lambda b, j, i: (i, 0)))
            args.append(tab)
    n_lat_blocks = 0
    for seg in (lat_kv, ctx_kv):
        if seg is None:
            continue
        k_s, vt_s, seq_s = seg
        in_specs.append(pl.BlockSpec((seq_s, LANES), lambda b, j, i: (b, j)))
        in_specs.append(pl.BlockSpec((VT_ROWS, seq_s), lambda b, j, i: (b * ATT_KV + j, 0)))
        args += [k_s, vt_s]
    if lat_kv is not None:
        n_lat_blocks = lat_kv[2] // tk
    return pl.pallas_call(
        functools.partial(_flash_kernel, n_lat_blocks=n_lat_blocks, tk=tk,
                          has_ctx_kv=ctx_kv is not None, rope=tables is not None),
        grid=(batch, ATT_KV, nq),
        in_specs=in_specs,
        out_specs=pl.BlockSpec((tq, 2 * LANES), lambda b, j, i: (b * nq + i, j)),
        out_shape=jax.ShapeDtypeStruct((batch * seq_q, BRANCH_W), BF16),
        compiler_params=_cparams(("parallel", "parallel", "arbitrary")),
        name="flash_lat" if lat_kv is not None else "flash_ctx",
    )(*args)


def _chunk_rows(c):
    return pl.ds(pl.multiple_of(c * CHUNK, CHUNK), CHUNK)


def _gla_chunk(qk_ref, v_ref, wd_ref, c, wup, bias, masks, recv_state, send_state):
    tri, incl, _ = masks
    rows = _chunk_rows(c)
    qk = qk_ref[rows, :].astype(F32)
    q = qk[:, :GLA_DK] * (GLA_DK ** -0.5)
    k = qk[:, GLA_DK:]
    v = v_ref[rows, :]
    z = _dot(wd_ref[rows, :], wup) + bias
    yield
    g = (jnp.minimum(z, 0.0) - jnp.log1p(jnp.exp(-jnp.abs(z)))) * (1.0 / GLA_TAU)
    gc = _tri_cumsum(tri, g)
    yield
    tot = jnp.sum(g, axis=0, keepdims=True)
    q_dec = (q * jnp.exp(gc)).astype(BF16)
    k_inv = (k * jnp.exp(-gc)).astype(BF16)
    k_tail = (k * jnp.exp(tot - gc)).astype(BF16)
    a = jnp.where(incl, _dot_nt(q_dec, k_inv), 0.0)
    upd = _dot_tn(v, k_tail)
    yield
    st = recv_state()
    while st is None:
        yield
        st = recv_state()
    send_state(st * jnp.exp(tot) + upd)
    return _dot(a.astype(BF16), v) + _dot_nt(q_dec, st.astype(BF16))


def _gla_kernel(*refs, need_ctx):
    (qk_l, v_l, gate_l, wd_l, qk_c, v_c, gate_c, wd_c,
     wupf_ref, wupb_ref, bf_ref, bb_ref, gn_ref) = refs[:13]
    if need_ctx:
        y_l, y_c, of_l, ob_l, of_c, ob_c = refs[13:]
    else:
        y_l, of_l, ob_l, of_c, ob_c = refs[13:]
    nl = qk_l.shape[0] // CHUNK
    nc = qk_c.shape[0] // CHUNK
    wupf, wupb = wupf_ref[0], wupb_ref[0]
    bias_f, bias_b = bf_ref[0], bb_ref[0]
    masks_f, masks_b = _chunk_masks(False), _chunk_masks(True)

    def scan(qk, v, wd, o_f, o_b, n, carry):
        unroll = _unroll_factor(n, GLA_UNROLL)

        def body(i, carry):
            states = {("f", -1): carry[0], ("b", -1): carry[1]}
            gens = []
            for u in range(unroll):
                c = i * unroll + u
                for d, cc, wup, bias, masks in (("f", c, wupf, bias_f, masks_f),
                                                ("b", n - 1 - c, wupb, bias_b, masks_b)):
                    gens.append(_gla_chunk(
                        qk, v, wd, cc, wup, bias, masks,
                        functools.partial(states.get, (d, u - 1)),
                        functools.partial(states.__setitem__, (d, u))))
            outs = _run_interleaved(gens)
            for u in range(unroll):
                c = i * unroll + u
                o_f[_chunk_rows(c), :] = outs[2 * u]
                o_b[_chunk_rows(n - 1 - c), :] = outs[2 * u + 1]
            return states[("f", unroll - 1)], states[("b", unroll - 1)]
        return lax.fori_loop(0, n // unroll, body, carry)

    zero = jnp.zeros((GLA_DV, GLA_DK), F32)
    carry = scan(qk_c, v_c, wd_c, of_c, ob_c, nc, (zero, zero))
    scan(qk_l, v_l, wd_l, of_l, ob_l, nl, carry)

    gn = gn_ref[0]

    def finish(o_f, o_b, gate, y, n):
        def body(i, _):
            rows = _chunk_rows(i)
            o = o_f[rows, :] + o_b[rows, :]
            ms = jnp.mean(o * o, axis=-1, keepdims=True)
            o = o * lax.rsqrt(ms + NORM_EPS) * gn
            y[rows, :] = (o * _silu(gate[rows, :].astype(F32))).astype(y.dtype)
            return 0
        lax.fori_loop(0, n, body, 0, unroll=_unroll_factor(n, FINISH_UNROLL))

    finish(of_l, ob_l, gate_l, y_l, nl)
    if need_ctx:
        finish(of_c, ob_c, gate_c, y_c, nc)


def _gla(p_lat, p_ctx, batch, seq, ctx_len, wupf, wupb, bf, bb, gnorm, need_ctx):
    def seg_specs(n):
        return [pl.BlockSpec((n, LANES), lambda b, h: (b, CB_GLA_QK + h)),
                pl.BlockSpec((n, LANES), lambda b, h: (b, CB_GLA_V + h)),
                pl.BlockSpec((n, LANES), lambda b, h: (b, CB_GLA_GATE + h)),
                pl.BlockSpec((n, LANES), lambda b, h: (b, CB_GLA_WD))]
    w_spec = pl.BlockSpec((1, LANES, GLA_DK), lambda b, h: (h, 0, 0))
    b_spec = pl.BlockSpec((1, 1, GLA_DK), lambda b, h: (h, 0, 0))
    g_spec = pl.BlockSpec((1, 1, GLA_DV), lambda b, h: (h, 0, 0))
    out_specs = [pl.BlockSpec((seq, LANES), lambda b, h: (b, h))]
    out_shape = [jax.ShapeDtypeStruct((batch * seq, BRANCH_W), BF16)]
    if need_ctx:
        out_specs.append(pl.BlockSpec((ctx_len, LANES), lambda b, h: (b, h)))
        out_shape.append(jax.ShapeDtypeStruct((batch * ctx_len, BRANCH_W), BF16))
    outs = pl.pallas_call(
        functools.partial(_gla_kernel, need_ctx=need_ctx),
        grid=(batch, GLA_H),
        in_specs=seg_specs(seq) + seg_specs(ctx_len) + [w_spec, w_spec, b_spec, b_spec, g_spec],
        out_specs=out_specs,
        out_shape=out_shape,
        scratch_shapes=[pltpu.VMEM((seq, LANES), F32)] * 2 + [pltpu.VMEM((ctx_len, LANES), F32)] * 2,
        compiler_params=_cparams(("parallel", "parallel")),
        name="gla",
    )(p_lat, p_lat, p_lat, p_lat, p_ctx, p_ctx, p_ctx, p_ctx, wupf, wupb, bf, bb, gnorm)
    return (outs[0], outs[1]) if need_ctx else (outs[0], None)


def _seg_sum(x, lo):
    s0 = jnp.sum(jnp.where(lo, x, 0.0), axis=-1, keepdims=True)
    s1 = jnp.sum(jnp.where(lo, 0.0, x), axis=-1, keepdims=True)
    return jnp.where(lo, s0, s1)


def _run_interleaved(gens):
    gens = list(gens)
    out = [None] * len(gens)
    live = list(range(len(gens)))
    while live:
        still = []
        for j in live:
            try:
                next(gens[j])
                still.append(j)
            except StopIteration as stop:
                out[j] = stop.value
        live = still
    return out


def _unit_tri_inverse_offdiag(mats, blockdiag):
    def mm(x, y):
        return _dot(x.astype(BF16), y.astype(BF16))

    def mm2(x1, x2, y):
        r = mm(jnp.concatenate([x1, x2], axis=0), y)
        return r[:CHUNK], r[CHUNK:]

    nd = [-(a * blockdiag) for a in mats]
    no = [-a - d for a, d in zip(mats, nd)]
    n2 = [mm(d, d) for d in nd]
    yield
    r = [mm2(s, d, s) for d, s in zip(nd, n2)]
    n4 = [ri[0] for ri in r]
    x = [d + s + ri[1] for d, s, ri in zip(nd, n2, r)]
    yield
    r = [mm2(s, xi, s) for xi, s in zip(x, n4)]
    n8 = [ri[0] for ri in r]
    x = [xi + s + ri[1] for xi, s, ri in zip(x, n4, r)]
    yield
    xd = [xi + s + mm(xi, s) for xi, s in zip(x, n8)]
    yield
    m = [o + mm(xi, o) for xi, o in zip(xd, no)]
    yield
    m2 = [mm(mi, mi) for mi in m]
    yield
    xq = [mi + s + mm(mi, s) for mi, s in zip(m, m2)]
    yield
    return [q + d + mm(q, d) for q, d in zip(xq, xd)]


SHIFT_COLS = RW_BLOCKS * LANES // 2


def _rwkv_shift_kernel(cur_ref, prev_ref, next_ref, mu_ref, o_ref, *, tiles_per_seq):
    t = pl.program_id(0) % tiles_per_seq
    x = cur_ref[...].astype(F32)
    tm = x.shape[0]
    prev_row = jnp.where(t > 0, prev_ref[...].astype(F32)[15:16, :], 0.0)
    next_row = jnp.where(t < tiles_per_seq - 1, next_ref[...].astype(F32)[0:1, :], 0.0)
    row = lax.broadcasted_iota(jnp.int32, x.shape, 0)
    up = jnp.where(row == 0, prev_row, pltpu.roll(x, 1, 0))
    dn = jnp.where(row == tm - 1, next_row, pltpu.roll(x, tm - 1, 0))
    y = x + mu_ref[...] * (0.5 * (up + dn) - x)
    o_ref[...] = y.astype(o_ref.dtype)
    half = pl.program_id(1)

    @pl.when(half == 1)
    def _():
        g0 = (RW_GATE - RW_BLOCKS // 2) * LANES
        w0 = (RW_WD - RW_BLOCKS // 2) * LANES
        o_ref[:, g0:g0 + 4 * LANES] = _silu(y[:, g0:g0 + 4 * LANES]).astype(o_ref.dtype)
        o_ref[:, w0:w0 + LANES] = jnp.tanh(y[:, w0:w0 + LANES]).astype(o_ref.dtype)


def _rwkv_shift(p, mu, seq, tm):
    r = p.shape[0]
    tiles_per_seq = seq // tm
    sub = tm // 16
    cb = CB_RW * LANES // SHIFT_COLS
    return pl.pallas_call(
        functools.partial(_rwkv_shift_kernel, tiles_per_seq=tiles_per_seq),
        grid=(r // tm, 2),
        in_specs=[pl.BlockSpec((tm, SHIFT_COLS), lambda i, j: (i, cb + j)),
                  pl.BlockSpec((16, SHIFT_COLS), lambda i, j: (jnp.maximum(i * sub - 1, 0), cb + j)),
                  pl.BlockSpec((16, SHIFT_COLS),
                               lambda i, j: (jnp.minimum((i + 1) * sub, r // 16 - 1), cb + j)),
                  pl.BlockSpec((1, SHIFT_COLS), lambda i, j: (0, j))],
        out_specs=pl.BlockSpec((tm, SHIFT_COLS), lambda i, j: (i, j)),
        out_shape=jax.ShapeDtypeStruct((r, RW_BLOCKS * LANES), BF16),
        compiler_params=_cparams(("parallel", "parallel")),
        name="rwkv_shift",
    )(p, p, p, mu.reshape(1, RW_BLOCKS * LANES))


def _rwkv_chunk(seg, c, prm, consts, recv_state, send_state):
    (r_ref, k_ref, v_ref, _, wd_ref, ad_ref) = seg
    (w0, a0, kkw, kaw, rkw, wup, aup) = prm
    (lo_f, hi_f, lo, blockdiag16, headdiag, tri, strict, strict_hi, incl_wide) = consts
    rows = _chunk_rows(c)

    r = r_ref[rows, :].astype(F32)
    k = k_ref[rows, :].astype(F32)
    v16 = v_ref[rows, :]
    v = v16.astype(F32)
    z_w = _dot(wd_ref[rows, :], wup)
    z_a = _dot(ad_ref[rows, :], aup)
    yield
    lw = -RWKV_DECAY_SCALE * _sigmoid(w0 + z_w)
    a = _sigmoid(a0 + z_a)
    kd = k * (1.0 + (a - 1.0) * kaw)
    kk = k * kkw
    kk = kk * lax.rsqrt(_seg_sum(kk * kk, lo) + L2_EPS)
    kka = kk * a
    bonus = _seg_sum(r * kd * rkw, lo) * v

    cw = _tri_cumsum(tri, lw)
    yield
    tot = jnp.sum(lw, axis=0, keepdims=True)
    e_neg = jnp.exp(-cw)
    e_tail = jnp.exp(tot - cw)
    kap = kk * jnp.exp(cw - lw)
    rt = r * jnp.exp(cw)
    bt = (kka * e_neg).astype(BF16)
    kt = (kd * e_neg).astype(BF16)
    tails = jnp.concatenate([kka * e_tail, kd * e_tail], axis=0).astype(BF16)
    decay = jnp.exp(tot)
    kap_rt = jnp.concatenate([kap, rt], axis=0)

    yt = jnp.concatenate([bt, kt], axis=0)
    gram = _dot_nt(jnp.concatenate([kap_rt * lo_f, kap_rt * hi_f], axis=0).astype(BF16), yt)
    grams = (gram[:LANES], gram[LANES:])
    kap_rt = kap_rt.astype(BF16)
    yield
    vv = jnp.concatenate([v16, v16], axis=0)
    a_b = [g[:CHUNK, :CHUNK] * strict for g in grams]
    akv = _dot(jnp.concatenate([g[:CHUNK] * strict_hi for g in grams], axis=0).astype(BF16), vv)
    akv = jnp.where(lo, akv[:CHUNK], akv[CHUNK:])
    mats = jnp.concatenate([g[CHUNK:] * incl_wide for g in grams], axis=0).astype(BF16)
    x_inv = yield from _unit_tri_inverse_offdiag(a_b, blockdiag16)
    x_inv = jnp.concatenate(x_inv, axis=0).astype(BF16)
    yield

    s2 = recv_state()
    while s2 is None:
        yield
        s2 = recv_state()
    ks_rs = _dot_nt(kap_rt, s2.astype(BF16))
    ks, rs = ks_rs[:CHUNK], ks_rs[CHUNK:]
    yield
    rhs = ks + akv
    xr = _dot(x_inv, rhs.astype(BF16))
    e = rhs + jnp.where(lo, xr[:CHUNK], xr[CHUNK:])
    yield
    ev = jnp.concatenate([-e, v], axis=0).astype(BF16)
    upd = _dot_tn(ev, tails)
    send_state(s2 * decay + upd * headdiag)
    my = _dot(mats, ev)
    y = rs + jnp.where(lo, my[:CHUNK], my[CHUNK:])
    return y, bonus


RWKV_PAIRS_PER_STEP = 2
_RWKV_IN_REFS = 25
_RWKV_SCRATCH = 4
_MT_STRICT, _MT_STRICT_HI, _MT_INCL_WIDE = 0, 2, 4
_MT_BLOCKDIAG, _MT_HEADDIAG, _MT_HALVES, _MT_BANDS = 6, 7, 9, 10


def _rwkv_mask_tables():
    import numpy as np
    t = np.arange(CHUNK)[:, None]
    s = np.arange(CHUNK)[None, :]
    lane = np.arange(LANES)[None, :]
    table = np.zeros((_MT_BANDS, CHUNK, LANES), np.float32)
    tri = np.zeros((2, CHUNK, CHUNK), np.float32)
    for d, (strict, incl) in enumerate(((s < t, s <= t), (s > t, s >= t))):
        table[_MT_STRICT + d, :, :CHUNK] = strict
        table[_MT_STRICT_HI + d, :, CHUNK:] = strict
        table[_MT_INCL_WIDE + d] = np.concatenate([incl, incl], axis=1)
        tri[d] = incl
    table[_MT_BLOCKDIAG, :, :CHUNK] = (t // 16) == (s // 16)
    v = np.arange(LANES)[:, None]
    table[_MT_HEADDIAG:_MT_HEADDIAG + 2] = ((v < RWKV_HD) == (lane < RWKV_HD)).reshape(2, CHUNK, LANES)
    table[_MT_HALVES, 0] = lane[0] < RWKV_HD
    table[_MT_HALVES, 1] = lane[0] >= RWKV_HD
    return table.reshape(_MT_BANDS * CHUNK, LANES), tri


def _rwkv_kernel(*refs, need_ctx):
    n_pairs = RWKV_PAIRS_PER_STEP
    mt_ref, tri_ref = refs[:2]
    refs = refs[2:]
    n_in = n_pairs * _RWKV_IN_REFS
    rest = refs[n_in:]
    if need_ctx:
        y_l, y_c = rest[:2]
        scr = rest[2:]
    else:
        y_l, y_c = rest[0], None
        scr = rest[1:]
    nl = refs[0].shape[0] // CHUNK
    nc = refs[6].shape[0] // CHUNK

    def band(j, rows=CHUNK, lanes=LANES):
        return mt_ref[j * CHUNK:j * CHUNK + rows, :lanes]

    lo = lax.broadcasted_iota(jnp.int32, (1, LANES), 1) < RWKV_HD
    lo_f, hi_f = band(_MT_HALVES, 1), mt_ref[_MT_HALVES * CHUNK + 1:_MT_HALVES * CHUNK + 2, :]
    blockdiag16 = band(_MT_BLOCKDIAG, lanes=CHUNK)
    headdiag = band(_MT_HEADDIAG, rows=LANES)

    def consts(d):
        return (lo_f, hi_f, lo, blockdiag16, headdiag, tri_ref[d], band(_MT_STRICT + d, lanes=CHUNK),
                band(_MT_STRICT_HI + d), band(_MT_INCL_WIDE + d))
    consts_f, consts_b = consts(0), consts(1)

    pairs = []
    for g in range(n_pairs):
        r = refs[g * _RWKV_IN_REFS:(g + 1) * _RWKV_IN_REFS]
        (w0f, w0b, a0f, a0b, kkw, kaw, rkw, lng, lnb, wupf, wupb, aupf, aupb) = [x[0] for x in r[12:]]
        s = scr[g * 2 * _RWKV_SCRATCH:(g + 1) * 2 * _RWKV_SCRATCH]
        pairs.append(dict(
            lat=tuple(r[0:6]), ctx=tuple(r[6:12]),
            prm_f=(w0f, a0f, kkw, kaw, rkw, wupf, aupf),
            prm_b=(w0b, a0b, kkw, kaw, rkw, wupb, aupb),
            scr_lat=s[:_RWKV_SCRATCH], scr_ctx=s[_RWKV_SCRATCH:], lng=lng, lnb=lnb))

    def scan(which, n, carry):
        unroll = _unroll_factor(n, RWKV_UNROLL)

        def body(i, carry):
            states = {}
            gens = []
            for u in range(unroll):
                c = i * unroll + u
                for g, pr in enumerate(pairs):
                    states[(g, "f", -1)], states[(g, "b", -1)] = carry[2 * g], carry[2 * g + 1]
                    for d, cc, prm, consts in (("f", c, pr["prm_f"], consts_f),
                                               ("b", n - 1 - c, pr["prm_b"], consts_b)):
                        gens.append(_rwkv_chunk(
                            pr[which], cc, prm, consts,
                            functools.partial(states.get, (g, d, u - 1)),
                            functools.partial(states.__setitem__, (g, d, u))))
            outs = iter(_run_interleaved(gens))
            for u in range(unroll):
                c = i * unroll + u
                rows_f, rows_b = _chunk_rows(c), _chunk_rows(n - 1 - c)
                for pr in pairs:
                    yf, yb, bf, bb = pr["scr_" + which]
                    (y_f, bon_f), (y_b, bon_b) = next(outs), next(outs)
                    yf[rows_f, :] = y_f
                    bf[rows_f, :] = bon_f.astype(bf.dtype)
                    yb[rows_b, :] = y_b
                    bb[rows_b, :] = bon_b.astype(bb.dtype)
            return tuple(states[(g, d, unroll - 1)] for g in range(n_pairs) for d in ("f", "b"))
        return lax.fori_loop(0, n // unroll, body, carry)

    zero = jnp.zeros((LANES, LANES), F32)
    carry = scan("ctx", nc, (zero,) * (2 * n_pairs))
    scan("lat", nl, carry)

    def finish(which, y_out, n):
        def body(i, _):
            rows = _chunk_rows(i)
            for g, pr in enumerate(pairs):
                yf, yb, bf, bb = pr["scr_" + which]
                gs = pr[which][3]
                y = yf[rows, :] + yb[rows, :]
                mean = _seg_sum(y, lo) * (1.0 / RWKV_HD)
                d = y - mean
                var = _seg_sum(d * d, lo) * (1.0 / RWKV_HD)
                yn = d * lax.rsqrt(var + GN_EPS) * pr["lng"] + pr["lnb"]
                bonus = bf[rows, :].astype(F32) + bb[rows, :].astype(F32)
                y_out[rows, g * LANES:(g + 1) * LANES] = (
                    (yn + bonus) * gs[rows, :].astype(F32)).astype(y_out.dtype)
            return 0
        lax.fori_loop(0, n, body, 0, unroll=_unroll_factor(n, FINISH_UNROLL))

    finish("lat", y_l, nl)
    if need_ctx:
        finish("ctx", y_c, nc)


def _rwkv(ps_lat, ps_ctx, batch, seq, ctx_len, prm, need_ctx):
    n_pairs = RWKV_PAIRS_PER_STEP

    def pair_specs(g):
        hp = lambda s: n_pairs * s + g

        def seg_specs(n):
            return [pl.BlockSpec((n, LANES), lambda b, s: (b, RW_R + hp(s))),
                    pl.BlockSpec((n, LANES), lambda b, s: (b, RW_K + hp(s))),
                    pl.BlockSpec((n, LANES), lambda b, s: (b, RW_V + hp(s))),
                    pl.BlockSpec((n, LANES), lambda b, s: (b, RW_GATE + hp(s))),
                    pl.BlockSpec((n, LANES), lambda b, s: (b, RW_WD)),
                    pl.BlockSpec((n, LANES), lambda b, s: (b, RW_AD))]
        vec_h = pl.BlockSpec((1, 1, LANES), lambda b, s: (hp(s), 0, 0))
        mat_h = pl.BlockSpec((1, LANES, LANES), lambda b, s: (hp(s), 0, 0))
        return seg_specs(seq) + seg_specs(ctx_len) + [vec_h] * 9 + [mat_h] * 4

    table, tri = _rwkv_mask_tables()
    in_specs = [pl.BlockSpec(table.shape, lambda b, s: (0, 0)),
                pl.BlockSpec(tri.shape, lambda b, s: (0, 0, 0))]
    args = [jnp.asarray(table), jnp.asarray(tri, BF16)]
    for g in range(n_pairs):
        in_specs += pair_specs(g)
        args += [ps_lat] * 6 + [ps_ctx] * 6 + list(prm)
    width = n_pairs * LANES
    out_specs = [pl.BlockSpec((seq, width), lambda b, s: (b, s))]
    out_shape = [jax.ShapeDtypeStruct((batch * seq, BRANCH_W), BF16)]
    if need_ctx:
        out_specs.append(pl.BlockSpec((ctx_len, width), lambda b, s: (b, s)))
        out_shape.append(jax.ShapeDtypeStruct((batch * ctx_len, BRANCH_W), BF16))
    def seg_scratch(n):
        return [pltpu.VMEM((n, LANES), F32)] * 2 + [pltpu.VMEM((n, LANES), BF16)] * 2
    scratch = (seg_scratch(seq) + seg_scratch(ctx_len)) * n_pairs
    outs = pl.pallas_call(
        functools.partial(_rwkv_kernel, need_ctx=need_ctx),
        grid=(batch, RWKV_H // 2 // n_pairs),
        in_specs=in_specs,
        out_specs=out_specs,
        out_shape=out_shape,
        scratch_shapes=scratch,
        compiler_params=_cparams(("parallel", "parallel")),
        name="rwkv",
    )(*args)
    return (outs[0], outs[1]) if need_ctx else (outs[0], None)


def _merge_kernel(x_ref, y1_ref, y2_ref, y3_ref, g1_ref, g2_ref, g3_ref, mod_ref,
                  wo1_ref, wo2_ref, wo3_ref, wout_ref, gp_ref, o_ref):
    m = (_sigmoid(g1_ref[...].astype(F32)) * _dot(y1_ref[...], wo1_ref[...])
         + _sigmoid(g2_ref[...].astype(F32)) * _dot(y2_ref[...], wo2_ref[...])
         + _sigmoid(g3_ref[...].astype(F32)) * _dot(y3_ref[...], wo3_ref[...]))
    mo = _dot(m.astype(BF16), wout_ref[...])
    ms = jnp.mean(mo * mo, axis=-1, keepdims=True)
    out = mo * lax.rsqrt(ms + NORM_EPS) * gp_ref[...]
    o_ref[...] = x_ref[...] + mod_ref[0][2:3] * out


def _merge(x2d, y1, y2, y3, p, mod, mod_index, wo1, wo2, wo3, wout, g_post, tm):
    r, d = x2d.shape
    row = lambda i: (i, 0)
    const = lambda i: (0, 0)
    return pl.pallas_call(
        _merge_kernel,
        grid=(r // tm,),
        in_specs=[pl.BlockSpec((tm, d), row),
                  pl.BlockSpec((tm, BRANCH_W), row),
                  pl.BlockSpec((tm, BRANCH_W), row),
                  pl.BlockSpec((tm, BRANCH_W), row),
                  pl.BlockSpec((tm, d), lambda i: (i, 0)),
                  pl.BlockSpec((tm, d), lambda i: (i, 1)),
                  pl.BlockSpec((tm, d), lambda i: (i, 2)),
                  pl.BlockSpec((1, 3, d), lambda i: (mod_index(i), 0, 0)),
                  pl.BlockSpec((BRANCH_W, d), const),
                  pl.BlockSpec((BRANCH_W, d), const),
                  pl.BlockSpec((BRANCH_W, d), const),
                  pl.BlockSpec((d, d), const),
                  pl.BlockSpec((1, d), const)],
        out_specs=pl.BlockSpec((tm, d), row),
        out_shape=jax.ShapeDtypeStruct((r, d), F32),
        compiler_params=_cparams(("parallel",)),
        name="merge",
    )(x2d, y1, y2, y3, p, p, p, mod, wo1, wo2, wo3, wout, g_post.reshape(1, d))


def _pack_w_in(w_in):
    d = w_in.shape[0]
    o = 0
    gq = w_in[:, o:o + 256]; o += 256
    gk = w_in[:, o:o + 256]; o += 256
    gv = w_in[:, o:o + 512]; o += 512
    gg = w_in[:, o:o + 512]; o += 512
    gwd = w_in[:, o:o + 32]; o += 32
    att = w_in[:, o:o + 1536]; o += 1536
    rw = w_in[:, o:o + 2304]; o += 2304
    mg = w_in[:, o:o + 3072]
    qk = jnp.concatenate([gq.reshape(d, GLA_H, GLA_DK), gk.reshape(d, GLA_H, GLA_DK)],
                         axis=-1).reshape(d, GLA_H * 2 * GLA_DK)
    pad_wd = jnp.zeros((d, LANES - 32), w_in.dtype)
    pad_end = jnp.zeros((d, LANES), w_in.dtype)
    return jnp.concatenate([mg, att, rw, qk, gv, gg, gwd, pad_wd, pad_end], axis=1).astype(BF16)


def _rope_tables(seq):
    quarter = ATT_HD // 4
    inv = ROPE_THETA ** (-jnp.arange(quarter, dtype=F32) / quarter)
    t = jnp.arange(seq)
    row_pos = (t // GRID_W).astype(F32)
    col_pos = (t % GRID_W).astype(F32)
    ar = row_pos[:, None] * inv[None, :]
    ac = col_pos[:, None] * inv[None, :]
    z = jnp.zeros_like(ar)
    cos = jnp.concatenate([jnp.cos(ar), jnp.cos(ar), jnp.cos(ac), jnp.cos(ac)], axis=1)
    sa = jnp.concatenate([-jnp.sin(ar), z, -jnp.sin(ac), z], axis=1)
    sb = jnp.concatenate([z, jnp.sin(ar), z, jnp.sin(ac)], axis=1)
    return cos, sa, sb


def _pad_rows(w, lo_half):
    z = jnp.zeros((LANES - w.shape[0], w.shape[1]), w.dtype)
    if lo_half:
        return jnp.concatenate([w, z], axis=0)
    half = LANES // 2
    return jnp.concatenate([z[:half], w, z[half:]], axis=0)


def _pick(n, target):
    t = min(n, target)
    while n % t:
        t //= 2
    return t


def kernel(x, c, ctx, c_ctx, w_mod, b_mod, g_pre, w_in, gla_wup_f, gla_b_f, gla_wup_b, gla_b_b, gla_norm, att_qnorm, att_knorm, rwkv_mu, rwkv_w0_f, rwkv_wup_f, rwkv_w0_b, rwkv_wup_b, rwkv_a0_f, rwkv_aup_f, rwkv_a0_b, rwkv_aup_b, rwkv_kk, rwkv_ka, rwkv_rk, rwkv_ln_g, rwkv_ln_b, w_o_gla, w_o_att, w_o_rwkv, w_out, g_post):
    batch, seq, d = x.shape
    ctx_len = ctx.shape[1]
    depth = w_in.shape[0]
    assert seq % CHUNK == 0 and ctx_len % CHUNK == 0 and seq % GRID_W == 0

    nb = -(-(batch + 1) // 8) * 8
    cond = jnp.concatenate([c, c_ctx[None, :], jnp.zeros((nb - batch - 1, d), F32)], axis=0)
    tables = _rope_tables(seq)

    tm_lat = _pick(seq, 1024)
    tm_ctx = _pick(batch * ctx_len, 1024)

    def lat_mod(tm):
        return lambda i: i // (seq // tm)

    def ctx_mod(tm):
        return lambda i: batch
    tq_lat, tk = _pick(seq, 512), _pick(seq, 512)
    tq_ctx = _pick(ctx_len, 256)
    tm_prep = _pick(seq, 512)
    tm_prep_c = _pick(ctx_len, 512)
    tm_shift = _pick(seq, 512)
    tm_shift_c = _pick(ctx_len, 512)
    tm_merge = _pick(seq, 512)
    tm_merge_c = _pick(batch * ctx_len, 512)

    x2 = x.reshape(batch * seq, d)
    xc2 = ctx.reshape(batch * ctx_len, d)

    def hp(v):
        return v.reshape(RWKV_H // 2, 1, LANES)

    for l in range(depth):
        need_ctx = l < depth - 1
        mod = _modulation(cond, w_mod[l], b_mod[l]).reshape(nb, 3, d)
        w_cat = _pack_w_in(w_in[l])
        p_lat = _inproj(x2, mod, lat_mod(tm_lat), g_pre[l], w_cat, tm_lat)
        p_ctx = _inproj(xc2, mod, ctx_mod(tm_ctx), g_pre[l], w_cat, tm_ctx)

        q_gain = (att_qnorm[l] * (ATT_HD ** -0.5 * LOG2_E)).reshape(1, LANES)
        k_gain = att_knorm[l].reshape(1, LANES)
        kv_lat = tuple(_kvprep(p_lat, k_gain, tables, batch, seq, tm_prep)) + (seq,)
        kv_ctx = tuple(_kvprep(p_ctx, k_gain, None, batch, ctx_len, tm_prep_c)) + (ctx_len,)
        y2 = _flash(p_lat, seq, q_gain, tables, kv_lat, kv_ctx, batch, tq_lat, tk)
        y2c = None
        if need_ctx:
            y2c = _flash(p_ctx, ctx_len, q_gain, None, None, kv_ctx, batch, tq_ctx, tk)

        def gla_w(w, lo_rows):
            w = w.reshape(GLA_LR, GLA_H, GLA_DK).transpose(1, 0, 2)
            z = jnp.zeros((GLA_H, LANES, GLA_DK), w.dtype)
            start = 0 if lo_rows else GLA_LR
            return lax.dynamic_update_slice(z, w, (0, start, 0)).astype(BF16)
        y1, y1c = _gla(p_lat, p_ctx, batch, seq, ctx_len,
                       gla_w(gla_wup_f[l], True), gla_w(gla_wup_b[l], False),
                       gla_b_f[l].reshape(GLA_H, 1, GLA_DK), gla_b_b[l].reshape(GLA_H, 1, GLA_DK),
                       gla_norm[l].reshape(GLA_H, 1, GLA_DV), need_ctx)

        def rw_w(w, lo_rows):
            w = w.reshape(RWKV_LR, RWKV_H // 2, LANES).transpose(1, 0, 2)
            return jnp.stack([_pad_rows(w[i], lo_rows) for i in range(RWKV_H // 2)]).astype(BF16)
        prm = (hp(rwkv_w0_f[l]), hp(rwkv_w0_b[l]), hp(rwkv_a0_f[l]), hp(rwkv_a0_b[l]),
               hp(rwkv_kk[l]), hp(rwkv_ka[l]), hp(rwkv_rk[l]), hp(rwkv_ln_g[l]), hp(rwkv_ln_b[l]),
               rw_w(rwkv_wup_f[l], True), rw_w(rwkv_wup_b[l], False),
               rw_w(rwkv_aup_f[l], True), rw_w(rwkv_aup_b[l], False))
        ps_lat = _rwkv_shift(p_lat, rwkv_mu[l], seq, tm_shift)
        ps_ctx = _rwkv_shift(p_ctx, rwkv_mu[l], ctx_len, tm_shift_c)
        y3, y3c = _rwkv(ps_lat, ps_ctx, batch, seq, ctx_len, prm, need_ctx)

        wo1, wo2, wo3 = (w_o_gla[l].astype(BF16), w_o_att[l].astype(BF16),
                         w_o_rwkv[l].astype(BF16))
        wout = w_out[l].astype(BF16)
        x2 = _merge(x2, y1, y2, y3, p_lat, mod, lat_mod(tm_merge), wo1, wo2, wo3, wout, g_post[l], tm_merge)
        if need_ctx:
            xc2 = _merge(xc2, y1c, y2c, y3c, p_ctx, mod, ctx_mod(tm_merge_c), wo1, wo2, wo3, wout,
                         g_post[l], tm_merge_c)
    return x2.reshape(batch, seq, d)
```

```python
import functools

import jax
import jax.numpy as jnp
from jax import lax
from jax.experimental import pallas as pl
from jax.experimental.pallas import tpu as pltpu

F32 = jnp.float32
BF16 = jnp.bfloat16

GRID_W = 64
BRANCH_W = 512
GLA_H, GLA_DK, GLA_DV, GLA_LR = 4, 64, 128, 16
GLA_TAU = 16.0
ATT_H, ATT_KV, ATT_HD = 4, 2, 128
ROPE_THETA = 10000.0
RWKV_H, RWKV_HD, RWKV_LR = 8, 64, 64
RWKV_DECAY_SCALE = 0.6065306597
NORM_EPS = 1e-6
GN_EPS = 64e-5
L2_EPS = 1e-12
LOG2_E = 1.4426950408889634

CHUNK = 64
LANES = 128
VMEM_LIMIT = 56 * 1024 * 1024

CB_MG = 0
CB_ATT_Q = 24
CB_ATT_V = 30
CB_ATT_GATE = 32
CB_RW = 36
RW_BLOCKS = 18
CB_GLA_QK = 54
CB_GLA_V = 58
CB_GLA_GATE = 62
CB_GLA_WD = 66
NP_BLOCKS = 68
RW_R, RW_K, RW_V, RW_GATE, RW_WD, RW_AD = 0, 4, 8, 12, 16, 17
NP = NP_BLOCKS * LANES
TN_PROJ = NP // 4


def _cparams(sem):
    return pltpu.CompilerParams(dimension_semantics=sem, vmem_limit_bytes=VMEM_LIMIT)


def _dot(a, b):
    return jnp.dot(a, b, preferred_element_type=F32)


def _dot_nt(a, b):
    return lax.dot_general(a, b, (((1,), (1,)), ((), ())), preferred_element_type=F32)


def _dot_tn(a, b):
    return lax.dot_general(a, b, (((0,), (0,)), ((), ())), preferred_element_type=F32)


def _silu(x):
    return x / (1.0 + jnp.exp(-x))


def _sigmoid(x):
    return 1.0 / (1.0 + jnp.exp(-x))


def _tri_cumsum(tri, x):
    hi = x.astype(BF16)
    lo = (x - hi.astype(F32)).astype(BF16)
    n = x.shape[1]
    if n % LANES:
        return _dot(tri, hi) + _dot(tri, lo)
    r = _dot(tri, jnp.concatenate([hi, lo], axis=1))
    return r[:, :n] + r[:, n:]


GLA_UNROLL = 8
RWKV_UNROLL = 2
FINISH_UNROLL = 4


def _unroll_factor(n_chunks, target):
    u = target
    while n_chunks % u:
        u //= 2
    return u


def _chunk_masks(reverse):
    t = lax.broadcasted_iota(jnp.int32, (CHUNK, CHUNK), 0)
    s = lax.broadcasted_iota(jnp.int32, (CHUNK, CHUNK), 1)
    incl = (s >= t) if reverse else (s <= t)
    strict = (s > t) if reverse else (s < t)
    return incl.astype(BF16), incl, strict


def _mod_kernel(c_ref, w_ref, b_ref, o_ref):
    c = c_ref[...]
    s = _silu(c)
    o_ref[...] = jnp.dot(s, w_ref[...], preferred_element_type=F32,
                         precision=lax.Precision.HIGHEST) + b_ref[...]


def _modulation(cond, w_mod, b_mod):
    nb, d = cond.shape
    n = w_mod.shape[1]
    tn = 768
    return pl.pallas_call(
        _mod_kernel,
        grid=(n // tn,),
        in_specs=[pl.BlockSpec((nb, d), lambda j: (0, 0)),
                  pl.BlockSpec((d, tn), lambda j: (0, j)),
                  pl.BlockSpec((1, tn), lambda j: (0, j))],
        out_specs=pl.BlockSpec((nb, tn), lambda j: (0, j)),
        out_shape=jax.ShapeDtypeStruct((nb, n), F32),
        compiler_params=_cparams(("parallel",)),
        name="modulation",
    )(cond, w_mod, b_mod.reshape(1, n))


def _inproj_kernel(x_ref, mod_ref, g_ref, w_ref, o_ref, h_ref):
    @pl.when(pl.program_id(1) == 0)
    def _():
        x = x_ref[...]
        ms = jnp.mean(x * x, axis=-1, keepdims=True)
        y = x * lax.rsqrt(ms + NORM_EPS) * g_ref[...]
        mod = mod_ref[0]
        h_ref[...] = (y * (1.0 + mod[1:2]) + mod[0:1]).astype(BF16)

    o_ref[...] = _dot(h_ref[...], w_ref[...]).astype(o_ref.dtype)


def _inproj(x2d, mod, mod_index, g_pre, w_cat, tm):
    r, d = x2d.shape
    return pl.pallas_call(
        _inproj_kernel,
        grid=(r // tm, NP // TN_PROJ),
        in_specs=[pl.BlockSpec((tm, d), lambda i, j: (i, 0)),
                  pl.BlockSpec((1, 3, d), lambda i, j: (mod_index(i), 0, 0)),
                  pl.BlockSpec((1, d), lambda i, j: (0, 0)),
                  pl.BlockSpec((d, TN_PROJ), lambda i, j: (0, j))],
        out_specs=pl.BlockSpec((tm, TN_PROJ), lambda i, j: (i, j)),
        out_shape=jax.ShapeDtypeStruct((r, NP), BF16),
        scratch_shapes=[pltpu.VMEM((tm, d), BF16)],
        compiler_params=_cparams(("parallel", "arbitrary")),
        name="inproj",
    )(x2d, mod, g_pre.reshape(1, d), w_cat)


def _norm_rope(x, gain, tables):
    x = x.astype(F32)
    ms = jnp.mean(x * x, axis=-1, keepdims=True)
    y = x * lax.rsqrt(ms + NORM_EPS) * gain
    if tables is not None:
        cos, sa, sb = tables
        y = y * cos + pltpu.roll(y, 96, 1) * sa + pltpu.roll(y, 32, 1) * sb
    return y.astype(BF16)


def _kvprep_kernel(*refs, rope):
    if rope:
        k_ref, v_ref, g_ref, cos_ref, sa_ref, sb_ref, ko_ref, vt_ref = refs
        tables = (cos_ref[...], sa_ref[...], sb_ref[...])
    else:
        k_ref, v_ref, g_ref, ko_ref, vt_ref = refs
        tables = None
    ko_ref[...] = _norm_rope(k_ref[...], g_ref[...], tables)
    vt_ref[:ATT_HD, :] = v_ref[...].astype(F32).T.astype(BF16)
    vt_ref[ATT_HD:, :] = jnp.ones((VT_ROWS - ATT_HD, vt_ref.shape[1]), BF16)


VT_ROWS = ATT_HD + 16


def _kvprep(p, k_gain, tables, batch, seq, tm):
    nt = seq // tm
    in_specs = [pl.BlockSpec((tm, LANES), lambda i, j: (i, CB_ATT_Q + ATT_H + j)),
                pl.BlockSpec((tm, LANES), lambda i, j: (i, CB_ATT_V + j)),
                pl.BlockSpec((1, LANES), lambda i, j: (0, 0))]
    args = [p, p, k_gain]
    if tables is not None:
        for tab in tables:
            in_specs.append(pl.BlockSpec((tm, LANES), lambda i, j: (i % nt, 0)))
            args.append(tab)
    return pl.pallas_call(
        functools.partial(_kvprep_kernel, rope=tables is not None),
        grid=(batch * nt, ATT_KV),
        in_specs=in_specs,
        out_specs=[pl.BlockSpec((tm, LANES), lambda i, j: (i, j)),
                   pl.BlockSpec((VT_ROWS, tm), lambda i, j: ((i // nt) * ATT_KV + j, i % nt))],
        out_shape=[jax.ShapeDtypeStruct((batch * seq, ATT_KV * LANES), BF16),
                   jax.ShapeDtypeStruct((batch * ATT_KV * VT_ROWS, seq), BF16)],
        compiler_params=_cparams(("parallel", "parallel")),
        name="kvprep_rope" if tables is not None else "kvprep",
    )(*args)


FLASH_KV_INTERLEAVE = 8


def _flash_kernel(*refs, n_lat_blocks, tk, has_ctx_kv, rope):
    refs = list(refs)
    q0_ref, q1_ref, g0_ref, g1_ref, qg_ref = refs[:5]
    pos = 5
    tables = None
    if rope:
        tables = tuple(r[...] for r in refs[pos:pos + 3])
        pos += 3
    if n_lat_blocks:
        kl_ref, vtl_ref = refs[pos:pos + 2]
        pos += 2
    if has_ctx_kv:
        kc_ref, vtc_ref = refs[pos:pos + 2]
        pos += 2
    o_ref = refs[pos]
    tq = q0_ref.shape[0]
    gain = qg_ref[...]
    q = jnp.concatenate([_norm_rope(q0_ref[...], gain, tables),
                         _norm_rope(q1_ref[...], gain, tables)], axis=0)

    def blocks(kvs, carry):
        scores = [_dot_nt(k, q) for k, _ in kvs]
        m, acc = carry
        for s, (_, vt) in zip(scores, kvs):
            m_new = jnp.maximum(m, jnp.max(s, axis=0, keepdims=True))
            p = jnp.exp2(s - m_new)
            acc = jnp.exp2(m - m_new) * acc + _dot(vt, p.astype(BF16))
            m = m_new
        return m, acc

    carry = (jnp.full((1, 2 * tq), -jnp.inf, F32), jnp.zeros((VT_ROWS, 2 * tq), F32))
    if n_lat_blocks:
        nb = _unroll_factor(n_lat_blocks, FLASH_KV_INTERLEAVE)

        def body(i, c):
            kvs = []
            for u in range(nb):
                start = pl.multiple_of((i * nb + u) * tk, tk)
                kvs.append((kl_ref[pl.ds(start, tk), :], vtl_ref[:, pl.ds(start, tk)]))
            return blocks(kvs, c)
        carry = lax.fori_loop(0, n_lat_blocks // nb, body, carry)
    if has_ctx_kv:
        carry = blocks([(kc_ref[...], vtc_ref[...])], carry)
    _, acc = carry
    o = (acc[:ATT_HD] / acc[ATT_HD:ATT_HD + 1]).T
    o_ref[:, :LANES] = (o[:tq] * _silu(g0_ref[...].astype(F32))).astype(o_ref.dtype)
    o_ref[:, LANES:] = (o[tq:] * _silu(g1_ref[...].astype(F32))).astype(o_ref.dtype)


def _flash(p_q, seq_q, q_gain, tables, lat_kv, ctx_kv, batch, tq, tk):
    nq = seq_q // tq
    in_specs = [
        pl.BlockSpec((tq, LANES), lambda b, j, i: (b * nq + i, CB_ATT_Q + 2 * j)),
        pl.BlockSpec((tq, LANES), lambda b, j, i: (b * nq + i, CB_ATT_Q + 2 * j + 1)),
        pl.BlockSpec((tq, LANES), lambda b, j, i: (b * nq + i, CB_ATT_GATE + 2 * j)),
        pl.BlockSpec((tq, LANES), lambda b, j, i: (b * nq + i, CB_ATT_GATE + 2 * j + 1)),
        pl.BlockSpec((1, LANES), lambda b, j, i: (0, 0)),
    ]
    args = [p_q, p_q, p_q, p_q, q_gain]
    if tables is not None:
        for tab in tables:
            in_specs.append(pl.BlockSpec((tq, LANES), lambda b, j, i: (i, 0)))
            args.append(tab)
    n_lat_blocks = 0
    for seg in (lat_kv, ctx_kv):
        if seg is None:
            continue
        k_s, vt_s, seq_s = seg
        in_specs.append(pl.BlockSpec((seq_s, LANES), lambda b, j, i: (b, j)))
        in_specs.append(pl.BlockSpec((VT_ROWS, seq_s), lambda b, j, i: (b * ATT_KV + j, 0)))
        args += [k_s, vt_s]
    if lat_kv is not None:
        n_lat_blocks = lat_kv[2] // tk
    return pl.pallas_call(
        functools.partial(_flash_kernel, n_lat_blocks=n_lat_blocks, tk=tk,
                          has_ctx_kv=ctx_kv is not None, rope=tables is not None),
        grid=(batch, ATT_KV, nq),
        in_specs=in_specs,
        out_specs=pl.BlockSpec((tq, 2 * LANES), lambda b, j, i: (b * nq + i, j)),
        out_shape=jax.ShapeDtypeStruct((batch * seq_q, BRANCH_W), BF16),
        compiler_params=_cparams(("parallel", "parallel", "arbitrary")),
        name="flash_lat" if lat_kv is not None else "flash_ctx",
    )(*args)


def _chunk_rows(c):
    return pl.ds(pl.multiple_of(c * CHUNK, CHUNK), CHUNK)


def _gla_chunk(qk_ref, v_ref, wd_ref, c, wup, bias, masks, recv_state, send_state):
    tri, incl, _ = masks
    rows = _chunk_rows(c)
    qk = qk_ref[rows, :].astype(F32)
    q = qk[:, :GLA_DK] * (GLA_DK ** -0.5)
    k = qk[:, GLA_DK:]
    v = v_ref[rows, :]
    z = _dot(wd_ref[rows, :], wup) + bias
    yield
    g = (jnp.minimum(z, 0.0) - jnp.log1p(jnp.exp(-jnp.abs(z)))) * (1.0 / GLA_TAU)
    gc = _tri_cumsum(tri, g)
    yield
    tot = jnp.sum(g, axis=0, keepdims=True)
    q_dec = (q * jnp.exp(gc)).astype(BF16)
    k_inv = (k * jnp.exp(-gc)).astype(BF16)
    k_tail = (k * jnp.exp(tot - gc)).astype(BF16)
    a = jnp.where(incl, _dot_nt(q_dec, k_inv), 0.0)
    upd = _dot_tn(v, k_tail)
    yield
    st = recv_state()
    while st is None:
        yield
        st = recv_state()
    send_state(st * jnp.exp(tot) + upd)
    return _dot(a.astype(BF16), v) + _dot_nt(q_dec, st.astype(BF16))


def _gla_kernel(*refs, need_ctx):
    (qk_l, v_l, gate_l, wd_l, qk_c, v_c, gate_c, wd_c,
     wupf_ref, wupb_ref, bf_ref, bb_ref, gn_ref) = refs[:13]
    if need_ctx:
        y_l, y_c, of_l, ob_l, of_c, ob_c = refs[13:]
    else:
        y_l, of_l, ob_l, of_c, ob_c = refs[13:]
    nl = qk_l.shape[0] // CHUNK
    nc = qk_c.shape[0] // CHUNK
    wupf, wupb = wupf_ref[0], wupb_ref[0]
    bias_f, bias_b = bf_ref[0], bb_ref[0]
    masks_f, masks_b = _chunk_masks(False), _chunk_masks(True)

    def scan(qk, v, wd, o_f, o_b, n, carry):
        unroll = _unroll_factor(n, GLA_UNROLL)

        def body(i, carry):
            states = {("f", -1): carry[0], ("b", -1): carry[1]}
            gens = []
            for u in range(unroll):
                c = i * unroll + u
                for d, cc, wup, bias, masks in (("f", c, wupf, bias_f, masks_f),
                                                ("b", n - 1 - c, wupb, bias_b, masks_b)):
                    gens.append(_gla_chunk(
                        qk, v, wd, cc, wup, bias, masks,
                        functools.partial(states.get, (d, u - 1)),
                        functools.partial(states.__setitem__, (d, u))))
            outs = _run_interleaved(gens)
            for u in range(unroll):
                c = i * unroll + u
                o_f[_chunk_rows(c), :] = outs[2 * u]
                o_b[_chunk_rows(n - 1 - c), :] = outs[2 * u + 1]
            return states[("f", unroll - 1)], states[("b", unroll - 1)]
        return lax.fori_loop(0, n // unroll, body, carry)

    zero = jnp.zeros((GLA_DV, GLA_DK), F32)
    carry = scan(qk_c, v_c, wd_c, of_c, ob_c, nc, (zero, zero))
    scan(qk_l, v_l, wd_l, of_l, ob_l, nl, carry)

    gn = gn_ref[0]

    def finish(o_f, o_b, gate, y, n):
        def body(i, _):
            rows = _chunk_rows(i)
            o = o_f[rows, :] + o_b[rows, :]
            ms = jnp.mean(o * o, axis=-1, keepdims=True)
            o = o * lax.rsqrt(ms + NORM_EPS) * gn
            y[rows, :] = (o * _silu(gate[rows, :].astype(F32))).astype(y.dtype)
            return 0
        lax.fori_loop(0, n, body, 0, unroll=_unroll_factor(n, FINISH_UNROLL))

    finish(of_l, ob_l, gate_l, y_l, nl)
    if need_ctx:
        finish(of_c, ob_c, gate_c, y_c, nc)


def _gla(p_lat, p_ctx, batch, seq, ctx_len, wupf, wupb, bf, bb, gnorm, need_ctx):
    def seg_specs(n):
        return [pl.BlockSpec((n, LANES), lambda b, h: (b, CB_GLA_QK + h)),
                pl.BlockSpec((n, LANES), lambda b, h: (b, CB_GLA_V + h)),
                pl.BlockSpec((n, LANES), lambda b, h: (b, CB_GLA_GATE + h)),
                pl.BlockSpec((n, LANES), lambda b, h: (b, CB_GLA_WD))]
    w_spec = pl.BlockSpec((1, LANES, GLA_DK), lambda b, h: (h, 0, 0))
    b_spec = pl.BlockSpec((1, 1, GLA_DK), lambda b, h: (h, 0, 0))
    g_spec = pl.BlockSpec((1, 1, GLA_DV), lambda b, h: (h, 0, 0))
    out_specs = [pl.BlockSpec((seq, LANES), lambda b, h: (b, h))]
    out_shape = [jax.ShapeDtypeStruct((batch * seq, BRANCH_W), BF16)]
    if need_ctx:
        out_specs.append(pl.BlockSpec((ctx_len, LANES), lambda b, h: (b, h)))
        out_shape.append(jax.ShapeDtypeStruct((batch * ctx_len, BRANCH_W), BF16))
    outs = pl.pallas_call(
        functools.partial(_gla_kernel, need_ctx=need_ctx),
        grid=(batch, GLA_H),
        in_specs=seg_specs(seq) + seg_specs(ctx_len) + [w_spec, w_spec, b_spec, b_spec, g_spec],
        out_specs=out_specs,
        out_shape=out_shape,
        scratch_shapes=[pltpu.VMEM((seq, LANES), F32)] * 2 + [pltpu.VMEM((ctx_len, LANES), F32)] * 2,
        compiler_params=_cparams(("parallel", "parallel")),
        name="gla",
    )(p_lat, p_lat, p_lat, p_lat, p_ctx, p_ctx, p_ctx, p_ctx, wupf, wupb, bf, bb, gnorm)
    return (outs[0], outs[1]) if need_ctx else (outs[0], None)


def _seg_sum(x, lo):
    s0 = jnp.sum(jnp.where(lo, x, 0.0), axis=-1, keepdims=True)
    s1 = jnp.sum(jnp.where(lo, 0.0, x), axis=-1, keepdims=True)
    return jnp.where(lo, s0, s1)


def _run_interleaved(gens):
    gens = list(gens)
    out = [None] * len(gens)
    live = list(range(len(gens)))
    while live:
        still = []
        for j in live:
            try:
                next(gens[j])
                still.append(j)
            except StopIteration as stop:
                out[j] = stop.value
        live = still
    return out


def _unit_tri_inverse_offdiag(mats, blockdiag):
    def mm(x, y):
        return _dot(x.astype(BF16), y.astype(BF16))

    def mm2(x1, x2, y):
        r = mm(jnp.concatenate([x1, x2], axis=0), y)
        return r[:CHUNK], r[CHUNK:]

    nd = [-(a * blockdiag) for a in mats]
    no = [-a - d for a, d in zip(mats, nd)]
    n2 = [mm(d, d) for d in nd]
    yield
    r = [mm2(s, d, s) for d, s in zip(nd, n2)]
    n4 = [ri[0] for ri in r]
    x = [d + s + ri[1] for d, s, ri in zip(nd, n2, r)]
    yield
    r = [mm2(s, xi, s) for xi, s in zip(x, n4)]
    n8 = [ri[0] for ri in r]
    x = [xi + s + ri[1] for xi, s, ri in zip(x, n4, r)]
    yield
    xd = [xi + s + mm(xi, s) for xi, s in zip(x, n8)]
    yield
    m = [o + mm(xi, o) for xi, o in zip(xd, no)]
    yield
    m2 = [mm(mi, mi) for mi in m]
    yield
    xq = [mi + s + mm(mi, s) for mi, s in zip(m, m2)]
    yield
    return [q + d + mm(q, d) for q, d in zip(xq, xd)]


SHIFT_COLS = RW_BLOCKS * LANES


def _rwkv_shift_kernel(cur_ref, prev_ref, next_ref, mu_ref, o_ref, *, tiles_per_seq):
    t = pl.program_id(0) % tiles_per_seq
    x = cur_ref[...].astype(F32)
    tm = x.shape[0]
    prev_row = jnp.where(t > 0, prev_ref[...].astype(F32)[15:16, :], 0.0)
    next_row = jnp.where(t < tiles_per_seq - 1, next_ref[...].astype(F32)[0:1, :], 0.0)
    row8 = lax.broadcasted_iota(jnp.int32, (8, x.shape[1]), 0)
    up = pltpu.roll(x, 1, 0)
    up = jnp.concatenate([jnp.where(row8 == 0, prev_row, up[:8]), up[8:]], axis=0)
    dn = pltpu.roll(x, tm - 1, 0)
    dn = jnp.concatenate([dn[:tm - 8], jnp.where(row8 == 7, next_row, dn[tm - 8:])], axis=0)
    mu = mu_ref[...]
    y = x * (1.0 - mu) + (up + dn) * (0.5 * mu)
    g0, w0 = RW_GATE * LANES, RW_WD * LANES
    o_ref[:, :g0] = y[:, :g0].astype(o_ref.dtype)
    o_ref[:, g0:w0] = _silu(y[:, g0:w0]).astype(o_ref.dtype)
    o_ref[:, w0:w0 + LANES] = jnp.tanh(y[:, w0:w0 + LANES]).astype(o_ref.dtype)
    o_ref[:, w0 + LANES:] = y[:, w0 + LANES:].astype(o_ref.dtype)


def _rwkv_shift(p, mu, seq, tm):
    r = p.shape[0]
    tiles_per_seq = seq // tm
    sub = tm // 16
    cb = CB_RW * LANES // SHIFT_COLS
    return pl.pallas_call(
        functools.partial(_rwkv_shift_kernel, tiles_per_seq=tiles_per_seq),
        grid=(r // tm,),
        in_specs=[pl.BlockSpec((tm, SHIFT_COLS), lambda i: (i, cb)),
                  pl.BlockSpec((16, SHIFT_COLS), lambda i: (jnp.maximum(i * sub - 1, 0), cb)),
                  pl.BlockSpec((16, SHIFT_COLS), lambda i: (jnp.minimum((i + 1) * sub, r // 16 - 1), cb)),
                  pl.BlockSpec((1, SHIFT_COLS), lambda i: (0, 0))],
        out_specs=pl.BlockSpec((tm, SHIFT_COLS), lambda i: (i, 0)),
        out_shape=jax.ShapeDtypeStruct((r, RW_BLOCKS * LANES), BF16),
        compiler_params=_cparams(("parallel",)),
        name="rwkv_shift",
    )(p, p, p, mu.reshape(1, RW_BLOCKS * LANES))


def _rwkv_chunk(seg, c, prm, consts, recv_state, send_state):
    (r_ref, k_ref, v_ref, _, wd_ref, ad_ref) = seg
    (w0, a0, kkw, kaw, rkw, wup, aup) = prm
    (lo_f, hi_f, lo, blockdiag16, headdiag, tri, strict, strict_hi, incl_wide) = consts
    rows = _chunk_rows(c)

    r = r_ref[rows, :].astype(F32)
    k = k_ref[rows, :].astype(F32)
    v16 = v_ref[rows, :]
    v = v16.astype(F32)
    z_w = _dot(wd_ref[rows, :], wup)
    z_a = _dot(ad_ref[rows, :], aup)
    yield
    lw = -RWKV_DECAY_SCALE * _sigmoid(w0 + z_w)
    a = _sigmoid(a0 + z_a)
    kd = k * (1.0 + (a - 1.0) * kaw)
    kk = k * kkw
    kk = kk * lax.rsqrt(_seg_sum(kk * kk, lo) + L2_EPS)
    kka = kk * a
    bonus = _seg_sum(r * kd * rkw, lo) * v

    cw = _tri_cumsum(tri, lw)
    yield
    tot = jnp.sum(lw, axis=0, keepdims=True)
    e_neg = jnp.exp(-cw)
    e_tail = jnp.exp(tot - cw)
    kap = kk * jnp.exp(cw - lw)
    rt = r * jnp.exp(cw)
    bt = (kka * e_neg).astype(BF16)
    kt = (kd * e_neg).astype(BF16)
    tails = jnp.concatenate([kka * e_tail, kd * e_tail], axis=0).astype(BF16)
    decay = jnp.exp(tot)
    kap_rt = jnp.concatenate([kap, rt], axis=0)

    yt = jnp.concatenate([bt, kt], axis=0)
    gram = _dot_nt(jnp.concatenate([kap_rt * lo_f, kap_rt * hi_f], axis=0).astype(BF16), yt)
    grams = (gram[:LANES], gram[LANES:])
    kap_rt = kap_rt.astype(BF16)
    yield
    vv = jnp.concatenate([v16, v16], axis=0)
    a_b = [g[:CHUNK, :CHUNK] * strict for g in grams]
    akv = _dot(jnp.concatenate([g[:CHUNK] * strict_hi for g in grams], axis=0).astype(BF16), vv)
    akv = jnp.where(lo, akv[:CHUNK], akv[CHUNK:])
    mats = jnp.concatenate([g[CHUNK:] * incl_wide for g in grams], axis=0).astype(BF16)
    x_inv = yield from _unit_tri_inverse_offdiag(a_b, blockdiag16)
    x_inv = jnp.concatenate(x_inv, axis=0).astype(BF16)
    yield

    s2 = recv_state()
    while s2 is None:
        yield
        s2 = recv_state()
    ks_rs = _dot_nt(kap_rt, s2.astype(BF16))
    ks, rs = ks_rs[:CHUNK], ks_rs[CHUNK:]
    yield
    rhs = ks + akv
    xr = _dot(x_inv, rhs.astype(BF16))
    e = rhs + jnp.where(lo, xr[:CHUNK], xr[CHUNK:])
    yield
    ev = jnp.concatenate([-e, v], axis=0).astype(BF16)
    upd = _dot_tn(ev, tails)
    send_state(s2 * decay + upd * headdiag)
    my = _dot(mats, ev)
    y = rs + jnp.where(lo, my[:CHUNK], my[CHUNK:])
    return y, bonus


RWKV_PAIRS_PER_STEP = 2
_RWKV_IN_REFS = 25
_RWKV_SCRATCH = 4
_MT_STRICT, _MT_STRICT_HI, _MT_INCL_WIDE = 0, 2, 4
_MT_BLOCKDIAG, _MT_HEADDIAG, _MT_HALVES, _MT_BANDS = 6, 7, 9, 10


def _rwkv_mask_tables():
    import numpy as np
    t = np.arange(CHUNK)[:, None]
    s = np.arange(CHUNK)[None, :]
    lane = np.arange(LANES)[None, :]
    table = np.zeros((_MT_BANDS, CHUNK, LANES), np.float32)
    tri = np.zeros((2, CHUNK, CHUNK), np.float32)
    for d, (strict, incl) in enumerate(((s < t, s <= t), (s > t, s >= t))):
        table[_MT_STRICT + d, :, :CHUNK] = strict
        table[_MT_STRICT_HI + d, :, CHUNK:] = strict
        table[_MT_INCL_WIDE + d] = np.concatenate([incl, incl], axis=1)
        tri[d] = incl
    table[_MT_BLOCKDIAG, :, :CHUNK] = (t // 16) == (s // 16)
    v = np.arange(LANES)[:, None]
    table[_MT_HEADDIAG:_MT_HEADDIAG + 2] = ((v < RWKV_HD) == (lane < RWKV_HD)).reshape(2, CHUNK, LANES)
    table[_MT_HALVES, 0] = lane[0] < RWKV_HD
    table[_MT_HALVES, 1] = lane[0] >= RWKV_HD
    return table.reshape(_MT_BANDS * CHUNK, LANES), tri


def _rwkv_kernel(*refs, need_ctx):
    n_pairs = RWKV_PAIRS_PER_STEP
    mt_ref, tri_ref = refs[:2]
    refs = refs[2:]
    n_in = n_pairs * _RWKV_IN_REFS
    rest = refs[n_in:]
    if need_ctx:
        y_l, y_c = rest[:2]
        scr = rest[2:]
    else:
        y_l, y_c = rest[0], None
        scr = rest[1:]
    nl = refs[0].shape[0] // CHUNK
    nc = refs[6].shape[0] // CHUNK

    def band(j, rows=CHUNK, lanes=LANES):
        return mt_ref[j * CHUNK:j * CHUNK + rows, :lanes]

    lo = lax.broadcasted_iota(jnp.int32, (1, LANES), 1) < RWKV_HD
    lo_f, hi_f = band(_MT_HALVES, 1), mt_ref[_MT_HALVES * CHUNK + 1:_MT_HALVES * CHUNK + 2, :]
    blockdiag16 = band(_MT_BLOCKDIAG, lanes=CHUNK)
    headdiag = band(_MT_HEADDIAG, rows=LANES)

    def consts(d):
        return (lo_f, hi_f, lo, blockdiag16, headdiag, tri_ref[d], band(_MT_STRICT + d, lanes=CHUNK),
                band(_MT_STRICT_HI + d), band(_MT_INCL_WIDE + d))
    consts_f, consts_b = consts(0), consts(1)

    pairs = []
    for g in range(n_pairs):
        r = refs[g * _RWKV_IN_REFS:(g + 1) * _RWKV_IN_REFS]
        (w0f, w0b, a0f, a0b, kkw, kaw, rkw, lng, lnb, wupf, wupb, aupf, aupb) = [x[0] for x in r[12:]]
        s = scr[g * 2 * _RWKV_SCRATCH:(g + 1) * 2 * _RWKV_SCRATCH]
        pairs.append(dict(
            lat=tuple(r[0:6]), ctx=tuple(r[6:12]),
            prm_f=(w0f, a0f, kkw, kaw, rkw, wupf, aupf),
            prm_b=(w0b, a0b, kkw, kaw, rkw, wupb, aupb),
            scr_lat=s[:_RWKV_SCRATCH], scr_ctx=s[_RWKV_SCRATCH:], lng=lng, lnb=lnb))

    def scan(which, n, carry):
        unroll = _unroll_factor(n, RWKV_UNROLL)

        def body(i, carry):
            states = {}
            gens = []
            for u in range(unroll):
                c = i * unroll + u
                for g, pr in enumerate(pairs):
                    states[(g, "f", -1)], states[(g, "b", -1)] = carry[2 * g], carry[2 * g + 1]
                    for d, cc, prm, consts in (("f", c, pr["prm_f"], consts_f),
                                               ("b", n - 1 - c, pr["prm_b"], consts_b)):
                        gens.append(_rwkv_chunk(
                            pr[which], cc, prm, consts,
                            functools.partial(states.get, (g, d, u - 1)),
                            functools.partial(states.__setitem__, (g, d, u))))
            outs = iter(_run_interleaved(gens))
            for u in range(unroll):
                c = i * unroll + u
                rows_f, rows_b = _chunk_rows(c), _chunk_rows(n - 1 - c)
                for pr in pairs:
                    yf, yb, bf, bb = pr["scr_" + which]
                    (y_f, bon_f), (y_b, bon_b) = next(outs), next(outs)
                    yf[rows_f, :] = y_f
                    bf[rows_f, :] = bon_f.astype(bf.dtype)
                    yb[rows_b, :] = y_b
                    bb[rows_b, :] = bon_b.astype(bb.dtype)
            return tuple(states[(g, d, unroll - 1)] for g in range(n_pairs) for d in ("f", "b"))
        return lax.fori_loop(0, n // unroll, body, carry)

    zero = jnp.zeros((LANES, LANES), F32)
    carry = scan("ctx", nc, (zero,) * (2 * n_pairs))
    scan("lat", nl, carry)

    head_mean = (headdiag * (1.0 / RWKV_HD)).astype(BF16)

    def seg_mean(x):
        hi = x.astype(BF16)
        lo_part = (x - hi.astype(F32)).astype(BF16)
        r = _dot(jnp.concatenate([hi, lo_part], axis=0), head_mean)
        return r[:CHUNK] + r[CHUNK:]

    def finish(which, y_out, n):
        def body(i, _):
            rows = _chunk_rows(i)
            for g, pr in enumerate(pairs):
                yf, yb, bf, bb = pr["scr_" + which]
                gs = pr[which][3]
                y = yf[rows, :] + yb[rows, :]
                d = y - seg_mean(y)
                var = seg_mean(d * d)
                yn = d * lax.rsqrt(var + GN_EPS) * pr["lng"] + pr["lnb"]
                bonus = bf[rows, :].astype(F32) + bb[rows, :].astype(F32)
                y_out[rows, g * LANES:(g + 1) * LANES] = (
                    (yn + bonus) * gs[rows, :].astype(F32)).astype(y_out.dtype)
            return 0
        lax.fori_loop(0, n, body, 0, unroll=_unroll_factor(n, FINISH_UNROLL))

    finish("lat", y_l, nl)
    if need_ctx:
        finish("ctx", y_c, nc)


def _rwkv(ps_lat, ps_ctx, batch, seq, ctx_len, prm, need_ctx):
    n_pairs = RWKV_PAIRS_PER_STEP

    def pair_specs(g):
        hp = lambda s: n_pairs * s + g

        def seg_specs(n):
            return [pl.BlockSpec((n, LANES), lambda b, s: (b, RW_R + hp(s))),
                    pl.BlockSpec((n, LANES), lambda b, s: (b, RW_K + hp(s))),
                    pl.BlockSpec((n, LANES), lambda b, s: (b, RW_V + hp(s))),
                    pl.BlockSpec((n, LANES), lambda b, s: (b, RW_GATE + hp(s))),
                    pl.BlockSpec((n, LANES), lambda b, s: (b, RW_WD)),
                    pl.BlockSpec((n, LANES), lambda b, s: (b, RW_AD))]
        vec_h = pl.BlockSpec((1, 1, LANES), lambda b, s: (hp(s), 0, 0))
        mat_h = pl.BlockSpec((1, LANES, LANES), lambda b, s: (hp(s), 0, 0))
        return seg_specs(seq) + seg_specs(ctx_len) + [vec_h] * 9 + [mat_h] * 4

    table, tri = _rwkv_mask_tables()
    in_specs = [pl.BlockSpec(table.shape, lambda b, s: (0, 0)),
                pl.BlockSpec(tri.shape, lambda b, s: (0, 0, 0))]
    args = [jnp.asarray(table), jnp.asarray(tri, BF16)]
    for g in range(n_pairs):
        in_specs += pair_specs(g)
        args += [ps_lat] * 6 + [ps_ctx] * 6 + list(prm)
    width = n_pairs * LANES
    out_specs = [pl.BlockSpec((seq, width), lambda b, s: (b, s))]
    out_shape = [jax.ShapeDtypeStruct((batch * seq, BRANCH_W), BF16)]
    if need_ctx:
        out_specs.append(pl.BlockSpec((ctx_len, width), lambda b, s: (b, s)))
        out_shape.append(jax.ShapeDtypeStruct((batch * ctx_len, BRANCH_W), BF16))
    def seg_scratch(n):
        return [pltpu.VMEM((n, LANES), F32)] * 2 + [pltpu.VMEM((n, LANES), BF16)] * 2
    scratch = (seg_scratch(seq) + seg_scratch(ctx_len)) * n_pairs
    outs = pl.pallas_call(
        functools.partial(_rwkv_kernel, need_ctx=need_ctx),
        grid=(batch, RWKV_H // 2 // n_pairs),
        in_specs=in_specs,
        out_specs=out_specs,
        out_shape=out_shape,
        scratch_shapes=scratch,
        compiler_params=_cparams(("parallel", "parallel")),
        name="rwkv",
    )(*args)
    return (outs[0], outs[1]) if need_ctx else (outs[0], None)


def _merge_kernel(x_ref, y1_ref, y2_ref, y3_ref, g1_ref, g2_ref, g3_ref, mod_ref,
                  wo1_ref, wo2_ref, wo3_ref, wout_ref, gp_ref, o_ref):
    m = (_sigmoid(g1_ref[...].astype(F32)) * _dot(y1_ref[...], wo1_ref[...])
         + _sigmoid(g2_ref[...].astype(F32)) * _dot(y2_ref[...], wo2_ref[...])
         + _sigmoid(g3_ref[...].astype(F32)) * _dot(y3_ref[...], wo3_ref[...]))
    mo = _dot(m.astype(BF16), wout_ref[...])
    ms = jnp.mean(mo * mo, axis=-1, keepdims=True)
    out = mo * lax.rsqrt(ms + NORM_EPS) * gp_ref[...]
    o_ref[...] = x_ref[...] + mod_ref[0][2:3] * out


def _merge(x2d, y1, y2, y3, p, mod, mod_index, wo1, wo2, wo3, wout, g_post, tm):
    r, d = x2d.shape
    row = lambda i: (i, 0)
    const = lambda i: (0, 0)
    return pl.pallas_call(
        _merge_kernel,
        grid=(r // tm,),
        in_specs=[pl.BlockSpec((tm, d), row),
                  pl.BlockSpec((tm, BRANCH_W), row),
                  pl.BlockSpec((tm, BRANCH_W), row),
                  pl.BlockSpec((tm, BRANCH_W), row),
                  pl.BlockSpec((tm, d), lambda i: (i, 0)),
                  pl.BlockSpec((tm, d), lambda i: (i, 1)),
                  pl.BlockSpec((tm, d), lambda i: (i, 2)),
                  pl.BlockSpec((1, 3, d), lambda i: (mod_index(i), 0, 0)),
                  pl.BlockSpec((BRANCH_W, d), const),
                  pl.BlockSpec((BRANCH_W, d), const),
                  pl.BlockSpec((BRANCH_W, d), const),
                  pl.BlockSpec((d, d), const),
                  pl.BlockSpec((1, d), const)],
        out_specs=pl.BlockSpec((tm, d), row),
        out_shape=jax.ShapeDtypeStruct((r, d), F32),
        compiler_params=_cparams(("parallel",)),
        name="merge",
    )(x2d, y1, y2, y3, p, p, p, mod, wo1, wo2, wo3, wout, g_post.reshape(1, d))


def _pack_w_in(w_in):
    d = w_in.shape[0]
    o = 0
    gq = w_in[:, o:o + 256]; o += 256
    gk = w_in[:, o:o + 256]; o += 256
    gv = w_in[:, o:o + 512]; o += 512
    gg = w_in[:, o:o + 512]; o += 512
    gwd = w_in[:, o:o + 32]; o += 32
    att = w_in[:, o:o + 1536]; o += 1536
    rw = w_in[:, o:o + 2304]; o += 2304
    mg = w_in[:, o:o + 3072]
    qk = jnp.concatenate([gq.reshape(d, GLA_H, GLA_DK), gk.reshape(d, GLA_H, GLA_DK)],
                         axis=-1).reshape(d, GLA_H * 2 * GLA_DK)
    pad_wd = jnp.zeros((d, LANES - 32), w_in.dtype)
    pad_end = jnp.zeros((d, LANES), w_in.dtype)
    return jnp.concatenate([mg, att, rw, qk, gv, gg, gwd, pad_wd, pad_end], axis=1).astype(BF16)


def _rope_tables(seq):
    quarter = ATT_HD // 4
    inv = ROPE_THETA ** (-jnp.arange(quarter, dtype=F32) / quarter)
    t = jnp.arange(seq)
    row_pos = (t // GRID_W).astype(F32)
    col_pos = (t % GRID_W).astype(F32)
    ar = row_pos[:, None] * inv[None, :]
    ac = col_pos[:, None] * inv[None, :]
    z = jnp.zeros_like(ar)
    cos = jnp.concatenate([jnp.cos(ar), jnp.cos(ar), jnp.cos(ac), jnp.cos(ac)], axis=1)
    sa = jnp.concatenate([-jnp.sin(ar), z, -jnp.sin(ac), z], axis=1)
    sb = jnp.concatenate([z, jnp.sin(ar), z, jnp.sin(ac)], axis=1)
    return cos, sa, sb


def _pad_rows(w, lo_half):
    z = jnp.zeros((LANES - w.shape[0], w.shape[1]), w.dtype)
    if lo_half:
        return jnp.concatenate([w, z], axis=0)
    half = LANES // 2
    return jnp.concatenate([z[:half], w, z[half:]], axis=0)


def _pick(n, target):
    t = min(n, target)
    while n % t:
        t //= 2
    return t


def kernel(x, c, ctx, c_ctx, w_mod, b_mod, g_pre, w_in, gla_wup_f, gla_b_f, gla_wup_b, gla_b_b, gla_norm, att_qnorm, att_knorm, rwkv_mu, rwkv_w0_f, rwkv_wup_f, rwkv_w0_b, rwkv_wup_b, rwkv_a0_f, rwkv_aup_f, rwkv_a0_b, rwkv_aup_b, rwkv_kk, rwkv_ka, rwkv_rk, rwkv_ln_g, rwkv_ln_b, w_o_gla, w_o_att, w_o_rwkv, w_out, g_post):
    batch, seq, d = x.shape
    ctx_len = ctx.shape[1]
    depth = w_in.shape[0]
    assert seq % CHUNK == 0 and ctx_len % CHUNK == 0 and seq % GRID_W == 0

    nb = -(-(batch + 1) // 8) * 8
    cond = jnp.concatenate([c, c_ctx[None, :], jnp.zeros((nb - batch - 1, d), F32)], axis=0)
    tables = _rope_tables(seq)

    tm_lat = _pick(seq, 1024)
    tm_ctx = _pick(batch * ctx_len, 1024)

    def lat_mod(tm):
        return lambda i: i // (seq // tm)

    def ctx_mod(tm):
        return lambda i: batch
    tq_lat, tk = _pick(seq, 512), _pick(seq, 512)
    tq_ctx = _pick(ctx_len, 256)
    tm_prep = _pick(seq, 512)
    tm_prep_c = _pick(ctx_len, 512)
    tm_shift = _pick(seq, 512)
    tm_shift_c = _pick(ctx_len, 512)
    tm_merge = _pick(seq, 1024)
    tm_merge_c = _pick(batch * ctx_len, 1024)

    x2 = x.reshape(batch * seq, d)
    xc2 = ctx.reshape(batch * ctx_len, d)

    def hp(v):
        return v.reshape(RWKV_H // 2, 1, LANES)

    for l in range(depth):
        need_ctx = l < depth - 1
        mod = _modulation(cond, w_mod[l], b_mod[l]).reshape(nb, 3, d)
        w_cat = _pack_w_in(w_in[l])
        p_lat = _inproj(x2, mod, lat_mod(tm_lat), g_pre[l], w_cat, tm_lat)
        p_ctx = _inproj(xc2, mod, ctx_mod(tm_ctx), g_pre[l], w_cat, tm_ctx)

        q_gain = (att_qnorm[l] * (ATT_HD ** -0.5 * LOG2_E)).reshape(1, LANES)
        k_gain = att_knorm[l].reshape(1, LANES)
        kv_lat = tuple(_kvprep(p_lat, k_gain, tables, batch, seq, tm_prep)) + (seq,)
        kv_ctx = tuple(_kvprep(p_ctx, k_gain, None, batch, ctx_len, tm_prep_c)) + (ctx_len,)
        y2 = _flash(p_lat, seq, q_gain, tables, kv_lat, kv_ctx, batch, tq_lat, tk)
        y2c = None
        if need_ctx:
            y2c = _flash(p_ctx, ctx_len, q_gain, None, None, kv_ctx, batch, tq_ctx, tk)

        def gla_w(w, lo_rows):
            w = w.reshape(GLA_LR, GLA_H, GLA_DK).transpose(1, 0, 2)
            z = jnp.zeros((GLA_H, LANES, GLA_DK), w.dtype)
            start = 0 if lo_rows else GLA_LR
            return lax.dynamic_update_slice(z, w, (0, start, 0)).astype(BF16)
        y1, y1c = _gla(p_lat, p_ctx, batch, seq, ctx_len,
                       gla_w(gla_wup_f[l], True), gla_w(gla_wup_b[l], False),
                       gla_b_f[l].reshape(GLA_H, 1, GLA_DK), gla_b_b[l].reshape(GLA_H, 1, GLA_DK),
                       gla_norm[l].reshape(GLA_H, 1, GLA_DV), need_ctx)

        def rw_w(w, lo_rows):
            w = w.reshape(RWKV_LR, RWKV_H // 2, LANES).transpose(1, 0, 2)
            return jnp.stack([_pad_rows(w[i], lo_rows) for i in range(RWKV_H // 2)]).astype(BF16)
        prm = (hp(rwkv_w0_f[l]), hp(rwkv_w0_b[l]), hp(rwkv_a0_f[l]), hp(rwkv_a0_b[l]),
               hp(rwkv_kk[l]), hp(rwkv_ka[l]), hp(rwkv_rk[l]), hp(rwkv_ln_g[l]), hp(rwkv_ln_b[l]),
               rw_w(rwkv_wup_f[l], True), rw_w(rwkv_wup_b[l], False),
               rw_w(rwkv_aup_f[l], True), rw_w(rwkv_aup_b[l], False))
        ps_lat = _rwkv_shift(p_lat, rwkv_mu[l], seq, tm_shift)
        ps_ctx = _rwkv_shift(p_ctx, rwkv_mu[l], ctx_len, tm_shift_c)
        y3, y3c = _rwkv(ps_lat, ps_ctx, batch, seq, ctx_len, prm, need_ctx)

        wo1, wo2, wo3 = (w_o_gla[l].astype(BF16), w_o_att[l].astype(BF16),
                         w_o_rwkv[l].astype(BF16))
        wout = w_out[l].astype(BF16)
        x2 = _merge(x2, y1, y2, y3, p_lat, mod, lat_mod(tm_merge), wo1, wo2, wo3, wout, g_post[l], tm_merge)
        if need_ctx:
            xc2 = _merge(xc2, y1c, y2c, y3c, p_ctx, mod, ctx_mod(tm_merge_c), wo1, wo2, wo3, wout,
                         g_post[l], tm_merge_c)
    return x2.reshape(batch, seq, d)
```

```python
import functools

import jax
import jax.numpy as jnp
from jax import lax
from jax.experimental import pallas as pl
from jax.experimental.pallas import tpu as pltpu

F32 = jnp.float32
BF16 = jnp.bfloat16

GRID_W = 64
BRANCH_W = 512
GLA_H, GLA_DK, GLA_DV, GLA_LR = 4, 64, 128, 16
GLA_TAU = 16.0
ATT_H, ATT_KV, ATT_HD = 4, 2, 128
ROPE_THETA = 10000.0
RWKV_H, RWKV_HD, RWKV_LR = 8, 64, 64
RWKV_DECAY_SCALE = 0.6065306597
NORM_EPS = 1e-6
GN_EPS = 64e-5
L2_EPS = 1e-12
LOG2_E = 1.4426950408889634

CHUNK = 64
LANES = 128
VMEM_LIMIT = 56 * 1024 * 1024

CB_MG = 0
CB_ATT_Q = 24
CB_ATT_V = 30
CB_ATT_GATE = 32
CB_RW = 36
RW_BLOCKS = 18
CB_GLA_Q = 54
CB_GLA_K = 56
CB_GLA_V = 58
CB_GLA_GATE = 62
CB_GLA_WD = 66
NP_BLOCKS = 68
RW_R, RW_K, RW_V, RW_GATE, RW_WD, RW_AD = 0, 4, 8, 12, 16, 17
NP = NP_BLOCKS * LANES
TN_PROJ = NP // 4


def _cparams(sem):
    return pltpu.CompilerParams(dimension_semantics=sem, vmem_limit_bytes=VMEM_LIMIT)


def _dot(a, b):
    return jnp.dot(a, b, preferred_element_type=F32)


def _dot_nt(a, b):
    return lax.dot_general(a, b, (((1,), (1,)), ((), ())), preferred_element_type=F32)


def _dot_tn(a, b):
    return lax.dot_general(a, b, (((0,), (0,)), ((), ())), preferred_element_type=F32)


def _silu(x):
    return x / (1.0 + jnp.exp(-x))


def _sigmoid(x):
    return 1.0 / (1.0 + jnp.exp(-x))


def _tri_cumsum(tri, x):
    hi = x.astype(BF16)
    lo = (x - hi.astype(F32)).astype(BF16)
    n = x.shape[1]
    if n % LANES:
        return _dot(tri, hi) + _dot(tri, lo)
    r = _dot(tri, jnp.concatenate([hi, lo], axis=1))
    return r[:, :n] + r[:, n:]


GLA_UNROLL = 16
RWKV_UNROLL = 2
FINISH_UNROLL = 4


def _unroll_factor(n_chunks, target):
    u = target
    while n_chunks % u:
        u //= 2
    return u


def _mod_kernel(c_ref, w_ref, b_ref, o_ref):
    c = c_ref[...]
    s = _silu(c)
    o_ref[...] = jnp.dot(s, w_ref[...], preferred_element_type=F32,
                         precision=lax.Precision.HIGHEST) + b_ref[...]


def _modulation(cond, w_mod, b_mod):
    nb, d = cond.shape
    n = w_mod.shape[1]
    tn = 768
    return pl.pallas_call(
        _mod_kernel,
        grid=(n // tn,),
        in_specs=[pl.BlockSpec((nb, d), lambda j: (0, 0)),
                  pl.BlockSpec((d, tn), lambda j: (0, j)),
                  pl.BlockSpec((1, tn), lambda j: (0, j))],
        out_specs=pl.BlockSpec((nb, tn), lambda j: (0, j)),
        out_shape=jax.ShapeDtypeStruct((nb, n), F32),
        compiler_params=_cparams(("parallel",)),
        name="modulation",
    )(cond, w_mod, b_mod.reshape(1, n))


def _inproj_kernel(x_ref, mod_ref, g_ref, w_ref, o_ref, h_ref):
    @pl.when(pl.program_id(1) == 0)
    def _():
        x = x_ref[...]
        ms = jnp.mean(x * x, axis=-1, keepdims=True)
        y = x * lax.rsqrt(ms + NORM_EPS) * g_ref[...]
        mod = mod_ref[0]
        h_ref[...] = (y * (1.0 + mod[1:2]) + mod[0:1]).astype(BF16)

    o_ref[...] = _dot(h_ref[...], w_ref[...]).astype(o_ref.dtype)


def _inproj(x2d, mod, mod_index, g_pre, w_cat, tm):
    r, d = x2d.shape
    return pl.pallas_call(
        _inproj_kernel,
        grid=(r // tm, NP // TN_PROJ),
        in_specs=[pl.BlockSpec((tm, d), lambda i, j: (i, 0)),
                  pl.BlockSpec((1, 3, d), lambda i, j: (mod_index(i), 0, 0)),
                  pl.BlockSpec((1, d), lambda i, j: (0, 0)),
                  pl.BlockSpec((d, TN_PROJ), lambda i, j: (0, j))],
        out_specs=pl.BlockSpec((tm, TN_PROJ), lambda i, j: (i, j)),
        out_shape=jax.ShapeDtypeStruct((r, NP), BF16),
        scratch_shapes=[pltpu.VMEM((tm, d), BF16)],
        compiler_params=_cparams(("parallel", "arbitrary")),
        name="inproj",
    )(x2d, mod, g_pre.reshape(1, d), w_cat)


def _norm_rope(x, gain, tables):
    x = x.astype(F32)
    ms = jnp.mean(x * x, axis=-1, keepdims=True)
    y = x * lax.rsqrt(ms + NORM_EPS) * gain
    if tables is not None:
        cos, sa, sb = tables
        y = y * cos + pltpu.roll(y, 96, 1) * sa + pltpu.roll(y, 32, 1) * sb
    return y.astype(BF16)


def _kvprep_kernel(*refs, rope):
    if rope:
        k_ref, v_ref, g_ref, cos_ref, sa_ref, sb_ref, ko_ref, vt_ref = refs
        tables = (cos_ref[...], sa_ref[...], sb_ref[...])
    else:
        k_ref, v_ref, g_ref, ko_ref, vt_ref = refs
        tables = None
    for j in range(ATT_KV):
        cols = slice(j * LANES, (j + 1) * LANES)
        ko_ref[:, cols] = _norm_rope(k_ref[:, cols], g_ref[...], tables)
        base = j * VT_ROWS
        vt_ref[base:base + ATT_HD, :] = v_ref[:, cols].astype(F32).T.astype(BF16)
        vt_ref[base + ATT_HD:base + VT_ROWS, :] = jnp.ones((VT_ROWS - ATT_HD, vt_ref.shape[1]), BF16)


VT_ROWS = ATT_HD + 16


def _kvprep(p, k_gain, tables, batch, seq, tm):
    nt = seq // tm
    width = ATT_KV * LANES
    kb = (CB_ATT_Q + ATT_H) * LANES // width
    vb = CB_ATT_V * LANES // width
    in_specs = [pl.BlockSpec((tm, width), lambda i: (i, kb)),
                pl.BlockSpec((tm, width), lambda i: (i, vb)),
                pl.BlockSpec((1, LANES), lambda i: (0, 0))]
    args = [p, p, k_gain]
    if tables is not None:
        for tab in tables:
            in_specs.append(pl.BlockSpec((tm, LANES), lambda i: (i % nt, 0)))
            args.append(tab)
    return pl.pallas_call(
        functools.partial(_kvprep_kernel, rope=tables is not None),
        grid=(batch * nt,),
        in_specs=in_specs,
        out_specs=[pl.BlockSpec((tm, width), lambda i: (i, 0)),
                   pl.BlockSpec((ATT_KV * VT_ROWS, tm), lambda i: (i // nt, i % nt))],
        out_shape=[jax.ShapeDtypeStruct((batch * seq, width), BF16),
                   jax.ShapeDtypeStruct((batch * ATT_KV * VT_ROWS, seq), BF16)],
        compiler_params=_cparams(("parallel",)),
        name="kvprep_rope" if tables is not None else "kvprep",
    )(*args)


FLASH_KV_INTERLEAVE = 8


def _flash_kernel(*refs, n_lat_blocks, tk, has_ctx_kv, rope):
    refs = list(refs)
    q0_ref, q1_ref, g0_ref, g1_ref, qg_ref = refs[:5]
    pos = 5
    tables = None
    if rope:
        tables = tuple(r[...] for r in refs[pos:pos + 3])
        pos += 3
    if n_lat_blocks:
        kl_ref, vtl_ref = refs[pos:pos + 2]
        pos += 2
    if has_ctx_kv:
        kc_ref, vtc_ref = refs[pos:pos + 2]
        pos += 2
    o_ref = refs[pos]
    tq = q0_ref.shape[0]
    gain = qg_ref[...]
    q = jnp.concatenate([_norm_rope(q0_ref[...], gain, tables),
                         _norm_rope(q1_ref[...], gain, tables)], axis=0)

    def blocks(kvs, carry):
        scores = [_dot_nt(k, q) for k, _ in kvs]
        m, acc = carry
        for s, (_, vt) in zip(scores, kvs):
            m_new = jnp.maximum(m, jnp.max(s, axis=0, keepdims=True))
            p = jnp.exp2(s - m_new)
            acc = jnp.exp2(m - m_new) * acc + _dot(vt, p.astype(BF16))
            m = m_new
        return m, acc

    carry = (jnp.full((1, 2 * tq), -jnp.inf, F32), jnp.zeros((VT_ROWS, 2 * tq), F32))
    if n_lat_blocks:
        nb = _unroll_factor(n_lat_blocks, FLASH_KV_INTERLEAVE)

        def body(i, c):
            kvs = []
            for u in range(nb):
                start = pl.multiple_of((i * nb + u) * tk, tk)
                kvs.append((kl_ref[pl.ds(start, tk), :], vtl_ref[:, pl.ds(start, tk)]))
            return blocks(kvs, c)
        carry = lax.fori_loop(0, n_lat_blocks // nb, body, carry)
    if has_ctx_kv:
        carry = blocks([(kc_ref[...], vtc_ref[...])], carry)
    _, acc = carry
    o = (acc[:ATT_HD] / acc[ATT_HD:ATT_HD + 1]).T
    o_ref[:, :LANES] = (o[:tq] * _silu(g0_ref[...].astype(F32))).astype(o_ref.dtype)
    o_ref[:, LANES:] = (o[tq:] * _silu(g1_ref[...].astype(F32))).astype(o_ref.dtype)


def _flash(p_q, seq_q, q_gain, tables, lat_kv, ctx_kv, batch, tq, tk):
    nq = seq_q // tq
    in_specs = [
        pl.BlockSpec((tq, LANES), lambda b, j, i: (b * nq + i, CB_ATT_Q + 2 * j)),
        pl.BlockSpec((tq, LANES), lambda b, j, i: (b * nq + i, CB_ATT_Q + 2 * j + 1)),
        pl.BlockSpec((tq, LANES), lambda b, j, i: (b * nq + i, CB_ATT_GATE + 2 * j)),
        pl.BlockSpec((tq, LANES), lambda b, j, i: (b * nq + i, CB_ATT_GATE + 2 * j + 1)),
        pl.BlockSpec((1, LANES), lambda b, j, i: (0, 0)),
    ]
    args = [p_q, p_q, p_q, p_q, q_gain]
    if tables is not None:
        for tab in tables:
            in_specs.append(pl.BlockSpec((tq, LANES), lambda b, j, i: (i, 0)))
            args.append(tab)
    n_lat_blocks = 0
    for seg in (lat_kv, ctx_kv):
        if seg is None:
            continue
        k_s, vt_s, seq_s = seg
        in_specs.append(pl.BlockSpec((seq_s, LANES), lambda b, j, i: (b, j)))
        in_specs.append(pl.BlockSpec((VT_ROWS, seq_s), lambda b, j, i: (b * ATT_KV + j, 0)))
        args += [k_s, vt_s]
    if lat_kv is not None:
        n_lat_blocks = lat_kv[2] // tk
    return pl.pallas_call(
        functools.partial(_flash_kernel, n_lat_blocks=n_lat_blocks, tk=tk,
                          has_ctx_kv=ctx_kv is not None, rope=tables is not None),
        grid=(batch, ATT_KV, nq),
        in_specs=in_specs,
        out_specs=pl.BlockSpec((tq, 2 * LANES), lambda b, j, i: (b * nq + i, j)),
        out_shape=jax.ShapeDtypeStruct((batch * seq_q, BRANCH_W), BF16),
        compiler_params=_cparams(("parallel", "parallel", "arbitrary")),
        name="flash_lat" if lat_kv is not None else "flash_ctx",
    )(*args)


def _chunk_rows(c):
    return pl.ds(pl.multiple_of(c * CHUNK, CHUNK), CHUNK)


def _gla_chunk(seg, c, wup, bias, consts, recv_state, send_state):
    q_ref, k_ref, v_refs, wd_ref = seg
    tri, incl2, halves = consts
    rows = _chunk_rows(c)
    q = q_ref[rows, :].astype(F32) * (GLA_DK ** -0.5)
    k = k_ref[rows, :].astype(F32)
    vs = [v_ref[rows, :] for v_ref in v_refs]
    z = _dot(wd_ref[rows, :], wup) + bias
    yield
    g = (jnp.minimum(z, 0.0) - jnp.log1p(jnp.exp(-jnp.abs(z)))) * (1.0 / GLA_TAU)
    gc = _tri_cumsum(tri, g)
    yield
    tot = jnp.sum(g, axis=0, keepdims=True)
    q_dec = q * jnp.exp(gc)
    k_inv = (k * jnp.exp(-gc)).astype(BF16)
    k_tail = k * jnp.exp(tot - gc)
    q_heads = [(q_dec * hm).astype(BF16) for hm in halves]
    a = (_dot_nt(jnp.concatenate(q_heads, axis=0), k_inv) * incl2).astype(BF16)
    upds = [_dot_tn(v, (k_tail * hm).astype(BF16)) for v, hm in zip(vs, halves)]
    yield
    sts = recv_state()
    while sts is None:
        yield
        sts = recv_state()
    decay = jnp.exp(tot)
    send_state(tuple(st * decay + upd for st, upd in zip(sts, upds)))
    return tuple(_dot(a[h * CHUNK:(h + 1) * CHUNK], vs[h]) + _dot_nt(q_heads[h], sts[h].astype(BF16))
                 for h in range(2))


_GLA_SEG_REFS = 7


def _gla_kernel(*refs, need_ctx):
    mt_ref, tri_ref = refs[:2]
    lat = refs[2:2 + _GLA_SEG_REFS]
    ctx = refs[2 + _GLA_SEG_REFS:2 + 2 * _GLA_SEG_REFS]
    wupf_ref, wupb_ref, bf_ref, bb_ref, gn_ref = refs[2 + 2 * _GLA_SEG_REFS:7 + 2 * _GLA_SEG_REFS]
    rest = refs[7 + 2 * _GLA_SEG_REFS:]
    if need_ctx:
        y_l, y_c = rest[:2]
        scr = rest[2:]
    else:
        y_l, y_c = rest[0], None
        scr = rest[1:]
    scr_l, scr_c = scr[:4], scr[4:]
    nl = lat[0].shape[0] // CHUNK
    nc = ctx[0].shape[0] // CHUNK
    wupf, wupb = wupf_ref[0], wupb_ref[0]
    bias_f, bias_b = bf_ref[0], bb_ref[0]
    row0 = _MT_HALVES * CHUNK
    halves = (mt_ref[row0:row0 + 1, :], mt_ref[row0 + 1:row0 + 2, :])

    def consts(d):
        incl = mt_ref[(_MT_INCL_WIDE + d) * CHUNK:(_MT_INCL_WIDE + d + 1) * CHUNK, :CHUNK]
        return tri_ref[d], jnp.concatenate([incl, incl], axis=0), halves
    consts_f, consts_b = consts(0), consts(1)

    def scan(seg, scr, n, carry):
        unroll = _unroll_factor(n, GLA_UNROLL)
        seg = (seg[0], seg[1], seg[2:4], seg[6])

        def body(i, carry):
            states = {("f", -1): carry[0], ("b", -1): carry[1]}
            gens = []
            for u in range(unroll):
                c = i * unroll + u
                for d, cc, wup, bias, cst in (("f", c, wupf, bias_f, consts_f),
                                              ("b", n - 1 - c, wupb, bias_b, consts_b)):
                    gens.append(_gla_chunk(
                        seg, cc, wup, bias, cst,
                        functools.partial(states.get, (d, u - 1)),
                        functools.partial(states.__setitem__, (d, u))))
            outs = _run_interleaved(gens)
            for u in range(unroll):
                c = i * unroll + u
                for h in range(2):
                    scr[h][_chunk_rows(c), :] = outs[2 * u][h]
                    scr[2 + h][_chunk_rows(n - 1 - c), :] = outs[2 * u + 1][h]
            return states[("f", unroll - 1)], states[("b", unroll - 1)]
        return lax.fori_loop(0, n // unroll, body, carry)

    zero = (jnp.zeros((GLA_DV, LANES), F32),) * 2
    carry = scan(ctx, scr_c, nc, (zero, zero))
    scan(lat, scr_l, nl, carry)

    def finish(seg, scr, y, n):
        def body(i, _):
            rows = _chunk_rows(i)
            for h in range(2):
                o = scr[h][rows, :] + scr[2 + h][rows, :]
                ms = jnp.mean(o * o, axis=-1, keepdims=True)
                o = o * lax.rsqrt(ms + NORM_EPS) * gn_ref[h]
                gate = seg[4 + h][rows, :].astype(F32)
                y[rows, h * LANES:(h + 1) * LANES] = (o * _silu(gate)).astype(y.dtype)
            return 0
        lax.fori_loop(0, n, body, 0, unroll=_unroll_factor(n, FINISH_UNROLL))

    finish(lat, scr_l, y_l, nl)
    if need_ctx:
        finish(ctx, scr_c, y_c, nc)


def _gla(p_lat, p_ctx, batch, seq, ctx_len, wupf, wupb, bf, bb, gnorm, need_ctx):
    def seg_specs(n):
        return [pl.BlockSpec((n, LANES), lambda b, g: (b, CB_GLA_Q + g)),
                pl.BlockSpec((n, LANES), lambda b, g: (b, CB_GLA_K + g)),
                pl.BlockSpec((n, LANES), lambda b, g: (b, CB_GLA_V + 2 * g)),
                pl.BlockSpec((n, LANES), lambda b, g: (b, CB_GLA_V + 2 * g + 1)),
                pl.BlockSpec((n, LANES), lambda b, g: (b, CB_GLA_GATE + 2 * g)),
                pl.BlockSpec((n, LANES), lambda b, g: (b, CB_GLA_GATE + 2 * g + 1)),
                pl.BlockSpec((n, LANES), lambda b, g: (b, CB_GLA_WD))]
    table, tri = _rwkv_mask_tables()
    w_spec = pl.BlockSpec((1, LANES, LANES), lambda b, g: (g, 0, 0))
    b_spec = pl.BlockSpec((1, 1, LANES), lambda b, g: (g, 0, 0))
    g_spec = pl.BlockSpec((2, 1, GLA_DV), lambda b, g: (g, 0, 0))
    out_specs = [pl.BlockSpec((seq, 2 * LANES), lambda b, g: (b, g))]
    out_shape = [jax.ShapeDtypeStruct((batch * seq, BRANCH_W), BF16)]
    if need_ctx:
        out_specs.append(pl.BlockSpec((ctx_len, 2 * LANES), lambda b, g: (b, g)))
        out_shape.append(jax.ShapeDtypeStruct((batch * ctx_len, BRANCH_W), BF16))
    outs = pl.pallas_call(
        functools.partial(_gla_kernel, need_ctx=need_ctx),
        grid=(batch, GLA_H // 2),
        in_specs=([pl.BlockSpec(table.shape, lambda b, g: (0, 0)),
                   pl.BlockSpec(tri.shape, lambda b, g: (0, 0, 0))]
                  + seg_specs(seq) + seg_specs(ctx_len) + [w_spec, w_spec, b_spec, b_spec, g_spec]),
        out_specs=out_specs,
        out_shape=out_shape,
        scratch_shapes=[pltpu.VMEM((seq, LANES), F32)] * 4 + [pltpu.VMEM((ctx_len, LANES), F32)] * 4,
        compiler_params=_cparams(("parallel", "parallel")),
        name="gla",
    )(jnp.asarray(table), jnp.asarray(tri, BF16), *([p_lat] * _GLA_SEG_REFS + [p_ctx] * _GLA_SEG_REFS),
      wupf, wupb, bf, bb, gnorm)
    return (outs[0], outs[1]) if need_ctx else (outs[0], None)


def _seg_sum(x, lo):
    s0 = jnp.sum(jnp.where(lo, x, 0.0), axis=-1, keepdims=True)
    s1 = jnp.sum(jnp.where(lo, 0.0, x), axis=-1, keepdims=True)
    return jnp.where(lo, s0, s1)


def _run_interleaved(gens):
    gens = list(gens)
    out = [None] * len(gens)
    live = list(range(len(gens)))
    while live:
        still = []
        for j in live:
            try:
                next(gens[j])
                still.append(j)
            except StopIteration as stop:
                out[j] = stop.value
        live = still
    return out


def _unit_tri_inverse_offdiag(mats, blockdiag):
    def mm(x, y):
        return _dot(x.astype(BF16), y.astype(BF16))

    def mm2(x1, x2, y):
        r = mm(jnp.concatenate([x1, x2], axis=0), y)
        return r[:CHUNK], r[CHUNK:]

    nd = [-(a * blockdiag) for a in mats]
    no = [-a - d for a, d in zip(mats, nd)]
    n2 = [mm(d, d) for d in nd]
    yield
    r = [mm2(s, d, s) for d, s in zip(nd, n2)]
    n4 = [ri[0] for ri in r]
    x = [d + s + ri[1] for d, s, ri in zip(nd, n2, r)]
    yield
    r = [mm2(s, xi, s) for xi, s in zip(x, n4)]
    n8 = [ri[0] for ri in r]
    x = [xi + s + ri[1] for xi, s, ri in zip(x, n4, r)]
    yield
    xd = [xi + s + mm(xi, s) for xi, s in zip(x, n8)]
    yield
    m = [o + mm(xi, o) for xi, o in zip(xd, no)]
    yield
    m2 = [mm(mi, mi) for mi in m]
    yield
    xq = [mi + s + mm(mi, s) for mi, s in zip(m, m2)]
    yield
    return [q + d + mm(q, d) for q, d in zip(xq, xd)]


SHIFT_COLS = RW_BLOCKS * LANES


def _rwkv_shift_kernel(cur_ref, prev_ref, next_ref, mu_ref, o_ref, *, tiles_per_seq):
    t = pl.program_id(0) % tiles_per_seq
    x = cur_ref[...].astype(F32)
    tm = x.shape[0]
    prev_row = jnp.where(t > 0, prev_ref[...].astype(F32)[15:16, :], 0.0)
    next_row = jnp.where(t < tiles_per_seq - 1, next_ref[...].astype(F32)[0:1, :], 0.0)
    row8 = lax.broadcasted_iota(jnp.int32, (8, x.shape[1]), 0)
    up = pltpu.roll(x, 1, 0)
    up = jnp.concatenate([jnp.where(row8 == 0, prev_row, up[:8]), up[8:]], axis=0)
    dn = pltpu.roll(x, tm - 1, 0)
    dn = jnp.concatenate([dn[:tm - 8], jnp.where(row8 == 7, next_row, dn[tm - 8:])], axis=0)
    mu = mu_ref[...]
    y = x * (1.0 - mu) + (up + dn) * (0.5 * mu)
    g0, w0 = RW_GATE * LANES, RW_WD * LANES
    o_ref[:, :g0] = y[:, :g0].astype(o_ref.dtype)
    o_ref[:, g0:w0] = _silu(y[:, g0:w0]).astype(o_ref.dtype)
    o_ref[:, w0:w0 + LANES] = jnp.tanh(y[:, w0:w0 + LANES]).astype(o_ref.dtype)
    o_ref[:, w0 + LANES:] = y[:, w0 + LANES:].astype(o_ref.dtype)


def _rwkv_shift(p, mu, seq, tm):
    r = p.shape[0]
    tiles_per_seq = seq // tm
    sub = tm // 16
    cb = CB_RW * LANES // SHIFT_COLS
    return pl.pallas_call(
        functools.partial(_rwkv_shift_kernel, tiles_per_seq=tiles_per_seq),
        grid=(r // tm,),
        in_specs=[pl.BlockSpec((tm, SHIFT_COLS), lambda i: (i, cb)),
                  pl.BlockSpec((16, SHIFT_COLS), lambda i: (jnp.maximum(i * sub - 1, 0), cb)),
                  pl.BlockSpec((16, SHIFT_COLS), lambda i: (jnp.minimum((i + 1) * sub, r // 16 - 1), cb)),
                  pl.BlockSpec((1, SHIFT_COLS), lambda i: (0, 0))],
        out_specs=pl.BlockSpec((tm, SHIFT_COLS), lambda i: (i, 0)),
        out_shape=jax.ShapeDtypeStruct((r, RW_BLOCKS * LANES), BF16),
        compiler_params=_cparams(("parallel",)),
        name="rwkv_shift",
    )(p, p, p, mu.reshape(1, RW_BLOCKS * LANES))


def _rwkv_chunk(seg, c, prm, consts, recv_state, send_state):
    (r_ref, k_ref, v_ref, _, wd_ref, ad_ref) = seg
    (w0, a0, kkw, kaw, rkw, wup, aup) = prm
    (lo_f, hi_f, lo, blockdiag16, headdiag, tri, strict, strict_hi, incl_wide) = consts
    rows = _chunk_rows(c)

    r = r_ref[rows, :].astype(F32)
    k = k_ref[rows, :].astype(F32)
    v16 = v_ref[rows, :]
    v = v16.astype(F32)
    z_w = _dot(wd_ref[rows, :], wup)
    z_a = _dot(ad_ref[rows, :], aup)
    yield
    lw = -RWKV_DECAY_SCALE * _sigmoid(w0 + z_w)
    a = _sigmoid(a0 + z_a)
    kd = k * (1.0 + (a - 1.0) * kaw)
    kk = k * kkw
    kk = kk * lax.rsqrt(_seg_sum(kk * kk, lo) + L2_EPS)
    kka = kk * a
    bonus = _seg_sum(r * kd * rkw, lo) * v

    cw = _tri_cumsum(tri, lw)
    yield
    tot = jnp.sum(lw, axis=0, keepdims=True)
    e_neg = jnp.exp(-cw)
    e_tail = jnp.exp(tot - cw)
    kap = kk * jnp.exp(cw - lw)
    rt = r * jnp.exp(cw)
    bt = (kka * e_neg).astype(BF16)
    kt = (kd * e_neg).astype(BF16)
    tails = jnp.concatenate([kka * e_tail, kd * e_tail], axis=0).astype(BF16)
    decay = jnp.exp(tot)
    kap_rt = jnp.concatenate([kap, rt], axis=0)

    yt = jnp.concatenate([bt, kt], axis=0)
    gram = _dot_nt(jnp.concatenate([kap_rt * lo_f, kap_rt * hi_f], axis=0).astype(BF16), yt)
    grams = (gram[:LANES], gram[LANES:])
    kap_rt = kap_rt.astype(BF16)
    yield
    vv = jnp.concatenate([v16, v16], axis=0)
    a_b = [g[:CHUNK, :CHUNK] * strict for g in grams]
    akv = _dot(jnp.concatenate([g[:CHUNK] * strict_hi for g in grams], axis=0).astype(BF16), vv)
    akv = jnp.where(lo, akv[:CHUNK], akv[CHUNK:])
    mats = jnp.concatenate([g[CHUNK:] * incl_wide for g in grams], axis=0).astype(BF16)
    x_inv = yield from _unit_tri_inverse_offdiag(a_b, blockdiag16)
    x_inv = jnp.concatenate(x_inv, axis=0).astype(BF16)
    yield

    s2 = recv_state()
    while s2 is None:
        yield
        s2 = recv_state()
    ks_rs = _dot_nt(kap_rt, s2.astype(BF16))
    ks, rs = ks_rs[:CHUNK], ks_rs[CHUNK:]
    yield
    rhs = ks + akv
    xr = _dot(x_inv, rhs.astype(BF16))
    e = rhs + jnp.where(lo, xr[:CHUNK], xr[CHUNK:])
    yield
    ev = jnp.concatenate([-e, v], axis=0).astype(BF16)
    upd = _dot_tn(ev, tails)
    send_state(s2 * decay + upd * headdiag)
    my = _dot(mats, ev)
    y = rs + jnp.where(lo, my[:CHUNK], my[CHUNK:])
    return y, bonus


RWKV_PAIRS_PER_STEP = 2
_RWKV_IN_REFS = 25
_RWKV_SCRATCH = 4
_MT_STRICT, _MT_STRICT_HI, _MT_INCL_WIDE = 0, 2, 4
_MT_BLOCKDIAG, _MT_HEADDIAG, _MT_HALVES, _MT_BANDS = 6, 7, 9, 10


def _rwkv_mask_tables():
    import numpy as np
    t = np.arange(CHUNK)[:, None]
    s = np.arange(CHUNK)[None, :]
    lane = np.arange(LANES)[None, :]
    table = np.zeros((_MT_BANDS, CHUNK, LANES), np.float32)
    tri = np.zeros((2, CHUNK, CHUNK), np.float32)
    for d, (strict, incl) in enumerate(((s < t, s <= t), (s > t, s >= t))):
        table[_MT_STRICT + d, :, :CHUNK] = strict
        table[_MT_STRICT_HI + d, :, CHUNK:] = strict
        table[_MT_INCL_WIDE + d] = np.concatenate([incl, incl], axis=1)
        tri[d] = incl
    table[_MT_BLOCKDIAG, :, :CHUNK] = (t // 16) == (s // 16)
    v = np.arange(LANES)[:, None]
    table[_MT_HEADDIAG:_MT_HEADDIAG + 2] = ((v < RWKV_HD) == (lane < RWKV_HD)).reshape(2, CHUNK, LANES)
    table[_MT_HALVES, 0] = lane[0] < RWKV_HD
    table[_MT_HALVES, 1] = lane[0] >= RWKV_HD
    return table.reshape(_MT_BANDS * CHUNK, LANES), tri


def _rwkv_kernel(*refs, need_ctx):
    n_pairs = RWKV_PAIRS_PER_STEP
    mt_ref, tri_ref = refs[:2]
    refs = refs[2:]
    n_in = n_pairs * _RWKV_IN_REFS
    rest = refs[n_in:]
    if need_ctx:
        y_l, y_c = rest[:2]
        scr = rest[2:]
    else:
        y_l, y_c = rest[0], None
        scr = rest[1:]
    nl = refs[0].shape[0] // CHUNK
    nc = refs[6].shape[0] // CHUNK

    def band(j, rows=CHUNK, lanes=LANES):
        return mt_ref[j * CHUNK:j * CHUNK + rows, :lanes]

    lo = lax.broadcasted_iota(jnp.int32, (1, LANES), 1) < RWKV_HD
    lo_f, hi_f = band(_MT_HALVES, 1), mt_ref[_MT_HALVES * CHUNK + 1:_MT_HALVES * CHUNK + 2, :]
    blockdiag16 = band(_MT_BLOCKDIAG, lanes=CHUNK)
    headdiag = band(_MT_HEADDIAG, rows=LANES)

    def consts(d):
        return (lo_f, hi_f, lo, blockdiag16, headdiag, tri_ref[d], band(_MT_STRICT + d, lanes=CHUNK),
                band(_MT_STRICT_HI + d), band(_MT_INCL_WIDE + d))
    consts_f, consts_b = consts(0), consts(1)

    pairs = []
    for g in range(n_pairs):
        r = refs[g * _RWKV_IN_REFS:(g + 1) * _RWKV_IN_REFS]
        (w0f, w0b, a0f, a0b, kkw, kaw, rkw, lng, lnb, wupf, wupb, aupf, aupb) = [x[0] for x in r[12:]]
        s = scr[g * 2 * _RWKV_SCRATCH:(g + 1) * 2 * _RWKV_SCRATCH]
        pairs.append(dict(
            lat=tuple(r[0:6]), ctx=tuple(r[6:12]),
            prm_f=(w0f, a0f, kkw, kaw, rkw, wupf, aupf),
            prm_b=(w0b, a0b, kkw, kaw, rkw, wupb, aupb),
            scr_lat=s[:_RWKV_SCRATCH], scr_ctx=s[_RWKV_SCRATCH:], lng=lng, lnb=lnb))

    def scan(which, n, carry):
        unroll = _unroll_factor(n, RWKV_UNROLL)

        def body(i, carry):
            states = {}
            gens = []
            for u in range(unroll):
                c = i * unroll + u
                for g, pr in enumerate(pairs):
                    states[(g, "f", -1)], states[(g, "b", -1)] = carry[2 * g], carry[2 * g + 1]
                    for d, cc, prm, consts in (("f", c, pr["prm_f"], consts_f),
                                               ("b", n - 1 - c, pr["prm_b"], consts_b)):
                        gens.append(_rwkv_chunk(
                            pr[which], cc, prm, consts,
                            functools.partial(states.get, (g, d, u - 1)),
                            functools.partial(states.__setitem__, (g, d, u))))
            outs = iter(_run_interleaved(gens))
            for u in range(unroll):
                c = i * unroll + u
                rows_f, rows_b = _chunk_rows(c), _chunk_rows(n - 1 - c)
                for pr in pairs:
                    yf, yb, bf, bb = pr["scr_" + which]
                    (y_f, bon_f), (y_b, bon_b) = next(outs), next(outs)
                    yf[rows_f, :] = y_f
                    bf[rows_f, :] = bon_f.astype(bf.dtype)
                    yb[rows_b, :] = y_b
                    bb[rows_b, :] = bon_b.astype(bb.dtype)
            return tuple(states[(g, d, unroll - 1)] for g in range(n_pairs) for d in ("f", "b"))
        return lax.fori_loop(0, n // unroll, body, carry)

    zero = jnp.zeros((LANES, LANES), F32)
    carry = scan("ctx", nc, (zero,) * (2 * n_pairs))
    scan("lat", nl, carry)

    head_mean = (headdiag * (1.0 / RWKV_HD)).astype(BF16)

    def seg_mean(x):
        hi = x.astype(BF16)
        lo_part = (x - hi.astype(F32)).astype(BF16)
        r = _dot(jnp.concatenate([hi, lo_part], axis=0), head_mean)
        return r[:CHUNK] + r[CHUNK:]

    def finish(which, y_out, n):
        def body(i, _):
            rows = _chunk_rows(i)
            for g, pr in enumerate(pairs):
                yf, yb, bf, bb = pr["scr_" + which]
                gs = pr[which][3]
                y = yf[rows, :] + yb[rows, :]
                d = y - seg_mean(y)
                var = seg_mean(d * d)
                yn = d * lax.rsqrt(var + GN_EPS) * pr["lng"] + pr["lnb"]
                bonus = bf[rows, :].astype(F32) + bb[rows, :].astype(F32)
                y_out[rows, g * LANES:(g + 1) * LANES] = (
                    (yn + bonus) * gs[rows, :].astype(F32)).astype(y_out.dtype)
            return 0
        lax.fori_loop(0, n, body, 0, unroll=_unroll_factor(n, FINISH_UNROLL))

    finish("lat", y_l, nl)
    if need_ctx:
        finish("ctx", y_c, nc)


def _rwkv(ps_lat, ps_ctx, batch, seq, ctx_len, prm, need_ctx):
    n_pairs = RWKV_PAIRS_PER_STEP

    def pair_specs(g):
        hp = lambda s: n_pairs * s + g

        def seg_specs(n):
            return [pl.BlockSpec((n, LANES), lambda b, s: (b, RW_R + hp(s))),
                    pl.BlockSpec((n, LANES), lambda b, s: (b, RW_K + hp(s))),
                    pl.BlockSpec((n, LANES), lambda b, s: (b, RW_V + hp(s))),
                    pl.BlockSpec((n, LANES), lambda b, s: (b, RW_GATE + hp(s))),
                    pl.BlockSpec((n, LANES), lambda b, s: (b, RW_WD)),
                    pl.BlockSpec((n, LANES), lambda b, s: (b, RW_AD))]
        vec_h = pl.BlockSpec((1, 1, LANES), lambda b, s: (hp(s), 0, 0))
        mat_h = pl.BlockSpec((1, LANES, LANES), lambda b, s: (hp(s), 0, 0))
        return seg_specs(seq) + seg_specs(ctx_len) + [vec_h] * 9 + [mat_h] * 4

    table, tri = _rwkv_mask_tables()
    in_specs = [pl.BlockSpec(table.shape, lambda b, s: (0, 0)),
                pl.BlockSpec(tri.shape, lambda b, s: (0, 0, 0))]
    args = [jnp.asarray(table), jnp.asarray(tri, BF16)]
    for g in range(n_pairs):
        in_specs += pair_specs(g)
        args += [ps_lat] * 6 + [ps_ctx] * 6 + list(prm)
    width = n_pairs * LANES
    out_specs = [pl.BlockSpec((seq, width), lambda b, s: (b, s))]
    out_shape = [jax.ShapeDtypeStruct((batch * seq, BRANCH_W), BF16)]
    if need_ctx:
        out_specs.append(pl.BlockSpec((ctx_len, width), lambda b, s: (b, s)))
        out_shape.append(jax.ShapeDtypeStruct((batch * ctx_len, BRANCH_W), BF16))
    def seg_scratch(n):
        return [pltpu.VMEM((n, LANES), F32)] * 2 + [pltpu.VMEM((n, LANES), BF16)] * 2
    scratch = (seg_scratch(seq) + seg_scratch(ctx_len)) * n_pairs
    outs = pl.pallas_call(
        functools.partial(_rwkv_kernel, need_ctx=need_ctx),
        grid=(batch, RWKV_H // 2 // n_pairs),
        in_specs=in_specs,
        out_specs=out_specs,
        out_shape=out_shape,
        scratch_shapes=scratch,
        compiler_params=_cparams(("parallel", "parallel")),
        name="rwkv",
    )(*args)
    return (outs[0], outs[1]) if need_ctx else (outs[0], None)


def _merge_kernel(x_ref, y1_ref, y2_ref, y3_ref, g1_ref, g2_ref, g3_ref, mod_ref,
                  wo1_ref, wo2_ref, wo3_ref, wout_ref, gp_ref, o_ref):
    m = (_sigmoid(g1_ref[...].astype(F32)) * _dot(y1_ref[...], wo1_ref[...])
         + _sigmoid(g2_ref[...].astype(F32)) * _dot(y2_ref[...], wo2_ref[...])
         + _sigmoid(g3_ref[...].astype(F32)) * _dot(y3_ref[...], wo3_ref[...]))
    mo = _dot(m.astype(BF16), wout_ref[...])
    ms = jnp.mean(mo * mo, axis=-1, keepdims=True)
    out = mo * lax.rsqrt(ms + NORM_EPS) * gp_ref[...]
    o_ref[...] = x_ref[...] + mod_ref[0][2:3] * out


def _merge(x2d, y1, y2, y3, p, mod, mod_index, wo1, wo2, wo3, wout, g_post, tm):
    r, d = x2d.shape
    row = lambda i: (i, 0)
    const = lambda i: (0, 0)
    return pl.pallas_call(
        _merge_kernel,
        grid=(r // tm,),
        in_specs=[pl.BlockSpec((tm, d), row),
                  pl.BlockSpec((tm, BRANCH_W), row),
                  pl.BlockSpec((tm, BRANCH_W), row),
                  pl.BlockSpec((tm, BRANCH_W), row),
                  pl.BlockSpec((tm, d), lambda i: (i, 0)),
                  pl.BlockSpec((tm, d), lambda i: (i, 1)),
                  pl.BlockSpec((tm, d), lambda i: (i, 2)),
                  pl.BlockSpec((1, 3, d), lambda i: (mod_index(i), 0, 0)),
                  pl.BlockSpec((BRANCH_W, d), const),
                  pl.BlockSpec((BRANCH_W, d), const),
                  pl.BlockSpec((BRANCH_W, d), const),
                  pl.BlockSpec((d, d), const),
                  pl.BlockSpec((1, d), const)],
        out_specs=pl.BlockSpec((tm, d), row),
        out_shape=jax.ShapeDtypeStruct((r, d), F32),
        compiler_params=_cparams(("parallel",)),
        name="merge",
    )(x2d, y1, y2, y3, p, p, p, mod, wo1, wo2, wo3, wout, g_post.reshape(1, d))


def _pack_w_in(w_in):
    d = w_in.shape[0]
    o = 0
    gq = w_in[:, o:o + 256]; o += 256
    gk = w_in[:, o:o + 256]; o += 256
    gv = w_in[:, o:o + 512]; o += 512
    gg = w_in[:, o:o + 512]; o += 512
    gwd = w_in[:, o:o + 32]; o += 32
    att = w_in[:, o:o + 1536]; o += 1536
    rw = w_in[:, o:o + 2304]; o += 2304
    mg = w_in[:, o:o + 3072]
    pad_wd = jnp.zeros((d, LANES - 32), w_in.dtype)
    pad_end = jnp.zeros((d, LANES), w_in.dtype)
    return jnp.concatenate([mg, att, rw, gq, gk, gv, gg, gwd, pad_wd, pad_end], axis=1).astype(BF16)


def _rope_tables(seq):
    quarter = ATT_HD // 4
    inv = ROPE_THETA ** (-jnp.arange(quarter, dtype=F32) / quarter)
    t = jnp.arange(seq)
    row_pos = (t // GRID_W).astype(F32)
    col_pos = (t % GRID_W).astype(F32)
    ar = row_pos[:, None] * inv[None, :]
    ac = col_pos[:, None] * inv[None, :]
    z = jnp.zeros_like(ar)
    cos = jnp.concatenate([jnp.cos(ar), jnp.cos(ar), jnp.cos(ac), jnp.cos(ac)], axis=1)
    sa = jnp.concatenate([-jnp.sin(ar), z, -jnp.sin(ac), z], axis=1)
    sb = jnp.concatenate([z, jnp.sin(ar), z, jnp.sin(ac)], axis=1)
    return cos, sa, sb


def _pad_rows(w, lo_half):
    z = jnp.zeros((LANES - w.shape[0], w.shape[1]), w.dtype)
    if lo_half:
        return jnp.concatenate([w, z], axis=0)
    half = LANES // 2
    return jnp.concatenate([z[:half], w, z[half:]], axis=0)


def _pick(n, target):
    t = min(n, target)
    while n % t:
        t //= 2
    return t


def kernel(x, c, ctx, c_ctx, w_mod, b_mod, g_pre, w_in, gla_wup_f, gla_b_f, gla_wup_b, gla_b_b, gla_norm, att_qnorm, att_knorm, rwkv_mu, rwkv_w0_f, rwkv_wup_f, rwkv_w0_b, rwkv_wup_b, rwkv_a0_f, rwkv_aup_f, rwkv_a0_b, rwkv_aup_b, rwkv_kk, rwkv_ka, rwkv_rk, rwkv_ln_g, rwkv_ln_b, w_o_gla, w_o_att, w_o_rwkv, w_out, g_post):
    batch, seq, d = x.shape
    ctx_len = ctx.shape[1]
    depth = w_in.shape[0]
    assert seq % CHUNK == 0 and ctx_len % CHUNK == 0 and seq % GRID_W == 0

    nb = -(-(batch + 1) // 8) * 8
    cond = jnp.concatenate([c, c_ctx[None, :], jnp.zeros((nb - batch - 1, d), F32)], axis=0)
    tables = _rope_tables(seq)

    tm_lat = _pick(seq, 1024)
    tm_ctx = _pick(batch * ctx_len, 1024)

    def lat_mod(tm):
        return lambda i: i // (seq // tm)

    def ctx_mod(tm):
        return lambda i: batch
    tq_lat, tk = _pick(seq, 512), _pick(seq, 512)
    tq_ctx = _pick(ctx_len, 256)
    tm_prep = _pick(seq, 512)
    tm_prep_c = _pick(ctx_len, 512)
    tm_shift = _pick(seq, 512)
    tm_shift_c = _pick(ctx_len, 512)
    tm_merge = _pick(seq, 1024)
    tm_merge_c = _pick(batch * ctx_len, 1024)

    x2 = x.reshape(batch * seq, d)
    xc2 = ctx.reshape(batch * ctx_len, d)

    def hp(v):
        return v.reshape(RWKV_H // 2, 1, LANES)

    for l in range(depth):
        need_ctx = l < depth - 1
        mod = _modulation(cond, w_mod[l], b_mod[l]).reshape(nb, 3, d)
        w_cat = _pack_w_in(w_in[l])
        p_lat = _inproj(x2, mod, lat_mod(tm_lat), g_pre[l], w_cat, tm_lat)
        p_ctx = _inproj(xc2, mod, ctx_mod(tm_ctx), g_pre[l], w_cat, tm_ctx)

        q_gain = (att_qnorm[l] * (ATT_HD ** -0.5 * LOG2_E)).reshape(1, LANES)
        k_gain = att_knorm[l].reshape(1, LANES)
        kv_lat = tuple(_kvprep(p_lat, k_gain, tables, batch, seq, tm_prep)) + (seq,)
        kv_ctx = tuple(_kvprep(p_ctx, k_gain, None, batch, ctx_len, tm_prep_c)) + (ctx_len,)
        y2 = _flash(p_lat, seq, q_gain, tables, kv_lat, kv_ctx, batch, tq_lat, tk)
        y2c = None
        if need_ctx:
            y2c = _flash(p_ctx, ctx_len, q_gain, None, None, kv_ctx, batch, tq_ctx, tk)

        def gla_w(w, lo_rows):
            w = w.reshape(GLA_LR, GLA_H // 2, LANES).transpose(1, 0, 2)
            z = jnp.zeros((GLA_H // 2, LANES, LANES), w.dtype)
            start = 0 if lo_rows else GLA_LR
            return lax.dynamic_update_slice(z, w, (0, start, 0)).astype(BF16)
        y1, y1c = _gla(p_lat, p_ctx, batch, seq, ctx_len,
                       gla_w(gla_wup_f[l], True), gla_w(gla_wup_b[l], False),
                       gla_b_f[l].reshape(GLA_H // 2, 1, LANES), gla_b_b[l].reshape(GLA_H // 2, 1, LANES),
                       gla_norm[l].reshape(GLA_H, 1, GLA_DV), need_ctx)

        def rw_w(w, lo_rows):
            w = w.reshape(RWKV_LR, RWKV_H // 2, LANES).transpose(1, 0, 2)
            return jnp.stack([_pad_rows(w[i], lo_rows) for i in range(RWKV_H // 2)]).astype(BF16)
        prm = (hp(rwkv_w0_f[l]), hp(rwkv_w0_b[l]), hp(rwkv_a0_f[l]), hp(rwkv_a0_b[l]),
               hp(rwkv_kk[l]), hp(rwkv_ka[l]), hp(rwkv_rk[l]), hp(rwkv_ln_g[l]), hp(rwkv_ln_b[l]),
               rw_w(rwkv_wup_f[l], True), rw_w(rwkv_wup_b[l], False),
               rw_w(rwkv_aup_f[l], True), rw_w(rwkv_aup_b[l], False))
        ps_lat = _rwkv_shift(p_lat, rwkv_mu[l], seq, tm_shift)
        ps_ctx = _rwkv_shift(p_ctx, rwkv_mu[l], ctx_len, tm_shift_c)
        y3, y3c = _rwkv(ps_lat, ps_ctx, batch, seq, ctx_len, prm, need_ctx)

        wo1, wo2, wo3 = (w_o_gla[l].astype(BF16), w_o_att[l].astype(BF16),
                         w_o_rwkv[l].astype(BF16))
        wout = w_out[l].astype(BF16)
        x2 = _merge(x2, y1, y2, y3, p_lat, mod, lat_mod(tm_merge), wo1, wo2, wo3, wout, g_post[l], tm_merge)
        if need_ctx:
            xc2 = _merge(xc2, y1c, y2c, y3c, p_ctx, mod, ctx_mod(tm_merge_c), wo1, wo2, wo3, wout,
                         g_post[l], tm_merge_c)
    return x2.reshape(batch, seq, d)
```

```python
import functools

import jax
import jax.numpy as jnp
from jax import lax
from jax.experimental import pallas as pl
from jax.experimental.pallas import tpu as pltpu

F32 = jnp.float32
BF16 = jnp.bfloat16

GRID_W = 64
BRANCH_W = 512
GLA_H, GLA_DK, GLA_DV, GLA_LR = 4, 64, 128, 16
GLA_TAU = 16.0
ATT_H, ATT_KV, ATT_HD = 4, 2, 128
ROPE_THETA = 10000.0
RWKV_H, RWKV_HD, RWKV_LR = 8, 64, 64
RWKV_DECAY_SCALE = 0.6065306597
NORM_EPS = 1e-6
GN_EPS = 64e-5
L2_EPS = 1e-12
LOG2_E = 1.4426950408889634

CHUNK = 64
LANES = 128
VMEM_LIMIT = 56 * 1024 * 1024

CB_MG = 0
CB_ATT_Q = 24
CB_ATT_V = 30
CB_ATT_GATE = 32
CB_RW = 36
RW_BLOCKS = 18
CB_GLA_Q = 54
CB_GLA_K = 56
CB_GLA_V = 58
CB_GLA_GATE = 62
CB_GLA_WD = 66
NP_BLOCKS = 68
RW_R, RW_K, RW_V, RW_GATE, RW_WD, RW_AD = 0, 4, 8, 12, 16, 17
NP = NP_BLOCKS * LANES
TN_PROJ = NP // 4


def _cparams(sem):
    return pltpu.CompilerParams(dimension_semantics=sem, vmem_limit_bytes=VMEM_LIMIT)


def _dot(a, b):
    return jnp.dot(a, b, preferred_element_type=F32)


def _dot_nt(a, b):
    return lax.dot_general(a, b, (((1,), (1,)), ((), ())), preferred_element_type=F32)


def _dot_tn(a, b):
    return lax.dot_general(a, b, (((0,), (0,)), ((), ())), preferred_element_type=F32)


def _silu(x):
    return x / (1.0 + jnp.exp(-x))


def _sigmoid(x):
    return 1.0 / (1.0 + jnp.exp(-x))


def _tri_cumsum(tri, x):
    hi = x.astype(BF16)
    lo = (x - hi.astype(F32)).astype(BF16)
    n = x.shape[1]
    if n % LANES:
        return _dot(tri, hi) + _dot(tri, lo)
    r = _dot(tri, jnp.concatenate([hi, lo], axis=1))
    return r[:, :n] + r[:, n:]


GLA_UNROLL = 16
RWKV_UNROLL = 2
FINISH_UNROLL = 4


def _unroll_factor(n_chunks, target):
    u = target
    while n_chunks % u:
        u //= 2
    return u


def _mod_kernel(c_ref, w_ref, b_ref, o_ref):
    c = c_ref[...]
    s = _silu(c)
    o_ref[...] = jnp.dot(s, w_ref[...], preferred_element_type=F32,
                         precision=lax.Precision.HIGHEST) + b_ref[...]


def _modulation(cond, w_mod, b_mod):
    nb, d = cond.shape
    n = w_mod.shape[1]
    tn = 768
    return pl.pallas_call(
        _mod_kernel,
        grid=(n // tn,),
        in_specs=[pl.BlockSpec((nb, d), lambda j: (0, 0)),
                  pl.BlockSpec((d, tn), lambda j: (0, j)),
                  pl.BlockSpec((1, tn), lambda j: (0, j))],
        out_specs=pl.BlockSpec((nb, tn), lambda j: (0, j)),
        out_shape=jax.ShapeDtypeStruct((nb, n), F32),
        compiler_params=_cparams(("parallel",)),
        name="modulation",
    )(cond, w_mod, b_mod.reshape(1, n))


def _inproj_kernel(x_ref, mod_ref, g_ref, w_ref, o_ref, h_ref):
    @pl.when(pl.program_id(1) == 0)
    def _():
        x = x_ref[...]
        ms = jnp.mean(x * x, axis=-1, keepdims=True)
        y = x * lax.rsqrt(ms + NORM_EPS) * g_ref[...]
        mod = mod_ref[0]
        h_ref[...] = (y * (1.0 + mod[1:2]) + mod[0:1]).astype(BF16)

    o_ref[...] = _dot(h_ref[...], w_ref[...]).astype(o_ref.dtype)


def _inproj(x2d, mod, mod_index, g_pre, w_cat, tm):
    r, d = x2d.shape
    return pl.pallas_call(
        _inproj_kernel,
        grid=(r // tm, NP // TN_PROJ),
        in_specs=[pl.BlockSpec((tm, d), lambda i, j: (i, 0)),
                  pl.BlockSpec((1, 3, d), lambda i, j: (mod_index(i), 0, 0)),
                  pl.BlockSpec((1, d), lambda i, j: (0, 0)),
                  pl.BlockSpec((d, TN_PROJ), lambda i, j: (0, j))],
        out_specs=pl.BlockSpec((tm, TN_PROJ), lambda i, j: (i, j)),
        out_shape=jax.ShapeDtypeStruct((r, NP), BF16),
        scratch_shapes=[pltpu.VMEM((tm, d), BF16)],
        compiler_params=_cparams(("parallel", "arbitrary")),
        name="inproj",
    )(x2d, mod, g_pre.reshape(1, d), w_cat)


def _norm_rope(x, gain, tables):
    x = x.astype(F32)
    ms = jnp.mean(x * x, axis=-1, keepdims=True)
    y = x * lax.rsqrt(ms + NORM_EPS) * gain
    if tables is not None:
        cos, sin = tables
        y = y * cos + pltpu.roll(y, ATT_HD // 2, 1) * sin
    return y.astype(BF16)


ATT_QKV_BLOCKS = ATT_H + 2 * ATT_KV


def _qkvprep_kernel(*refs, rope):
    if rope:
        x_ref, g_ref, cos_ref, sin_ref, qk_ref, vt_ref = refs
        tables = (cos_ref[...], sin_ref[...])
    else:
        x_ref, g_ref, qk_ref, vt_ref = refs
        tables = None
    for h in range(ATT_H + ATT_KV):
        cols = slice(h * LANES, (h + 1) * LANES)
        gain = g_ref[0:1, :] if h < ATT_H else g_ref[1:2, :]
        qk_ref[:, cols] = _norm_rope(x_ref[:, cols], gain, tables)
    for j in range(ATT_KV):
        cols = slice((ATT_H + ATT_KV + j) * LANES, (ATT_H + ATT_KV + j + 1) * LANES)
        base = j * VT_ROWS
        vt_ref[base:base + ATT_HD, :] = x_ref[:, cols].astype(F32).T.astype(BF16)
        vt_ref[base + ATT_HD:base + VT_ROWS, :] = jnp.ones((VT_ROWS - ATT_HD, vt_ref.shape[1]), BF16)


VT_ROWS = ATT_HD + 16


def _qkvprep(p, gains, tables, batch, seq, tm):
    nt = seq // tm
    width = ATT_QKV_BLOCKS * LANES
    cb = CB_ATT_Q * LANES // width
    in_specs = [pl.BlockSpec((tm, width), lambda i: (i, cb)),
                pl.BlockSpec((2, LANES), lambda i: (0, 0))]
    args = [p, gains]
    if tables is not None:
        for tab in tables:
            in_specs.append(pl.BlockSpec((tm, LANES), lambda i: (i % nt, 0)))
            args.append(tab)
    qk_width = (ATT_H + ATT_KV) * LANES
    return pl.pallas_call(
        functools.partial(_qkvprep_kernel, rope=tables is not None),
        grid=(batch * nt,),
        in_specs=in_specs,
        out_specs=[pl.BlockSpec((tm, qk_width), lambda i: (i, 0)),
                   pl.BlockSpec((ATT_KV * VT_ROWS, tm), lambda i: (i // nt, i % nt))],
        out_shape=[jax.ShapeDtypeStruct((batch * seq, qk_width), BF16),
                   jax.ShapeDtypeStruct((batch * ATT_KV * VT_ROWS, seq), BF16)],
        compiler_params=_cparams(("parallel",)),
        name="qkvprep_rope" if tables is not None else "qkvprep",
    )(*args)


FLASH_KV_INTERLEAVE = 8


def _flash_kernel(*refs, n_lat_blocks, tk, has_ctx_kv):
    refs = list(refs)
    q0_ref, q1_ref, g0_ref, g1_ref = refs[:4]
    pos = 4
    if n_lat_blocks:
        kl_ref, vtl_ref = refs[pos:pos + 2]
        pos += 2
    if has_ctx_kv:
        kc_ref, vtc_ref = refs[pos:pos + 2]
        pos += 2
    o_ref = refs[pos]
    tq = q0_ref.shape[0]
    q = jnp.concatenate([q0_ref[...], q1_ref[...]], axis=0)

    def blocks(kvs, carry):
        scores = [_dot_nt(k, q) for k, _ in kvs]
        m, acc = carry
        for s, (_, vt) in zip(scores, kvs):
            m_new = jnp.maximum(m, jnp.max(s, axis=0, keepdims=True))
            p = jnp.exp2(s - m_new)
            acc = jnp.exp2(m - m_new) * acc + _dot(vt, p.astype(BF16))
            m = m_new
        return m, acc

    carry = (jnp.full((1, 2 * tq), -jnp.inf, F32), jnp.zeros((VT_ROWS, 2 * tq), F32))
    if n_lat_blocks:
        nb = _unroll_factor(n_lat_blocks, FLASH_KV_INTERLEAVE)

        def body(i, c):
            kvs = []
            for u in range(nb):
                start = pl.multiple_of((i * nb + u) * tk, tk)
                kvs.append((kl_ref[pl.ds(start, tk), :], vtl_ref[:, pl.ds(start, tk)]))
            return blocks(kvs, c)
        carry = lax.fori_loop(0, n_lat_blocks // nb, body, carry)
    if has_ctx_kv:
        carry = blocks([(kc_ref[...], vtc_ref[...])], carry)
    _, acc = carry
    o = (acc[:ATT_HD] / acc[ATT_HD:ATT_HD + 1]).T
    o_ref[:, :LANES] = (o[:tq] * _silu(g0_ref[...].astype(F32))).astype(o_ref.dtype)
    o_ref[:, LANES:] = (o[tq:] * _silu(g1_ref[...].astype(F32))).astype(o_ref.dtype)


def _flash(qk_q, p_q, seq_q, lat_kv, ctx_kv, batch, tq, tk):
    nq = seq_q // tq
    in_specs = [
        pl.BlockSpec((tq, LANES), lambda b, j, i: (b * nq + i, 2 * j)),
        pl.BlockSpec((tq, LANES), lambda b, j, i: (b * nq + i, 2 * j + 1)),
        pl.BlockSpec((tq, LANES), lambda b, j, i: (b * nq + i, CB_ATT_GATE + 2 * j)),
        pl.BlockSpec((tq, LANES), lambda b, j, i: (b * nq + i, CB_ATT_GATE + 2 * j + 1)),
    ]
    args = [qk_q, qk_q, p_q, p_q]
    n_lat_blocks = 0
    for seg in (lat_kv, ctx_kv):
        if seg is None:
            continue
        qk_s, vt_s, seq_s = seg
        in_specs.append(pl.BlockSpec((seq_s, LANES), lambda b, j, i: (b, ATT_H + j)))
        in_specs.append(pl.BlockSpec((VT_ROWS, seq_s), lambda b, j, i: (b * ATT_KV + j, 0)))
        args += [qk_s, vt_s]
    if lat_kv is not None:
        n_lat_blocks = lat_kv[2] // tk
    return pl.pallas_call(
        functools.partial(_flash_kernel, n_lat_blocks=n_lat_blocks, tk=tk,
                          has_ctx_kv=ctx_kv is not None),
        grid=(batch, ATT_KV, nq),
        in_specs=in_specs,
        out_specs=pl.BlockSpec((tq, 2 * LANES), lambda b, j, i: (b * nq + i, j)),
        out_shape=jax.ShapeDtypeStruct((batch * seq_q, BRANCH_W), BF16),
        compiler_params=_cparams(("parallel", "parallel", "arbitrary")),
        name="flash_lat" if lat_kv is not None else "flash_ctx",
    )(*args)


def _chunk_rows(c):
    return pl.ds(pl.multiple_of(c * CHUNK, CHUNK), CHUNK)


def _gla_chunk(seg, c, wup, bias, consts, recv_state, send_state):
    q_ref, k_ref, v_refs, wd_ref = seg
    tri, incl2, halves = consts
    rows = _chunk_rows(c)
    q = q_ref[rows, :].astype(F32) * (GLA_DK ** -0.5)
    k = k_ref[rows, :].astype(F32)
    vs = [v_ref[rows, :] for v_ref in v_refs]
    z = _dot(wd_ref[rows, :], wup) + bias
    yield
    g = (jnp.minimum(z, 0.0) - jnp.log1p(jnp.exp(-jnp.abs(z)))) * (1.0 / GLA_TAU)
    gc = _tri_cumsum(tri, g)
    yield
    tot = jnp.sum(g, axis=0, keepdims=True)
    q_dec = q * jnp.exp(gc)
    k_inv = (k * jnp.exp(-gc)).astype(BF16)
    k_tail = k * jnp.exp(tot - gc)
    q_heads = [(q_dec * hm).astype(BF16) for hm in halves]
    a = (_dot_nt(jnp.concatenate(q_heads, axis=0), k_inv) * incl2).astype(BF16)
    upds = [_dot_tn(v, (k_tail * hm).astype(BF16)) for v, hm in zip(vs, halves)]
    yield
    sts = recv_state()
    while sts is None:
        yield
        sts = recv_state()
    decay = jnp.exp(tot)
    send_state(tuple(st * decay + upd for st, upd in zip(sts, upds)))
    return tuple(_dot(a[h * CHUNK:(h + 1) * CHUNK], vs[h]) + _dot_nt(q_heads[h], sts[h].astype(BF16))
                 for h in range(2))


_GLA_SEG_REFS = 7


def _gla_kernel(*refs, need_ctx):
    mt_ref, tri_ref = refs[:2]
    lat = refs[2:2 + _GLA_SEG_REFS]
    ctx = refs[2 + _GLA_SEG_REFS:2 + 2 * _GLA_SEG_REFS]
    wupf_ref, wupb_ref, bf_ref, bb_ref, gn_ref = refs[2 + 2 * _GLA_SEG_REFS:7 + 2 * _GLA_SEG_REFS]
    rest = refs[7 + 2 * _GLA_SEG_REFS:]
    if need_ctx:
        y_l, y_c = rest[:2]
        scr = rest[2:]
    else:
        y_l, y_c = rest[0], None
        scr = rest[1:]
    scr_l, scr_c = scr[:4], scr[4:]
    nl = lat[0].shape[0] // CHUNK
    nc = ctx[0].shape[0] // CHUNK
    wupf, wupb = wupf_ref[0], wupb_ref[0]
    bias_f, bias_b = bf_ref[0], bb_ref[0]
    row0 = _MT_HALVES * CHUNK
    halves = (mt_ref[row0:row0 + 1, :], mt_ref[row0 + 1:row0 + 2, :])

    def consts(d):
        incl = mt_ref[(_MT_INCL_WIDE + d) * CHUNK:(_MT_INCL_WIDE + d + 1) * CHUNK, :CHUNK]
        return tri_ref[d], jnp.concatenate([incl, incl], axis=0), halves
    consts_f, consts_b = consts(0), consts(1)

    def scan(seg, scr, n, carry):
        unroll = _unroll_factor(n, GLA_UNROLL)
        seg = (seg[0], seg[1], seg[2:4], seg[6])

        def body(i, carry):
            states = {("f", -1): carry[0], ("b", -1): carry[1]}
            gens = []
            for u in range(unroll):
                c = i * unroll + u
                for d, cc, wup, bias, cst in (("f", c, wupf, bias_f, consts_f),
                                              ("b", n - 1 - c, wupb, bias_b, consts_b)):
                    gens.append(_gla_chunk(
                        seg, cc, wup, bias, cst,
                        functools.partial(states.get, (d, u - 1)),
                        functools.partial(states.__setitem__, (d, u))))
            outs = _run_interleaved(gens)
            for u in range(unroll):
                c = i * unroll + u
                for h in range(2):
                    scr[h][_chunk_rows(c), :] = outs[2 * u][h]
                    scr[2 + h][_chunk_rows(n - 1 - c), :] = outs[2 * u + 1][h]
            return states[("f", unroll - 1)], states[("b", unroll - 1)]
        return lax.fori_loop(0, n // unroll, body, carry)

    zero = (jnp.zeros((GLA_DV, LANES), F32),) * 2
    carry = scan(ctx, scr_c, nc, (zero, zero))
    scan(lat, scr_l, nl, carry)

    def finish(seg, scr, y, n):
        def body(i, _):
            rows = _chunk_rows(i)
            for h in range(2):
                o = scr[h][rows, :] + scr[2 + h][rows, :]
                ms = jnp.mean(o * o, axis=-1, keepdims=True)
                o = o * lax.rsqrt(ms + NORM_EPS) * gn_ref[h]
                gate = seg[4 + h][rows, :].astype(F32)
                y[rows, h * LANES:(h + 1) * LANES] = (o * _silu(gate)).astype(y.dtype)
            return 0
        lax.fori_loop(0, n, body, 0, unroll=_unroll_factor(n, FINISH_UNROLL))

    finish(lat, scr_l, y_l, nl)
    if need_ctx:
        finish(ctx, scr_c, y_c, nc)


def _gla(p_lat, p_ctx, batch, seq, ctx_len, wupf, wupb, bf, bb, gnorm, need_ctx):
    def seg_specs(n):
        return [pl.BlockSpec((n, LANES), lambda b, g: (b, CB_GLA_Q + g)),
                pl.BlockSpec((n, LANES), lambda b, g: (b, CB_GLA_K + g)),
                pl.BlockSpec((n, LANES), lambda b, g: (b, CB_GLA_V + 2 * g)),
                pl.BlockSpec((n, LANES), lambda b, g: (b, CB_GLA_V + 2 * g + 1)),
                pl.BlockSpec((n, LANES), lambda b, g: (b, CB_GLA_GATE + 2 * g)),
                pl.BlockSpec((n, LANES), lambda b, g: (b, CB_GLA_GATE + 2 * g + 1)),
                pl.BlockSpec((n, LANES), lambda b, g: (b, CB_GLA_WD))]
    table, tri = _rwkv_mask_tables()
    w_spec = pl.BlockSpec((1, LANES, LANES), lambda b, g: (g, 0, 0))
    b_spec = pl.BlockSpec((1, 1, LANES), lambda b, g: (g, 0, 0))
    g_spec = pl.BlockSpec((2, 1, GLA_DV), lambda b, g: (g, 0, 0))
    out_specs = [pl.BlockSpec((seq, 2 * LANES), lambda b, g: (b, g))]
    out_shape = [jax.ShapeDtypeStruct((batch * seq, BRANCH_W), BF16)]
    if need_ctx:
        out_specs.append(pl.BlockSpec((ctx_len, 2 * LANES), lambda b, g: (b, g)))
        out_shape.append(jax.ShapeDtypeStruct((batch * ctx_len, BRANCH_W), BF16))
    outs = pl.pallas_call(
        functools.partial(_gla_kernel, need_ctx=need_ctx),
        grid=(batch, GLA_H // 2),
        in_specs=([pl.BlockSpec(table.shape, lambda b, g: (0, 0)),
                   pl.BlockSpec(tri.shape, lambda b, g: (0, 0, 0))]
                  + seg_specs(seq) + seg_specs(ctx_len) + [w_spec, w_spec, b_spec, b_spec, g_spec]),
        out_specs=out_specs,
        out_shape=out_shape,
        scratch_shapes=[pltpu.VMEM((seq, LANES), F32)] * 4 + [pltpu.VMEM((ctx_len, LANES), F32)] * 4,
        compiler_params=_cparams(("parallel", "parallel")),
        name="gla",
    )(jnp.asarray(table), jnp.asarray(tri, BF16), *([p_lat] * _GLA_SEG_REFS + [p_ctx] * _GLA_SEG_REFS),
      wupf, wupb, bf, bb, gnorm)
    return (outs[0], outs[1]) if need_ctx else (outs[0], None)


def _seg_sum(x, lo):
    s0 = jnp.sum(jnp.where(lo, x, 0.0), axis=-1, keepdims=True)
    s1 = jnp.sum(jnp.where(lo, 0.0, x), axis=-1, keepdims=True)
    return jnp.where(lo, s0, s1)


def _run_interleaved(gens):
    gens = list(gens)
    out = [None] * len(gens)
    live = list(range(len(gens)))
    while live:
        still = []
        for j in live:
            try:
                next(gens[j])
                still.append(j)
            except StopIteration as stop:
                out[j] = stop.value
        live = still
    return out


def _unit_tri_inverse_offdiag(mats, blockdiag):
    def mm(x, y):
        return _dot(x.astype(BF16), y.astype(BF16))

    def mm2(x1, x2, y):
        r = mm(jnp.concatenate([x1, x2], axis=0), y)
        return r[:CHUNK], r[CHUNK:]

    nd = [-(a * blockdiag) for a in mats]
    no = [-a - d for a, d in zip(mats, nd)]
    n2 = [mm(d, d) for d in nd]
    yield
    r = [mm2(s, d, s) for d, s in zip(nd, n2)]
    n4 = [ri[0] for ri in r]
    x = [d + s + ri[1] for d, s, ri in zip(nd, n2, r)]
    yield
    r = [mm2(s, xi, s) for xi, s in zip(x, n4)]
    n8 = [ri[0] for ri in r]
    x = [xi + s + ri[1] for xi, s, ri in zip(x, n4, r)]
    yield
    xd = [xi + s + mm(xi, s) for xi, s in zip(x, n8)]
    yield
    m = [o + mm(xi, o) for xi, o in zip(xd, no)]
    yield
    m2 = [mm(mi, mi) for mi in m]
    yield
    xq = [mi + s + mm(mi, s) for mi, s in zip(m, m2)]
    yield
    return [q + d + mm(q, d) for q, d in zip(xq, xd)]


SHIFT_COLS = RW_BLOCKS * LANES


def _rwkv_shift_kernel(cur_ref, prev_ref, next_ref, mu_ref, o_ref, *, tiles_per_seq):
    t = pl.program_id(0) % tiles_per_seq
    x = cur_ref[...].astype(F32)
    tm = x.shape[0]
    prev_row = jnp.where(t > 0, prev_ref[...].astype(F32)[15:16, :], 0.0)
    next_row = jnp.where(t < tiles_per_seq - 1, next_ref[...].astype(F32)[0:1, :], 0.0)
    row8 = lax.broadcasted_iota(jnp.int32, (8, x.shape[1]), 0)
    up = pltpu.roll(x, 1, 0)
    up = jnp.concatenate([jnp.where(row8 == 0, prev_row, up[:8]), up[8:]], axis=0)
    dn = pltpu.roll(x, tm - 1, 0)
    dn = jnp.concatenate([dn[:tm - 8], jnp.where(row8 == 7, next_row, dn[tm - 8:])], axis=0)
    mu = mu_ref[...]
    y = x * (1.0 - mu) + (up + dn) * (0.5 * mu)
    g0, w0 = RW_GATE * LANES, RW_WD * LANES
    o_ref[:, :g0] = y[:, :g0].astype(o_ref.dtype)
    o_ref[:, g0:w0] = _silu(y[:, g0:w0]).astype(o_ref.dtype)
    o_ref[:, w0:w0 + LANES] = jnp.tanh(y[:, w0:w0 + LANES]).astype(o_ref.dtype)
    o_ref[:, w0 + LANES:] = y[:, w0 + LANES:].astype(o_ref.dtype)


def _rwkv_shift(p, mu, seq, tm):
    r = p.shape[0]
    tiles_per_seq = seq // tm
    sub = tm // 16
    cb = CB_RW * LANES // SHIFT_COLS
    return pl.pallas_call(
        functools.partial(_rwkv_shift_kernel, tiles_per_seq=tiles_per_seq),
        grid=(r // tm,),
        in_specs=[pl.BlockSpec((tm, SHIFT_COLS), lambda i: (i, cb)),
                  pl.BlockSpec((16, SHIFT_COLS), lambda i: (jnp.maximum(i * sub - 1, 0), cb)),
                  pl.BlockSpec((16, SHIFT_COLS), lambda i: (jnp.minimum((i + 1) * sub, r // 16 - 1), cb)),
                  pl.BlockSpec((1, SHIFT_COLS), lambda i: (0, 0))],
        out_specs=pl.BlockSpec((tm, SHIFT_COLS), lambda i: (i, 0)),
        out_shape=jax.ShapeDtypeStruct((r, RW_BLOCKS * LANES), BF16),
        compiler_params=_cparams(("parallel",)),
        name="rwkv_shift",
    )(p, p, p, mu.reshape(1, RW_BLOCKS * LANES))


def _rwkv_chunk(seg, c, prm, consts, recv_state, send_state):
    (r_ref, k_ref, v_ref, _, wd_ref, ad_ref) = seg
    (w0, a0, kkw, kaw, rkw, wup, aup) = prm
    (lo_f, hi_f, lo, blockdiag16, headdiag, tri, strict, strict_hi, incl_wide) = consts
    rows = _chunk_rows(c)

    r = r_ref[rows, :].astype(F32)
    k = k_ref[rows, :].astype(F32)
    v16 = v_ref[rows, :]
    v = v16.astype(F32)
    z_w = _dot(wd_ref[rows, :], wup)
    z_a = _dot(ad_ref[rows, :], aup)
    yield
    lw = -RWKV_DECAY_SCALE * _sigmoid(w0 + z_w)
    a = _sigmoid(a0 + z_a)
    kd = k * (1.0 + (a - 1.0) * kaw)
    kk = k * kkw
    kk = kk * lax.rsqrt(_seg_sum(kk * kk, lo) + L2_EPS)
    kka = kk * a
    bonus = _seg_sum(r * kd * rkw, lo) * v

    cw = _tri_cumsum(tri, lw)
    yield
    tot = jnp.sum(lw, axis=0, keepdims=True)
    e_neg = jnp.exp(-cw)
    e_tail = jnp.exp(tot - cw)
    kap = kk * jnp.exp(cw - lw)
    rt = r * jnp.exp(cw)
    bt = (kka * e_neg).astype(BF16)
    kt = (kd * e_neg).astype(BF16)
    tails = jnp.concatenate([kka * e_tail, kd * e_tail], axis=0).astype(BF16)
    decay = jnp.exp(tot)
    kap_rt = jnp.concatenate([kap, rt], axis=0)

    yt = jnp.concatenate([bt, kt], axis=0)
    gram = _dot_nt(jnp.concatenate([kap_rt * lo_f, kap_rt * hi_f], axis=0).astype(BF16), yt)
    grams = (gram[:LANES], gram[LANES:])
    kap_rt = kap_rt.astype(BF16)
    yield
    vv = jnp.concatenate([v16, v16], axis=0)
    a_b = [g[:CHUNK, :CHUNK] * strict for g in grams]
    akv = _dot(jnp.concatenate([g[:CHUNK] * strict_hi for g in grams], axis=0).astype(BF16), vv)
    akv = jnp.where(lo, akv[:CHUNK], akv[CHUNK:])
    mats = jnp.concatenate([g[CHUNK:] * incl_wide for g in grams], axis=0).astype(BF16)
    x_inv = yield from _unit_tri_inverse_offdiag(a_b, blockdiag16)
    x_inv = jnp.concatenate(x_inv, axis=0).astype(BF16)
    yield

    s2 = recv_state()
    while s2 is None:
        yield
        s2 = recv_state()
    ks_rs = _dot_nt(kap_rt, s2.astype(BF16))
    ks, rs = ks_rs[:CHUNK], ks_rs[CHUNK:]
    yield
    rhs = ks + akv
    xr = _dot(x_inv, rhs.astype(BF16))
    e = rhs + jnp.where(lo, xr[:CHUNK], xr[CHUNK:])
    yield
    ev = jnp.concatenate([-e, v], axis=0).astype(BF16)
    upd = _dot_tn(ev, tails)
    send_state(s2 * decay + upd * headdiag)
    my = _dot(mats, ev)
    y = rs + jnp.where(lo, my[:CHUNK], my[CHUNK:])
    return y, bonus


RWKV_PAIRS_PER_STEP = 2
_RWKV_IN_REFS = 25
_RWKV_SCRATCH = 4
_MT_STRICT, _MT_STRICT_HI, _MT_INCL_WIDE = 0, 2, 4
_MT_BLOCKDIAG, _MT_HEADDIAG, _MT_HALVES, _MT_BANDS = 6, 7, 9, 10


def _rwkv_mask_tables():
    import numpy as np
    t = np.arange(CHUNK)[:, None]
    s = np.arange(CHUNK)[None, :]
    lane = np.arange(LANES)[None, :]
    table = np.zeros((_MT_BANDS, CHUNK, LANES), np.float32)
    tri = np.zeros((2, CHUNK, CHUNK), np.float32)
    for d, (strict, incl) in enumerate(((s < t, s <= t), (s > t, s >= t))):
        table[_MT_STRICT + d, :, :CHUNK] = strict
        table[_MT_STRICT_HI + d, :, CHUNK:] = strict
        table[_MT_INCL_WIDE + d] = np.concatenate([incl, incl], axis=1)
        tri[d] = incl
    table[_MT_BLOCKDIAG, :, :CHUNK] = (t // 16) == (s // 16)
    v = np.arange(LANES)[:, None]
    table[_MT_HEADDIAG:_MT_HEADDIAG + 2] = ((v < RWKV_HD) == (lane < RWKV_HD)).reshape(2, CHUNK, LANES)
    table[_MT_HALVES, 0] = lane[0] < RWKV_HD
    table[_MT_HALVES, 1] = lane[0] >= RWKV_HD
    return table.reshape(_MT_BANDS * CHUNK, LANES), tri


def _rwkv_kernel(*refs, need_ctx):
    n_pairs = RWKV_PAIRS_PER_STEP
    mt_ref, tri_ref = refs[:2]
    refs = refs[2:]
    n_in = n_pairs * _RWKV_IN_REFS
    rest = refs[n_in:]
    if need_ctx:
        y_l, y_c = rest[:2]
        scr = rest[2:]
    else:
        y_l, y_c = rest[0], None
        scr = rest[1:]
    nl = refs[0].shape[0] // CHUNK
    nc = refs[6].shape[0] // CHUNK

    def band(j, rows=CHUNK, lanes=LANES):
        return mt_ref[j * CHUNK:j * CHUNK + rows, :lanes]

    lo = lax.broadcasted_iota(jnp.int32, (1, LANES), 1) < RWKV_HD
    lo_f, hi_f = band(_MT_HALVES, 1), mt_ref[_MT_HALVES * CHUNK + 1:_MT_HALVES * CHUNK + 2, :]
    blockdiag16 = band(_MT_BLOCKDIAG, lanes=CHUNK)
    headdiag = band(_MT_HEADDIAG, rows=LANES)

    def consts(d):
        return (lo_f, hi_f, lo, blockdiag16, headdiag, tri_ref[d], band(_MT_STRICT + d, lanes=CHUNK),
                band(_MT_STRICT_HI + d), band(_MT_INCL_WIDE + d))
    consts_f, consts_b = consts(0), consts(1)

    pairs = []
    for g in range(n_pairs):
        r = refs[g * _RWKV_IN_REFS:(g + 1) * _RWKV_IN_REFS]
        (w0f, w0b, a0f, a0b, kkw, kaw, rkw, lng, lnb, wupf, wupb, aupf, aupb) = [x[0] for x in r[12:]]
        s = scr[g * 2 * _RWKV_SCRATCH:(g + 1) * 2 * _RWKV_SCRATCH]
        pairs.append(dict(
            lat=tuple(r[0:6]), ctx=tuple(r[6:12]),
            prm_f=(w0f, a0f, kkw, kaw, rkw, wupf, aupf),
            prm_b=(w0b, a0b, kkw, kaw, rkw, wupb, aupb),
            scr_lat=s[:_RWKV_SCRATCH], scr_ctx=s[_RWKV_SCRATCH:], lng=lng, lnb=lnb))

    def scan(which, n, carry):
        unroll = _unroll_factor(n, RWKV_UNROLL)

        def body(i, carry):
            states = {}
            gens = []
            for u in range(unroll):
                c = i * unroll + u
                for g, pr in enumerate(pairs):
                    states[(g, "f", -1)], states[(g, "b", -1)] = carry[2 * g], carry[2 * g + 1]
                    for d, cc, prm, consts in (("f", c, pr["prm_f"], consts_f),
                                               ("b", n - 1 - c, pr["prm_b"], consts_b)):
                        gens.append(_rwkv_chunk(
                            pr[which], cc, prm, consts,
                            functools.partial(states.get, (g, d, u - 1)),
                            functools.partial(states.__setitem__, (g, d, u))))
            outs = iter(_run_interleaved(gens))
            for u in range(unroll):
                c = i * unroll + u
                rows_f, rows_b = _chunk_rows(c), _chunk_rows(n - 1 - c)
                for pr in pairs:
                    yf, yb, bf, bb = pr["scr_" + which]
                    (y_f, bon_f), (y_b, bon_b) = next(outs), next(outs)
                    yf[rows_f, :] = y_f
                    bf[rows_f, :] = bon_f.astype(bf.dtype)
                    yb[rows_b, :] = y_b
                    bb[rows_b, :] = bon_b.astype(bb.dtype)
            return tuple(states[(g, d, unroll - 1)] for g in range(n_pairs) for d in ("f", "b"))
        return lax.fori_loop(0, n // unroll, body, carry)

    zero = jnp.zeros((LANES, LANES), F32)
    carry = scan("ctx", nc, (zero,) * (2 * n_pairs))
    scan("lat", nl, carry)

    head_mean = (headdiag * (1.0 / RWKV_HD)).astype(BF16)

    def seg_mean(x):
        hi = x.astype(BF16)
        lo_part = (x - hi.astype(F32)).astype(BF16)
        r = _dot(jnp.concatenate([hi, lo_part], axis=0), head_mean)
        return r[:CHUNK] + r[CHUNK:]

    def finish(which, y_out, n):
        def body(i, _):
            rows = _chunk_rows(i)
            for g, pr in enumerate(pairs):
                yf, yb, bf, bb = pr["scr_" + which]
                gs = pr[which][3]
                y = yf[rows, :] + yb[rows, :]
                d = y - seg_mean(y)
                var = seg_mean(d * d)
                yn = d * lax.rsqrt(var + GN_EPS) * pr["lng"] + pr["lnb"]
                bonus = bf[rows, :].astype(F32) + bb[rows, :].astype(F32)
                y_out[rows, g * LANES:(g + 1) * LANES] = (
                    (yn + bonus) * gs[rows, :].astype(F32)).astype(y_out.dtype)
            return 0
        lax.fori_loop(0, n, body, 0, unroll=_unroll_factor(n, FINISH_UNROLL))

    finish("lat", y_l, nl)
    if need_ctx:
        finish("ctx", y_c, nc)


def _rwkv(ps_lat, ps_ctx, batch, seq, ctx_len, prm, need_ctx):
    n_pairs = RWKV_PAIRS_PER_STEP

    def pair_specs(g):
        hp = lambda s: n_pairs * s + g

        def seg_specs(n):
            return [pl.BlockSpec((n, LANES), lambda b, s: (b, RW_R + hp(s))),
                    pl.BlockSpec((n, LANES), lambda b, s: (b, RW_K + hp(s))),
                    pl.BlockSpec((n, LANES), lambda b, s: (b, RW_V + hp(s))),
                    pl.BlockSpec((n, LANES), lambda b, s: (b, RW_GATE + hp(s))),
                    pl.BlockSpec((n, LANES), lambda b, s: (b, RW_WD)),
                    pl.BlockSpec((n, LANES), lambda b, s: (b, RW_AD))]
        vec_h = pl.BlockSpec((1, 1, LANES), lambda b, s: (hp(s), 0, 0))
        mat_h = pl.BlockSpec((1, LANES, LANES), lambda b, s: (hp(s), 0, 0))
        return seg_specs(seq) + seg_specs(ctx_len) + [vec_h] * 9 + [mat_h] * 4

    table, tri = _rwkv_mask_tables()
    in_specs = [pl.BlockSpec(table.shape, lambda b, s: (0, 0)),
                pl.BlockSpec(tri.shape, lambda b, s: (0, 0, 0))]
    args = [jnp.asarray(table), jnp.asarray(tri, BF16)]
    for g in range(n_pairs):
        in_specs += pair_specs(g)
        args += [ps_lat] * 6 + [ps_ctx] * 6 + list(prm)
    width = n_pairs * LANES
    out_specs = [pl.BlockSpec((seq, width), lambda b, s: (b, s))]
    out_shape = [jax.ShapeDtypeStruct((batch * seq, BRANCH_W), BF16)]
    if need_ctx:
        out_specs.append(pl.BlockSpec((ctx_len, width), lambda b, s: (b, s)))
        out_shape.append(jax.ShapeDtypeStruct((batch * ctx_len, BRANCH_W), BF16))
    def seg_scratch(n):
        return [pltpu.VMEM((n, LANES), F32)] * 2 + [pltpu.VMEM((n, LANES), BF16)] * 2
    scratch = (seg_scratch(seq) + seg_scratch(ctx_len)) * n_pairs
    outs = pl.pallas_call(
        functools.partial(_rwkv_kernel, need_ctx=need_ctx),
        grid=(batch, RWKV_H // 2 // n_pairs),
        in_specs=in_specs,
        out_specs=out_specs,
        out_shape=out_shape,
        scratch_shapes=scratch,
        compiler_params=_cparams(("parallel", "parallel")),
        name="rwkv",
    )(*args)
    return (outs[0], outs[1]) if need_ctx else (outs[0], None)


def _merge_kernel(x_ref, y1_ref, y2_ref, y3_ref, g1_ref, g2_ref, g3_ref, mod_ref,
                  wo1_ref, wo2_ref, wo3_ref, wout_ref, gp_ref, o_ref):
    m = (_sigmoid(g1_ref[...].astype(F32)) * _dot(y1_ref[...], wo1_ref[...])
         + _sigmoid(g2_ref[...].astype(F32)) * _dot(y2_ref[...], wo2_ref[...])
         + _sigmoid(g3_ref[...].astype(F32)) * _dot(y3_ref[...], wo3_ref[...]))
    mo = _dot(m.astype(BF16), wout_ref[...])
    ms = jnp.mean(mo * mo, axis=-1, keepdims=True)
    out = mo * lax.rsqrt(ms + NORM_EPS) * gp_ref[...]
    o_ref[...] = x_ref[...] + mod_ref[0][2:3] * out


def _merge(x2d, y1, y2, y3, p, mod, mod_index, wo1, wo2, wo3, wout, g_post, tm):
    r, d = x2d.shape
    row = lambda i: (i, 0)
    const = lambda i: (0, 0)
    return pl.pallas_call(
        _merge_kernel,
        grid=(r // tm,),
        in_specs=[pl.BlockSpec((tm, d), row),
                  pl.BlockSpec((tm, BRANCH_W), row),
                  pl.BlockSpec((tm, BRANCH_W), row),
                  pl.BlockSpec((tm, BRANCH_W), row),
                  pl.BlockSpec((tm, d), lambda i: (i, 0)),
                  pl.BlockSpec((tm, d), lambda i: (i, 1)),
                  pl.BlockSpec((tm, d), lambda i: (i, 2)),
                  pl.BlockSpec((1, 3, d), lambda i: (mod_index(i), 0, 0)),
                  pl.BlockSpec((BRANCH_W, d), const),
                  pl.BlockSpec((BRANCH_W, d), const),
                  pl.BlockSpec((BRANCH_W, d), const),
                  pl.BlockSpec((d, d), const),
                  pl.BlockSpec((1, d), const)],
        out_specs=pl.BlockSpec((tm, d), row),
        out_shape=jax.ShapeDtypeStruct((r, d), F32),
        compiler_params=_cparams(("parallel",)),
        name="merge",
    )(x2d, y1, y2, y3, p, p, p, mod, wo1, wo2, wo3, wout, g_post.reshape(1, d))


def _pack_w_in(w_in):
    d = w_in.shape[0]
    o = 0
    gq = w_in[:, o:o + 256]; o += 256
    gk = w_in[:, o:o + 256]; o += 256
    gv = w_in[:, o:o + 512]; o += 512
    gg = w_in[:, o:o + 512]; o += 512
    gwd = w_in[:, o:o + 32]; o += 32
    att = w_in[:, o:o + 1536]; o += 1536
    rw = w_in[:, o:o + 2304]; o += 2304
    mg = w_in[:, o:o + 3072]
    n_qk = (ATT_H + ATT_KV) * ATT_HD
    att_qk = att[:, :n_qk].reshape(d, ATT_H + ATT_KV, ATT_HD)[:, :, _rope_lane_order()].reshape(d, n_qk)
    att = jnp.concatenate([att_qk, att[:, n_qk:]], axis=1)
    pad_wd = jnp.zeros((d, LANES - 32), w_in.dtype)
    pad_end = jnp.zeros((d, LANES), w_in.dtype)
    return jnp.concatenate([mg, att, rw, gq, gk, gv, gg, gwd, pad_wd, pad_end], axis=1).astype(BF16)


def _rope_tables(seq):
    quarter = ATT_HD // 4
    inv = ROPE_THETA ** (-jnp.arange(quarter, dtype=F32) / quarter)
    t = jnp.arange(seq)
    row_pos = (t // GRID_W).astype(F32)
    col_pos = (t % GRID_W).astype(F32)
    ar = row_pos[:, None] * inv[None, :]
    ac = col_pos[:, None] * inv[None, :]
    cos = jnp.concatenate([jnp.cos(ar), jnp.cos(ac), jnp.cos(ar), jnp.cos(ac)], axis=1)
    sin = jnp.concatenate([-jnp.sin(ar), -jnp.sin(ac), jnp.sin(ar), jnp.sin(ac)], axis=1)
    return cos, sin


def _rope_lane_order():
    quarter = ATT_HD // 4
    i = jnp.arange(quarter)
    return jnp.concatenate([i, 2 * quarter + i, quarter + i, 3 * quarter + i])


def _pad_rows(w, lo_half):
    z = jnp.zeros((LANES - w.shape[0], w.shape[1]), w.dtype)
    if lo_half:
        return jnp.concatenate([w, z], axis=0)
    half = LANES // 2
    return jnp.concatenate([z[:half], w, z[half:]], axis=0)


def _pick(n, target):
    t = min(n, target)
    while n % t:
        t //= 2
    return t


def kernel(x, c, ctx, c_ctx, w_mod, b_mod, g_pre, w_in, gla_wup_f, gla_b_f, gla_wup_b, gla_b_b, gla_norm, att_qnorm, att_knorm, rwkv_mu, rwkv_w0_f, rwkv_wup_f, rwkv_w0_b, rwkv_wup_b, rwkv_a0_f, rwkv_aup_f, rwkv_a0_b, rwkv_aup_b, rwkv_kk, rwkv_ka, rwkv_rk, rwkv_ln_g, rwkv_ln_b, w_o_gla, w_o_att, w_o_rwkv, w_out, g_post):
    batch, seq, d = x.shape
    ctx_len = ctx.shape[1]
    depth = w_in.shape[0]
    assert seq % CHUNK == 0 and ctx_len % CHUNK == 0 and seq % GRID_W == 0

    nb = -(-(batch + 1) // 8) * 8
    cond = jnp.concatenate([c, c_ctx[None, :], jnp.zeros((nb - batch - 1, d), F32)], axis=0)
    tables = _rope_tables(seq)

    tm_lat = _pick(seq, 1024)
    tm_ctx = _pick(batch * ctx_len, 1024)

    def lat_mod(tm):
        return lambda i: i // (seq // tm)

    def ctx_mod(tm):
        return lambda i: batch
    tq_lat, tk = _pick(seq, 512), _pick(seq, 512)
    tq_ctx = _pick(ctx_len, 256)
    tm_prep = _pick(seq, 512)
    tm_prep_c = _pick(ctx_len, 512)
    tm_merge = _pick(seq, 1024)
    tm_merge_c = _pick(batch * ctx_len, 1024)

    x2 = x.reshape(batch * seq, d)
    xc2 = ctx.reshape(batch * ctx_len, d)

    def hp(v):
        return v.reshape(RWKV_H // 2, 1, LANES)

    for l in range(depth):
        need_ctx = l < depth - 1
        mod = _modulation(cond, w_mod[l], b_mod[l]).reshape(nb, 3, d)
        w_cat = _pack_w_in(w_in[l])
        p_lat = _inproj(x2, mod, lat_mod(tm_lat), g_pre[l], w_cat, tm_lat)
        p_ctx = _inproj(xc2, mod, ctx_mod(tm_ctx), g_pre[l], w_cat, tm_ctx)

        gains = jnp.stack([att_qnorm[l] * (ATT_HD ** -0.5 * LOG2_E), att_knorm[l]])[:, _rope_lane_order()]
        att_lat = tuple(_qkvprep(p_lat, gains, tables, batch, seq, tm_prep)) + (seq,)
        att_ctx = tuple(_qkvprep(p_ctx, gains, None, batch, ctx_len, tm_prep_c)) + (ctx_len,)
        ps_lat = _rwkv_shift(p_lat, rwkv_mu[l], seq, tm_prep)
        ps_ctx = _rwkv_shift(p_ctx, rwkv_mu[l], ctx_len, tm_prep_c)
        y2 = _flash(att_lat[0], p_lat, seq, att_lat, att_ctx, batch, tq_lat, tk)
        y2c = None
        if need_ctx:
            y2c = _flash(att_ctx[0], p_ctx, ctx_len, None, att_ctx, batch, tq_ctx, tk)

        def gla_w(w, lo_rows):
            w = w.reshape(GLA_LR, GLA_H // 2, LANES).transpose(1, 0, 2)
            z = jnp.zeros((GLA_H // 2, LANES, LANES), w.dtype)
            start = 0 if lo_rows else GLA_LR
            return lax.dynamic_update_slice(z, w, (0, start, 0)).astype(BF16)
        y1, y1c = _gla(p_lat, p_ctx, batch, seq, ctx_len,
                       gla_w(gla_wup_f[l], True), gla_w(gla_wup_b[l], False),
                       gla_b_f[l].reshape(GLA_H // 2, 1, LANES), gla_b_b[l].reshape(GLA_H // 2, 1, LANES),
                       gla_norm[l].reshape(GLA_H, 1, GLA_DV), need_ctx)

        def rw_w(w, lo_rows):
            w = w.reshape(RWKV_LR, RWKV_H // 2, LANES).transpose(1, 0, 2)
            return jnp.stack([_pad_rows(w[i], lo_rows) for i in range(RWKV_H // 2)]).astype(BF16)
        prm = (hp(rwkv_w0_f[l]), hp(rwkv_w0_b[l]), hp(rwkv_a0_f[l]), hp(rwkv_a0_b[l]),
               hp(rwkv_kk[l]), hp(rwkv_ka[l]), hp(rwkv_rk[l]), hp(rwkv_ln_g[l]), hp(rwkv_ln_b[l]),
               rw_w(rwkv_wup_f[l], True), rw_w(rwkv_wup_b[l], False),
               rw_w(rwkv_aup_f[l], True), rw_w(rwkv_aup_b[l], False))
        y3, y3c = _rwkv(ps_lat, ps_ctx, batch, seq, ctx_len, prm, need_ctx)

        wo1, wo2, wo3 = (w_o_gla[l].astype(BF16), w_o_att[l].astype(BF16),
                         w_o_rwkv[l].astype(BF16))
        wout = w_out[l].astype(BF16)
        x2 = _merge(x2, y1, y2, y3, p_lat, mod, lat_mod(tm_merge), wo1, wo2, wo3, wout, g_post[l], tm_merge)
        if need_ctx:
            xc2 = _merge(xc2, y1c, y2c, y3c, p_ctx, mod, ctx_mod(tm_merge_c), wo1, wo2, wo3, wout,
                         g_post[l], tm_merge_c)
    return x2.reshape(batch, seq, d)
```

```python
import functools

import jax
import jax.numpy as jnp
from jax import lax
from jax.experimental import pallas as pl
from jax.experimental.pallas import tpu as pltpu

F32 = jnp.float32
BF16 = jnp.bfloat16

GRID_W = 64
BRANCH_W = 512
GLA_H, GLA_DK, GLA_DV, GLA_LR = 4, 64, 128, 16
GLA_TAU = 16.0
ATT_H, ATT_KV, ATT_HD = 4, 2, 128
ROPE_THETA = 10000.0
RWKV_H, RWKV_HD, RWKV_LR = 8, 64, 64
RWKV_DECAY_SCALE = 0.6065306597
NORM_EPS = 1e-6
GN_EPS = 64e-5
L2_EPS = 1e-12
LOG2_E = 1.4426950408889634

CHUNK = 64
LANES = 128
SUBLANES = 8
BF16_ROWS = 16
MATMUL_ROWS = 1024
PREP_ROWS = 512
FLASH_ROWS = 512
MOD_COLS = 768
VMEM_CAPACITY = 64 * 1024 * 1024
VMEM_LIMIT = VMEM_CAPACITY * 7 // 8

CB_MG = 0
CB_ATT_Q = 24
CB_ATT_V = 30
CB_ATT_GATE = 32
CB_RW = 36
RW_BLOCKS = 18
CB_GLA_Q = 54
CB_GLA_K = 56
CB_GLA_V = 58
CB_GLA_GATE = 62
CB_GLA_WD = 66
NP_BLOCKS = 68
RW_R, RW_K, RW_V, RW_GATE, RW_WD, RW_AD = 0, 4, 8, 12, 16, 17
NP = NP_BLOCKS * LANES
TN_PROJ = NP // 4


def _cparams(sem):
    return pltpu.CompilerParams(dimension_semantics=sem, vmem_limit_bytes=VMEM_LIMIT)


def _dot(a, b):
    return jnp.dot(a, b, preferred_element_type=F32)


def _dot_nt(a, b):
    return lax.dot_general(a, b, (((1,), (1,)), ((), ())), preferred_element_type=F32)


def _dot_tn(a, b):
    return lax.dot_general(a, b, (((0,), (0,)), ((), ())), preferred_element_type=F32)


def _sigmoid(x):
    return 0.5 + 0.5 * jnp.tanh(0.5 * x)


def _silu(x):
    h = 0.5 * x
    return h + h * jnp.tanh(h)


def _tri_cumsum(tri, x):
    hi = x.astype(BF16)
    lo = (x - hi.astype(F32)).astype(BF16)
    n = x.shape[1]
    if n % LANES:
        return _dot(tri, hi) + _dot(tri, lo)
    r = _dot(tri, jnp.concatenate([hi, lo], axis=1))
    return r[:, :n] + r[:, n:]


GLA_UNROLL = 16
RWKV_UNROLL = 2
FINISH_UNROLL = 8


def _unroll_factor(n_chunks, target):
    u = target
    while n_chunks % u:
        u //= 2
    return u


def _mod_kernel(c_ref, w_ref, b_ref, o_ref):
    c = c_ref[...]
    s = _silu(c)
    o_ref[...] = jnp.dot(s, w_ref[...], preferred_element_type=F32,
                         precision=lax.Precision.HIGHEST) + b_ref[...]


def _modulation(cond, w_mod, b_mod):
    nb, d = cond.shape
    n = w_mod.shape[1]
    tn = _pick(n, MOD_COLS)
    return pl.pallas_call(
        _mod_kernel,
        grid=(n // tn,),
        in_specs=[pl.BlockSpec((nb, d), lambda j: (0, 0)),
                  pl.BlockSpec((d, tn), lambda j: (0, j)),
                  pl.BlockSpec((1, tn), lambda j: (0, j))],
        out_specs=pl.BlockSpec((nb, tn), lambda j: (0, j)),
        out_shape=jax.ShapeDtypeStruct((nb, n), F32),
        compiler_params=_cparams(("parallel",)),
        name="modulation",
    )(cond, w_mod, b_mod.reshape(1, n))


def _inproj_kernel(x_ref, mod_ref, g_ref, w_ref, o_ref, h_ref):
    @pl.when(pl.program_id(1) == 0)
    def _():
        x = x_ref[...]
        ms = jnp.mean(x * x, axis=-1, keepdims=True)
        y = x * lax.rsqrt(ms + NORM_EPS) * g_ref[...]
        mod = mod_ref[0]
        h_ref[...] = (y * (1.0 + mod[1:2]) + mod[0:1]).astype(BF16)

    o_ref[...] = _dot(h_ref[...], w_ref[...]).astype(o_ref.dtype)


def _inproj(x2d, mod, mod_index, g_pre, w_cat, tm):
    r, d = x2d.shape
    return pl.pallas_call(
        _inproj_kernel,
        grid=(r // tm, NP // TN_PROJ),
        in_specs=[pl.BlockSpec((tm, d), lambda i, j: (i, 0)),
                  pl.BlockSpec((1, 3, d), lambda i, j: (mod_index(i), 0, 0)),
                  pl.BlockSpec((1, d), lambda i, j: (0, 0)),
                  pl.BlockSpec((d, TN_PROJ), lambda i, j: (0, j))],
        out_specs=pl.BlockSpec((tm, TN_PROJ), lambda i, j: (i, j)),
        out_shape=jax.ShapeDtypeStruct((r, NP), BF16),
        scratch_shapes=[pltpu.VMEM((tm, d), BF16)],
        compiler_params=_cparams(("parallel", "arbitrary")),
        name="inproj",
    )(x2d, mod, g_pre.reshape(1, d), w_cat)


def _norm_rope(x, gain, tables):
    x = x.astype(F32)
    ms = jnp.mean(x * x, axis=-1, keepdims=True)
    y = x * lax.rsqrt(ms + NORM_EPS) * gain
    if tables is not None:
        cos, sin = tables
        y = y * cos + pltpu.roll(y, ATT_HD // 2, 1) * sin
    return y.astype(BF16)


ATT_QKV_BLOCKS = ATT_H + 2 * ATT_KV


def _qkvprep_kernel(*refs, rope):
    if rope:
        x_ref, g_ref, cos_ref, sin_ref, qk_ref, vt_ref = refs
        tables = (cos_ref[...], sin_ref[...])
    else:
        x_ref, g_ref, qk_ref, vt_ref = refs
        tables = None
    for h in range(ATT_H + ATT_KV):
        cols = slice(h * LANES, (h + 1) * LANES)
        gain = g_ref[0:1, :] if h < ATT_H else g_ref[1:2, :]
        qk_ref[:, cols] = _norm_rope(x_ref[:, cols], gain, tables)
    for j in range(ATT_KV):
        cols = slice((ATT_H + ATT_KV + j) * LANES, (ATT_H + ATT_KV + j + 1) * LANES)
        base = j * VT_ROWS
        vt_ref[base:base + ATT_HD, :] = x_ref[:, cols].astype(F32).T.astype(BF16)
        vt_ref[base + ATT_HD:base + VT_ROWS, :] = jnp.ones((VT_ROWS - ATT_HD, vt_ref.shape[1]), BF16)


VT_ROWS = ATT_HD + BF16_ROWS


def _qkvprep(p, gains, tables, batch, seq, tm):
    nt = seq // tm
    width = ATT_QKV_BLOCKS * LANES
    cb = CB_ATT_Q * LANES // width
    in_specs = [pl.BlockSpec((tm, width), lambda i: (i, cb)),
                pl.BlockSpec((2, LANES), lambda i: (0, 0))]
    args = [p, gains]
    if tables is not None:
        for tab in tables:
            in_specs.append(pl.BlockSpec((tm, LANES), lambda i: (i % nt, 0)))
            args.append(tab)
    qk_width = (ATT_H + ATT_KV) * LANES
    return pl.pallas_call(
        functools.partial(_qkvprep_kernel, rope=tables is not None),
        grid=(batch * nt,),
        in_specs=in_specs,
        out_specs=[pl.BlockSpec((tm, qk_width), lambda i: (i, 0)),
                   pl.BlockSpec((ATT_KV * VT_ROWS, tm), lambda i: (i // nt, i % nt))],
        out_shape=[jax.ShapeDtypeStruct((batch * seq, qk_width), BF16),
                   jax.ShapeDtypeStruct((batch * ATT_KV * VT_ROWS, seq), BF16)],
        compiler_params=_cparams(("parallel",)),
        name="qkvprep_rope" if tables is not None else "qkvprep",
    )(*args)


FLASH_KV_INTERLEAVE = 8


def _flash_kernel(*refs, n_lat_blocks, tk, has_ctx_kv):
    refs = list(refs)
    q0_ref, q1_ref, g0_ref, g1_ref = refs[:4]
    pos = 4
    if n_lat_blocks:
        kl_ref, vtl_ref = refs[pos:pos + 2]
        pos += 2
    if has_ctx_kv:
        kc_ref, vtc_ref = refs[pos:pos + 2]
        pos += 2
    o_ref = refs[pos]
    tq = q0_ref.shape[0]
    q = jnp.concatenate([q0_ref[...], q1_ref[...]], axis=0)

    def blocks(kvs, carry):
        scores = [_dot_nt(k, q) for k, _ in kvs]
        m, acc = carry
        for s, (_, vt) in zip(scores, kvs):
            m_new = jnp.maximum(m, jnp.max(s, axis=0, keepdims=True))
            p = jnp.exp2(s - m_new)
            acc = jnp.exp2(m - m_new) * acc + _dot(vt, p.astype(BF16))
            m = m_new
        return m, acc

    carry = (jnp.full((1, 2 * tq), -jnp.inf, F32), jnp.zeros((VT_ROWS, 2 * tq), F32))
    if n_lat_blocks:
        nb = _unroll_factor(n_lat_blocks, FLASH_KV_INTERLEAVE)

        def body(i, c):
            kvs = []
            for u in range(nb):
                start = pl.multiple_of((i * nb + u) * tk, tk)
                kvs.append((kl_ref[pl.ds(start, tk), :], vtl_ref[:, pl.ds(start, tk)]))
            return blocks(kvs, c)
        carry = lax.fori_loop(0, n_lat_blocks // nb, body, carry)
    if has_ctx_kv:
        carry = blocks([(kc_ref[...], vtc_ref[...])], carry)
    _, acc = carry
    o = (acc[:ATT_HD] / acc[ATT_HD:ATT_HD + 1]).T
    o_ref[:, :LANES] = (o[:tq] * _silu(g0_ref[...].astype(F32))).astype(o_ref.dtype)
    o_ref[:, LANES:] = (o[tq:] * _silu(g1_ref[...].astype(F32))).astype(o_ref.dtype)


def _flash(qk_q, p_q, seq_q, lat_kv, ctx_kv, batch, tq, tk):
    nq = seq_q // tq
    in_specs = [
        pl.BlockSpec((tq, LANES), lambda b, j, i: (b * nq + i, 2 * j)),
        pl.BlockSpec((tq, LANES), lambda b, j, i: (b * nq + i, 2 * j + 1)),
        pl.BlockSpec((tq, LANES), lambda b, j, i: (b * nq + i, CB_ATT_GATE + 2 * j)),
        pl.BlockSpec((tq, LANES), lambda b, j, i: (b * nq + i, CB_ATT_GATE + 2 * j + 1)),
    ]
    args = [qk_q, qk_q, p_q, p_q]
    n_lat_blocks = 0
    for seg in (lat_kv, ctx_kv):
        if seg is None:
            continue
        qk_s, vt_s, seq_s = seg
        in_specs.append(pl.BlockSpec((seq_s, LANES), lambda b, j, i: (b, ATT_H + j)))
        in_specs.append(pl.BlockSpec((VT_ROWS, seq_s), lambda b, j, i: (b * ATT_KV + j, 0)))
        args += [qk_s, vt_s]
    if lat_kv is not None:
        n_lat_blocks = lat_kv[2] // tk
    return pl.pallas_call(
        functools.partial(_flash_kernel, n_lat_blocks=n_lat_blocks, tk=tk,
                          has_ctx_kv=ctx_kv is not None),
        grid=(batch, ATT_KV, nq),
        in_specs=in_specs,
        out_specs=pl.BlockSpec((tq, 2 * LANES), lambda b, j, i: (b * nq + i, j)),
        out_shape=jax.ShapeDtypeStruct((batch * seq_q, BRANCH_W), BF16),
        compiler_params=_cparams(("parallel", "parallel", "arbitrary")),
        name="flash_lat" if lat_kv is not None else "flash_ctx",
    )(*args)


def _chunk_rows(c):
    return pl.ds(pl.multiple_of(c * CHUNK, CHUNK), CHUNK)


def _gla_chunk(seg, c, wup, bias, consts, recv_state, send_state):
    q_ref, k_ref, v_refs, wd_ref = seg
    tri, incl2, halves = consts
    rows = _chunk_rows(c)
    q = q_ref[rows, :].astype(F32) * (GLA_DK ** -0.5)
    k = k_ref[rows, :].astype(F32)
    vs = [v_ref[rows, :] for v_ref in v_refs]
    z = _dot(wd_ref[rows, :], wup) + bias
    yield
    g = (jnp.minimum(z, 0.0) - jnp.log1p(jnp.exp(-jnp.abs(z)))) * (1.0 / GLA_TAU)
    gc = _tri_cumsum(tri, g)
    yield
    tot = jnp.sum(g, axis=0, keepdims=True)
    q_dec = q * jnp.exp(gc)
    k_inv = (k * jnp.exp(-gc)).astype(BF16)
    k_tail = k * jnp.exp(tot - gc)
    q_heads = [(q_dec * hm).astype(BF16) for hm in halves]
    a = (_dot_nt(jnp.concatenate(q_heads, axis=0), k_inv) * incl2).astype(BF16)
    upds = [_dot_tn(v, (k_tail * hm).astype(BF16)) for v, hm in zip(vs, halves)]
    yield
    sts = recv_state()
    while sts is None:
        yield
        sts = recv_state()
    decay = jnp.exp(tot)
    send_state(tuple(st * decay + upd for st, upd in zip(sts, upds)))
    return tuple(_dot(a[h * CHUNK:(h + 1) * CHUNK], vs[h]) + _dot_nt(q_heads[h], sts[h].astype(BF16))
                 for h in range(2))


_GLA_SEG_REFS = 7


def _gla_kernel(*refs, need_ctx):
    mt_ref, tri_ref = refs[:2]
    lat = refs[2:2 + _GLA_SEG_REFS]
    ctx = refs[2 + _GLA_SEG_REFS:2 + 2 * _GLA_SEG_REFS]
    wupf_ref, wupb_ref, bf_ref, bb_ref, gn_ref = refs[2 + 2 * _GLA_SEG_REFS:7 + 2 * _GLA_SEG_REFS]
    rest = refs[7 + 2 * _GLA_SEG_REFS:]
    if need_ctx:
        y_l, y_c = rest[:2]
        scr = rest[2:]
    else:
        y_l, y_c = rest[0], None
        scr = rest[1:]
    scr_l, scr_c = scr[:4], scr[4:]
    nl = lat[0].shape[0] // CHUNK
    nc = ctx[0].shape[0] // CHUNK
    wupf, wupb = wupf_ref[0], wupb_ref[0]
    bias_f, bias_b = bf_ref[0], bb_ref[0]
    row0 = _MT_HALVES * CHUNK
    halves = (mt_ref[row0:row0 + 1, :], mt_ref[row0 + 1:row0 + 2, :])

    def consts(d):
        incl = mt_ref[(_MT_INCL_WIDE + d) * CHUNK:(_MT_INCL_WIDE + d + 1) * CHUNK, :CHUNK]
        return tri_ref[d], jnp.concatenate([incl, incl], axis=0), halves
    consts_f, consts_b = consts(0), consts(1)

    def scan(seg, scr, n, carry):
        unroll = _unroll_factor(n, GLA_UNROLL)
        seg = (seg[0], seg[1], seg[2:4], seg[6])

        def body(i, carry):
            states = {("f", -1): carry[0], ("b", -1): carry[1]}
            gens = []
            for u in range(unroll):
                c = i * unroll + u
                for d, cc, wup, bias, cst in (("f", c, wupf, bias_f, consts_f),
                                              ("b", n - 1 - c, wupb, bias_b, consts_b)):
                    gens.append(_gla_chunk(
                        seg, cc, wup, bias, cst,
                        functools.partial(states.get, (d, u - 1)),
                        functools.partial(states.__setitem__, (d, u))))
            outs = _run_interleaved(gens)
            for u in range(unroll):
                c = i * unroll + u
                for h in range(2):
                    scr[h][_chunk_rows(c), :] = outs[2 * u][h]
                    scr[2 + h][_chunk_rows(n - 1 - c), :] = outs[2 * u + 1][h]
            return states[("f", unroll - 1)], states[("b", unroll - 1)]
        return lax.fori_loop(0, n // unroll, body, carry)

    zero = (jnp.zeros((GLA_DV, LANES), F32),) * 2
    carry = scan(ctx, scr_c, nc, (zero, zero))
    scan(lat, scr_l, nl, carry)

    def finish(seg, scr, y, n):
        def body(i, _):
            rows = _chunk_rows(i)
            for h in range(2):
                o = scr[h][rows, :] + scr[2 + h][rows, :]
                ms = jnp.mean(o * o, axis=-1, keepdims=True)
                o = o * lax.rsqrt(ms + NORM_EPS) * gn_ref[h]
                gate = seg[4 + h][rows, :].astype(F32)
                y[rows, h * LANES:(h + 1) * LANES] = (o * _silu(gate)).astype(y.dtype)
            return 0
        lax.fori_loop(0, n, body, 0, unroll=_unroll_factor(n, FINISH_UNROLL))

    finish(lat, scr_l, y_l, nl)
    if need_ctx:
        finish(ctx, scr_c, y_c, nc)


def _gla(p_lat, p_ctx, batch, seq, ctx_len, wupf, wupb, bf, bb, gnorm, need_ctx):
    def seg_specs(n):
        return [pl.BlockSpec((n, LANES), lambda b, g: (b, CB_GLA_Q + g)),
                pl.BlockSpec((n, LANES), lambda b, g: (b, CB_GLA_K + g)),
                pl.BlockSpec((n, LANES), lambda b, g: (b, CB_GLA_V + 2 * g)),
                pl.BlockSpec((n, LANES), lambda b, g: (b, CB_GLA_V + 2 * g + 1)),
                pl.BlockSpec((n, LANES), lambda b, g: (b, CB_GLA_GATE + 2 * g)),
                pl.BlockSpec((n, LANES), lambda b, g: (b, CB_GLA_GATE + 2 * g + 1)),
                pl.BlockSpec((n, LANES), lambda b, g: (b, CB_GLA_WD))]
    table, tri = _rwkv_mask_tables()
    w_spec = pl.BlockSpec((1, LANES, LANES), lambda b, g: (g, 0, 0))
    b_spec = pl.BlockSpec((1, 1, LANES), lambda b, g: (g, 0, 0))
    g_spec = pl.BlockSpec((2, 1, GLA_DV), lambda b, g: (g, 0, 0))
    out_specs = [pl.BlockSpec((seq, 2 * LANES), lambda b, g: (b, g))]
    out_shape = [jax.ShapeDtypeStruct((batch * seq, BRANCH_W), BF16)]
    if need_ctx:
        out_specs.append(pl.BlockSpec((ctx_len, 2 * LANES), lambda b, g: (b, g)))
        out_shape.append(jax.ShapeDtypeStruct((batch * ctx_len, BRANCH_W), BF16))
    outs = pl.pallas_call(
        functools.partial(_gla_kernel, need_ctx=need_ctx),
        grid=(batch, GLA_H // 2),
        in_specs=([pl.BlockSpec(table.shape, lambda b, g: (0, 0)),
                   pl.BlockSpec(tri.shape, lambda b, g: (0, 0, 0))]
                  + seg_specs(seq) + seg_specs(ctx_len) + [w_spec, w_spec, b_spec, b_spec, g_spec]),
        out_specs=out_specs,
        out_shape=out_shape,
        scratch_shapes=[pltpu.VMEM((seq, LANES), F32)] * 4 + [pltpu.VMEM((ctx_len, LANES), F32)] * 4,
        compiler_params=_cparams(("parallel", "parallel")),
        name="gla",
    )(jnp.asarray(table), jnp.asarray(tri, BF16), *([p_lat] * _GLA_SEG_REFS + [p_ctx] * _GLA_SEG_REFS),
      wupf, wupb, bf, bb, gnorm)
    return (outs[0], outs[1]) if need_ctx else (outs[0], None)


def _seg_sum(x, lo):
    s0 = jnp.sum(jnp.where(lo, x, 0.0), axis=-1, keepdims=True)
    s1 = jnp.sum(jnp.where(lo, 0.0, x), axis=-1, keepdims=True)
    return jnp.where(lo, s0, s1)


def _run_interleaved(gens):
    gens = list(gens)
    out = [None] * len(gens)
    live = list(range(len(gens)))
    while live:
        still = []
        for j in live:
            try:
                next(gens[j])
                still.append(j)
            except StopIteration as stop:
                out[j] = stop.value
        live = still
    return out


def _unit_tri_inverse_offdiag(mats, blockdiag):
    def mm(x, y):
        return _dot(x.astype(BF16), y.astype(BF16))

    def mm2(x1, x2, y):
        r = mm(jnp.concatenate([x1, x2], axis=0), y)
        return r[:CHUNK], r[CHUNK:]

    nd = [-(a * blockdiag) for a in mats]
    no = [-a - d for a, d in zip(mats, nd)]
    n2 = [mm(d, d) for d in nd]
    yield
    r = [mm2(s, d, s) for d, s in zip(nd, n2)]
    n4 = [ri[0] for ri in r]
    x = [d + s + ri[1] for d, s, ri in zip(nd, n2, r)]
    yield
    r = [mm2(s, xi, s) for xi, s in zip(x, n4)]
    n8 = [ri[0] for ri in r]
    x = [xi + s + ri[1] for xi, s, ri in zip(x, n4, r)]
    yield
    xd = [xi + s + mm(xi, s) for xi, s in zip(x, n8)]
    yield
    m = [o + mm(xi, o) for xi, o in zip(xd, no)]
    yield
    m2 = [mm(mi, mi) for mi in m]
    yield
    xq = [mi + s + mm(mi, s) for mi, s in zip(m, m2)]
    yield
    return [q + d + mm(q, d) for q, d in zip(xq, xd)]


SHIFT_COLS = RW_BLOCKS * LANES


def _rwkv_shift_kernel(cur_ref, prev_ref, next_ref, mu_ref, o_ref, *, tiles_per_seq):
    t = pl.program_id(0) % tiles_per_seq
    x = cur_ref[...].astype(F32)
    tm = x.shape[0]
    prev_row = jnp.where(t > 0, prev_ref[...].astype(F32)[BF16_ROWS - 1:BF16_ROWS, :], 0.0)
    next_row = jnp.where(t < tiles_per_seq - 1, next_ref[...].astype(F32)[0:1, :], 0.0)
    row8 = lax.broadcasted_iota(jnp.int32, (8, x.shape[1]), 0)
    up = pltpu.roll(x, 1, 0)
    up = jnp.concatenate([jnp.where(row8 == 0, prev_row, up[:8]), up[8:]], axis=0)
    dn = pltpu.roll(x, tm - 1, 0)
    dn = jnp.concatenate([dn[:tm - 8], jnp.where(row8 == 7, next_row, dn[tm - 8:])], axis=0)
    mu = mu_ref[...]
    y = x * (1.0 - mu) + (up + dn) * (0.5 * mu)
    g0, w0 = RW_GATE * LANES, RW_WD * LANES
    o_ref[:, :g0] = y[:, :g0].astype(o_ref.dtype)
    o_ref[:, g0:w0] = _silu(y[:, g0:w0]).astype(o_ref.dtype)
    o_ref[:, w0:w0 + LANES] = jnp.tanh(y[:, w0:w0 + LANES]).astype(o_ref.dtype)
    o_ref[:, w0 + LANES:] = y[:, w0 + LANES:].astype(o_ref.dtype)


def _rwkv_shift(p, mu, seq, tm):
    r = p.shape[0]
    tiles_per_seq = seq // tm
    sub = tm // BF16_ROWS
    last = r // BF16_ROWS - 1
    cb = CB_RW * LANES // SHIFT_COLS
    return pl.pallas_call(
        functools.partial(_rwkv_shift_kernel, tiles_per_seq=tiles_per_seq),
        grid=(r // tm,),
        in_specs=[pl.BlockSpec((tm, SHIFT_COLS), lambda i: (i, cb)),
                  pl.BlockSpec((BF16_ROWS, SHIFT_COLS), lambda i: (jnp.maximum(i * sub - 1, 0), cb)),
                  pl.BlockSpec((BF16_ROWS, SHIFT_COLS), lambda i: (jnp.minimum((i + 1) * sub, last), cb)),
                  pl.BlockSpec((1, SHIFT_COLS), lambda i: (0, 0))],
        out_specs=pl.BlockSpec((tm, SHIFT_COLS), lambda i: (i, 0)),
        out_shape=jax.ShapeDtypeStruct((r, RW_BLOCKS * LANES), BF16),
        compiler_params=_cparams(("parallel",)),
        name="rwkv_shift",
    )(p, p, p, mu.reshape(1, RW_BLOCKS * LANES))


def _rwkv_chunk(seg, c, prm, consts, recv_state, send_state):
    (r_ref, k_ref, v_ref, _, wd_ref, ad_ref) = seg
    (w0, a0, kkw, kaw, rkw, wup, aup) = prm
    (lo_f, hi_f, lo, blockdiag16, headdiag, tri, strict, strict_hi, incl_wide) = consts
    rows = _chunk_rows(c)

    r = r_ref[rows, :].astype(F32)
    k = k_ref[rows, :].astype(F32)
    v16 = v_ref[rows, :]
    v = v16.astype(F32)
    z_w = _dot(wd_ref[rows, :], wup)
    z_a = _dot(ad_ref[rows, :], aup)
    yield
    lw = -RWKV_DECAY_SCALE * _sigmoid(w0 + z_w)
    a = _sigmoid(a0 + z_a)
    kd = k * (1.0 + (a - 1.0) * kaw)
    kk = k * kkw
    kk = kk * lax.rsqrt(_seg_sum(kk * kk, lo) + L2_EPS)
    kka = kk * a
    bonus = _seg_sum(r * kd * rkw, lo) * v

    cw = _tri_cumsum(tri, lw)
    yield
    tot = jnp.sum(lw, axis=0, keepdims=True)
    e_neg = jnp.exp(-cw)
    e_tail = jnp.exp(tot - cw)
    kap = kk * jnp.exp(cw - lw)
    rt = r * jnp.exp(cw)
    bt = (kka * e_neg).astype(BF16)
    kt = (kd * e_neg).astype(BF16)
    tails = jnp.concatenate([kka * e_tail, kd * e_tail], axis=0).astype(BF16)
    decay = jnp.exp(tot)
    kap_rt = jnp.concatenate([kap, rt], axis=0)

    yt = jnp.concatenate([bt, kt], axis=0)
    gram = _dot_nt(jnp.concatenate([kap_rt * lo_f, kap_rt * hi_f], axis=0).astype(BF16), yt)
    grams = (gram[:LANES], gram[LANES:])
    kap_rt = kap_rt.astype(BF16)
    yield
    vv = jnp.concatenate([v16, v16], axis=0)
    a_b = [g[:CHUNK, :CHUNK] * strict for g in grams]
    akv = _dot(jnp.concatenate([g[:CHUNK] * strict_hi for g in grams], axis=0).astype(BF16), vv)
    akv = jnp.where(lo, akv[:CHUNK], akv[CHUNK:])
    mats = jnp.concatenate([g[CHUNK:] * incl_wide for g in grams], axis=0).astype(BF16)
    x_inv = yield from _unit_tri_inverse_offdiag(a_b, blockdiag16)
    x_inv = jnp.concatenate(x_inv, axis=0).astype(BF16)
    yield

    s2 = recv_state()
    while s2 is None:
        yield
        s2 = recv_state()
    ks_rs = _dot_nt(kap_rt, s2.astype(BF16))
    ks, rs = ks_rs[:CHUNK], ks_rs[CHUNK:]
    yield
    rhs = ks + akv
    xr = _dot(x_inv, rhs.astype(BF16))
    e = rhs + jnp.where(lo, xr[:CHUNK], xr[CHUNK:])
    yield
    ev = jnp.concatenate([-e, v], axis=0).astype(BF16)
    upd = _dot_tn(ev, tails)
    send_state(s2 * decay + upd * headdiag)
    my = _dot(mats, ev)
    y = rs + jnp.where(lo, my[:CHUNK], my[CHUNK:])
    return y, bonus


RWKV_PAIRS_PER_STEP = 2
_RWKV_IN_REFS = 25
_RWKV_SCRATCH = 4
_MT_STRICT, _MT_STRICT_HI, _MT_INCL_WIDE = 0, 2, 4
_MT_BLOCKDIAG, _MT_HEADDIAG, _MT_HALVES, _MT_BANDS = 6, 7, 9, 10


def _rwkv_mask_tables():
    import numpy as np
    t = np.arange(CHUNK)[:, None]
    s = np.arange(CHUNK)[None, :]
    lane = np.arange(LANES)[None, :]
    table = np.zeros((_MT_BANDS, CHUNK, LANES), np.float32)
    tri = np.zeros((2, CHUNK, CHUNK), np.float32)
    for d, (strict, incl) in enumerate(((s < t, s <= t), (s > t, s >= t))):
        table[_MT_STRICT + d, :, :CHUNK] = strict
        table[_MT_STRICT_HI + d, :, CHUNK:] = strict
        table[_MT_INCL_WIDE + d] = np.concatenate([incl, incl], axis=1)
        tri[d] = incl
    table[_MT_BLOCKDIAG, :, :CHUNK] = (t // 16) == (s // 16)
    v = np.arange(LANES)[:, None]
    table[_MT_HEADDIAG:_MT_HEADDIAG + 2] = ((v < RWKV_HD) == (lane < RWKV_HD)).reshape(2, CHUNK, LANES)
    table[_MT_HALVES, 0] = lane[0] < RWKV_HD
    table[_MT_HALVES, 1] = lane[0] >= RWKV_HD
    return table.reshape(_MT_BANDS * CHUNK, LANES), tri


def _rwkv_kernel(*refs, need_ctx):
    n_pairs = RWKV_PAIRS_PER_STEP
    mt_ref, tri_ref = refs[:2]
    refs = refs[2:]
    n_in = n_pairs * _RWKV_IN_REFS
    rest = refs[n_in:]
    if need_ctx:
        y_l, y_c = rest[:2]
        scr = rest[2:]
    else:
        y_l, y_c = rest[0], None
        scr = rest[1:]
    nl = refs[0].shape[0] // CHUNK
    nc = refs[6].shape[0] // CHUNK

    def band(j, rows=CHUNK, lanes=LANES):
        return mt_ref[j * CHUNK:j * CHUNK + rows, :lanes]

    lo = lax.broadcasted_iota(jnp.int32, (1, LANES), 1) < RWKV_HD
    lo_f, hi_f = band(_MT_HALVES, 1), mt_ref[_MT_HALVES * CHUNK + 1:_MT_HALVES * CHUNK + 2, :]
    blockdiag16 = band(_MT_BLOCKDIAG, lanes=CHUNK)
    headdiag = band(_MT_HEADDIAG, rows=LANES)

    def consts(d):
        return (lo_f, hi_f, lo, blockdiag16, headdiag, tri_ref[d], band(_MT_STRICT + d, lanes=CHUNK),
                band(_MT_STRICT_HI + d), band(_MT_INCL_WIDE + d))
    consts_f, consts_b = consts(0), consts(1)

    pairs = []
    for g in range(n_pairs):
        r = refs[g * _RWKV_IN_REFS:(g + 1) * _RWKV_IN_REFS]
        (w0f, w0b, a0f, a0b, kkw, kaw, rkw, lng, lnb, wupf, wupb, aupf, aupb) = [x[0] for x in r[12:]]
        s = scr[g * 2 * _RWKV_SCRATCH:(g + 1) * 2 * _RWKV_SCRATCH]
        pairs.append(dict(
            lat=tuple(r[0:6]), ctx=tuple(r[6:12]),
            prm_f=(w0f, a0f, kkw, kaw, rkw, wupf, aupf),
            prm_b=(w0b, a0b, kkw, kaw, rkw, wupb, aupb),
            scr_lat=s[:_RWKV_SCRATCH], scr_ctx=s[_RWKV_SCRATCH:], lng=lng, lnb=lnb))

    def scan(which, n, carry):
        unroll = _unroll_factor(n, RWKV_UNROLL)

        def body(i, carry):
            states = {}
            gens = []
            for u in range(unroll):
                c = i * unroll + u
                for g, pr in enumerate(pairs):
                    states[(g, "f", -1)], states[(g, "b", -1)] = carry[2 * g], carry[2 * g + 1]
                    for d, cc, prm, consts in (("f", c, pr["prm_f"], consts_f),
                                               ("b", n - 1 - c, pr["prm_b"], consts_b)):
                        gens.append(_rwkv_chunk(
                            pr[which], cc, prm, consts,
                            functools.partial(states.get, (g, d, u - 1)),
                            functools.partial(states.__setitem__, (g, d, u))))
            outs = iter(_run_interleaved(gens))
            for u in range(unroll):
                c = i * unroll + u
                rows_f, rows_b = _chunk_rows(c), _chunk_rows(n - 1 - c)
                for pr in pairs:
                    yf, yb, bf, bb = pr["scr_" + which]
                    (y_f, bon_f), (y_b, bon_b) = next(outs), next(outs)
                    yf[rows_f, :] = y_f
                    bf[rows_f, :] = bon_f.astype(bf.dtype)
                    yb[rows_b, :] = y_b
                    bb[rows_b, :] = bon_b.astype(bb.dtype)
            return tuple(states[(g, d, unroll - 1)] for g in range(n_pairs) for d in ("f", "b"))
        return lax.fori_loop(0, n // unroll, body, carry)

    zero = jnp.zeros((LANES, LANES), F32)
    carry = scan("ctx", nc, (zero,) * (2 * n_pairs))
    scan("lat", nl, carry)

    head_mean = (headdiag * (1.0 / RWKV_HD)).astype(BF16)

    def seg_mean(x):
        hi = x.astype(BF16)
        lo_part = (x - hi.astype(F32)).astype(BF16)
        r = _dot(jnp.concatenate([hi, lo_part], axis=0), head_mean)
        return r[:CHUNK] + r[CHUNK:]

    def finish(which, y_out, n):
        def body(i, _):
            rows = _chunk_rows(i)
            for g, pr in enumerate(pairs):
                yf, yb, bf, bb = pr["scr_" + which]
                gs = pr[which][3]
                y = yf[rows, :] + yb[rows, :]
                d = y - seg_mean(y)
                var = seg_mean(d * d)
                yn = d * lax.rsqrt(var + GN_EPS) * pr["lng"] + pr["lnb"]
                bonus = bf[rows, :].astype(F32) + bb[rows, :].astype(F32)
                y_out[rows, g * LANES:(g + 1) * LANES] = (
                    (yn + bonus) * gs[rows, :].astype(F32)).astype(y_out.dtype)
            return 0
        lax.fori_loop(0, n, body, 0, unroll=_unroll_factor(n, FINISH_UNROLL))

    finish("lat", y_l, nl)
    if need_ctx:
        finish("ctx", y_c, nc)


def _rwkv(ps_lat, ps_ctx, batch, seq, ctx_len, prm, need_ctx):
    n_pairs = RWKV_PAIRS_PER_STEP

    def pair_specs(g):
        hp = lambda s: n_pairs * s + g

        def seg_specs(n):
            return [pl.BlockSpec((n, LANES), lambda b, s: (b, RW_R + hp(s))),
                    pl.BlockSpec((n, LANES), lambda b, s: (b, RW_K + hp(s))),
                    pl.BlockSpec((n, LANES), lambda b, s: (b, RW_V + hp(s))),
                    pl.BlockSpec((n, LANES), lambda b, s: (b, RW_GATE + hp(s))),
                    pl.BlockSpec((n, LANES), lambda b, s: (b, RW_WD)),
                    pl.BlockSpec((n, LANES), lambda b, s: (b, RW_AD))]
        vec_h = pl.BlockSpec((1, 1, LANES), lambda b, s: (hp(s), 0, 0))
        mat_h = pl.BlockSpec((1, LANES, LANES), lambda b, s: (hp(s), 0, 0))
        return seg_specs(seq) + seg_specs(ctx_len) + [vec_h] * 9 + [mat_h] * 4

    table, tri = _rwkv_mask_tables()
    in_specs = [pl.BlockSpec(table.shape, lambda b, s: (0, 0)),
                pl.BlockSpec(tri.shape, lambda b, s: (0, 0, 0))]
    args = [jnp.asarray(table), jnp.asarray(tri, BF16)]
    for g in range(n_pairs):
        in_specs += pair_specs(g)
        args += [ps_lat] * 6 + [ps_ctx] * 6 + list(prm)
    width = n_pairs * LANES
    out_specs = [pl.BlockSpec((seq, width), lambda b, s: (b, s))]
    out_shape = [jax.ShapeDtypeStruct((batch * seq, BRANCH_W), BF16)]
    if need_ctx:
        out_specs.append(pl.BlockSpec((ctx_len, width), lambda b, s: (b, s)))
        out_shape.append(jax.ShapeDtypeStruct((batch * ctx_len, BRANCH_W), BF16))
    def seg_scratch(n):
        return [pltpu.VMEM((n, LANES), F32)] * 2 + [pltpu.VMEM((n, LANES), BF16)] * 2
    scratch = (seg_scratch(seq) + seg_scratch(ctx_len)) * n_pairs
    outs = pl.pallas_call(
        functools.partial(_rwkv_kernel, need_ctx=need_ctx),
        grid=(batch, RWKV_H // 2 // n_pairs),
        in_specs=in_specs,
        out_specs=out_specs,
        out_shape=out_shape,
        scratch_shapes=scratch,
        compiler_params=_cparams(("parallel", "parallel")),
        name="rwkv",
    )(*args)
    return (outs[0], outs[1]) if need_ctx else (outs[0], None)


def _merge_kernel(x_ref, y1_ref, y2_ref, y3_ref, g1_ref, g2_ref, g3_ref, mod_ref,
                  wo1_ref, wo2_ref, wo3_ref, wout_ref, gp_ref, o_ref):
    m = (_sigmoid(g1_ref[...].astype(F32)) * _dot(y1_ref[...], wo1_ref[...])
         + _sigmoid(g2_ref[...].astype(F32)) * _dot(y2_ref[...], wo2_ref[...])
         + _sigmoid(g3_ref[...].astype(F32)) * _dot(y3_ref[...], wo3_ref[...]))
    mo = _dot(m.astype(BF16), wout_ref[...])
    ms = jnp.mean(mo * mo, axis=-1, keepdims=True)
    out = mo * lax.rsqrt(ms + NORM_EPS) * gp_ref[...]
    o_ref[...] = x_ref[...] + mod_ref[0][2:3] * out


def _merge(x2d, y1, y2, y3, p, mod, mod_index, wo1, wo2, wo3, wout, g_post, tm):
    r, d = x2d.shape
    row = lambda i: (i, 0)
    const = lambda i: (0, 0)
    return pl.pallas_call(
        _merge_kernel,
        grid=(r // tm,),
        in_specs=[pl.BlockSpec((tm, d), row),
                  pl.BlockSpec((tm, BRANCH_W), row),
                  pl.BlockSpec((tm, BRANCH_W), row),
                  pl.BlockSpec((tm, BRANCH_W), row),
                  pl.BlockSpec((tm, d), lambda i: (i, 0)),
                  pl.BlockSpec((tm, d), lambda i: (i, 1)),
                  pl.BlockSpec((tm, d), lambda i: (i, 2)),
                  pl.BlockSpec((1, 3, d), lambda i: (mod_index(i), 0, 0)),
                  pl.BlockSpec((BRANCH_W, d), const),
                  pl.BlockSpec((BRANCH_W, d), const),
                  pl.BlockSpec((BRANCH_W, d), const),
                  pl.BlockSpec((d, d), const),
                  pl.BlockSpec((1, d), const)],
        out_specs=pl.BlockSpec((tm, d), row),
        out_shape=jax.ShapeDtypeStruct((r, d), F32),
        compiler_params=_cparams(("parallel",)),
        name="merge",
    )(x2d, y1, y2, y3, p, p, p, mod, wo1, wo2, wo3, wout, g_post.reshape(1, d))


def _pack_w_in(w_in):
    d = w_in.shape[0]
    o = 0
    gq = w_in[:, o:o + 256]; o += 256
    gk = w_in[:, o:o + 256]; o += 256
    gv = w_in[:, o:o + 512]; o += 512
    gg = w_in[:, o:o + 512]; o += 512
    gwd = w_in[:, o:o + 32]; o += 32
    att = w_in[:, o:o + 1536]; o += 1536
    rw = w_in[:, o:o + 2304]; o += 2304
    mg = w_in[:, o:o + 3072]
    n_qk = (ATT_H + ATT_KV) * ATT_HD
    att_qk = att[:, :n_qk].reshape(d, ATT_H + ATT_KV, ATT_HD)[:, :, _rope_lane_order()].reshape(d, n_qk)
    att = jnp.concatenate([att_qk, att[:, n_qk:]], axis=1)
    pad_wd = jnp.zeros((d, LANES - 32), w_in.dtype)
    pad_end = jnp.zeros((d, LANES), w_in.dtype)
    return jnp.concatenate([mg, att, rw, gq, gk, gv, gg, gwd, pad_wd, pad_end], axis=1).astype(BF16)


def _rope_tables(seq):
    quarter = ATT_HD // 4
    inv = ROPE_THETA ** (-jnp.arange(quarter, dtype=F32) / quarter)
    t = jnp.arange(seq)
    row_pos = (t // GRID_W).astype(F32)
    col_pos = (t % GRID_W).astype(F32)
    ar = row_pos[:, None] * inv[None, :]
    ac = col_pos[:, None] * inv[None, :]
    cos = jnp.concatenate([jnp.cos(ar), jnp.cos(ac), jnp.cos(ar), jnp.cos(ac)], axis=1)
    sin = jnp.concatenate([-jnp.sin(ar), -jnp.sin(ac), jnp.sin(ar), jnp.sin(ac)], axis=1)
    return cos, sin


def _rope_lane_order():
    quarter = ATT_HD // 4
    i = jnp.arange(quarter)
    return jnp.concatenate([i, 2 * quarter + i, quarter + i, 3 * quarter + i])


def _pad_rows(w, lo_half):
    z = jnp.zeros((LANES - w.shape[0], w.shape[1]), w.dtype)
    if lo_half:
        return jnp.concatenate([w, z], axis=0)
    half = LANES // 2
    return jnp.concatenate([z[:half], w, z[half:]], axis=0)


def _pick(n, target):
    t = min(n, target)
    while n % t:
        t //= 2
    return t


def kernel(x, c, ctx, c_ctx, w_mod, b_mod, g_pre, w_in, gla_wup_f, gla_b_f, gla_wup_b, gla_b_b, gla_norm, att_qnorm, att_knorm, rwkv_mu, rwkv_w0_f, rwkv_wup_f, rwkv_w0_b, rwkv_wup_b, rwkv_a0_f, rwkv_aup_f, rwkv_a0_b, rwkv_aup_b, rwkv_kk, rwkv_ka, rwkv_rk, rwkv_ln_g, rwkv_ln_b, w_o_gla, w_o_att, w_o_rwkv, w_out, g_post):
    batch, seq, d = x.shape
    ctx_len = ctx.shape[1]
    depth = w_in.shape[0]
    assert seq % CHUNK == 0 and ctx_len % CHUNK == 0 and seq % GRID_W == 0

    nb = -(-(batch + 1) // SUBLANES) * SUBLANES
    cond = jnp.concatenate([c, c_ctx[None, :], jnp.zeros((nb - batch - 1, d), F32)], axis=0)
    tables = _rope_tables(seq)

    tm_lat = _pick(seq, MATMUL_ROWS)
    tm_ctx = _pick(batch * ctx_len, MATMUL_ROWS)
    tm_merge = _pick(seq, MATMUL_ROWS)
    tm_merge_c = _pick(batch * ctx_len, MATMUL_ROWS)
    tm_prep = _pick(seq, PREP_ROWS)
    tm_prep_c = _pick(ctx_len, PREP_ROWS)
    tq_lat, tk = _pick(seq, FLASH_ROWS), _pick(seq, FLASH_ROWS)
    tq_ctx = _pick(ctx_len, FLASH_ROWS)

    def lat_mod(tm):
        return lambda i: i // (seq // tm)

    def ctx_mod(tm):
        return lambda i: batch

    x2 = x.reshape(batch * seq, d)
    xc2 = ctx.reshape(batch * ctx_len, d)

    def hp(v):
        return v.reshape(RWKV_H // 2, 1, LANES)

    for l in range(depth):
        need_ctx = l < depth - 1
        mod = _modulation(cond, w_mod[l], b_mod[l]).reshape(nb, 3, d)
        w_cat = _pack_w_in(w_in[l])
        p_lat = _inproj(x2, mod, lat_mod(tm_lat), g_pre[l], w_cat, tm_lat)
        p_ctx = _inproj(xc2, mod, ctx_mod(tm_ctx), g_pre[l], w_cat, tm_ctx)

        gains = jnp.stack([att_qnorm[l] * (ATT_HD ** -0.5 * LOG2_E), att_knorm[l]])[:, _rope_lane_order()]
        att_lat = tuple(_qkvprep(p_lat, gains, tables, batch, seq, tm_prep)) + (seq,)
        att_ctx = tuple(_qkvprep(p_ctx, gains, None, batch, ctx_len, tm_prep_c)) + (ctx_len,)
        ps_lat = _rwkv_shift(p_lat, rwkv_mu[l], seq, tm_prep)
        ps_ctx = _rwkv_shift(p_ctx, rwkv_mu[l], ctx_len, tm_prep_c)
        y2 = _flash(att_lat[0], p_lat, seq, att_lat, att_ctx, batch, tq_lat, tk)
        y2c = None
        if need_ctx:
            y2c = _flash(att_ctx[0], p_ctx, ctx_len, None, att_ctx, batch, tq_ctx, tk)

        def gla_w(w, lo_rows):
            w = w.reshape(GLA_LR, GLA_H // 2, LANES).transpose(1, 0, 2)
            z = jnp.zeros((GLA_H // 2, LANES, LANES), w.dtype)
            start = 0 if lo_rows else GLA_LR
            return lax.dynamic_update_slice(z, w, (0, start, 0)).astype(BF16)
        y1, y1c = _gla(p_lat, p_ctx, batch, seq, ctx_len,
                       gla_w(gla_wup_f[l], True), gla_w(gla_wup_b[l], False),
                       gla_b_f[l].reshape(GLA_H // 2, 1, LANES), gla_b_b[l].reshape(GLA_H // 2, 1, LANES),
                       gla_norm[l].reshape(GLA_H, 1, GLA_DV), need_ctx)

        def rw_w(w, lo_rows):
            w = w.reshape(RWKV_LR, RWKV_H // 2, LANES).transpose(1, 0, 2)
            return jnp.stack([_pad_rows(w[i], lo_rows) for i in range(RWKV_H // 2)]).astype(BF16)
        prm = (hp(rwkv_w0_f[l]), hp(rwkv_w0_b[l]), hp(rwkv_a0_f[l]), hp(rwkv_a0_b[l]),
               hp(rwkv_kk[l]), hp(rwkv_ka[l]), hp(rwkv_rk[l]), hp(rwkv_ln_g[l]), hp(rwkv_ln_b[l]),
               rw_w(rwkv_wup_f[l], True), rw_w(rwkv_wup_b[l], False),
               rw_w(rwkv_aup_f[l], True), rw_w(rwkv_aup_b[l], False))
        y3, y3c = _rwkv(ps_lat, ps_ctx, batch, seq, ctx_len, prm, need_ctx)

        wo1, wo2, wo3 = (w_o_gla[l].astype(BF16), w_o_att[l].astype(BF16),
                         w_o_rwkv[l].astype(BF16))
        wout = w_out[l].astype(BF16)
        x2 = _merge(x2, y1, y2, y3, p_lat, mod, lat_mod(tm_merge), wo1, wo2, wo3, wout, g_post[l], tm_merge)
        if need_ctx:
            xc2 = _merge(xc2, y1c, y2c, y3c, p_ctx, mod, ctx_mod(tm_merge_c), wo1, wo2, wo3, wout,
                         g_post[l], tm_merge_c)
    return x2.reshape(batch, seq, d)
```

```python
import functools

import jax
import jax.numpy as jnp
from jax import lax
from jax.experimental import pallas as pl
from jax.experimental.pallas import tpu as pltpu

F32 = jnp.float32
BF16 = jnp.bfloat16

GRID_W = 64
BRANCH_W = 512
GLA_H, GLA_DK, GLA_DV, GLA_LR = 4, 64, 128, 16
GLA_TAU = 16.0
ATT_H, ATT_KV, ATT_HD = 4, 2, 128
ROPE_THETA = 10000.0
RWKV_H, RWKV_HD, RWKV_LR = 8, 64, 64
RWKV_DECAY_SCALE = 0.6065306597
NORM_EPS = 1e-6
GN_EPS = 64e-5
L2_EPS = 1e-12
LOG2_E = 1.4426950408889634

CHUNK = 64
LANES = 128
SUBLANES = 8
BF16_ROWS = 16
MATMUL_ROWS = 1024
PREP_ROWS = 512
FLASH_ROWS = 512
MOD_COLS = 768
VMEM_CAPACITY = 64 * 1024 * 1024
VMEM_LIMIT = VMEM_CAPACITY * 7 // 8

CB_MG = 0
CB_ATT_Q = 24
CB_ATT_V = 30
CB_ATT_GATE = 32
CB_RW = 36
RW_BLOCKS = 18
CB_GLA_Q = 54
CB_GLA_K = 56
CB_GLA_V = 58
CB_GLA_GATE = 62
CB_GLA_WD = 66
NP_BLOCKS = 68
RW_R, RW_K, RW_V, RW_GATE, RW_WD, RW_AD = 0, 4, 8, 12, 16, 17
NP = NP_BLOCKS * LANES
TN_PROJ = NP // 4


def _cparams(sem):
    return pltpu.CompilerParams(dimension_semantics=sem, vmem_limit_bytes=VMEM_LIMIT)


def _dot(a, b):
    return jnp.dot(a, b, preferred_element_type=F32)


def _dot_nt(a, b):
    return lax.dot_general(a, b, (((1,), (1,)), ((), ())), preferred_element_type=F32)


def _dot_tn(a, b):
    return lax.dot_general(a, b, (((0,), (0,)), ((), ())), preferred_element_type=F32)


def _sigmoid(x):
    return 0.5 + 0.5 * jnp.tanh(0.5 * x)


def _silu(x):
    h = 0.5 * x
    return h + h * jnp.tanh(h)


def _tri_cumsum(tri, x):
    hi = x.astype(BF16)
    lo = (x - hi.astype(F32)).astype(BF16)
    n = x.shape[1]
    if n % LANES:
        return _dot(tri, hi) + _dot(tri, lo)
    r = _dot(tri, jnp.concatenate([hi, lo], axis=1))
    return r[:, :n] + r[:, n:]


GLA_UNROLL = 16
RWKV_UNROLL = 2
FINISH_UNROLL = 8


def _unroll_factor(n_chunks, target):
    u = target
    while n_chunks % u:
        u //= 2
    return u


def _mod_kernel(c_ref, w_ref, b_ref, o_ref):
    c = c_ref[...]
    s = _silu(c)
    o_ref[...] = jnp.dot(s, w_ref[...], preferred_element_type=F32,
                         precision=lax.Precision.HIGHEST) + b_ref[...]


def _modulation(cond, w_mod, b_mod):
    nb, d = cond.shape
    n = w_mod.shape[1]
    tn = _pick(n, MOD_COLS)
    return pl.pallas_call(
        _mod_kernel,
        grid=(n // tn,),
        in_specs=[pl.BlockSpec((nb, d), lambda j: (0, 0)),
                  pl.BlockSpec((d, tn), lambda j: (0, j)),
                  pl.BlockSpec((1, tn), lambda j: (0, j))],
        out_specs=pl.BlockSpec((nb, tn), lambda j: (0, j)),
        out_shape=jax.ShapeDtypeStruct((nb, n), F32),
        compiler_params=_cparams(("parallel",)),
        name="modulation",
    )(cond, w_mod, b_mod.reshape(1, n))


def _inproj_kernel(x_ref, mod_ref, g_ref, w_ref, o_ref, h_ref):
    @pl.when(pl.program_id(1) == 0)
    def _():
        x = x_ref[...]
        ms = jnp.mean(x * x, axis=-1, keepdims=True)
        y = x * lax.rsqrt(ms + NORM_EPS) * g_ref[...]
        mod = mod_ref[0]
        h_ref[...] = (y * (1.0 + mod[1:2]) + mod[0:1]).astype(BF16)

    o_ref[...] = _dot(h_ref[...], w_ref[...]).astype(o_ref.dtype)


def _inproj(x2d, mod, mod_index, g_pre, w_cat, tm):
    r, d = x2d.shape
    return pl.pallas_call(
        _inproj_kernel,
        grid=(r // tm, NP // TN_PROJ),
        in_specs=[pl.BlockSpec((tm, d), lambda i, j: (i, 0)),
                  pl.BlockSpec((1, 3, d), lambda i, j: (mod_index(i), 0, 0)),
                  pl.BlockSpec((1, d), lambda i, j: (0, 0)),
                  pl.BlockSpec((d, TN_PROJ), lambda i, j: (0, j))],
        out_specs=pl.BlockSpec((tm, TN_PROJ), lambda i, j: (i, j)),
        out_shape=jax.ShapeDtypeStruct((r, NP), BF16),
        scratch_shapes=[pltpu.VMEM((tm, d), BF16)],
        compiler_params=_cparams(("parallel", "arbitrary")),
        name="inproj",
    )(x2d, mod, g_pre.reshape(1, d), w_cat)


def _norm_rope(x, gain, tables):
    x = x.astype(F32)
    ms = jnp.mean(x * x, axis=-1, keepdims=True)
    y = x * lax.rsqrt(ms + NORM_EPS) * gain
    if tables is not None:
        cos, sin = tables
        y = y * cos + pltpu.roll(y, ATT_HD // 2, 1) * sin
    return y.astype(BF16)


ATT_QKV_BLOCKS = ATT_H + 2 * ATT_KV


def _qkvprep_kernel(*refs, rope):
    if rope:
        x_ref, g_ref, cos_ref, sin_ref, qk_ref, vt_ref = refs
        tables = (cos_ref[...], sin_ref[...])
    else:
        x_ref, g_ref, qk_ref, vt_ref = refs
        tables = None
    for h in range(ATT_H + ATT_KV):
        cols = slice(h * LANES, (h + 1) * LANES)
        gain = g_ref[0:1, :] if h < ATT_H else g_ref[1:2, :]
        qk_ref[:, cols] = _norm_rope(x_ref[:, cols], gain, tables)
    for j in range(ATT_KV):
        cols = slice((ATT_H + ATT_KV + j) * LANES, (ATT_H + ATT_KV + j + 1) * LANES)
        base = j * VT_ROWS
        vt_ref[base:base + ATT_HD, :] = x_ref[:, cols].astype(F32).T.astype(BF16)
        vt_ref[base + ATT_HD:base + VT_ROWS, :] = jnp.ones((VT_ROWS - ATT_HD, vt_ref.shape[1]), BF16)


VT_ROWS = ATT_HD + BF16_ROWS


def _qkvprep(p, gains, tables, batch, seq, tm):
    nt = seq // tm
    width = ATT_QKV_BLOCKS * LANES
    cb = CB_ATT_Q * LANES // width
    in_specs = [pl.BlockSpec((tm, width), lambda i: (i, cb)),
                pl.BlockSpec((2, LANES), lambda i: (0, 0))]
    args = [p, gains]
    if tables is not None:
        for tab in tables:
            in_specs.append(pl.BlockSpec((tm, LANES), lambda i: (i % nt, 0)))
            args.append(tab)
    qk_width = (ATT_H + ATT_KV) * LANES
    return pl.pallas_call(
        functools.partial(_qkvprep_kernel, rope=tables is not None),
        grid=(batch * nt,),
        in_specs=in_specs,
        out_specs=[pl.BlockSpec((tm, qk_width), lambda i: (i, 0)),
                   pl.BlockSpec((ATT_KV * VT_ROWS, tm), lambda i: (i // nt, i % nt))],
        out_shape=[jax.ShapeDtypeStruct((batch * seq, qk_width), BF16),
                   jax.ShapeDtypeStruct((batch * ATT_KV * VT_ROWS, seq), BF16)],
        compiler_params=_cparams(("parallel",)),
        name="qkvprep_rope" if tables is not None else "qkvprep",
    )(*args)


FLASH_KV_INTERLEAVE = 8


def _flash_kernel(*refs, n_lat_blocks, tk, has_ctx_kv):
    refs = list(refs)
    q0_ref, q1_ref, g0_ref, g1_ref = refs[:4]
    pos = 4
    if n_lat_blocks:
        kl_ref, vtl_ref = refs[pos:pos + 2]
        pos += 2
    if has_ctx_kv:
        kc_ref, vtc_ref = refs[pos:pos + 2]
        pos += 2
    o_ref = refs[pos]
    tq = q0_ref.shape[0]
    q = jnp.concatenate([q0_ref[...], q1_ref[...]], axis=0)

    def blocks(kvs, carry):
        scores = [_dot_nt(k, q) for k, _ in kvs]
        m, acc = carry
        for s, (_, vt) in zip(scores, kvs):
            m_new = jnp.maximum(m, jnp.max(s, axis=0, keepdims=True))
            p = jnp.exp2(s - m_new)
            acc = jnp.exp2(m - m_new) * acc + _dot(vt, p.astype(BF16))
            m = m_new
        return m, acc

    carry = (jnp.full((1, 2 * tq), -jnp.inf, F32), jnp.zeros((VT_ROWS, 2 * tq), F32))
    if n_lat_blocks:
        nb = _unroll_factor(n_lat_blocks, FLASH_KV_INTERLEAVE)

        def body(i, c):
            kvs = []
            for u in range(nb):
                start = pl.multiple_of((i * nb + u) * tk, tk)
                kvs.append((kl_ref[pl.ds(start, tk), :], vtl_ref[:, pl.ds(start, tk)]))
            return blocks(kvs, c)
        carry = lax.fori_loop(0, n_lat_blocks // nb, body, carry)
    if has_ctx_kv:
        carry = blocks([(kc_ref[...], vtc_ref[...])], carry)
    _, acc = carry
    o = (acc[:ATT_HD] / acc[ATT_HD:ATT_HD + 1]).T
    o_ref[:, :LANES] = (o[:tq] * _silu(g0_ref[...].astype(F32))).astype(o_ref.dtype)
    o_ref[:, LANES:] = (o[tq:] * _silu(g1_ref[...].astype(F32))).astype(o_ref.dtype)


def _flash(qk_q, p_q, seq_q, lat_kv, ctx_kv, batch, tq, tk):
    nq = seq_q // tq
    in_specs = [
        pl.BlockSpec((tq, LANES), lambda b, j, i: (b * nq + i, 2 * j)),
        pl.BlockSpec((tq, LANES), lambda b, j, i: (b * nq + i, 2 * j + 1)),
        pl.BlockSpec((tq, LANES), lambda b, j, i: (b * nq + i, CB_ATT_GATE + 2 * j)),
        pl.BlockSpec((tq, LANES), lambda b, j, i: (b * nq + i, CB_ATT_GATE + 2 * j + 1)),
    ]
    args = [qk_q, qk_q, p_q, p_q]
    n_lat_blocks = 0
    for seg in (lat_kv, ctx_kv):
        if seg is None:
            continue
        qk_s, vt_s, seq_s = seg
        in_specs.append(pl.BlockSpec((seq_s, LANES), lambda b, j, i: (b, ATT_H + j)))
        in_specs.append(pl.BlockSpec((VT_ROWS, seq_s), lambda b, j, i: (b * ATT_KV + j, 0)))
        args += [qk_s, vt_s]
    if lat_kv is not None:
        n_lat_blocks = lat_kv[2] // tk
    return pl.pallas_call(
        functools.partial(_flash_kernel, n_lat_blocks=n_lat_blocks, tk=tk,
                          has_ctx_kv=ctx_kv is not None),
        grid=(batch, ATT_KV, nq),
        in_specs=in_specs,
        out_specs=pl.BlockSpec((tq, 2 * LANES), lambda b, j, i: (b * nq + i, j)),
        out_shape=jax.ShapeDtypeStruct((batch * seq_q, BRANCH_W), BF16),
        compiler_params=_cparams(("parallel", "parallel", "arbitrary")),
        name="flash_lat" if lat_kv is not None else "flash_ctx",
    )(*args)


def _chunk_rows(c):
    return pl.ds(pl.multiple_of(c * CHUNK, CHUNK), CHUNK)


def _gla_chunk(seg, c, wup, bias, consts, recv_state, send_state):
    q_ref, k_ref, v_refs, wd_ref = seg
    tri, incl2, halves = consts
    rows = _chunk_rows(c)
    q = q_ref[rows, :].astype(F32) * (GLA_DK ** -0.5)
    k = k_ref[rows, :].astype(F32)
    vs = [v_ref[rows, :] for v_ref in v_refs]
    z = _dot(wd_ref[rows, :], wup) + bias
    yield
    g = (jnp.minimum(z, 0.0) - jnp.log1p(jnp.exp(-jnp.abs(z)))) * (1.0 / GLA_TAU)
    gc = _tri_cumsum(tri, g)
    yield
    tot = jnp.sum(g, axis=0, keepdims=True)
    q_dec = q * jnp.exp(gc)
    k_inv = (k * jnp.exp(-gc)).astype(BF16)
    k_tail = k * jnp.exp(tot - gc)
    q_heads = [(q_dec * hm).astype(BF16) for hm in halves]
    a = (_dot_nt(jnp.concatenate(q_heads, axis=0), k_inv) * incl2).astype(BF16)
    upds = [_dot_tn(v, (k_tail * hm).astype(BF16)) for v, hm in zip(vs, halves)]
    yield
    sts = recv_state()
    while sts is None:
        yield
        sts = recv_state()
    decay = jnp.exp(tot)
    send_state(tuple(st * decay + upd for st, upd in zip(sts, upds)))
    return tuple(_dot(a[h * CHUNK:(h + 1) * CHUNK], vs[h]) + _dot_nt(q_heads[h], sts[h].astype(BF16))
                 for h in range(2))


_GLA_SEG_REFS = 7


def _gla_kernel(*refs, need_ctx):
    mt_ref, tri_ref = refs[:2]
    lat = refs[2:2 + _GLA_SEG_REFS]
    ctx = refs[2 + _GLA_SEG_REFS:2 + 2 * _GLA_SEG_REFS]
    wupf_ref, wupb_ref, bf_ref, bb_ref, gn_ref = refs[2 + 2 * _GLA_SEG_REFS:7 + 2 * _GLA_SEG_REFS]
    rest = refs[7 + 2 * _GLA_SEG_REFS:]
    if need_ctx:
        y_l, y_c = rest[:2]
        scr = rest[2:]
    else:
        y_l, y_c = rest[0], None
        scr = rest[1:]
    scr_l, scr_c = scr[:4], scr[4:]
    nl = lat[0].shape[0] // CHUNK
    nc = ctx[0].shape[0] // CHUNK
    wupf, wupb = wupf_ref[0], wupb_ref[0]
    bias_f, bias_b = bf_ref[0], bb_ref[0]
    row0 = _MT_HALVES * CHUNK
    halves = (mt_ref[row0:row0 + 1, :], mt_ref[row0 + 1:row0 + 2, :])

    def consts(d):
        incl = mt_ref[(_MT_INCL_WIDE + d) * CHUNK:(_MT_INCL_WIDE + d + 1) * CHUNK, :CHUNK]
        return tri_ref[d], jnp.concatenate([incl, incl], axis=0), halves
    consts_f, consts_b = consts(0), consts(1)

    def scan(seg, scr, n, carry):
        unroll = _unroll_factor(n, GLA_UNROLL)
        seg = (seg[0], seg[1], seg[2:4], seg[6])

        def body(i, carry):
            states = {("f", -1): carry[0], ("b", -1): carry[1]}
            gens = []
            for u in range(unroll):
                c = i * unroll + u
                for d, cc, wup, bias, cst in (("f", c, wupf, bias_f, consts_f),
                                              ("b", n - 1 - c, wupb, bias_b, consts_b)):
                    gens.append(_gla_chunk(
                        seg, cc, wup, bias, cst,
                        functools.partial(states.get, (d, u - 1)),
                        functools.partial(states.__setitem__, (d, u))))
            outs = _run_interleaved(gens)
            for u in range(unroll):
                c = i * unroll + u
                for h in range(2):
                    scr[h][_chunk_rows(c), :] = outs[2 * u][h]
                    scr[2 + h][_chunk_rows(n - 1 - c), :] = outs[2 * u + 1][h]
            return states[("f", unroll - 1)], states[("b", unroll - 1)]
        return lax.fori_loop(0, n // unroll, body, carry)

    zero = (jnp.zeros((GLA_DV, LANES), F32),) * 2
    carry = scan(ctx, scr_c, nc, (zero, zero))
    scan(lat, scr_l, nl, carry)

    def finish(seg, scr, y, n):
        def body(i, _):
            rows = _chunk_rows(i)
            for h in range(2):
                o = scr[h][rows, :] + scr[2 + h][rows, :]
                ms = jnp.mean(o * o, axis=-1, keepdims=True)
                o = o * lax.rsqrt(ms + NORM_EPS) * gn_ref[h]
                gate = seg[4 + h][rows, :].astype(F32)
                y[rows, h * LANES:(h + 1) * LANES] = (o * _silu(gate)).astype(y.dtype)
            return 0
        lax.fori_loop(0, n, body, 0, unroll=_unroll_factor(n, FINISH_UNROLL))

    finish(lat, scr_l, y_l, nl)
    if need_ctx:
        finish(ctx, scr_c, y_c, nc)


def _gla(p_lat, p_ctx, batch, seq, ctx_len, wupf, wupb, bf, bb, gnorm, need_ctx):
    def seg_specs(n):
        return [pl.BlockSpec((n, LANES), lambda b, g: (b, CB_GLA_Q + g)),
                pl.BlockSpec((n, LANES), lambda b, g: (b, CB_GLA_K + g)),
                pl.BlockSpec((n, LANES), lambda b, g: (b, CB_GLA_V + 2 * g)),
                pl.BlockSpec((n, LANES), lambda b, g: (b, CB_GLA_V + 2 * g + 1)),
                pl.BlockSpec((n, LANES), lambda b, g: (b, CB_GLA_GATE + 2 * g)),
                pl.BlockSpec((n, LANES), lambda b, g: (b, CB_GLA_GATE + 2 * g + 1)),
                pl.BlockSpec((n, LANES), lambda b, g: (b, CB_GLA_WD))]
    table, tri = _rwkv_mask_tables()
    w_spec = pl.BlockSpec((1, LANES, LANES), lambda b, g: (g, 0, 0))
    b_spec = pl.BlockSpec((1, 1, LANES), lambda b, g: (g, 0, 0))
    g_spec = pl.BlockSpec((2, 1, GLA_DV), lambda b, g: (g, 0, 0))
    out_specs = [pl.BlockSpec((seq, 2 * LANES), lambda b, g: (b, g))]
    out_shape = [jax.ShapeDtypeStruct((batch * seq, BRANCH_W), BF16)]
    if need_ctx:
        out_specs.append(pl.BlockSpec((ctx_len, 2 * LANES), lambda b, g: (b, g)))
        out_shape.append(jax.ShapeDtypeStruct((batch * ctx_len, BRANCH_W), BF16))
    outs = pl.pallas_call(
        functools.partial(_gla_kernel, need_ctx=need_ctx),
        grid=(batch, GLA_H // 2),
        in_specs=([pl.BlockSpec(table.shape, lambda b, g: (0, 0)),
                   pl.BlockSpec(tri.shape, lambda b, g: (0, 0, 0))]
                  + seg_specs(seq) + seg_specs(ctx_len) + [w_spec, w_spec, b_spec, b_spec, g_spec]),
        out_specs=out_specs,
        out_shape=out_shape,
        scratch_shapes=[pltpu.VMEM((seq, LANES), F32)] * 4 + [pltpu.VMEM((ctx_len, LANES), F32)] * 4,
        compiler_params=_cparams(("parallel", "parallel")),
        name="gla",
    )(jnp.asarray(table), jnp.asarray(tri, BF16), *([p_lat] * _GLA_SEG_REFS + [p_ctx] * _GLA_SEG_REFS),
      wupf, wupb, bf, bb, gnorm)
    return (outs[0], outs[1]) if need_ctx else (outs[0], None)


def _seg_sum(x, lo):
    s0 = jnp.sum(jnp.where(lo, x, 0.0), axis=-1, keepdims=True)
    s1 = jnp.sum(jnp.where(lo, 0.0, x), axis=-1, keepdims=True)
    return jnp.where(lo, s0, s1)


def _run_interleaved(gens):
    gens = list(gens)
    out = [None] * len(gens)
    live = list(range(len(gens)))
    while live:
        still = []
        for j in live:
            try:
                next(gens[j])
                still.append(j)
            except StopIteration as stop:
                out[j] = stop.value
        live = still
    return out


def _unit_tri_inverse_offdiag(mats, blockdiag):
    def mm(x, y):
        return _dot(x.astype(BF16), y.astype(BF16))

    def mm2(x1, x2, y):
        r = mm(jnp.concatenate([x1, x2], axis=0), y)
        return r[:CHUNK], r[CHUNK:]

    nd = [-(a * blockdiag) for a in mats]
    no = [-a - d for a, d in zip(mats, nd)]
    n2 = [mm(d, d) for d in nd]
    yield
    r = [mm2(s, d, s) for d, s in zip(nd, n2)]
    n4 = [ri[0] for ri in r]
    x = [d + s + ri[1] for d, s, ri in zip(nd, n2, r)]
    yield
    r = [mm2(s, xi, s) for xi, s in zip(x, n4)]
    n8 = [ri[0] for ri in r]
    x = [xi + s + ri[1] for xi, s, ri in zip(x, n4, r)]
    yield
    xd = [xi + s + mm(xi, s) for xi, s in zip(x, n8)]
    yield
    m = [o + mm(xi, o) for xi, o in zip(xd, no)]
    yield
    m2 = [mm(mi, mi) for mi in m]
    yield
    xq = [mi + s + mm(mi, s) for mi, s in zip(m, m2)]
    yield
    return [q + d + mm(q, d) for q, d in zip(xq, xd)]


SHIFT_COLS = RW_BLOCKS * LANES


SHIFT_SUB = 128


def _shift_band():
    import numpy as np
    band = np.zeros((SHIFT_SUB, SHIFT_SUB + 2 * BF16_ROWS), np.float32)
    r = np.arange(SHIFT_SUB)
    band[r, r + BF16_ROWS - 1] = 1.0
    band[r, r + BF16_ROWS + 1] = 1.0
    return band


def _rwkv_shift_kernel(cur_ref, prev_ref, next_ref, band_ref, mu_ref, o_ref, *, tiles_per_seq):
    t = pl.program_id(0) % tiles_per_seq
    tm = cur_ref.shape[0]
    prev = jnp.where(t > 0, prev_ref[...], jnp.zeros_like(prev_ref))
    nxt = jnp.where(t < tiles_per_seq - 1, next_ref[...], jnp.zeros_like(next_ref))
    band = band_ref[...]
    mu = mu_ref[...]
    keep, half_mu = 1.0 - mu, 0.5 * mu
    g0, w0 = RW_GATE * LANES, RW_WD * LANES
    for s in range(tm // SHIFT_SUB):
        lo, hi = s * SHIFT_SUB - BF16_ROWS, (s + 1) * SHIFT_SUB + BF16_ROWS
        parts = [prev] if lo < 0 else []
        parts.append(cur_ref[max(lo, 0):min(hi, tm), :])
        if hi > tm:
            parts.append(nxt)
        window = jnp.concatenate(parts, axis=0) if len(parts) > 1 else parts[0]
        rows = slice(s * SHIFT_SUB, (s + 1) * SHIFT_SUB)
        x = cur_ref[rows, :].astype(F32)
        y = x * keep + _dot(band, window) * half_mu
        o_ref[rows, :g0] = y[:, :g0].astype(o_ref.dtype)
        o_ref[rows, g0:w0] = _silu(y[:, g0:w0]).astype(o_ref.dtype)
        o_ref[rows, w0:w0 + LANES] = jnp.tanh(y[:, w0:w0 + LANES]).astype(o_ref.dtype)
        o_ref[rows, w0 + LANES:] = y[:, w0 + LANES:].astype(o_ref.dtype)


def _rwkv_shift(p, mu, seq, tm):
    r = p.shape[0]
    tiles_per_seq = seq // tm
    sub = tm // BF16_ROWS
    last = r // BF16_ROWS - 1
    cb = CB_RW * LANES // SHIFT_COLS
    band = _shift_band()
    assert tm % SHIFT_SUB == 0
    return pl.pallas_call(
        functools.partial(_rwkv_shift_kernel, tiles_per_seq=tiles_per_seq),
        grid=(r // tm,),
        in_specs=[pl.BlockSpec((tm, SHIFT_COLS), lambda i: (i, cb)),
                  pl.BlockSpec((BF16_ROWS, SHIFT_COLS), lambda i: (jnp.maximum(i * sub - 1, 0), cb)),
                  pl.BlockSpec((BF16_ROWS, SHIFT_COLS), lambda i: (jnp.minimum((i + 1) * sub, last), cb)),
                  pl.BlockSpec(band.shape, lambda i: (0, 0)),
                  pl.BlockSpec((1, SHIFT_COLS), lambda i: (0, 0))],
        out_specs=pl.BlockSpec((tm, SHIFT_COLS), lambda i: (i, 0)),
        out_shape=jax.ShapeDtypeStruct((r, RW_BLOCKS * LANES), BF16),
        compiler_params=_cparams(("parallel",)),
        name="rwkv_shift",
    )(p, p, p, jnp.asarray(band, BF16), mu.reshape(1, RW_BLOCKS * LANES))


def _rwkv_chunk(seg, c, prm, consts, recv_state, send_state):
    (r_ref, k_ref, v_ref, _, wd_ref, ad_ref) = seg
    (w0, a0, kkw, kaw, rkw, wup, aup) = prm
    (lo_f, hi_f, lo, blockdiag16, headdiag, tri, strict, strict_hi, incl_wide) = consts
    rows = _chunk_rows(c)

    r = r_ref[rows, :].astype(F32)
    k = k_ref[rows, :].astype(F32)
    v16 = v_ref[rows, :]
    v = v16.astype(F32)
    z_w = _dot(wd_ref[rows, :], wup)
    z_a = _dot(ad_ref[rows, :], aup)
    yield
    lw = -RWKV_DECAY_SCALE * _sigmoid(w0 + z_w)
    a = _sigmoid(a0 + z_a)
    kd = k * (1.0 + (a - 1.0) * kaw)
    kk = k * kkw
    kk = kk * lax.rsqrt(_seg_sum(kk * kk, lo) + L2_EPS)
    kka = kk * a
    bonus = _seg_sum(r * kd * rkw, lo) * v

    cw = _tri_cumsum(tri, lw)
    yield
    tot = jnp.sum(lw, axis=0, keepdims=True)
    e_neg = jnp.exp(-cw)
    e_tail = jnp.exp(tot - cw)
    kap = kk * jnp.exp(cw - lw)
    rt = r * jnp.exp(cw)
    bt = (kka * e_neg).astype(BF16)
    kt = (kd * e_neg).astype(BF16)
    tails = jnp.concatenate([kka * e_tail, kd * e_tail], axis=0).astype(BF16)
    decay = jnp.exp(tot)
    kap_rt = jnp.concatenate([kap, rt], axis=0)

    yt = jnp.concatenate([bt, kt], axis=0)
    gram = _dot_nt(jnp.concatenate([kap_rt * lo_f, kap_rt * hi_f], axis=0).astype(BF16), yt)
    grams = (gram[:LANES], gram[LANES:])
    kap_rt = kap_rt.astype(BF16)
    yield
    vv = jnp.concatenate([v16, v16], axis=0)
    a_b = [g[:CHUNK, :CHUNK] * strict for g in grams]
    akv = _dot(jnp.concatenate([g[:CHUNK] * strict_hi for g in grams], axis=0).astype(BF16), vv)
    akv = jnp.where(lo, akv[:CHUNK], akv[CHUNK:])
    mats = jnp.concatenate([g[CHUNK:] * incl_wide for g in grams], axis=0).astype(BF16)
    x_inv = yield from _unit_tri_inverse_offdiag(a_b, blockdiag16)
    x_inv = jnp.concatenate(x_inv, axis=0).astype(BF16)
    yield

    s2 = recv_state()
    while s2 is None:
        yield
        s2 = recv_state()
    ks_rs = _dot_nt(kap_rt, s2.astype(BF16))
    ks, rs = ks_rs[:CHUNK], ks_rs[CHUNK:]
    yield
    rhs = ks + akv
    xr = _dot(x_inv, rhs.astype(BF16))
    e = rhs + jnp.where(lo, xr[:CHUNK], xr[CHUNK:])
    yield
    ev = jnp.concatenate([-e, v], axis=0).astype(BF16)
    upd = _dot_tn(ev, tails)
    send_state(s2 * decay + upd * headdiag)
    my = _dot(mats, ev)
    y = rs + jnp.where(lo, my[:CHUNK], my[CHUNK:])
    return y, bonus


RWKV_PAIRS_PER_STEP = 2
_RWKV_IN_REFS = 25
_RWKV_SCRATCH = 4
_MT_STRICT, _MT_STRICT_HI, _MT_INCL_WIDE = 0, 2, 4
_MT_BLOCKDIAG, _MT_HEADDIAG, _MT_HALVES, _MT_BANDS = 6, 7, 9, 10


def _rwkv_mask_tables():
    import numpy as np
    t = np.arange(CHUNK)[:, None]
    s = np.arange(CHUNK)[None, :]
    lane = np.arange(LANES)[None, :]
    table = np.zeros((_MT_BANDS, CHUNK, LANES), np.float32)
    tri = np.zeros((2, CHUNK, CHUNK), np.float32)
    for d, (strict, incl) in enumerate(((s < t, s <= t), (s > t, s >= t))):
        table[_MT_STRICT + d, :, :CHUNK] = strict
        table[_MT_STRICT_HI + d, :, CHUNK:] = strict
        table[_MT_INCL_WIDE + d] = np.concatenate([incl, incl], axis=1)
        tri[d] = incl
    table[_MT_BLOCKDIAG, :, :CHUNK] = (t // 16) == (s // 16)
    v = np.arange(LANES)[:, None]
    table[_MT_HEADDIAG:_MT_HEADDIAG + 2] = ((v < RWKV_HD) == (lane < RWKV_HD)).reshape(2, CHUNK, LANES)
    table[_MT_HALVES, 0] = lane[0] < RWKV_HD
    table[_MT_HALVES, 1] = lane[0] >= RWKV_HD
    return table.reshape(_MT_BANDS * CHUNK, LANES), tri


def _rwkv_kernel(*refs, need_ctx):
    n_pairs = RWKV_PAIRS_PER_STEP
    mt_ref, tri_ref = refs[:2]
    refs = refs[2:]
    n_in = n_pairs * _RWKV_IN_REFS
    rest = refs[n_in:]
    if need_ctx:
        y_l, y_c = rest[:2]
        scr = rest[2:]
    else:
        y_l, y_c = rest[0], None
        scr = rest[1:]
    nl = refs[0].shape[0] // CHUNK
    nc = refs[6].shape[0] // CHUNK

    def band(j, rows=CHUNK, lanes=LANES):
        return mt_ref[j * CHUNK:j * CHUNK + rows, :lanes]

    lo = lax.broadcasted_iota(jnp.int32, (1, LANES), 1) < RWKV_HD
    lo_f, hi_f = band(_MT_HALVES, 1), mt_ref[_MT_HALVES * CHUNK + 1:_MT_HALVES * CHUNK + 2, :]
    blockdiag16 = band(_MT_BLOCKDIAG, lanes=CHUNK)
    headdiag = band(_MT_HEADDIAG, rows=LANES)

    def consts(d):
        return (lo_f, hi_f, lo, blockdiag16, headdiag, tri_ref[d], band(_MT_STRICT + d, lanes=CHUNK),
                band(_MT_STRICT_HI + d), band(_MT_INCL_WIDE + d))
    consts_f, consts_b = consts(0), consts(1)

    pairs = []
    for g in range(n_pairs):
        r = refs[g * _RWKV_IN_REFS:(g + 1) * _RWKV_IN_REFS]
        (w0f, w0b, a0f, a0b, kkw, kaw, rkw, lng, lnb, wupf, wupb, aupf, aupb) = [x[0] for x in r[12:]]
        s = scr[g * 2 * _RWKV_SCRATCH:(g + 1) * 2 * _RWKV_SCRATCH]
        pairs.append(dict(
            lat=tuple(r[0:6]), ctx=tuple(r[6:12]),
            prm_f=(w0f, a0f, kkw, kaw, rkw, wupf, aupf),
            prm_b=(w0b, a0b, kkw, kaw, rkw, wupb, aupb),
            scr_lat=s[:_RWKV_SCRATCH], scr_ctx=s[_RWKV_SCRATCH:], lng=lng, lnb=lnb))

    def scan(which, n, carry):
        unroll = _unroll_factor(n, RWKV_UNROLL)

        def body(i, carry):
            states = {}
            gens = []
            for u in range(unroll):
                c = i * unroll + u
                for g, pr in enumerate(pairs):
                    states[(g, "f", -1)], states[(g, "b", -1)] = carry[2 * g], carry[2 * g + 1]
                    for d, cc, prm, consts in (("f", c, pr["prm_f"], consts_f),
                                               ("b", n - 1 - c, pr["prm_b"], consts_b)):
                        gens.append(_rwkv_chunk(
                            pr[which], cc, prm, consts,
                            functools.partial(states.get, (g, d, u - 1)),
                            functools.partial(states.__setitem__, (g, d, u))))
            outs = iter(_run_interleaved(gens))
            for u in range(unroll):
                c = i * unroll + u
                rows_f, rows_b = _chunk_rows(c), _chunk_rows(n - 1 - c)
                for pr in pairs:
                    yf, yb, bf, bb = pr["scr_" + which]
                    (y_f, bon_f), (y_b, bon_b) = next(outs), next(outs)
                    yf[rows_f, :] = y_f
                    bf[rows_f, :] = bon_f.astype(bf.dtype)
                    yb[rows_b, :] = y_b
                    bb[rows_b, :] = bon_b.astype(bb.dtype)
            return tuple(states[(g, d, unroll - 1)] for g in range(n_pairs) for d in ("f", "b"))
        return lax.fori_loop(0, n // unroll, body, carry)

    zero = jnp.zeros((LANES, LANES), F32)
    carry = scan("ctx", nc, (zero,) * (2 * n_pairs))
    scan("lat", nl, carry)

    head_mean = (headdiag * (1.0 / RWKV_HD)).astype(BF16)

    def seg_mean(x):
        hi = x.astype(BF16)
        lo_part = (x - hi.astype(F32)).astype(BF16)
        r = _dot(jnp.concatenate([hi, lo_part], axis=0), head_mean)
        return r[:CHUNK] + r[CHUNK:]

    def finish(which, y_out, n):
        def body(i, _):
            rows = _chunk_rows(i)
            for g, pr in enumerate(pairs):
                yf, yb, bf, bb = pr["scr_" + which]
                gs = pr[which][3]
                y = yf[rows, :] + yb[rows, :]
                d = y - seg_mean(y)
                var = seg_mean(d * d)
                yn = d * lax.rsqrt(var + GN_EPS) * pr["lng"] + pr["lnb"]
                bonus = bf[rows, :].astype(F32) + bb[rows, :].astype(F32)
                y_out[rows, g * LANES:(g + 1) * LANES] = (
                    (yn + bonus) * gs[rows, :].astype(F32)).astype(y_out.dtype)
            return 0
        lax.fori_loop(0, n, body, 0, unroll=_unroll_factor(n, FINISH_UNROLL))

    finish("lat", y_l, nl)
    if need_ctx:
        finish("ctx", y_c, nc)


def _rwkv(ps_lat, ps_ctx, batch, seq, ctx_len, prm, need_ctx):
    n_pairs = RWKV_PAIRS_PER_STEP

    def pair_specs(g):
        hp = lambda s: n_pairs * s + g

        def seg_specs(n):
            return [pl.BlockSpec((n, LANES), lambda b, s: (b, RW_R + hp(s))),
                    pl.BlockSpec((n, LANES), lambda b, s: (b, RW_K + hp(s))),
                    pl.BlockSpec((n, LANES), lambda b, s: (b, RW_V + hp(s))),
                    pl.BlockSpec((n, LANES), lambda b, s: (b, RW_GATE + hp(s))),
                    pl.BlockSpec((n, LANES), lambda b, s: (b, RW_WD)),
                    pl.BlockSpec((n, LANES), lambda b, s: (b, RW_AD))]
        vec_h = pl.BlockSpec((1, 1, LANES), lambda b, s: (hp(s), 0, 0))
        mat_h = pl.BlockSpec((1, LANES, LANES), lambda b, s: (hp(s), 0, 0))
        return seg_specs(seq) + seg_specs(ctx_len) + [vec_h] * 9 + [mat_h] * 4

    table, tri = _rwkv_mask_tables()
    in_specs = [pl.BlockSpec(table.shape, lambda b, s: (0, 0)),
                pl.BlockSpec(tri.shape, lambda b, s: (0, 0, 0))]
    args = [jnp.asarray(table), jnp.asarray(tri, BF16)]
    for g in range(n_pairs):
        in_specs += pair_specs(g)
        args += [ps_lat] * 6 + [ps_ctx] * 6 + list(prm)
    width = n_pairs * LANES
    out_specs = [pl.BlockSpec((seq, width), lambda b, s: (b, s))]
    out_shape = [jax.ShapeDtypeStruct((batch * seq, BRANCH_W), BF16)]
    if need_ctx:
        out_specs.append(pl.BlockSpec((ctx_len, width), lambda b, s: (b, s)))
        out_shape.append(jax.ShapeDtypeStruct((batch * ctx_len, BRANCH_W), BF16))
    def seg_scratch(n):
        return [pltpu.VMEM((n, LANES), F32)] * 2 + [pltpu.VMEM((n, LANES), BF16)] * 2
    scratch = (seg_scratch(seq) + seg_scratch(ctx_len)) * n_pairs
    outs = pl.pallas_call(
        functools.partial(_rwkv_kernel, need_ctx=need_ctx),
        grid=(batch, RWKV_H // 2 // n_pairs),
        in_specs=in_specs,
        out_specs=out_specs,
        out_shape=out_shape,
        scratch_shapes=scratch,
        compiler_params=_cparams(("parallel", "parallel")),
        name="rwkv",
    )(*args)
    return (outs[0], outs[1]) if need_ctx else (outs[0], None)


def _merge_kernel(x_ref, y1_ref, y2_ref, y3_ref, g1_ref, g2_ref, g3_ref, mod_ref,
                  wo1_ref, wo2_ref, wo3_ref, wout_ref, gp_ref, o_ref):
    m = (_sigmoid(g1_ref[...].astype(F32)) * _dot(y1_ref[...], wo1_ref[...])
         + _sigmoid(g2_ref[...].astype(F32)) * _dot(y2_ref[...], wo2_ref[...])
         + _sigmoid(g3_ref[...].astype(F32)) * _dot(y3_ref[...], wo3_ref[...]))
    mo = _dot(m.astype(BF16), wout_ref[...])
    ms = jnp.mean(mo * mo, axis=-1, keepdims=True)
    out = mo * lax.rsqrt(ms + NORM_EPS) * gp_ref[...]
    o_ref[...] = x_ref[...] + mod_ref[0][2:3] * out


def _merge(x2d, y1, y2, y3, p, mod, mod_index, wo1, wo2, wo3, wout, g_post, tm):
    r, d = x2d.shape
    row = lambda i: (i, 0)
    const = lambda i: (0, 0)
    return pl.pallas_call(
        _merge_kernel,
        grid=(r // tm,),
        in_specs=[pl.BlockSpec((tm, d), row),
                  pl.BlockSpec((tm, BRANCH_W), row),
                  pl.BlockSpec((tm, BRANCH_W), row),
                  pl.BlockSpec((tm, BRANCH_W), row),
                  pl.BlockSpec((tm, d), lambda i: (i, 0)),
                  pl.BlockSpec((tm, d), lambda i: (i, 1)),
                  pl.BlockSpec((tm, d), lambda i: (i, 2)),
                  pl.BlockSpec((1, 3, d), lambda i: (mod_index(i), 0, 0)),
                  pl.BlockSpec((BRANCH_W, d), const),
                  pl.BlockSpec((BRANCH_W, d), const),
                  pl.BlockSpec((BRANCH_W, d), const),
                  pl.BlockSpec((d, d), const),
                  pl.BlockSpec((1, d), const)],
        out_specs=pl.BlockSpec((tm, d), row),
        out_shape=jax.ShapeDtypeStruct((r, d), F32),
        compiler_params=_cparams(("parallel",)),
        name="merge",
    )(x2d, y1, y2, y3, p, p, p, mod, wo1, wo2, wo3, wout, g_post.reshape(1, d))


def _pack_w_in(w_in):
    d = w_in.shape[0]
    o = 0
    gq = w_in[:, o:o + 256]; o += 256
    gk = w_in[:, o:o + 256]; o += 256
    gv = w_in[:, o:o + 512]; o += 512
    gg = w_in[:, o:o + 512]; o += 512
    gwd = w_in[:, o:o + 32]; o += 32
    att = w_in[:, o:o + 1536]; o += 1536
    rw = w_in[:, o:o + 2304]; o += 2304
    mg = w_in[:, o:o + 3072]
    n_qk = (ATT_H + ATT_KV) * ATT_HD
    att_qk = att[:, :n_qk].reshape(d, ATT_H + ATT_KV, ATT_HD)[:, :, _rope_lane_order()].reshape(d, n_qk)
    att = jnp.concatenate([att_qk, att[:, n_qk:]], axis=1)
    pad_wd = jnp.zeros((d, LANES - 32), w_in.dtype)
    pad_end = jnp.zeros((d, LANES), w_in.dtype)
    return jnp.concatenate([mg, att, rw, gq, gk, gv, gg, gwd, pad_wd, pad_end], axis=1).astype(BF16)


def _rope_tables(seq):
    quarter = ATT_HD // 4
    inv = ROPE_THETA ** (-jnp.arange(quarter, dtype=F32) / quarter)
    t = jnp.arange(seq)
    row_pos = (t // GRID_W).astype(F32)
    col_pos = (t % GRID_W).astype(F32)
    ar = row_pos[:, None] * inv[None, :]
    ac = col_pos[:, None] * inv[None, :]
    cos = jnp.concatenate([jnp.cos(ar), jnp.cos(ac), jnp.cos(ar), jnp.cos(ac)], axis=1)
    sin = jnp.concatenate([-jnp.sin(ar), -jnp.sin(ac), jnp.sin(ar), jnp.sin(ac)], axis=1)
    return cos, sin


def _rope_lane_order():
    quarter = ATT_HD // 4
    i = jnp.arange(quarter)
    return jnp.concatenate([i, 2 * quarter + i, quarter + i, 3 * quarter + i])


def _pad_rows(w, lo_half):
    z = jnp.zeros((LANES - w.shape[0], w.shape[1]), w.dtype)
    if lo_half:
        return jnp.concatenate([w, z], axis=0)
    half = LANES // 2
    return jnp.concatenate([z[:half], w, z[half:]], axis=0)


def _pick(n, target):
    t = min(n, target)
    while n % t:
        t //= 2
    return t


def kernel(x, c, ctx, c_ctx, w_mod, b_mod, g_pre, w_in, gla_wup_f, gla_b_f, gla_wup_b, gla_b_b, gla_norm, att_qnorm, att_knorm, rwkv_mu, rwkv_w0_f, rwkv_wup_f, rwkv_w0_b, rwkv_wup_b, rwkv_a0_f, rwkv_aup_f, rwkv_a0_b, rwkv_aup_b, rwkv_kk, rwkv_ka, rwkv_rk, rwkv_ln_g, rwkv_ln_b, w_o_gla, w_o_att, w_o_rwkv, w_out, g_post):
    batch, seq, d = x.shape
    ctx_len = ctx.shape[1]
    depth = w_in.shape[0]
    assert seq % CHUNK == 0 and ctx_len % CHUNK == 0 and seq % GRID_W == 0

    nb = -(-(batch + 1) // SUBLANES) * SUBLANES
    cond = jnp.concatenate([c, c_ctx[None, :], jnp.zeros((nb - batch - 1, d), F32)], axis=0)
    tables = _rope_tables(seq)

    tm_lat = _pick(seq, MATMUL_ROWS)
    tm_ctx = _pick(batch * ctx_len, MATMUL_ROWS)
    tm_merge = _pick(seq, MATMUL_ROWS)
    tm_merge_c = _pick(batch * ctx_len, MATMUL_ROWS)
    tm_prep = _pick(seq, PREP_ROWS)
    tm_prep_c = _pick(ctx_len, PREP_ROWS)
    tq_lat, tk = _pick(seq, FLASH_ROWS), _pick(seq, FLASH_ROWS)
    tq_ctx = _pick(ctx_len, FLASH_ROWS)

    def lat_mod(tm):
        return lambda i: i // (seq // tm)

    def ctx_mod(tm):
        return lambda i: batch

    x2 = x.reshape(batch * seq, d)
    xc2 = ctx.reshape(batch * ctx_len, d)

    def hp(v):
        return v.reshape(RWKV_H // 2, 1, LANES)

    for l in range(depth):
        need_ctx = l < depth - 1
        mod = _modulation(cond, w_mod[l], b_mod[l]).reshape(nb, 3, d)
        w_cat = _pack_w_in(w_in[l])
        p_lat = _inproj(x2, mod, lat_mod(tm_lat), g_pre[l], w_cat, tm_lat)
        p_ctx = _inproj(xc2, mod, ctx_mod(tm_ctx), g_pre[l], w_cat, tm_ctx)

        gains = jnp.stack([att_qnorm[l] * (ATT_HD ** -0.5 * LOG2_E), att_knorm[l]])[:, _rope_lane_order()]
        att_lat = tuple(_qkvprep(p_lat, gains, tables, batch, seq, tm_prep)) + (seq,)
        att_ctx = tuple(_qkvprep(p_ctx, gains, None, batch, ctx_len, tm_prep_c)) + (ctx_len,)
        ps_lat = _rwkv_shift(p_lat, rwkv_mu[l], seq, tm_prep)
        ps_ctx = _rwkv_shift(p_ctx, rwkv_mu[l], ctx_len, tm_prep_c)
        y2 = _flash(att_lat[0], p_lat, seq, att_lat, att_ctx, batch, tq_lat, tk)
        y2c = None
        if need_ctx:
            y2c = _flash(att_ctx[0], p_ctx, ctx_len, None, att_ctx, batch, tq_ctx, tk)

        def gla_w(w, lo_rows):
            w = w.reshape(GLA_LR, GLA_H // 2, LANES).transpose(1, 0, 2)
            z = jnp.zeros((GLA_H // 2, LANES, LANES), w.dtype)
            start = 0 if lo_rows else GLA_LR
            return lax.dynamic_update_slice(z, w, (0, start, 0)).astype(BF16)
        y1, y1c = _gla(p_lat, p_ctx, batch, seq, ctx_len,
                       gla_w(gla_wup_f[l], True), gla_w(gla_wup_b[l], False),
                       gla_b_f[l].reshape(GLA_H // 2, 1, LANES), gla_b_b[l].reshape(GLA_H // 2, 1, LANES),
                       gla_norm[l].reshape(GLA_H, 1, GLA_DV), need_ctx)

        def rw_w(w, lo_rows):
            w = w.reshape(RWKV_LR, RWKV_H // 2, LANES).transpose(1, 0, 2)
            return jnp.stack([_pad_rows(w[i], lo_rows) for i in range(RWKV_H // 2)]).astype(BF16)
        prm = (hp(rwkv_w0_f[l]), hp(rwkv_w0_b[l]), hp(rwkv_a0_f[l]), hp(rwkv_a0_b[l]),
               hp(rwkv_kk[l]), hp(rwkv_ka[l]), hp(rwkv_rk[l]), hp(rwkv_ln_g[l]), hp(rwkv_ln_b[l]),
               rw_w(rwkv_wup_f[l], True), rw_w(rwkv_wup_b[l], False),
               rw_w(rwkv_aup_f[l], True), rw_w(rwkv_aup_b[l], False))
        y3, y3c = _rwkv(ps_lat, ps_ctx, batch, seq, ctx_len, prm, need_ctx)

        wo1, wo2, wo3 = (w_o_gla[l].astype(BF16), w_o_att[l].astype(BF16),
                         w_o_rwkv[l].astype(BF16))
        wout = w_out[l].astype(BF16)
        x2 = _merge(x2, y1, y2, y3, p_lat, mod, lat_mod(tm_merge), wo1, wo2, wo3, wout, g_post[l], tm_merge)
        if need_ctx:
            xc2 = _merge(xc2, y1c, y2c, y3c, p_ctx, mod, ctx_mod(tm_merge_c), wo1, wo2, wo3, wout,
                         g_post[l], tm_merge_c)
    return x2.reshape(batch, seq, d)
```

```python
import functools

import jax
import jax.numpy as jnp
from jax import lax
from jax.experimental import pallas as pl
from jax.experimental.pallas import tpu as pltpu

F32 = jnp.float32
BF16 = jnp.bfloat16

GRID_W = 64
BRANCH_W = 512
GLA_H, GLA_DK, GLA_DV, GLA_LR = 4, 64, 128, 16
GLA_TAU = 16.0
ATT_H, ATT_KV, ATT_HD = 4, 2, 128
ROPE_THETA = 10000.0
RWKV_H, RWKV_HD, RWKV_LR = 8, 64, 64
RWKV_DECAY_SCALE = 0.6065306597
NORM_EPS = 1e-6
GN_EPS = 64e-5
L2_EPS = 1e-12
LOG2_E = 1.4426950408889634

CHUNK = 64
LANES = 128
SUBLANES = 8
BF16_ROWS = 16
MATMUL_ROWS = 1024
PREP_ROWS = 512
FLASH_ROWS = 512
MOD_COLS = 768
VMEM_CAPACITY = 64 * 1024 * 1024
VMEM_LIMIT = VMEM_CAPACITY * 7 // 8
RWKV_VMEM_LIMIT = VMEM_CAPACITY - 2 * 1024 * 1024

CB_MG = 0
CB_ATT_Q = 24
CB_ATT_V = 30
CB_ATT_GATE = 32
CB_RW = 36
RW_BLOCKS = 18
CB_GLA_Q = 54
CB_GLA_K = 56
CB_GLA_V = 58
CB_GLA_GATE = 62
CB_GLA_WD = 66
NP_BLOCKS = 68
RW_R, RW_K, RW_V, RW_GATE, RW_WD, RW_AD = 0, 4, 8, 12, 16, 17
NP = NP_BLOCKS * LANES
TN_PROJ = NP // 4


def _cparams(sem, vmem_limit=VMEM_LIMIT):
    return pltpu.CompilerParams(dimension_semantics=sem, vmem_limit_bytes=vmem_limit)


def _dot(a, b):
    return jnp.dot(a, b, preferred_element_type=F32)


def _dot_nt(a, b):
    return lax.dot_general(a, b, (((1,), (1,)), ((), ())), preferred_element_type=F32)


def _dot_tn(a, b):
    return lax.dot_general(a, b, (((0,), (0,)), ((), ())), preferred_element_type=F32)


def _sigmoid(x):
    return 0.5 + 0.5 * jnp.tanh(0.5 * x)


def _silu(x):
    h = 0.5 * x
    return h + h * jnp.tanh(h)


def _tri_cumsum(tri, x):
    hi = x.astype(BF16)
    lo = (x - hi.astype(F32)).astype(BF16)
    n = x.shape[1]
    if n % LANES:
        return _dot(tri, hi) + _dot(tri, lo)
    r = _dot(tri, jnp.concatenate([hi, lo], axis=1))
    return r[:, :n] + r[:, n:]


GLA_UNROLL = 16
RWKV_UNROLL = 4
FINISH_UNROLL = 8


def _unroll_factor(n_chunks, target):
    u = target
    while n_chunks % u:
        u //= 2
    return u


def _mod_kernel(c_ref, w_ref, b_ref, o_ref):
    c = c_ref[...]
    s = _silu(c)
    o_ref[...] = jnp.dot(s, w_ref[...], preferred_element_type=F32,
                         precision=lax.Precision.HIGHEST) + b_ref[...]


def _modulation(cond, w_mod, b_mod):
    nb, d = cond.shape
    n = w_mod.shape[1]
    tn = _pick(n, MOD_COLS)
    return pl.pallas_call(
        _mod_kernel,
        grid=(n // tn,),
        in_specs=[pl.BlockSpec((nb, d), lambda j: (0, 0)),
                  pl.BlockSpec((d, tn), lambda j: (0, j)),
                  pl.BlockSpec((1, tn), lambda j: (0, j))],
        out_specs=pl.BlockSpec((nb, tn), lambda j: (0, j)),
        out_shape=jax.ShapeDtypeStruct((nb, n), F32),
        compiler_params=_cparams(("parallel",)),
        name="modulation",
    )(cond, w_mod, b_mod.reshape(1, n))


def _inproj_kernel(x_ref, mod_ref, g_ref, w_ref, o_ref, h_ref):
    @pl.when(pl.program_id(1) == 0)
    def _():
        x = x_ref[...]
        ms = jnp.mean(x * x, axis=-1, keepdims=True)
        y = x * lax.rsqrt(ms + NORM_EPS) * g_ref[...]
        mod = mod_ref[0]
        h_ref[...] = (y * (1.0 + mod[1:2]) + mod[0:1]).astype(BF16)

    o_ref[...] = _dot(h_ref[...], w_ref[...]).astype(o_ref.dtype)


def _inproj(x2d, mod, mod_index, g_pre, w_cat, tm):
    r, d = x2d.shape
    return pl.pallas_call(
        _inproj_kernel,
        grid=(r // tm, NP // TN_PROJ),
        in_specs=[pl.BlockSpec((tm, d), lambda i, j: (i, 0)),
                  pl.BlockSpec((1, 3, d), lambda i, j: (mod_index(i), 0, 0)),
                  pl.BlockSpec((1, d), lambda i, j: (0, 0)),
                  pl.BlockSpec((d, TN_PROJ), lambda i, j: (0, j))],
        out_specs=pl.BlockSpec((tm, TN_PROJ), lambda i, j: (i, j)),
        out_shape=jax.ShapeDtypeStruct((r, NP), BF16),
        scratch_shapes=[pltpu.VMEM((tm, d), BF16)],
        compiler_params=_cparams(("parallel", "arbitrary")),
        name="inproj",
    )(x2d, mod, g_pre.reshape(1, d), w_cat)


def _norm_rope(x, gain, tables):
    x = x.astype(F32)
    ms = jnp.mean(x * x, axis=-1, keepdims=True)
    y = x * lax.rsqrt(ms + NORM_EPS) * gain
    if tables is not None:
        cos, sin = tables
        y = y * cos + pltpu.roll(y, ATT_HD // 2, 1) * sin
    return y.astype(BF16)


ATT_QKV_BLOCKS = ATT_H + 2 * ATT_KV


def _qkvprep_kernel(*refs, rope):
    if rope:
        x_ref, g_ref, cos_ref, sin_ref, qk_ref, vt_ref = refs
        tables = (cos_ref[...], sin_ref[...])
    else:
        x_ref, g_ref, qk_ref, vt_ref = refs
        tables = None
    for h in range(ATT_H + ATT_KV):
        cols = slice(h * LANES, (h + 1) * LANES)
        gain = g_ref[0:1, :] if h < ATT_H else g_ref[1:2, :]
        qk_ref[:, cols] = _norm_rope(x_ref[:, cols], gain, tables)
    for j in range(ATT_KV):
        cols = slice((ATT_H + ATT_KV + j) * LANES, (ATT_H + ATT_KV + j + 1) * LANES)
        base = j * VT_ROWS
        vt_ref[base:base + ATT_HD, :] = x_ref[:, cols].astype(F32).T.astype(BF16)
        vt_ref[base + ATT_HD:base + VT_ROWS, :] = jnp.ones((VT_ROWS - ATT_HD, vt_ref.shape[1]), BF16)


VT_ROWS = ATT_HD + BF16_ROWS


def _qkvprep(p, gains, tables, batch, seq, tm):
    nt = seq // tm
    width = ATT_QKV_BLOCKS * LANES
    cb = CB_ATT_Q * LANES // width
    in_specs = [pl.BlockSpec((tm, width), lambda i: (i, cb)),
                pl.BlockSpec((2, LANES), lambda i: (0, 0))]
    args = [p, gains]
    if tables is not None:
        for tab in tables:
            in_specs.append(pl.BlockSpec((tm, LANES), lambda i: (i % nt, 0)))
            args.append(tab)
    qk_width = (ATT_H + ATT_KV) * LANES
    return pl.pallas_call(
        functools.partial(_qkvprep_kernel, rope=tables is not None),
        grid=(batch * nt,),
        in_specs=in_specs,
        out_specs=[pl.BlockSpec((tm, qk_width), lambda i: (i, 0)),
                   pl.BlockSpec((ATT_KV * VT_ROWS, tm), lambda i: (i // nt, i % nt))],
        out_shape=[jax.ShapeDtypeStruct((batch * seq, qk_width), BF16),
                   jax.ShapeDtypeStruct((batch * ATT_KV * VT_ROWS, seq), BF16)],
        compiler_params=_cparams(("parallel",)),
        name="qkvprep_rope" if tables is not None else "qkvprep",
    )(*args)


FLASH_KV_INTERLEAVE = 8


def _flash_kernel(*refs, n_lat_blocks, tk, has_ctx_kv):
    refs = list(refs)
    q0_ref, q1_ref, g0_ref, g1_ref = refs[:4]
    pos = 4
    if n_lat_blocks:
        kl_ref, vtl_ref = refs[pos:pos + 2]
        pos += 2
    if has_ctx_kv:
        kc_ref, vtc_ref = refs[pos:pos + 2]
        pos += 2
    o_ref = refs[pos]
    tq = q0_ref.shape[0]
    q = jnp.concatenate([q0_ref[...], q1_ref[...]], axis=0)

    def blocks(kvs, carry):
        scores = [_dot_nt(k, q) for k, _ in kvs]
        m, acc = carry
        for s, (_, vt) in zip(scores, kvs):
            m_new = jnp.maximum(m, jnp.max(s, axis=0, keepdims=True))
            p = jnp.exp2(s - m_new)
            acc = jnp.exp2(m - m_new) * acc + _dot(vt, p.astype(BF16))
            m = m_new
        return m, acc

    carry = (jnp.full((1, 2 * tq), -jnp.inf, F32), jnp.zeros((VT_ROWS, 2 * tq), F32))
    if n_lat_blocks:
        nb = _unroll_factor(n_lat_blocks, FLASH_KV_INTERLEAVE)

        def body(i, c):
            kvs = []
            for u in range(nb):
                start = pl.multiple_of((i * nb + u) * tk, tk)
                kvs.append((kl_ref[pl.ds(start, tk), :], vtl_ref[:, pl.ds(start, tk)]))
            return blocks(kvs, c)
        carry = lax.fori_loop(0, n_lat_blocks // nb, body, carry)
    if has_ctx_kv:
        carry = blocks([(kc_ref[...], vtc_ref[...])], carry)
    _, acc = carry
    o = (acc[:ATT_HD] / acc[ATT_HD:ATT_HD + 1]).T
    o_ref[:, :LANES] = (o[:tq] * _silu(g0_ref[...].astype(F32))).astype(o_ref.dtype)
    o_ref[:, LANES:] = (o[tq:] * _silu(g1_ref[...].astype(F32))).astype(o_ref.dtype)


def _flash(qk_q, p_q, seq_q, lat_kv, ctx_kv, batch, tq, tk):
    nq = seq_q // tq
    in_specs = [
        pl.BlockSpec((tq, LANES), lambda b, j, i: (b * nq + i, 2 * j)),
        pl.BlockSpec((tq, LANES), lambda b, j, i: (b * nq + i, 2 * j + 1)),
        pl.BlockSpec((tq, LANES), lambda b, j, i: (b * nq + i, CB_ATT_GATE + 2 * j)),
        pl.BlockSpec((tq, LANES), lambda b, j, i: (b * nq + i, CB_ATT_GATE + 2 * j + 1)),
    ]
    args = [qk_q, qk_q, p_q, p_q]
    n_lat_blocks = 0
    for seg in (lat_kv, ctx_kv):
        if seg is None:
            continue
        qk_s, vt_s, seq_s = seg
        in_specs.append(pl.BlockSpec((seq_s, LANES), lambda b, j, i: (b, ATT_H + j)))
        in_specs.append(pl.BlockSpec((VT_ROWS, seq_s), lambda b, j, i: (b * ATT_KV + j, 0)))
        args += [qk_s, vt_s]
    if lat_kv is not None:
        n_lat_blocks = lat_kv[2] // tk
    return pl.pallas_call(
        functools.partial(_flash_kernel, n_lat_blocks=n_lat_blocks, tk=tk,
                          has_ctx_kv=ctx_kv is not None),
        grid=(batch, ATT_KV, nq),
        in_specs=in_specs,
        out_specs=pl.BlockSpec((tq, 2 * LANES), lambda b, j, i: (b * nq + i, j)),
        out_shape=jax.ShapeDtypeStruct((batch * seq_q, BRANCH_W), BF16),
        compiler_params=_cparams(("parallel", "parallel", "arbitrary")),
        name="flash_lat" if lat_kv is not None else "flash_ctx",
    )(*args)


def _chunk_rows(c):
    return pl.ds(pl.multiple_of(c * CHUNK, CHUNK), CHUNK)


def _gla_chunk(seg, c, wup, bias, consts, recv_state, send_state):
    q_ref, k_ref, v_refs, wd_ref = seg
    tri, incl2, halves = consts
    rows = _chunk_rows(c)
    q = q_ref[rows, :].astype(F32) * (GLA_DK ** -0.5)
    k = k_ref[rows, :].astype(F32)
    vs = [v_ref[rows, :] for v_ref in v_refs]
    z = _dot(wd_ref[rows, :], wup) + bias
    yield
    g = (jnp.minimum(z, 0.0) - jnp.log1p(jnp.exp(-jnp.abs(z)))) * (1.0 / GLA_TAU)
    gc = _tri_cumsum(tri, g)
    yield
    tot = jnp.sum(g, axis=0, keepdims=True)
    q_dec = q * jnp.exp(gc)
    k_inv = (k * jnp.exp(-gc)).astype(BF16)
    k_tail = k * jnp.exp(tot - gc)
    q_heads = [(q_dec * hm).astype(BF16) for hm in halves]
    a = (_dot_nt(jnp.concatenate(q_heads, axis=0), k_inv) * incl2).astype(BF16)
    upds = [_dot_tn(v, (k_tail * hm).astype(BF16)) for v, hm in zip(vs, halves)]
    yield
    sts = recv_state()
    while sts is None:
        yield
        sts = recv_state()
    decay = jnp.exp(tot)
    send_state(tuple(st * decay + upd for st, upd in zip(sts, upds)))
    return tuple(_dot(a[h * CHUNK:(h + 1) * CHUNK], vs[h]) + _dot_nt(q_heads[h], sts[h].astype(BF16))
                 for h in range(2))


_GLA_SEG_REFS = 7


def _gla_kernel(*refs, need_ctx):
    mt_ref, tri_ref = refs[:2]
    lat = refs[2:2 + _GLA_SEG_REFS]
    ctx = refs[2 + _GLA_SEG_REFS:2 + 2 * _GLA_SEG_REFS]
    wupf_ref, wupb_ref, bf_ref, bb_ref, gn_ref = refs[2 + 2 * _GLA_SEG_REFS:7 + 2 * _GLA_SEG_REFS]
    rest = refs[7 + 2 * _GLA_SEG_REFS:]
    if need_ctx:
        y_l, y_c = rest[:2]
        scr = rest[2:]
    else:
        y_l, y_c = rest[0], None
        scr = rest[1:]
    scr_l, scr_c = scr[:4], scr[4:]
    nl = lat[0].shape[0] // CHUNK
    nc = ctx[0].shape[0] // CHUNK
    wupf, wupb = wupf_ref[0], wupb_ref[0]
    bias_f, bias_b = bf_ref[0], bb_ref[0]
    row0 = _MT_HALVES * CHUNK
    halves = (mt_ref[row0:row0 + 1, :], mt_ref[row0 + 1:row0 + 2, :])

    def consts(d):
        incl = mt_ref[(_MT_INCL_WIDE + d) * CHUNK:(_MT_INCL_WIDE + d + 1) * CHUNK, :CHUNK]
        return tri_ref[d], jnp.concatenate([incl, incl], axis=0), halves
    consts_f, consts_b = consts(0), consts(1)

    def scan(seg, scr, n, carry):
        unroll = _unroll_factor(n, GLA_UNROLL)
        seg = (seg[0], seg[1], seg[2:4], seg[6])

        def body(i, carry):
            states = {("f", -1): carry[0], ("b", -1): carry[1]}
            gens = []
            for u in range(unroll):
                c = i * unroll + u
                for d, cc, wup, bias, cst in (("f", c, wupf, bias_f, consts_f),
                                              ("b", n - 1 - c, wupb, bias_b, consts_b)):
                    gens.append(_gla_chunk(
                        seg, cc, wup, bias, cst,
                        functools.partial(states.get, (d, u - 1)),
                        functools.partial(states.__setitem__, (d, u))))
            outs = _run_interleaved(gens)
            for u in range(unroll):
                c = i * unroll + u
                for h in range(2):
                    scr[h][_chunk_rows(c), :] = outs[2 * u][h]
                    scr[2 + h][_chunk_rows(n - 1 - c), :] = outs[2 * u + 1][h]
            return states[("f", unroll - 1)], states[("b", unroll - 1)]
        return lax.fori_loop(0, n // unroll, body, carry)

    zero = (jnp.zeros((GLA_DV, LANES), F32),) * 2
    carry = scan(ctx, scr_c, nc, (zero, zero))
    scan(lat, scr_l, nl, carry)

    def finish(seg, scr, y, n):
        def body(i, _):
            rows = _chunk_rows(i)
            for h in range(2):
                o = scr[h][rows, :] + scr[2 + h][rows, :]
                ms = jnp.mean(o * o, axis=-1, keepdims=True)
                o = o * lax.rsqrt(ms + NORM_EPS) * gn_ref[h]
                gate = seg[4 + h][rows, :].astype(F32)
                y[rows, h * LANES:(h + 1) * LANES] = (o * _silu(gate)).astype(y.dtype)
            return 0
        lax.fori_loop(0, n, body, 0, unroll=_unroll_factor(n, FINISH_UNROLL))

    finish(lat, scr_l, y_l, nl)
    if need_ctx:
        finish(ctx, scr_c, y_c, nc)


def _gla(p_lat, p_ctx, batch, seq, ctx_len, wupf, wupb, bf, bb, gnorm, need_ctx):
    def seg_specs(n):
        return [pl.BlockSpec((n, LANES), lambda b, g: (b, CB_GLA_Q + g)),
                pl.BlockSpec((n, LANES), lambda b, g: (b, CB_GLA_K + g)),
                pl.BlockSpec((n, LANES), lambda b, g: (b, CB_GLA_V + 2 * g)),
                pl.BlockSpec((n, LANES), lambda b, g: (b, CB_GLA_V + 2 * g + 1)),
                pl.BlockSpec((n, LANES), lambda b, g: (b, CB_GLA_GATE + 2 * g)),
                pl.BlockSpec((n, LANES), lambda b, g: (b, CB_GLA_GATE + 2 * g + 1)),
                pl.BlockSpec((n, LANES), lambda b, g: (b, CB_GLA_WD))]
    table, tri = _rwkv_mask_tables()
    w_spec = pl.BlockSpec((1, LANES, LANES), lambda b, g: (g, 0, 0))
    b_spec = pl.BlockSpec((1, 1, LANES), lambda b, g: (g, 0, 0))
    g_spec = pl.BlockSpec((2, 1, GLA_DV), lambda b, g: (g, 0, 0))
    out_specs = [pl.BlockSpec((seq, 2 * LANES), lambda b, g: (b, g))]
    out_shape = [jax.ShapeDtypeStruct((batch * seq, BRANCH_W), BF16)]
    if need_ctx:
        out_specs.append(pl.BlockSpec((ctx_len, 2 * LANES), lambda b, g: (b, g)))
        out_shape.append(jax.ShapeDtypeStruct((batch * ctx_len, BRANCH_W), BF16))
    outs = pl.pallas_call(
        functools.partial(_gla_kernel, need_ctx=need_ctx),
        grid=(batch, GLA_H // 2),
        in_specs=([pl.BlockSpec(table.shape, lambda b, g: (0, 0)),
                   pl.BlockSpec(tri.shape, lambda b, g: (0, 0, 0))]
                  + seg_specs(seq) + seg_specs(ctx_len) + [w_spec, w_spec, b_spec, b_spec, g_spec]),
        out_specs=out_specs,
        out_shape=out_shape,
        scratch_shapes=[pltpu.VMEM((seq, LANES), F32)] * 4 + [pltpu.VMEM((ctx_len, LANES), F32)] * 4,
        compiler_params=_cparams(("parallel", "parallel")),
        name="gla",
    )(jnp.asarray(table), jnp.asarray(tri, BF16), *([p_lat] * _GLA_SEG_REFS + [p_ctx] * _GLA_SEG_REFS),
      wupf, wupb, bf, bb, gnorm)
    return (outs[0], outs[1]) if need_ctx else (outs[0], None)


def _seg_sum(x, lo):
    s0 = jnp.sum(jnp.where(lo, x, 0.0), axis=-1, keepdims=True)
    s1 = jnp.sum(jnp.where(lo, 0.0, x), axis=-1, keepdims=True)
    return jnp.where(lo, s0, s1)


def _run_interleaved(gens):
    gens = list(gens)
    out = [None] * len(gens)
    live = list(range(len(gens)))
    while live:
        still = []
        for j in live:
            try:
                next(gens[j])
                still.append(j)
            except StopIteration as stop:
                out[j] = stop.value
        live = still
    return out


def _unit_tri_inverse_offdiag(mats, blockdiag):
    def mm(x, y):
        return _dot(x.astype(BF16), y.astype(BF16))

    def mm2(x1, x2, y):
        r = mm(jnp.concatenate([x1, x2], axis=0), y)
        return r[:CHUNK], r[CHUNK:]

    nd = [-(a * blockdiag) for a in mats]
    no = [-a - d for a, d in zip(mats, nd)]
    n2 = [mm(d, d) for d in nd]
    yield
    r = [mm2(s, d, s) for d, s in zip(nd, n2)]
    n4 = [ri[0] for ri in r]
    x = [d + s + ri[1] for d, s, ri in zip(nd, n2, r)]
    yield
    r = [mm2(s, xi, s) for xi, s in zip(x, n4)]
    n8 = [ri[0] for ri in r]
    x = [xi + s + ri[1] for xi, s, ri in zip(x, n4, r)]
    yield
    xd = [xi + s + mm(xi, s) for xi, s in zip(x, n8)]
    yield
    m = [o + mm(xi, o) for xi, o in zip(xd, no)]
    yield
    m2 = [mm(mi, mi) for mi in m]
    yield
    xq = [mi + s + mm(mi, s) for mi, s in zip(m, m2)]
    yield
    return [q + d + mm(q, d) for q, d in zip(xq, xd)]


SHIFT_COLS = RW_BLOCKS * LANES


def _rwkv_shift_kernel(cur_ref, prev_ref, next_ref, mu_ref, o_ref, *, tiles_per_seq):
    t = pl.program_id(0) % tiles_per_seq
    x = cur_ref[...].astype(F32)
    tm = x.shape[0]
    prev_row = jnp.where(t > 0, prev_ref[...].astype(F32)[BF16_ROWS - 1:BF16_ROWS, :], 0.0)
    next_row = jnp.where(t < tiles_per_seq - 1, next_ref[...].astype(F32)[0:1, :], 0.0)
    row8 = lax.broadcasted_iota(jnp.int32, (8, x.shape[1]), 0)
    up = pltpu.roll(x, 1, 0)
    up = jnp.concatenate([jnp.where(row8 == 0, prev_row, up[:8]), up[8:]], axis=0)
    dn = pltpu.roll(x, tm - 1, 0)
    dn = jnp.concatenate([dn[:tm - 8], jnp.where(row8 == 7, next_row, dn[tm - 8:])], axis=0)
    mu = mu_ref[...]
    y = x * (1.0 - mu) + (up + dn) * (0.5 * mu)
    g0, w0 = RW_GATE * LANES, RW_WD * LANES
    o_ref[:, :g0] = y[:, :g0].astype(o_ref.dtype)
    o_ref[:, g0:w0] = _silu(y[:, g0:w0]).astype(o_ref.dtype)
    o_ref[:, w0:w0 + LANES] = jnp.tanh(y[:, w0:w0 + LANES]).astype(o_ref.dtype)
    o_ref[:, w0 + LANES:] = y[:, w0 + LANES:].astype(o_ref.dtype)


def _rwkv_shift(p, mu, seq, tm):
    r = p.shape[0]
    tiles_per_seq = seq // tm
    sub = tm // BF16_ROWS
    last = r // BF16_ROWS - 1
    cb = CB_RW * LANES // SHIFT_COLS
    return pl.pallas_call(
        functools.partial(_rwkv_shift_kernel, tiles_per_seq=tiles_per_seq),
        grid=(r // tm,),
        in_specs=[pl.BlockSpec((tm, SHIFT_COLS), lambda i: (i, cb)),
                  pl.BlockSpec((BF16_ROWS, SHIFT_COLS), lambda i: (jnp.maximum(i * sub - 1, 0), cb)),
                  pl.BlockSpec((BF16_ROWS, SHIFT_COLS), lambda i: (jnp.minimum((i + 1) * sub, last), cb)),
                  pl.BlockSpec((1, SHIFT_COLS), lambda i: (0, 0))],
        out_specs=pl.BlockSpec((tm, SHIFT_COLS), lambda i: (i, 0)),
        out_shape=jax.ShapeDtypeStruct((r, RW_BLOCKS * LANES), BF16),
        compiler_params=_cparams(("parallel",)),
        name="rwkv_shift",
    )(p, p, p, mu.reshape(1, RW_BLOCKS * LANES))


def _rwkv_chunk(seg, c, prm, consts, recv_state, send_state):
    (r_ref, k_ref, v_ref, _, wd_ref, ad_ref) = seg
    (w0, a0, kkw, kaw, rkw, wup, aup) = prm
    (lo_f, hi_f, lo, blockdiag16, headdiag, tri, strict, strict_hi, incl_wide) = consts
    rows = _chunk_rows(c)

    r = r_ref[rows, :].astype(F32)
    k = k_ref[rows, :].astype(F32)
    v16 = v_ref[rows, :]
    v = v16.astype(F32)
    z_w = _dot(wd_ref[rows, :], wup)
    z_a = _dot(ad_ref[rows, :], aup)
    yield
    lw = -RWKV_DECAY_SCALE * _sigmoid(w0 + z_w)
    a = _sigmoid(a0 + z_a)
    kd = k * (1.0 + (a - 1.0) * kaw)
    kk = k * kkw
    kk = kk * lax.rsqrt(_seg_sum(kk * kk, lo) + L2_EPS)
    kka = kk * a
    bonus = _seg_sum(r * kd * rkw, lo) * v

    cw = _tri_cumsum(tri, lw)
    yield
    tot = jnp.sum(lw, axis=0, keepdims=True)
    e_neg = jnp.exp(-cw)
    e_tail = jnp.exp(tot - cw)
    kap = kk * jnp.exp(cw - lw)
    rt = r * jnp.exp(cw)
    bt = (kka * e_neg).astype(BF16)
    kt = (kd * e_neg).astype(BF16)
    tails = jnp.concatenate([kka * e_tail, kd * e_tail], axis=0).astype(BF16)
    decay = jnp.exp(tot)
    kap_rt = jnp.concatenate([kap, rt], axis=0)

    yt = jnp.concatenate([bt, kt], axis=0)
    gram = _dot_nt(jnp.concatenate([kap_rt * lo_f, kap_rt * hi_f], axis=0).astype(BF16), yt)
    grams = (gram[:LANES], gram[LANES:])
    kap_rt = kap_rt.astype(BF16)
    yield
    vv = jnp.concatenate([v16, v16], axis=0)
    a_b = [g[:CHUNK, :CHUNK] * strict for g in grams]
    akv = _dot(jnp.concatenate([g[:CHUNK] * strict_hi for g in grams], axis=0).astype(BF16), vv)
    akv = jnp.where(lo, akv[:CHUNK], akv[CHUNK:])
    mats = jnp.concatenate([g[CHUNK:] * incl_wide for g in grams], axis=0).astype(BF16)
    x_inv = yield from _unit_tri_inverse_offdiag(a_b, blockdiag16)
    x_inv = jnp.concatenate(x_inv, axis=0).astype(BF16)
    yield

    s2 = recv_state()
    while s2 is None:
        yield
        s2 = recv_state()
    ks_rs = _dot_nt(kap_rt, s2.astype(BF16))
    ks, rs = ks_rs[:CHUNK], ks_rs[CHUNK:]
    yield
    rhs = ks + akv
    xr = _dot(x_inv, rhs.astype(BF16))
    e = rhs + jnp.where(lo, xr[:CHUNK], xr[CHUNK:])
    yield
    ev = jnp.concatenate([-e, v], axis=0).astype(BF16)
    upd = _dot_tn(ev, tails)
    send_state(s2 * decay + upd * headdiag)
    my = _dot(mats, ev)
    y = rs + jnp.where(lo, my[:CHUNK], my[CHUNK:])
    return y, bonus


RWKV_PAIRS_PER_STEP = 2
_RWKV_IN_REFS = 25
_RWKV_SCRATCH = 4
_MT_STRICT, _MT_STRICT_HI, _MT_INCL_WIDE = 0, 2, 4
_MT_BLOCKDIAG, _MT_HEADDIAG, _MT_HALVES, _MT_BANDS = 6, 7, 9, 10


def _rwkv_mask_tables():
    import numpy as np
    t = np.arange(CHUNK)[:, None]
    s = np.arange(CHUNK)[None, :]
    lane = np.arange(LANES)[None, :]
    table = np.zeros((_MT_BANDS, CHUNK, LANES), np.float32)
    tri = np.zeros((2, CHUNK, CHUNK), np.float32)
    for d, (strict, incl) in enumerate(((s < t, s <= t), (s > t, s >= t))):
        table[_MT_STRICT + d, :, :CHUNK] = strict
        table[_MT_STRICT_HI + d, :, CHUNK:] = strict
        table[_MT_INCL_WIDE + d] = np.concatenate([incl, incl], axis=1)
        tri[d] = incl
    table[_MT_BLOCKDIAG, :, :CHUNK] = (t // 16) == (s // 16)
    v = np.arange(LANES)[:, None]
    table[_MT_HEADDIAG:_MT_HEADDIAG + 2] = ((v < RWKV_HD) == (lane < RWKV_HD)).reshape(2, CHUNK, LANES)
    table[_MT_HALVES, 0] = lane[0] < RWKV_HD
    table[_MT_HALVES, 1] = lane[0] >= RWKV_HD
    return table.reshape(_MT_BANDS * CHUNK, LANES), tri


def _rwkv_kernel(*refs, need_ctx):
    n_pairs = RWKV_PAIRS_PER_STEP
    mt_ref, tri_ref = refs[:2]
    refs = refs[2:]
    n_in = n_pairs * _RWKV_IN_REFS
    rest = refs[n_in:]
    if need_ctx:
        y_l, y_c = rest[:2]
        scr = rest[2:]
    else:
        y_l, y_c = rest[0], None
        scr = rest[1:]
    nl = refs[0].shape[0] // CHUNK
    nc = refs[6].shape[0] // CHUNK

    def band(j, rows=CHUNK, lanes=LANES):
        return mt_ref[j * CHUNK:j * CHUNK + rows, :lanes]

    lo = lax.broadcasted_iota(jnp.int32, (1, LANES), 1) < RWKV_HD
    lo_f, hi_f = band(_MT_HALVES, 1), mt_ref[_MT_HALVES * CHUNK + 1:_MT_HALVES * CHUNK + 2, :]
    blockdiag16 = band(_MT_BLOCKDIAG, lanes=CHUNK)
    headdiag = band(_MT_HEADDIAG, rows=LANES)

    def consts(d):
        return (lo_f, hi_f, lo, blockdiag16, headdiag, tri_ref[d], band(_MT_STRICT + d, lanes=CHUNK),
                band(_MT_STRICT_HI + d), band(_MT_INCL_WIDE + d))
    consts_f, consts_b = consts(0), consts(1)

    pairs = []
    for g in range(n_pairs):
        r = refs[g * _RWKV_IN_REFS:(g + 1) * _RWKV_IN_REFS]
        (w0f, w0b, a0f, a0b, kkw, kaw, rkw, lng, lnb, wupf, wupb, aupf, aupb) = [x[0] for x in r[12:]]
        s = scr[g * 2 * _RWKV_SCRATCH:(g + 1) * 2 * _RWKV_SCRATCH]
        pairs.append(dict(
            lat=tuple(r[0:6]), ctx=tuple(r[6:12]),
            prm_f=(w0f, a0f, kkw, kaw, rkw, wupf, aupf),
            prm_b=(w0b, a0b, kkw, kaw, rkw, wupb, aupb),
            scr_lat=s[:_RWKV_SCRATCH], scr_ctx=s[_RWKV_SCRATCH:], lng=lng, lnb=lnb))

    def scan(which, n, carry):
        unroll = _unroll_factor(n, RWKV_UNROLL)

        def body(i, carry):
            states = {}
            gens = []
            for u in range(unroll):
                c = i * unroll + u
                for g, pr in enumerate(pairs):
                    states[(g, "f", -1)], states[(g, "b", -1)] = carry[2 * g], carry[2 * g + 1]
                    for d, cc, prm, consts in (("f", c, pr["prm_f"], consts_f),
                                               ("b", n - 1 - c, pr["prm_b"], consts_b)):
                        gens.append(_rwkv_chunk(
                            pr[which], cc, prm, consts,
                            functools.partial(states.get, (g, d, u - 1)),
                            functools.partial(states.__setitem__, (g, d, u))))
            outs = iter(_run_interleaved(gens))
            for u in range(unroll):
                c = i * unroll + u
                rows_f, rows_b = _chunk_rows(c), _chunk_rows(n - 1 - c)
                for pr in pairs:
                    yf, yb, bf, bb = pr["scr_" + which]
                    (y_f, bon_f), (y_b, bon_b) = next(outs), next(outs)
                    yf[rows_f, :] = y_f
                    bf[rows_f, :] = bon_f.astype(bf.dtype)
                    yb[rows_b, :] = y_b
                    bb[rows_b, :] = bon_b.astype(bb.dtype)
            return tuple(states[(g, d, unroll - 1)] for g in range(n_pairs) for d in ("f", "b"))
        return lax.fori_loop(0, n // unroll, body, carry)

    zero = jnp.zeros((LANES, LANES), F32)
    carry = scan("ctx", nc, (zero,) * (2 * n_pairs))
    scan("lat", nl, carry)

    head_mean = (headdiag * (1.0 / RWKV_HD)).astype(BF16)

    def seg_mean(x):
        hi = x.astype(BF16)
        lo_part = (x - hi.astype(F32)).astype(BF16)
        r = _dot(jnp.concatenate([hi, lo_part], axis=0), head_mean)
        return r[:CHUNK] + r[CHUNK:]

    def finish(which, y_out, n):
        def body(i, _):
            rows = _chunk_rows(i)
            for g, pr in enumerate(pairs):
                yf, yb, bf, bb = pr["scr_" + which]
                gs = pr[which][3]
                y = yf[rows, :] + yb[rows, :]
                d = y - seg_mean(y)
                var = seg_mean(d * d)
                yn = d * lax.rsqrt(var + GN_EPS) * pr["lng"] + pr["lnb"]
                bonus = bf[rows, :].astype(F32) + bb[rows, :].astype(F32)
                y_out[rows, g * LANES:(g + 1) * LANES] = (
                    (yn + bonus) * gs[rows, :].astype(F32)).astype(y_out.dtype)
            return 0
        lax.fori_loop(0, n, body, 0, unroll=_unroll_factor(n, FINISH_UNROLL))

    finish("lat", y_l, nl)
    if need_ctx:
        finish("ctx", y_c, nc)


def _rwkv(ps_lat, ps_ctx, batch, seq, ctx_len, prm, need_ctx):
    n_pairs = RWKV_PAIRS_PER_STEP

    def pair_specs(g):
        hp = lambda s: n_pairs * s + g

        def seg_specs(n):
            return [pl.BlockSpec((n, LANES), lambda b, s: (b, RW_R + hp(s))),
                    pl.BlockSpec((n, LANES), lambda b, s: (b, RW_K + hp(s))),
                    pl.BlockSpec((n, LANES), lambda b, s: (b, RW_V + hp(s))),
                    pl.BlockSpec((n, LANES), lambda b, s: (b, RW_GATE + hp(s))),
                    pl.BlockSpec((n, LANES), lambda b, s: (b, RW_WD)),
                    pl.BlockSpec((n, LANES), lambda b, s: (b, RW_AD))]
        vec_h = pl.BlockSpec((1, 1, LANES), lambda b, s: (hp(s), 0, 0))
        mat_h = pl.BlockSpec((1, LANES, LANES), lambda b, s: (hp(s), 0, 0))
        return seg_specs(seq) + seg_specs(ctx_len) + [vec_h] * 9 + [mat_h] * 4

    table, tri = _rwkv_mask_tables()
    in_specs = [pl.BlockSpec(table.shape, lambda b, s: (0, 0)),
                pl.BlockSpec(tri.shape, lambda b, s: (0, 0, 0))]
    args = [jnp.asarray(table), jnp.asarray(tri, BF16)]
    for g in range(n_pairs):
        in_specs += pair_specs(g)
        args += [ps_lat] * 6 + [ps_ctx] * 6 + list(prm)
    width = n_pairs * LANES
    out_specs = [pl.BlockSpec((seq, width), lambda b, s: (b, s))]
    out_shape = [jax.ShapeDtypeStruct((batch * seq, BRANCH_W), BF16)]
    if need_ctx:
        out_specs.append(pl.BlockSpec((ctx_len, width), lambda b, s: (b, s)))
        out_shape.append(jax.ShapeDtypeStruct((batch * ctx_len, BRANCH_W), BF16))
    def seg_scratch(n):
        return [pltpu.VMEM((n, LANES), F32)] * 2 + [pltpu.VMEM((n, LANES), BF16)] * 2
    scratch = (seg_scratch(seq) + seg_scratch(ctx_len)) * n_pairs
    outs = pl.pallas_call(
        functools.partial(_rwkv_kernel, need_ctx=need_ctx),
        grid=(batch, RWKV_H // 2 // n_pairs),
        in_specs=in_specs,
        out_specs=out_specs,
        out_shape=out_shape,
        scratch_shapes=scratch,
        compiler_params=_cparams(("parallel", "parallel"), RWKV_VMEM_LIMIT),
        name="rwkv",
    )(*args)
    return (outs[0], outs[1]) if need_ctx else (outs[0], None)


def _merge_kernel(x_ref, y1_ref, y2_ref, y3_ref, g1_ref, g2_ref, g3_ref, mod_ref,
                  wo1_ref, wo2_ref, wo3_ref, wout_ref, gp_ref, o_ref):
    m = (_sigmoid(g1_ref[...].astype(F32)) * _dot(y1_ref[...], wo1_ref[...])
         + _sigmoid(g2_ref[...].astype(F32)) * _dot(y2_ref[...], wo2_ref[...])
         + _sigmoid(g3_ref[...].astype(F32)) * _dot(y3_ref[...], wo3_ref[...]))
    mo = _dot(m.astype(BF16), wout_ref[...])
    ms = jnp.mean(mo * mo, axis=-1, keepdims=True)
    out = mo * lax.rsqrt(ms + NORM_EPS) * gp_ref[...]
    o_ref[...] = x_ref[...] + mod_ref[0][2:3] * out


def _merge(x2d, y1, y2, y3, p, mod, mod_index, wo1, wo2, wo3, wout, g_post, tm):
    r, d = x2d.shape
    row = lambda i: (i, 0)
    const = lambda i: (0, 0)
    return pl.pallas_call(
        _merge_kernel,
        grid=(r // tm,),
        in_specs=[pl.BlockSpec((tm, d), row),
                  pl.BlockSpec((tm, BRANCH_W), row),
                  pl.BlockSpec((tm, BRANCH_W), row),
                  pl.BlockSpec((tm, BRANCH_W), row),
                  pl.BlockSpec((tm, d), lambda i: (i, 0)),
                  pl.BlockSpec((tm, d), lambda i: (i, 1)),
                  pl.BlockSpec((tm, d), lambda i: (i, 2)),
                  pl.BlockSpec((1, 3, d), lambda i: (mod_index(i), 0, 0)),
                  pl.BlockSpec((BRANCH_W, d), const),
                  pl.BlockSpec((BRANCH_W, d), const),
                  pl.BlockSpec((BRANCH_W, d), const),
                  pl.BlockSpec((d, d), const),
                  pl.BlockSpec((1, d), const)],
        out_specs=pl.BlockSpec((tm, d), row),
        out_shape=jax.ShapeDtypeStruct((r, d), F32),
        compiler_params=_cparams(("parallel",)),
        name="merge",
    )(x2d, y1, y2, y3, p, p, p, mod, wo1, wo2, wo3, wout, g_post.reshape(1, d))


def _pack_w_in(w_in):
    d = w_in.shape[0]
    o = 0
    gq = w_in[:, o:o + 256]; o += 256
    gk = w_in[:, o:o + 256]; o += 256
    gv = w_in[:, o:o + 512]; o += 512
    gg = w_in[:, o:o + 512]; o += 512
    gwd = w_in[:, o:o + 32]; o += 32
    att = w_in[:, o:o + 1536]; o += 1536
    rw = w_in[:, o:o + 2304]; o += 2304
    mg = w_in[:, o:o + 3072]
    n_qk = (ATT_H + ATT_KV) * ATT_HD
    att_qk = att[:, :n_qk].reshape(d, ATT_H + ATT_KV, ATT_HD)[:, :, _rope_lane_order()].reshape(d, n_qk)
    att = jnp.concatenate([att_qk, att[:, n_qk:]], axis=1)
    pad_wd = jnp.zeros((d, LANES - 32), w_in.dtype)
    pad_end = jnp.zeros((d, LANES), w_in.dtype)
    return jnp.concatenate([mg, att, rw, gq, gk, gv, gg, gwd, pad_wd, pad_end], axis=1).astype(BF16)


def _rope_tables(seq):
    quarter = ATT_HD // 4
    inv = ROPE_THETA ** (-jnp.arange(quarter, dtype=F32) / quarter)
    t = jnp.arange(seq)
    row_pos = (t // GRID_W).astype(F32)
    col_pos = (t % GRID_W).astype(F32)
    ar = row_pos[:, None] * inv[None, :]
    ac = col_pos[:, None] * inv[None, :]
    cos = jnp.concatenate([jnp.cos(ar), jnp.cos(ac), jnp.cos(ar), jnp.cos(ac)], axis=1)
    sin = jnp.concatenate([-jnp.sin(ar), -jnp.sin(ac), jnp.sin(ar), jnp.sin(ac)], axis=1)
    return cos, sin


def _rope_lane_order():
    quarter = ATT_HD // 4
    i = jnp.arange(quarter)
    return jnp.concatenate([i, 2 * quarter + i, quarter + i, 3 * quarter + i])


def _pad_rows(w, lo_half):
    z = jnp.zeros((LANES - w.shape[0], w.shape[1]), w.dtype)
    if lo_half:
        return jnp.concatenate([w, z], axis=0)
    half = LANES // 2
    return jnp.concatenate([z[:half], w, z[half:]], axis=0)


def _pick(n, target):
    t = min(n, target)
    while n % t:
        t //= 2
    return t


def kernel(x, c, ctx, c_ctx, w_mod, b_mod, g_pre, w_in, gla_wup_f, gla_b_f, gla_wup_b, gla_b_b, gla_norm, att_qnorm, att_knorm, rwkv_mu, rwkv_w0_f, rwkv_wup_f, rwkv_w0_b, rwkv_wup_b, rwkv_a0_f, rwkv_aup_f, rwkv_a0_b, rwkv_aup_b, rwkv_kk, rwkv_ka, rwkv_rk, rwkv_ln_g, rwkv_ln_b, w_o_gla, w_o_att, w_o_rwkv, w_out, g_post):
    batch, seq, d = x.shape
    ctx_len = ctx.shape[1]
    depth = w_in.shape[0]
    assert seq % CHUNK == 0 and ctx_len % CHUNK == 0 and seq % GRID_W == 0

    nb = -(-(batch + 1) // SUBLANES) * SUBLANES
    cond = jnp.concatenate([c, c_ctx[None, :], jnp.zeros((nb - batch - 1, d), F32)], axis=0)
    tables = _rope_tables(seq)

    tm_lat = _pick(seq, MATMUL_ROWS)
    tm_ctx = _pick(batch * ctx_len, MATMUL_ROWS)
    tm_merge = _pick(seq, MATMUL_ROWS)
    tm_merge_c = _pick(batch * ctx_len, MATMUL_ROWS)
    tm_prep = _pick(seq, PREP_ROWS)
    tm_prep_c = _pick(ctx_len, PREP_ROWS)
    tq_lat, tk = _pick(seq, FLASH_ROWS), _pick(seq, FLASH_ROWS)
    tq_ctx = _pick(ctx_len, FLASH_ROWS)

    def lat_mod(tm):
        return lambda i: i // (seq // tm)

    def ctx_mod(tm):
        return lambda i: batch

    x2 = x.reshape(batch * seq, d)
    xc2 = ctx.reshape(batch * ctx_len, d)

    def hp(v):
        return v.reshape(RWKV_H // 2, 1, LANES)

    for l in range(depth):
        need_ctx = l < depth - 1
        mod = _modulation(cond, w_mod[l], b_mod[l]).reshape(nb, 3, d)
        w_cat = _pack_w_in(w_in[l])
        p_lat = _inproj(x2, mod, lat_mod(tm_lat), g_pre[l], w_cat, tm_lat)
        p_ctx = _inproj(xc2, mod, ctx_mod(tm_ctx), g_pre[l], w_cat, tm_ctx)

        gains = jnp.stack([att_qnorm[l] * (ATT_HD ** -0.5 * LOG2_E), att_knorm[l]])[:, _rope_lane_order()]
        att_lat = tuple(_qkvprep(p_lat, gains, tables, batch, seq, tm_prep)) + (seq,)
        att_ctx = tuple(_qkvprep(p_ctx, gains, None, batch, ctx_len, tm_prep_c)) + (ctx_len,)
        ps_lat = _rwkv_shift(p_lat, rwkv_mu[l], seq, tm_prep)
        ps_ctx = _rwkv_shift(p_ctx, rwkv_mu[l], ctx_len, tm_prep_c)
        y2 = _flash(att_lat[0], p_lat, seq, att_lat, att_ctx, batch, tq_lat, tk)
        y2c = None
        if need_ctx:
            y2c = _flash(att_ctx[0], p_ctx, ctx_len, None, att_ctx, batch, tq_ctx, tk)

        def gla_w(w, lo_rows):
            w = w.reshape(GLA_LR, GLA_H // 2, LANES).transpose(1, 0, 2)
            z = jnp.zeros((GLA_H // 2, LANES, LANES), w.dtype)
            start = 0 if lo_rows else GLA_LR
            return lax.dynamic_update_slice(z, w, (0, start, 0)).astype(BF16)
        y1, y1c = _gla(p_lat, p_ctx, batch, seq, ctx_len,
                       gla_w(gla_wup_f[l], True), gla_w(gla_wup_b[l], False),
                       gla_b_f[l].reshape(GLA_H // 2, 1, LANES), gla_b_b[l].reshape(GLA_H // 2, 1, LANES),
                       gla_norm[l].reshape(GLA_H, 1, GLA_DV), need_ctx)

        def rw_w(w, lo_rows):
            w = w.reshape(RWKV_LR, RWKV_H // 2, LANES).transpose(1, 0, 2)
            return jnp.stack([_pad_rows(w[i], lo_rows) for i in range(RWKV_H // 2)]).astype(BF16)
        prm = (hp(rwkv_w0_f[l]), hp(rwkv_w0_b[l]), hp(rwkv_a0_f[l]), hp(rwkv_a0_b[l]),
               hp(rwkv_kk[l]), hp(rwkv_ka[l]), hp(rwkv_rk[l]), hp(rwkv_ln_g[l]), hp(rwkv_ln_b[l]),
               rw_w(rwkv_wup_f[l], True), rw_w(rwkv_wup_b[l], False),
               rw_w(rwkv_aup_f[l], True), rw_w(rwkv_aup_b[l], False))
        y3, y3c = _rwkv(ps_lat, ps_ctx, batch, seq, ctx_len, prm, need_ctx)

        wo1, wo2, wo3 = (w_o_gla[l].astype(BF16), w_o_att[l].astype(BF16),
                         w_o_rwkv[l].astype(BF16))
        wout = w_out[l].astype(BF16)
        x2 = _merge(x2, y1, y2, y3, p_lat, mod, lat_mod(tm_merge), wo1, wo2, wo3, wout, g_post[l], tm_merge)
        if need_ctx:
            xc2 = _merge(xc2, y1c, y2c, y3c, p_ctx, mod, ctx_mod(tm_merge_c), wo1, wo2, wo3, wout,
                         g_post[l], tm_merge_c)
    return x2.reshape(batch, seq, d)
```

```python
import functools

import jax
import jax.numpy as jnp
from jax import lax
from jax.experimental import pallas as pl
from jax.experimental.pallas import tpu as pltpu

F32 = jnp.float32
BF16 = jnp.bfloat16

GRID_W = 64
BRANCH_W = 512
GLA_H, GLA_DK, GLA_DV, GLA_LR = 4, 64, 128, 16
GLA_TAU = 16.0
ATT_H, ATT_KV, ATT_HD = 4, 2, 128
ROPE_THETA = 10000.0
RWKV_H, RWKV_HD, RWKV_LR = 8, 64, 64
RWKV_DECAY_SCALE = 0.6065306597
NORM_EPS = 1e-6
GN_EPS = 64e-5
L2_EPS = 1e-12
LOG2_E = 1.4426950408889634

CHUNK = 64
LANES = 128
SUBLANES = 8
BF16_ROWS = 16
MATMUL_ROWS = 1024
PREP_ROWS = 512
FLASH_ROWS = 512
MOD_COLS = 768
VMEM_CAPACITY = 64 * 1024 * 1024
VMEM_LIMIT = VMEM_CAPACITY * 7 // 8
RWKV_VMEM_LIMIT = VMEM_CAPACITY - 2 * 1024 * 1024

CB_MG = 0
CB_ATT_Q = 24
CB_ATT_V = 30
CB_ATT_GATE = 32
CB_RW = 36
RW_BLOCKS = 18
CB_GLA_Q = 54
CB_GLA_K = 56
CB_GLA_V = 58
CB_GLA_GATE = 62
CB_GLA_WD = 66
NP_BLOCKS = 68
RW_R, RW_K, RW_V, RW_GATE, RW_WD, RW_AD = 0, 4, 8, 12, 16, 17
NP = NP_BLOCKS * LANES
TN_PROJ = NP // 4


def _cparams(sem, vmem_limit=VMEM_LIMIT):
    return pltpu.CompilerParams(dimension_semantics=sem, vmem_limit_bytes=vmem_limit)


def _dot(a, b):
    return jnp.dot(a, b, preferred_element_type=F32)


def _dot_nt(a, b):
    return lax.dot_general(a, b, (((1,), (1,)), ((), ())), preferred_element_type=F32)


def _dot_tn(a, b):
    return lax.dot_general(a, b, (((0,), (0,)), ((), ())), preferred_element_type=F32)


def _sigmoid(x):
    return 0.5 + 0.5 * jnp.tanh(0.5 * x)


def _silu(x):
    h = 0.5 * x
    return h + h * jnp.tanh(h)


def _tri_cumsum(tri, x):
    hi = x.astype(BF16)
    lo = (x - hi.astype(F32)).astype(BF16)
    n = x.shape[1]
    if n % LANES:
        return _dot(tri, hi) + _dot(tri, lo)
    r = _dot(tri, jnp.concatenate([hi, lo], axis=1))
    return r[:, :n] + r[:, n:]


GLA_UNROLL = 16
RWKV_UNROLL = 4
FINISH_UNROLL = 8


def _unroll_factor(n_chunks, target):
    u = target
    while n_chunks % u:
        u //= 2
    return u


def _mod_kernel(c_ref, w_ref, b_ref, o_ref):
    c = c_ref[...]
    s = _silu(c)
    o_ref[...] = jnp.dot(s, w_ref[...], preferred_element_type=F32,
                         precision=lax.Precision.HIGHEST) + b_ref[...]


def _modulation(cond, w_mod, b_mod):
    nb, d = cond.shape
    n = w_mod.shape[1]
    tn = _pick(n, MOD_COLS)
    return pl.pallas_call(
        _mod_kernel,
        grid=(n // tn,),
        in_specs=[pl.BlockSpec((nb, d), lambda j: (0, 0)),
                  pl.BlockSpec((d, tn), lambda j: (0, j)),
                  pl.BlockSpec((1, tn), lambda j: (0, j))],
        out_specs=pl.BlockSpec((nb, tn), lambda j: (0, j)),
        out_shape=jax.ShapeDtypeStruct((nb, n), F32),
        compiler_params=_cparams(("parallel",)),
        name="modulation",
    )(cond, w_mod, b_mod.reshape(1, n))


def _inproj_kernel(x_ref, mod_ref, g_ref, w_ref, o_ref, h_ref):
    first = pl.program_id(1) == 0

    @pl.when(first)
    def _():
        x = x_ref[...]
        ms = jnp.mean(x * x, axis=-1, keepdims=True)
        y = x * lax.rsqrt(ms + NORM_EPS) * g_ref[...]
        mod = mod_ref[0]
        h = (y * (1.0 + mod[1:2]) + mod[0:1]).astype(BF16)
        h_ref[...] = h
        o_ref[...] = _dot(h, w_ref[...]).astype(o_ref.dtype)

    @pl.when(jnp.logical_not(first))
    def _():
        o_ref[...] = _dot(h_ref[...], w_ref[...]).astype(o_ref.dtype)


def _inproj(x2d, mod, mod_index, g_pre, w_cat, tm):
    r, d = x2d.shape
    return pl.pallas_call(
        _inproj_kernel,
        grid=(r // tm, NP // TN_PROJ),
        in_specs=[pl.BlockSpec((tm, d), lambda i, j: (i, 0)),
                  pl.BlockSpec((1, 3, d), lambda i, j: (mod_index(i), 0, 0)),
                  pl.BlockSpec((1, d), lambda i, j: (0, 0)),
                  pl.BlockSpec((d, TN_PROJ), lambda i, j: (0, j))],
        out_specs=pl.BlockSpec((tm, TN_PROJ), lambda i, j: (i, j)),
        out_shape=jax.ShapeDtypeStruct((r, NP), BF16),
        scratch_shapes=[pltpu.VMEM((tm, d), BF16)],
        compiler_params=_cparams(("parallel", "arbitrary")),
        name="inproj",
    )(x2d, mod, g_pre.reshape(1, d), w_cat)


def _norm_rope(x, gain, tables):
    x = x.astype(F32)
    ms = jnp.mean(x * x, axis=-1, keepdims=True)
    y = x * lax.rsqrt(ms + NORM_EPS) * gain
    if tables is not None:
        cos, sin = tables
        y = y * cos + pltpu.roll(y, ATT_HD // 2, 1) * sin
    return y.astype(BF16)


ATT_QKV_BLOCKS = ATT_H + 2 * ATT_KV


def _qkvprep_kernel(*refs, rope):
    if rope:
        x_ref, g_ref, cos_ref, sin_ref, qk_ref, vt_ref = refs
        tables = (cos_ref[...], sin_ref[...])
    else:
        x_ref, g_ref, qk_ref, vt_ref = refs
        tables = None
    for h in range(ATT_H + ATT_KV):
        cols = slice(h * LANES, (h + 1) * LANES)
        gain = g_ref[0:1, :] if h < ATT_H else g_ref[1:2, :]
        qk_ref[:, cols] = _norm_rope(x_ref[:, cols], gain, tables)
    for j in range(ATT_KV):
        cols = slice((ATT_H + ATT_KV + j) * LANES, (ATT_H + ATT_KV + j + 1) * LANES)
        base = j * VT_ROWS
        vt_ref[base:base + ATT_HD, :] = x_ref[:, cols].astype(F32).T.astype(BF16)
        vt_ref[base + ATT_HD:base + VT_ROWS, :] = jnp.ones((VT_ROWS - ATT_HD, vt_ref.shape[1]), BF16)


VT_ROWS = ATT_HD + BF16_ROWS


def _qkvprep(p, gains, tables, batch, seq, tm):
    nt = seq // tm
    width = ATT_QKV_BLOCKS * LANES
    cb = CB_ATT_Q * LANES // width
    in_specs = [pl.BlockSpec((tm, width), lambda i: (i, cb)),
                pl.BlockSpec((2, LANES), lambda i: (0, 0))]
    args = [p, gains]
    if tables is not None:
        for tab in tables:
            in_specs.append(pl.BlockSpec((tm, LANES), lambda i: (i % nt, 0)))
            args.append(tab)
    qk_width = (ATT_H + ATT_KV) * LANES
    return pl.pallas_call(
        functools.partial(_qkvprep_kernel, rope=tables is not None),
        grid=(batch * nt,),
        in_specs=in_specs,
        out_specs=[pl.BlockSpec((tm, qk_width), lambda i: (i, 0)),
                   pl.BlockSpec((ATT_KV * VT_ROWS, tm), lambda i: (i // nt, i % nt))],
        out_shape=[jax.ShapeDtypeStruct((batch * seq, qk_width), BF16),
                   jax.ShapeDtypeStruct((batch * ATT_KV * VT_ROWS, seq), BF16)],
        compiler_params=_cparams(("parallel",)),
        name="qkvprep_rope" if tables is not None else "qkvprep",
    )(*args)


FLASH_KV_INTERLEAVE = 8


def _flash_kernel(*refs, n_lat_blocks, tk, has_ctx_kv):
    refs = list(refs)
    q0_ref, q1_ref, g0_ref, g1_ref = refs[:4]
    pos = 4
    if n_lat_blocks:
        kl_ref, vtl_ref = refs[pos:pos + 2]
        pos += 2
    if has_ctx_kv:
        kc_ref, vtc_ref = refs[pos:pos + 2]
        pos += 2
    o_ref = refs[pos]
    tq = q0_ref.shape[0]
    q = jnp.concatenate([q0_ref[...], q1_ref[...]], axis=0)

    def blocks(kvs, carry):
        scores = [_dot_nt(k, q) for k, _ in kvs]
        m, acc = carry
        for s, (_, vt) in zip(scores, kvs):
            m_new = jnp.maximum(m, jnp.max(s, axis=0, keepdims=True))
            p = jnp.exp2(s - m_new)
            acc = jnp.exp2(m - m_new) * acc + _dot(vt, p.astype(BF16))
            m = m_new
        return m, acc

    carry = (jnp.full((1, 2 * tq), -jnp.inf, F32), jnp.zeros((VT_ROWS, 2 * tq), F32))
    if n_lat_blocks:
        nb = _unroll_factor(n_lat_blocks, FLASH_KV_INTERLEAVE)

        def body(i, c):
            kvs = []
            for u in range(nb):
                start = pl.multiple_of((i * nb + u) * tk, tk)
                kvs.append((kl_ref[pl.ds(start, tk), :], vtl_ref[:, pl.ds(start, tk)]))
            return blocks(kvs, c)
        carry = lax.fori_loop(0, n_lat_blocks // nb, body, carry)
    if has_ctx_kv:
        carry = blocks([(kc_ref[...], vtc_ref[...])], carry)
    _, acc = carry
    o = (acc[:ATT_HD] / acc[ATT_HD:ATT_HD + 1]).T
    o_ref[:, :LANES] = (o[:tq] * _silu(g0_ref[...].astype(F32))).astype(o_ref.dtype)
    o_ref[:, LANES:] = (o[tq:] * _silu(g1_ref[...].astype(F32))).astype(o_ref.dtype)


def _flash(qk_q, p_q, seq_q, lat_kv, ctx_kv, batch, tq, tk):
    nq = seq_q // tq
    in_specs = [
        pl.BlockSpec((tq, LANES), lambda b, j, i: (b * nq + i, 2 * j)),
        pl.BlockSpec((tq, LANES), lambda b, j, i: (b * nq + i, 2 * j + 1)),
        pl.BlockSpec((tq, LANES), lambda b, j, i: (b * nq + i, CB_ATT_GATE + 2 * j)),
        pl.BlockSpec((tq, LANES), lambda b, j, i: (b * nq + i, CB_ATT_GATE + 2 * j + 1)),
    ]
    args = [qk_q, qk_q, p_q, p_q]
    n_lat_blocks = 0
    for seg in (lat_kv, ctx_kv):
        if seg is None:
            continue
        qk_s, vt_s, seq_s = seg
        in_specs.append(pl.BlockSpec((seq_s, LANES), lambda b, j, i: (b, ATT_H + j)))
        in_specs.append(pl.BlockSpec((VT_ROWS, seq_s), lambda b, j, i: (b * ATT_KV + j, 0)))
        args += [qk_s, vt_s]
    if lat_kv is not None:
        n_lat_blocks = lat_kv[2] // tk
    return pl.pallas_call(
        functools.partial(_flash_kernel, n_lat_blocks=n_lat_blocks, tk=tk,
                          has_ctx_kv=ctx_kv is not None),
        grid=(batch, ATT_KV, nq),
        in_specs=in_specs,
        out_specs=pl.BlockSpec((tq, 2 * LANES), lambda b, j, i: (b * nq + i, j)),
        out_shape=jax.ShapeDtypeStruct((batch * seq_q, BRANCH_W), BF16),
        compiler_params=_cparams(("parallel", "parallel", "arbitrary")),
        name="flash_lat" if lat_kv is not None else "flash_ctx",
    )(*args)


def _chunk_rows(c):
    return pl.ds(pl.multiple_of(c * CHUNK, CHUNK), CHUNK)


def _gla_chunk(seg, c, wup, bias, consts, recv_state, send_state):
    q_ref, k_ref, v_refs, wd_ref = seg
    tri, incl2, halves = consts
    rows = _chunk_rows(c)
    q = q_ref[rows, :].astype(F32) * (GLA_DK ** -0.5)
    k = k_ref[rows, :].astype(F32)
    vs = [v_ref[rows, :] for v_ref in v_refs]
    z = _dot(wd_ref[rows, :], wup) + bias
    yield
    g = (jnp.minimum(z, 0.0) - jnp.log1p(jnp.exp(-jnp.abs(z)))) * (1.0 / GLA_TAU)
    gc = _tri_cumsum(tri, g)
    yield
    tot = jnp.sum(g, axis=0, keepdims=True)
    q_dec = q * jnp.exp(gc)
    k_inv = (k * jnp.exp(-gc)).astype(BF16)
    k_tail = k * jnp.exp(tot - gc)
    q_heads = [(q_dec * hm).astype(BF16) for hm in halves]
    a = (_dot_nt(jnp.concatenate(q_heads, axis=0), k_inv) * incl2).astype(BF16)
    upds = [_dot_tn(v, (k_tail * hm).astype(BF16)) for v, hm in zip(vs, halves)]
    yield
    sts = recv_state()
    while sts is None:
        yield
        sts = recv_state()
    decay = jnp.exp(tot)
    send_state(tuple(st * decay + upd for st, upd in zip(sts, upds)))
    return tuple(_dot(a[h * CHUNK:(h + 1) * CHUNK], vs[h]) + _dot_nt(q_heads[h], sts[h].astype(BF16))
                 for h in range(2))


_GLA_SEG_REFS = 7


def _gla_kernel(*refs, need_ctx):
    mt_ref, tri_ref = refs[:2]
    lat = refs[2:2 + _GLA_SEG_REFS]
    ctx = refs[2 + _GLA_SEG_REFS:2 + 2 * _GLA_SEG_REFS]
    wupf_ref, wupb_ref, bf_ref, bb_ref, gn_ref = refs[2 + 2 * _GLA_SEG_REFS:7 + 2 * _GLA_SEG_REFS]
    rest = refs[7 + 2 * _GLA_SEG_REFS:]
    if need_ctx:
        y_l, y_c = rest[:2]
        scr = rest[2:]
    else:
        y_l, y_c = rest[0], None
        scr = rest[1:]
    scr_l, scr_c = scr[:4], scr[4:]
    nl = lat[0].shape[0] // CHUNK
    nc = ctx[0].shape[0] // CHUNK
    wupf, wupb = wupf_ref[0], wupb_ref[0]
    bias_f, bias_b = bf_ref[0], bb_ref[0]
    row0 = _MT_HALVES * CHUNK
    halves = (mt_ref[row0:row0 + 1, :], mt_ref[row0 + 1:row0 + 2, :])

    def consts(d):
        incl = mt_ref[(_MT_INCL_WIDE + d) * CHUNK:(_MT_INCL_WIDE + d + 1) * CHUNK, :CHUNK]
        return tri_ref[d], jnp.concatenate([incl, incl], axis=0), halves
    consts_f, consts_b = consts(0), consts(1)

    def scan(seg, scr, n, carry):
        unroll = _unroll_factor(n, GLA_UNROLL)
        seg = (seg[0], seg[1], seg[2:4], seg[6])

        def body(i, carry):
            states = {("f", -1): carry[0], ("b", -1): carry[1]}
            gens = []
            for u in range(unroll):
                c = i * unroll + u
                for d, cc, wup, bias, cst in (("f", c, wupf, bias_f, consts_f),
                                              ("b", n - 1 - c, wupb, bias_b, consts_b)):
                    gens.append(_gla_chunk(
                        seg, cc, wup, bias, cst,
                        functools.partial(states.get, (d, u - 1)),
                        functools.partial(states.__setitem__, (d, u))))
            outs = _run_interleaved(gens)
            for u in range(unroll):
                c = i * unroll + u
                for h in range(2):
                    scr[h][_chunk_rows(c), :] = outs[2 * u][h]
                    scr[2 + h][_chunk_rows(n - 1 - c), :] = outs[2 * u + 1][h]
            return states[("f", unroll - 1)], states[("b", unroll - 1)]
        return lax.fori_loop(0, n // unroll, body, carry)

    zero = (jnp.zeros((GLA_DV, LANES), F32),) * 2
    carry = scan(ctx, scr_c, nc, (zero, zero))
    scan(lat, scr_l, nl, carry)

    def finish(seg, scr, y, n):
        def body(i, _):
            rows = _chunk_rows(i)
            for h in range(2):
                o = scr[h][rows, :] + scr[2 + h][rows, :]
                ms = jnp.mean(o * o, axis=-1, keepdims=True)
                o = o * lax.rsqrt(ms + NORM_EPS) * gn_ref[h]
                gate = seg[4 + h][rows, :].astype(F32)
                y[rows, h * LANES:(h + 1) * LANES] = (o * _silu(gate)).astype(y.dtype)
            return 0
        lax.fori_loop(0, n, body, 0, unroll=_unroll_factor(n, FINISH_UNROLL))

    finish(lat, scr_l, y_l, nl)
    if need_ctx:
        finish(ctx, scr_c, y_c, nc)


def _gla(p_lat, p_ctx, batch, seq, ctx_len, wupf, wupb, bf, bb, gnorm, need_ctx):
    def seg_specs(n):
        return [pl.BlockSpec((n, LANES), lambda b, g: (b, CB_GLA_Q + g)),
                pl.BlockSpec((n, LANES), lambda b, g: (b, CB_GLA_K + g)),
                pl.BlockSpec((n, LANES), lambda b, g: (b, CB_GLA_V + 2 * g)),
                pl.BlockSpec((n, LANES), lambda b, g: (b, CB_GLA_V + 2 * g + 1)),
                pl.BlockSpec((n, LANES), lambda b, g: (b, CB_GLA_GATE + 2 * g)),
                pl.BlockSpec((n, LANES), lambda b, g: (b, CB_GLA_GATE + 2 * g + 1)),
                pl.BlockSpec((n, LANES), lambda b, g: (b, CB_GLA_WD))]
    table, tri = _rwkv_mask_tables()
    w_spec = pl.BlockSpec((1, LANES, LANES), lambda b, g: (g, 0, 0))
    b_spec = pl.BlockSpec((1, 1, LANES), lambda b, g: (g, 0, 0))
    g_spec = pl.BlockSpec((2, 1, GLA_DV), lambda b, g: (g, 0, 0))
    out_specs = [pl.BlockSpec((seq, 2 * LANES), lambda b, g: (b, g))]
    out_shape = [jax.ShapeDtypeStruct((batch * seq, BRANCH_W), BF16)]
    if need_ctx:
        out_specs.append(pl.BlockSpec((ctx_len, 2 * LANES), lambda b, g: (b, g)))
        out_shape.append(jax.ShapeDtypeStruct((batch * ctx_len, BRANCH_W), BF16))
    outs = pl.pallas_call(
        functools.partial(_gla_kernel, need_ctx=need_ctx),
        grid=(batch, GLA_H // 2),
        in_specs=([pl.BlockSpec(table.shape, lambda b, g: (0, 0)),
                   pl.BlockSpec(tri.shape, lambda b, g: (0, 0, 0))]
                  + seg_specs(seq) + seg_specs(ctx_len) + [w_spec, w_spec, b_spec, b_spec, g_spec]),
        out_specs=out_specs,
        out_shape=out_shape,
        scratch_shapes=[pltpu.VMEM((seq, LANES), F32)] * 4 + [pltpu.VMEM((ctx_len, LANES), F32)] * 4,
        compiler_params=_cparams(("parallel", "parallel")),
        name="gla",
    )(jnp.asarray(table), jnp.asarray(tri, BF16), *([p_lat] * _GLA_SEG_REFS + [p_ctx] * _GLA_SEG_REFS),
      wupf, wupb, bf, bb, gnorm)
    return (outs[0], outs[1]) if need_ctx else (outs[0], None)


def _seg_sum(x, lo):
    s0 = jnp.sum(jnp.where(lo, x, 0.0), axis=-1, keepdims=True)
    s1 = jnp.sum(jnp.where(lo, 0.0, x), axis=-1, keepdims=True)
    return jnp.where(lo, s0, s1)


def _run_interleaved(gens):
    gens = list(gens)
    out = [None] * len(gens)
    live = list(range(len(gens)))
    while live:
        still = []
        for j in live:
            try:
                next(gens[j])
                still.append(j)
            except StopIteration as stop:
                out[j] = stop.value
        live = still
    return out


def _unit_tri_inverse_offdiag(mats, blockdiag):
    def mm(x, y):
        return _dot(x.astype(BF16), y.astype(BF16))

    def mm2(x1, x2, y):
        r = mm(jnp.concatenate([x1, x2], axis=0), y)
        return r[:CHUNK], r[CHUNK:]

    nd = [-(a * blockdiag) for a in mats]
    no = [-a - d for a, d in zip(mats, nd)]
    n2 = [mm(d, d) for d in nd]
    yield
    r = [mm2(s, d, s) for d, s in zip(nd, n2)]
    n4 = [ri[0] for ri in r]
    x = [d + s + ri[1] for d, s, ri in zip(nd, n2, r)]
    yield
    r = [mm2(s, xi, s) for xi, s in zip(x, n4)]
    n8 = [ri[0] for ri in r]
    x = [xi + s + ri[1] for xi, s, ri in zip(x, n4, r)]
    yield
    xd = [xi + s + mm(xi, s) for xi, s in zip(x, n8)]
    yield
    m = [o + mm(xi, o) for xi, o in zip(xd, no)]
    yield
    m2 = [mm(mi, mi) for mi in m]
    yield
    xq = [mi + s + mm(mi, s) for mi, s in zip(m, m2)]
    yield
    return [q + d + mm(q, d) for q, d in zip(xq, xd)]


SHIFT_COLS = RW_BLOCKS * LANES


def _rwkv_shift_kernel(cur_ref, prev_ref, next_ref, mu_ref, o_ref, *, tiles_per_seq):
    t = pl.program_id(0) % tiles_per_seq
    x = cur_ref[...].astype(F32)
    tm = x.shape[0]
    prev_row = jnp.where(t > 0, prev_ref[...].astype(F32)[BF16_ROWS - 1:BF16_ROWS, :], 0.0)
    next_row = jnp.where(t < tiles_per_seq - 1, next_ref[...].astype(F32)[0:1, :], 0.0)
    row8 = lax.broadcasted_iota(jnp.int32, (8, x.shape[1]), 0)
    up = pltpu.roll(x, 1, 0)
    up = jnp.concatenate([jnp.where(row8 == 0, prev_row, up[:8]), up[8:]], axis=0)
    dn = pltpu.roll(x, tm - 1, 0)
    dn = jnp.concatenate([dn[:tm - 8], jnp.where(row8 == 7, next_row, dn[tm - 8:])], axis=0)
    mu = mu_ref[...]
    y = x * (1.0 - mu) + (up + dn) * (0.5 * mu)
    g0, w0 = RW_GATE * LANES, RW_WD * LANES
    o_ref[:, :g0] = y[:, :g0].astype(o_ref.dtype)
    o_ref[:, g0:w0] = _silu(y[:, g0:w0]).astype(o_ref.dtype)
    o_ref[:, w0:w0 + LANES] = jnp.tanh(y[:, w0:w0 + LANES]).astype(o_ref.dtype)
    o_ref[:, w0 + LANES:] = y[:, w0 + LANES:].astype(o_ref.dtype)


def _rwkv_shift(p, mu, seq, tm):
    r = p.shape[0]
    tiles_per_seq = seq // tm
    sub = tm // BF16_ROWS
    last = r // BF16_ROWS - 1
    cb = CB_RW * LANES // SHIFT_COLS
    return pl.pallas_call(
        functools.partial(_rwkv_shift_kernel, tiles_per_seq=tiles_per_seq),
        grid=(r // tm,),
        in_specs=[pl.BlockSpec((tm, SHIFT_COLS), lambda i: (i, cb)),
                  pl.BlockSpec((BF16_ROWS, SHIFT_COLS), lambda i: (jnp.maximum(i * sub - 1, 0), cb)),
                  pl.BlockSpec((BF16_ROWS, SHIFT_COLS), lambda i: (jnp.minimum((i + 1) * sub, last), cb)),
                  pl.BlockSpec((1, SHIFT_COLS), lambda i: (0, 0))],
        out_specs=pl.BlockSpec((tm, SHIFT_COLS), lambda i: (i, 0)),
        out_shape=jax.ShapeDtypeStruct((r, RW_BLOCKS * LANES), BF16),
        compiler_params=_cparams(("parallel",)),
        name="rwkv_shift",
    )(p, p, p, mu.reshape(1, RW_BLOCKS * LANES))


def _rwkv_chunk(seg, c, prm, consts, recv_state, send_state):
    (r_ref, k_ref, v_ref, _, wd_ref, ad_ref) = seg
    (w0, a0, kkw, kaw, rkw, wup, aup) = prm
    (lo_f, hi_f, lo, blockdiag16, headdiag, tri, strict, strict_hi, incl_wide) = consts
    rows = _chunk_rows(c)

    r = r_ref[rows, :].astype(F32)
    k = k_ref[rows, :].astype(F32)
    v16 = v_ref[rows, :]
    v = v16.astype(F32)
    z_w = _dot(wd_ref[rows, :], wup)
    z_a = _dot(ad_ref[rows, :], aup)
    yield
    lw = -RWKV_DECAY_SCALE * _sigmoid(w0 + z_w)
    a = _sigmoid(a0 + z_a)
    kd = k * (1.0 + (a - 1.0) * kaw)
    kk = k * kkw
    kk = kk * lax.rsqrt(_seg_sum(kk * kk, lo) + L2_EPS)
    kka = kk * a
    bonus = _seg_sum(r * kd * rkw, lo) * v

    cw = _tri_cumsum(tri, lw)
    yield
    tot = jnp.sum(lw, axis=0, keepdims=True)
    e_neg = jnp.exp(-cw)
    e_tail = jnp.exp(tot - cw)
    kap = kk * jnp.exp(cw - lw)
    rt = r * jnp.exp(cw)
    bt = (kka * e_neg).astype(BF16)
    kt = (kd * e_neg).astype(BF16)
    tails = jnp.concatenate([kka * e_tail, kd * e_tail], axis=0).astype(BF16)
    decay = jnp.exp(tot)
    kap_rt = jnp.concatenate([kap, rt], axis=0)

    yt = jnp.concatenate([bt, kt], axis=0)
    gram = _dot_nt(jnp.concatenate([kap_rt * lo_f, kap_rt * hi_f], axis=0).astype(BF16), yt)
    grams = (gram[:LANES], gram[LANES:])
    kap_rt = kap_rt.astype(BF16)
    yield
    vv = jnp.concatenate([v16, v16], axis=0)
    a_b = [g[:CHUNK, :CHUNK] * strict for g in grams]
    akv = _dot(jnp.concatenate([g[:CHUNK] * strict_hi for g in grams], axis=0).astype(BF16), vv)
    akv = jnp.where(lo, akv[:CHUNK], akv[CHUNK:])
    mats = jnp.concatenate([g[CHUNK:] * incl_wide for g in grams], axis=0).astype(BF16)
    x_inv = yield from _unit_tri_inverse_offdiag(a_b, blockdiag16)
    x_inv = jnp.concatenate(x_inv, axis=0).astype(BF16)
    yield

    s2 = recv_state()
    while s2 is None:
        yield
        s2 = recv_state()
    ks_rs = _dot_nt(kap_rt, s2.astype(BF16))
    ks, rs = ks_rs[:CHUNK], ks_rs[CHUNK:]
    yield
    rhs = ks + akv
    xr = _dot(x_inv, rhs.astype(BF16))
    e = rhs + jnp.where(lo, xr[:CHUNK], xr[CHUNK:])
    yield
    ev = jnp.concatenate([-e, v], axis=0).astype(BF16)
    upd = _dot_tn(ev, tails)
    send_state(s2 * decay + upd * headdiag)
    my = _dot(mats, ev)
    y = rs + jnp.where(lo, my[:CHUNK], my[CHUNK:])
    return y, bonus


RWKV_PAIRS_PER_STEP = 2
_RWKV_IN_REFS = 25
_RWKV_SCRATCH = 4
_MT_STRICT, _MT_STRICT_HI, _MT_INCL_WIDE = 0, 2, 4
_MT_BLOCKDIAG, _MT_HEADDIAG, _MT_HALVES, _MT_BANDS = 6, 7, 9, 10


def _rwkv_mask_tables():
    import numpy as np
    t = np.arange(CHUNK)[:, None]
    s = np.arange(CHUNK)[None, :]
    lane = np.arange(LANES)[None, :]
    table = np.zeros((_MT_BANDS, CHUNK, LANES), np.float32)
    tri = np.zeros((2, CHUNK, CHUNK), np.float32)
    for d, (strict, incl) in enumerate(((s < t, s <= t), (s > t, s >= t))):
        table[_MT_STRICT + d, :, :CHUNK] = strict
        table[_MT_STRICT_HI + d, :, CHUNK:] = strict
        table[_MT_INCL_WIDE + d] = np.concatenate([incl, incl], axis=1)
        tri[d] = incl
    table[_MT_BLOCKDIAG, :, :CHUNK] = (t // 16) == (s // 16)
    v = np.arange(LANES)[:, None]
    table[_MT_HEADDIAG:_MT_HEADDIAG + 2] = ((v < RWKV_HD) == (lane < RWKV_HD)).reshape(2, CHUNK, LANES)
    table[_MT_HALVES, 0] = lane[0] < RWKV_HD
    table[_MT_HALVES, 1] = lane[0] >= RWKV_HD
    return table.reshape(_MT_BANDS * CHUNK, LANES), tri


def _rwkv_kernel(*refs, need_ctx):
    n_pairs = RWKV_PAIRS_PER_STEP
    mt_ref, tri_ref = refs[:2]
    refs = refs[2:]
    n_in = n_pairs * _RWKV_IN_REFS
    rest = refs[n_in:]
    if need_ctx:
        y_l, y_c = rest[:2]
        scr = rest[2:]
    else:
        y_l, y_c = rest[0], None
        scr = rest[1:]
    nl = refs[0].shape[0] // CHUNK
    nc = refs[6].shape[0] // CHUNK

    def band(j, rows=CHUNK, lanes=LANES):
        return mt_ref[j * CHUNK:j * CHUNK + rows, :lanes]

    lo = lax.broadcasted_iota(jnp.int32, (1, LANES), 1) < RWKV_HD
    lo_f, hi_f = band(_MT_HALVES, 1), mt_ref[_MT_HALVES * CHUNK + 1:_MT_HALVES * CHUNK + 2, :]
    blockdiag16 = band(_MT_BLOCKDIAG, lanes=CHUNK)
    headdiag = band(_MT_HEADDIAG, rows=LANES)

    def consts(d):
        return (lo_f, hi_f, lo, blockdiag16, headdiag, tri_ref[d], band(_MT_STRICT + d, lanes=CHUNK),
                band(_MT_STRICT_HI + d), band(_MT_INCL_WIDE + d))
    consts_f, consts_b = consts(0), consts(1)

    pairs = []
    for g in range(n_pairs):
        r = refs[g * _RWKV_IN_REFS:(g + 1) * _RWKV_IN_REFS]
        (w0f, w0b, a0f, a0b, kkw, kaw, rkw, lng, lnb, wupf, wupb, aupf, aupb) = [x[0] for x in r[12:]]
        s = scr[g * 2 * _RWKV_SCRATCH:(g + 1) * 2 * _RWKV_SCRATCH]
        pairs.append(dict(
            lat=tuple(r[0:6]), ctx=tuple(r[6:12]),
            prm_f=(w0f, a0f, kkw, kaw, rkw, wupf, aupf),
            prm_b=(w0b, a0b, kkw, kaw, rkw, wupb, aupb),
            scr_lat=s[:_RWKV_SCRATCH], scr_ctx=s[_RWKV_SCRATCH:], lng=lng, lnb=lnb))

    def scan(which, n, carry):
        unroll = _unroll_factor(n, RWKV_UNROLL)

        def body(i, carry):
            states = {}
            gens = []
            for u in range(unroll):
                c = i * unroll + u
                for g, pr in enumerate(pairs):
                    states[(g, "f", -1)], states[(g, "b", -1)] = carry[2 * g], carry[2 * g + 1]
                    for d, cc, prm, consts in (("f", c, pr["prm_f"], consts_f),
                                               ("b", n - 1 - c, pr["prm_b"], consts_b)):
                        gens.append(_rwkv_chunk(
                            pr[which], cc, prm, consts,
                            functools.partial(states.get, (g, d, u - 1)),
                            functools.partial(states.__setitem__, (g, d, u))))
            outs = iter(_run_interleaved(gens))
            for u in range(unroll):
                c = i * unroll + u
                rows_f, rows_b = _chunk_rows(c), _chunk_rows(n - 1 - c)
                for pr in pairs:
                    yf, yb, bf, bb = pr["scr_" + which]
                    (y_f, bon_f), (y_b, bon_b) = next(outs), next(outs)
                    yf[rows_f, :] = y_f
                    bf[rows_f, :] = bon_f.astype(bf.dtype)
                    yb[rows_b, :] = y_b
                    bb[rows_b, :] = bon_b.astype(bb.dtype)
            return tuple(states[(g, d, unroll - 1)] for g in range(n_pairs) for d in ("f", "b"))
        return lax.fori_loop(0, n // unroll, body, carry)

    zero = jnp.zeros((LANES, LANES), F32)
    carry = scan("ctx", nc, (zero,) * (2 * n_pairs))
    scan("lat", nl, carry)

    head_mean = (headdiag * (1.0 / RWKV_HD)).astype(BF16)

    def seg_mean(x):
        hi = x.astype(BF16)
        lo_part = (x - hi.astype(F32)).astype(BF16)
        r = _dot(jnp.concatenate([hi, lo_part], axis=0), head_mean)
        return r[:CHUNK] + r[CHUNK:]

    def finish(which, y_out, n):
        def body(i, _):
            rows = _chunk_rows(i)
            for g, pr in enumerate(pairs):
                yf, yb, bf, bb = pr["scr_" + which]
                gs = pr[which][3]
                y = yf[rows, :] + yb[rows, :]
                d = y - seg_mean(y)
                var = seg_mean(d * d)
                yn = d * lax.rsqrt(var + GN_EPS) * pr["lng"] + pr["lnb"]
                bonus = bf[rows, :].astype(F32) + bb[rows, :].astype(F32)
                y_out[rows, g * LANES:(g + 1) * LANES] = (
                    (yn + bonus) * gs[rows, :].astype(F32)).astype(y_out.dtype)
            return 0
        lax.fori_loop(0, n, body, 0, unroll=_unroll_factor(n, FINISH_UNROLL))

    finish("lat", y_l, nl)
    if need_ctx:
        finish("ctx", y_c, nc)


def _rwkv(ps_lat, ps_ctx, batch, seq, ctx_len, prm, need_ctx):
    n_pairs = RWKV_PAIRS_PER_STEP

    def pair_specs(g):
        hp = lambda s: n_pairs * s + g

        def seg_specs(n):
            return [pl.BlockSpec((n, LANES), lambda b, s: (b, RW_R + hp(s))),
                    pl.BlockSpec((n, LANES), lambda b, s: (b, RW_K + hp(s))),
                    pl.BlockSpec((n, LANES), lambda b, s: (b, RW_V + hp(s))),
                    pl.BlockSpec((n, LANES), lambda b, s: (b, RW_GATE + hp(s))),
                    pl.BlockSpec((n, LANES), lambda b, s: (b, RW_WD)),
                    pl.BlockSpec((n, LANES), lambda b, s: (b, RW_AD))]
        vec_h = pl.BlockSpec((1, 1, LANES), lambda b, s: (hp(s), 0, 0))
        mat_h = pl.BlockSpec((1, LANES, LANES), lambda b, s: (hp(s), 0, 0))
        return seg_specs(seq) + seg_specs(ctx_len) + [vec_h] * 9 + [mat_h] * 4

    table, tri = _rwkv_mask_tables()
    in_specs = [pl.BlockSpec(table.shape, lambda b, s: (0, 0)),
                pl.BlockSpec(tri.shape, lambda b, s: (0, 0, 0))]
    args = [jnp.asarray(table), jnp.asarray(tri, BF16)]
    for g in range(n_pairs):
        in_specs += pair_specs(g)
        args += [ps_lat] * 6 + [ps_ctx] * 6 + list(prm)
    width = n_pairs * LANES
    out_specs = [pl.BlockSpec((seq, width), lambda b, s: (b, s))]
    out_shape = [jax.ShapeDtypeStruct((batch * seq, BRANCH_W), BF16)]
    if need_ctx:
        out_specs.append(pl.BlockSpec((ctx_len, width), lambda b, s: (b, s)))
        out_shape.append(jax.ShapeDtypeStruct((batch * ctx_len, BRANCH_W), BF16))
    def seg_scratch(n):
        return [pltpu.VMEM((n, LANES), F32)] * 2 + [pltpu.VMEM((n, LANES), BF16)] * 2
    scratch = (seg_scratch(seq) + seg_scratch(ctx_len)) * n_pairs
    outs = pl.pallas_call(
        functools.partial(_rwkv_kernel, need_ctx=need_ctx),
        grid=(batch, RWKV_H // 2 // n_pairs),
        in_specs=in_specs,
        out_specs=out_specs,
        out_shape=out_shape,
        scratch_shapes=scratch,
        compiler_params=_cparams(("parallel", "parallel"), RWKV_VMEM_LIMIT),
        name="rwkv",
    )(*args)
    return (outs[0], outs[1]) if need_ctx else (outs[0], None)


def _merge_kernel(x_ref, y1_ref, y2_ref, y3_ref, g1_ref, g2_ref, g3_ref, mod_ref,
                  wo1_ref, wo2_ref, wo3_ref, wout_ref, gp_ref, o_ref):
    m = (_sigmoid(g1_ref[...].astype(F32)) * _dot(y1_ref[...], wo1_ref[...])
         + _sigmoid(g2_ref[...].astype(F32)) * _dot(y2_ref[...], wo2_ref[...])
         + _sigmoid(g3_ref[...].astype(F32)) * _dot(y3_ref[...], wo3_ref[...]))
    mo = _dot(m.astype(BF16), wout_ref[...])
    ms = jnp.mean(mo * mo, axis=-1, keepdims=True)
    out = mo * lax.rsqrt(ms + NORM_EPS) * gp_ref[...]
    o_ref[...] = x_ref[...] + mod_ref[0][2:3] * out


def _merge(x2d, y1, y2, y3, p, mod, mod_index, wo1, wo2, wo3, wout, g_post, tm):
    r, d = x2d.shape
    row = lambda i: (i, 0)
    const = lambda i: (0, 0)
    return pl.pallas_call(
        _merge_kernel,
        grid=(r // tm,),
        in_specs=[pl.BlockSpec((tm, d), row),
                  pl.BlockSpec((tm, BRANCH_W), row),
                  pl.BlockSpec((tm, BRANCH_W), row),
                  pl.BlockSpec((tm, BRANCH_W), row),
                  pl.BlockSpec((tm, d), lambda i: (i, 0)),
                  pl.BlockSpec((tm, d), lambda i: (i, 1)),
                  pl.BlockSpec((tm, d), lambda i: (i, 2)),
                  pl.BlockSpec((1, 3, d), lambda i: (mod_index(i), 0, 0)),
                  pl.BlockSpec((BRANCH_W, d), const),
                  pl.BlockSpec((BRANCH_W, d), const),
                  pl.BlockSpec((BRANCH_W, d), const),
                  pl.BlockSpec((d, d), const),
                  pl.BlockSpec((1, d), const)],
        out_specs=pl.BlockSpec((tm, d), row),
        out_shape=jax.ShapeDtypeStruct((r, d), F32),
        compiler_params=_cparams(("parallel",)),
        name="merge",
    )(x2d, y1, y2, y3, p, p, p, mod, wo1, wo2, wo3, wout, g_post.reshape(1, d))


def _pack_w_in(w_in):
    d = w_in.shape[0]
    o = 0
    gq = w_in[:, o:o + 256]; o += 256
    gk = w_in[:, o:o + 256]; o += 256
    gv = w_in[:, o:o + 512]; o += 512
    gg = w_in[:, o:o + 512]; o += 512
    gwd = w_in[:, o:o + 32]; o += 32
    att = w_in[:, o:o + 1536]; o += 1536
    rw = w_in[:, o:o + 2304]; o += 2304
    mg = w_in[:, o:o + 3072]
    n_qk = (ATT_H + ATT_KV) * ATT_HD
    att_qk = att[:, :n_qk].reshape(d, ATT_H + ATT_KV, ATT_HD)[:, :, _rope_lane_order()].reshape(d, n_qk)
    att = jnp.concatenate([att_qk, att[:, n_qk:]], axis=1)
    pad_wd = jnp.zeros((d, LANES - 32), w_in.dtype)
    pad_end = jnp.zeros((d, LANES), w_in.dtype)
    return jnp.concatenate([mg, att, rw, gq, gk, gv, gg, gwd, pad_wd, pad_end], axis=1).astype(BF16)


def _rope_tables(seq):
    quarter = ATT_HD // 4
    inv = ROPE_THETA ** (-jnp.arange(quarter, dtype=F32) / quarter)
    t = jnp.arange(seq)
    row_pos = (t // GRID_W).astype(F32)
    col_pos = (t % GRID_W).astype(F32)
    ar = row_pos[:, None] * inv[None, :]
    ac = col_pos[:, None] * inv[None, :]
    cos = jnp.concatenate([jnp.cos(ar), jnp.cos(ac), jnp.cos(ar), jnp.cos(ac)], axis=1)
    sin = jnp.concatenate([-jnp.sin(ar), -jnp.sin(ac), jnp.sin(ar), jnp.sin(ac)], axis=1)
    return cos, sin


def _rope_lane_order():
    quarter = ATT_HD // 4
    i = jnp.arange(quarter)
    return jnp.concatenate([i, 2 * quarter + i, quarter + i, 3 * quarter + i])


def _pad_rows(w, lo_half):
    z = jnp.zeros((LANES - w.shape[0], w.shape[1]), w.dtype)
    if lo_half:
        return jnp.concatenate([w, z], axis=0)
    half = LANES // 2
    return jnp.concatenate([z[:half], w, z[half:]], axis=0)


def _pick(n, target):
    t = min(n, target)
    while n % t:
        t //= 2
    return t


def kernel(x, c, ctx, c_ctx, w_mod, b_mod, g_pre, w_in, gla_wup_f, gla_b_f, gla_wup_b, gla_b_b, gla_norm, att_qnorm, att_knorm, rwkv_mu, rwkv_w0_f, rwkv_wup_f, rwkv_w0_b, rwkv_wup_b, rwkv_a0_f, rwkv_aup_f, rwkv_a0_b, rwkv_aup_b, rwkv_kk, rwkv_ka, rwkv_rk, rwkv_ln_g, rwkv_ln_b, w_o_gla, w_o_att, w_o_rwkv, w_out, g_post):
    batch, seq, d = x.shape
    ctx_len = ctx.shape[1]
    depth = w_in.shape[0]
    assert seq % CHUNK == 0 and ctx_len % CHUNK == 0 and seq % GRID_W == 0

    nb = -(-(batch + 1) // SUBLANES) * SUBLANES
    cond = jnp.concatenate([c, c_ctx[None, :], jnp.zeros((nb - batch - 1, d), F32)], axis=0)
    tables = _rope_tables(seq)

    tm_lat = _pick(seq, MATMUL_ROWS)
    tm_ctx = _pick(batch * ctx_len, MATMUL_ROWS)
    tm_merge = _pick(seq, MATMUL_ROWS)
    tm_merge_c = _pick(batch * ctx_len, MATMUL_ROWS)
    tm_prep = _pick(seq, PREP_ROWS)
    tm_prep_c = _pick(ctx_len, PREP_ROWS)
    tq_lat, tk = _pick(seq, FLASH_ROWS), _pick(seq, FLASH_ROWS)
    tq_ctx = _pick(ctx_len, FLASH_ROWS)

    def lat_mod(tm):
        return lambda i: i // (seq // tm)

    def ctx_mod(tm):
        return lambda i: batch

    x2 = x.reshape(batch * seq, d)
    xc2 = ctx.reshape(batch * ctx_len, d)

    def hp(v):
        return v.reshape(RWKV_H // 2, 1, LANES)

    for l in range(depth):
        need_ctx = l < depth - 1
        mod = _modulation(cond, w_mod[l], b_mod[l]).reshape(nb, 3, d)
        w_cat = _pack_w_in(w_in[l])
        p_lat = _inproj(x2, mod, lat_mod(tm_lat), g_pre[l], w_cat, tm_lat)
        p_ctx = _inproj(xc2, mod, ctx_mod(tm_ctx), g_pre[l], w_cat, tm_ctx)

        gains = jnp.stack([att_qnorm[l] * (ATT_HD ** -0.5 * LOG2_E), att_knorm[l]])[:, _rope_lane_order()]
        att_lat = tuple(_qkvprep(p_lat, gains, tables, batch, seq, tm_prep)) + (seq,)
        att_ctx = tuple(_qkvprep(p_ctx, gains, None, batch, ctx_len, tm_prep_c)) + (ctx_len,)
        ps_lat = _rwkv_shift(p_lat, rwkv_mu[l], seq, tm_prep)
        ps_ctx = _rwkv_shift(p_ctx, rwkv_mu[l], ctx_len, tm_prep_c)
        y2 = _flash(att_lat[0], p_lat, seq, att_lat, att_ctx, batch, tq_lat, tk)
        y2c = None
        if need_ctx:
            y2c = _flash(att_ctx[0], p_ctx, ctx_len, None, att_ctx, batch, tq_ctx, tk)

        def gla_w(w, lo_rows):
            w = w.reshape(GLA_LR, GLA_H // 2, LANES).transpose(1, 0, 2)
            z = jnp.zeros((GLA_H // 2, LANES, LANES), w.dtype)
            start = 0 if lo_rows else GLA_LR
            return lax.dynamic_update_slice(z, w, (0, start, 0)).astype(BF16)
        y1, y1c = _gla(p_lat, p_ctx, batch, seq, ctx_len,
                       gla_w(gla_wup_f[l], True), gla_w(gla_wup_b[l], False),
                       gla_b_f[l].reshape(GLA_H // 2, 1, LANES), gla_b_b[l].reshape(GLA_H // 2, 1, LANES),
                       gla_norm[l].reshape(GLA_H, 1, GLA_DV), need_ctx)

        def rw_w(w, lo_rows):
            w = w.reshape(RWKV_LR, RWKV_H // 2, LANES).transpose(1, 0, 2)
            return jnp.stack([_pad_rows(w[i], lo_rows) for i in range(RWKV_H // 2)]).astype(BF16)
        prm = (hp(rwkv_w0_f[l]), hp(rwkv_w0_b[l]), hp(rwkv_a0_f[l]), hp(rwkv_a0_b[l]),
               hp(rwkv_kk[l]), hp(rwkv_ka[l]), hp(rwkv_rk[l]), hp(rwkv_ln_g[l]), hp(rwkv_ln_b[l]),
               rw_w(rwkv_wup_f[l], True), rw_w(rwkv_wup_b[l], False),
               rw_w(rwkv_aup_f[l], True), rw_w(rwkv_aup_b[l], False))
        y3, y3c = _rwkv(ps_lat, ps_ctx, batch, seq, ctx_len, prm, need_ctx)

        wo1, wo2, wo3 = (w_o_gla[l].astype(BF16), w_o_att[l].astype(BF16),
                         w_o_rwkv[l].astype(BF16))
        wout = w_out[l].astype(BF16)
        x2 = _merge(x2, y1, y2, y3, p_lat, mod, lat_mod(tm_merge), wo1, wo2, wo3, wout, g_post[l], tm_merge)
        if need_ctx:
            xc2 = _merge(xc2, y1c, y2c, y3c, p_ctx, mod, ctx_mod(tm_merge_c), wo1, wo2, wo3, wout,
                         g_post[l], tm_merge_c)
    return x2.reshape(batch, seq, d)
```

```python
import functools

import jax
import jax.numpy as jnp
from jax import lax
from jax.experimental import pallas as pl
from jax.experimental.pallas import tpu as pltpu

F32 = jnp.float32
BF16 = jnp.bfloat16

GRID_W = 64
BRANCH_W = 512
GLA_H, GLA_DK, GLA_DV, GLA_LR = 4, 64, 128, 16
GLA_TAU = 16.0
ATT_H, ATT_KV, ATT_HD = 4, 2, 128
ROPE_THETA = 10000.0
RWKV_H, RWKV_HD, RWKV_LR = 8, 64, 64
RWKV_DECAY_SCALE = 0.6065306597
NORM_EPS = 1e-6
GN_EPS = 64e-5
L2_EPS = 1e-12
LOG2_E = 1.4426950408889634

CHUNK = 64
LANES = 128
SUBLANES = 8
BF16_ROWS = 16
MATMUL_ROWS = 1024
PREP_ROWS = 512
FLASH_ROWS = 512
MOD_COLS = 768
VMEM_CAPACITY = 64 * 1024 * 1024
VMEM_LIMIT = VMEM_CAPACITY * 7 // 8
RWKV_VMEM_LIMIT = VMEM_CAPACITY - 2 * 1024 * 1024

CB_MG = 0
CB_ATT_Q = 24
CB_ATT_V = 30
CB_ATT_GATE = 32
CB_RW = 36
RW_BLOCKS = 18
CB_GLA_Q = 54
CB_GLA_K = 56
CB_GLA_V = 58
CB_GLA_GATE = 62
CB_GLA_WD = 66
NP_BLOCKS = 68
RW_R, RW_K, RW_V, RW_GATE, RW_WD, RW_AD = 0, 4, 8, 12, 16, 17
NP = NP_BLOCKS * LANES
TN_PROJ = NP // 2


def _cparams(sem, vmem_limit=VMEM_LIMIT):
    return pltpu.CompilerParams(dimension_semantics=sem, vmem_limit_bytes=vmem_limit)


def _dot(a, b):
    return jnp.dot(a, b, preferred_element_type=F32)


def _dot_nt(a, b):
    return lax.dot_general(a, b, (((1,), (1,)), ((), ())), preferred_element_type=F32)


def _dot_tn(a, b):
    return lax.dot_general(a, b, (((0,), (0,)), ((), ())), preferred_element_type=F32)


def _sigmoid(x):
    return 0.5 + 0.5 * jnp.tanh(0.5 * x)


def _silu(x):
    h = 0.5 * x
    return h + h * jnp.tanh(h)


def _tri_cumsum(tri, x):
    hi = x.astype(BF16)
    lo = (x - hi.astype(F32)).astype(BF16)
    n = x.shape[1]
    if n % LANES:
        return _dot(tri, hi) + _dot(tri, lo)
    r = _dot(tri, jnp.concatenate([hi, lo], axis=1))
    return r[:, :n] + r[:, n:]


GLA_UNROLL = 16
RWKV_UNROLL = 4
FINISH_UNROLL = 8


def _unroll_factor(n_chunks, target):
    u = target
    while n_chunks % u:
        u //= 2
    return u


def _mod_kernel(c_ref, w_ref, b_ref, o_ref):
    c = c_ref[...]
    s = _silu(c)
    o_ref[...] = jnp.dot(s, w_ref[...], preferred_element_type=F32,
                         precision=lax.Precision.HIGHEST) + b_ref[...]


def _modulation(cond, w_mod, b_mod):
    nb, d = cond.shape
    n = w_mod.shape[1]
    tn = _pick(n, MOD_COLS)
    return pl.pallas_call(
        _mod_kernel,
        grid=(n // tn,),
        in_specs=[pl.BlockSpec((nb, d), lambda j: (0, 0)),
                  pl.BlockSpec((d, tn), lambda j: (0, j)),
                  pl.BlockSpec((1, tn), lambda j: (0, j))],
        out_specs=pl.BlockSpec((nb, tn), lambda j: (0, j)),
        out_shape=jax.ShapeDtypeStruct((nb, n), F32),
        compiler_params=_cparams(("parallel",)),
        name="modulation",
    )(cond, w_mod, b_mod.reshape(1, n))


def _inproj_kernel(x_ref, mod_ref, g_ref, w_ref, o_ref, h_ref):
    first = pl.program_id(1) == 0

    @pl.when(first)
    def _():
        x = x_ref[...]
        ms = jnp.mean(x * x, axis=-1, keepdims=True)
        y = x * lax.rsqrt(ms + NORM_EPS) * g_ref[...]
        mod = mod_ref[0]
        h = (y * (1.0 + mod[1:2]) + mod[0:1]).astype(BF16)
        h_ref[...] = h
        o_ref[...] = _dot(h, w_ref[...]).astype(o_ref.dtype)

    @pl.when(jnp.logical_not(first))
    def _():
        o_ref[...] = _dot(h_ref[...], w_ref[...]).astype(o_ref.dtype)


def _inproj(x2d, mod, mod_index, g_pre, w_cat, tm):
    r, d = x2d.shape
    return pl.pallas_call(
        _inproj_kernel,
        grid=(r // tm, NP // TN_PROJ),
        in_specs=[pl.BlockSpec((tm, d), lambda i, j: (i, 0)),
                  pl.BlockSpec((1, 3, d), lambda i, j: (mod_index(i), 0, 0)),
                  pl.BlockSpec((1, d), lambda i, j: (0, 0)),
                  pl.BlockSpec((d, TN_PROJ), lambda i, j: (0, j))],
        out_specs=pl.BlockSpec((tm, TN_PROJ), lambda i, j: (i, j)),
        out_shape=jax.ShapeDtypeStruct((r, NP), BF16),
        scratch_shapes=[pltpu.VMEM((tm, d), BF16)],
        compiler_params=_cparams(("parallel", "arbitrary")),
        name="inproj",
    )(x2d, mod, g_pre.reshape(1, d), w_cat)


def _norm_rope(x, gain, tables):
    x = x.astype(F32)
    ms = jnp.mean(x * x, axis=-1, keepdims=True)
    y = x * lax.rsqrt(ms + NORM_EPS) * gain
    if tables is not None:
        cos, sin = tables
        y = y * cos + pltpu.roll(y, ATT_HD // 2, 1) * sin
    return y.astype(BF16)


ATT_QKV_BLOCKS = ATT_H + 2 * ATT_KV


def _qkvprep_kernel(*refs, rope):
    if rope:
        x_ref, g_ref, cos_ref, sin_ref, qk_ref, vt_ref = refs
        tables = (cos_ref[...], sin_ref[...])
    else:
        x_ref, g_ref, qk_ref, vt_ref = refs
        tables = None
    for h in range(ATT_H + ATT_KV):
        cols = slice(h * LANES, (h + 1) * LANES)
        gain = g_ref[0:1, :] if h < ATT_H else g_ref[1:2, :]
        qk_ref[:, cols] = _norm_rope(x_ref[:, cols], gain, tables)
    for j in range(ATT_KV):
        cols = slice((ATT_H + ATT_KV + j) * LANES, (ATT_H + ATT_KV + j + 1) * LANES)
        base = j * VT_ROWS
        vt_ref[base:base + ATT_HD, :] = x_ref[:, cols].astype(F32).T.astype(BF16)
        vt_ref[base + ATT_HD:base + VT_ROWS, :] = jnp.ones((VT_ROWS - ATT_HD, vt_ref.shape[1]), BF16)


VT_ROWS = ATT_HD + BF16_ROWS


def _qkvprep(p, gains, tables, batch, seq, tm):
    nt = seq // tm
    width = ATT_QKV_BLOCKS * LANES
    cb = CB_ATT_Q * LANES // width
    in_specs = [pl.BlockSpec((tm, width), lambda i: (i, cb)),
                pl.BlockSpec((2, LANES), lambda i: (0, 0))]
    args = [p, gains]
    if tables is not None:
        for tab in tables:
            in_specs.append(pl.BlockSpec((tm, LANES), lambda i: (i % nt, 0)))
            args.append(tab)
    qk_width = (ATT_H + ATT_KV) * LANES
    return pl.pallas_call(
        functools.partial(_qkvprep_kernel, rope=tables is not None),
        grid=(batch * nt,),
        in_specs=in_specs,
        out_specs=[pl.BlockSpec((tm, qk_width), lambda i: (i, 0)),
                   pl.BlockSpec((ATT_KV * VT_ROWS, tm), lambda i: (i // nt, i % nt))],
        out_shape=[jax.ShapeDtypeStruct((batch * seq, qk_width), BF16),
                   jax.ShapeDtypeStruct((batch * ATT_KV * VT_ROWS, seq), BF16)],
        compiler_params=_cparams(("parallel",)),
        name="qkvprep_rope" if tables is not None else "qkvprep",
    )(*args)


FLASH_KV_INTERLEAVE = 8


def _flash_kernel(*refs, n_lat_blocks, tk, has_ctx_kv):
    refs = list(refs)
    q0_ref, q1_ref, g0_ref, g1_ref = refs[:4]
    pos = 4
    if n_lat_blocks:
        kl_ref, vtl_ref = refs[pos:pos + 2]
        pos += 2
    if has_ctx_kv:
        kc_ref, vtc_ref = refs[pos:pos + 2]
        pos += 2
    o_ref = refs[pos]
    tq = q0_ref.shape[0]
    q = jnp.concatenate([q0_ref[...], q1_ref[...]], axis=0)

    def blocks(kvs, carry):
        scores = [_dot_nt(k, q) for k, _ in kvs]
        m, acc = carry
        for s, (_, vt) in zip(scores, kvs):
            m_new = jnp.maximum(m, jnp.max(s, axis=0, keepdims=True))
            p = jnp.exp2(s - m_new)
            acc = jnp.exp2(m - m_new) * acc + _dot(vt, p.astype(BF16))
            m = m_new
        return m, acc

    carry = (jnp.full((1, 2 * tq), -jnp.inf, F32), jnp.zeros((VT_ROWS, 2 * tq), F32))
    if n_lat_blocks:
        nb = _unroll_factor(n_lat_blocks, FLASH_KV_INTERLEAVE)

        def body(i, c):
            kvs = []
            for u in range(nb):
                start = pl.multiple_of((i * nb + u) * tk, tk)
                kvs.append((kl_ref[pl.ds(start, tk), :], vtl_ref[:, pl.ds(start, tk)]))
            return blocks(kvs, c)
        carry = lax.fori_loop(0, n_lat_blocks // nb, body, carry)
    if has_ctx_kv:
        carry = blocks([(kc_ref[...], vtc_ref[...])], carry)
    _, acc = carry
    o = (acc[:ATT_HD] / acc[ATT_HD:ATT_HD + 1]).T
    o_ref[:, :LANES] = (o[:tq] * _silu(g0_ref[...].astype(F32))).astype(o_ref.dtype)
    o_ref[:, LANES:] = (o[tq:] * _silu(g1_ref[...].astype(F32))).astype(o_ref.dtype)


def _flash(qk_q, p_q, seq_q, lat_kv, ctx_kv, batch, tq, tk):
    nq = seq_q // tq
    in_specs = [
        pl.BlockSpec((tq, LANES), lambda b, j, i: (b * nq + i, 2 * j)),
        pl.BlockSpec((tq, LANES), lambda b, j, i: (b * nq + i, 2 * j + 1)),
        pl.BlockSpec((tq, LANES), lambda b, j, i: (b * nq + i, CB_ATT_GATE + 2 * j)),
        pl.BlockSpec((tq, LANES), lambda b, j, i: (b * nq + i, CB_ATT_GATE + 2 * j + 1)),
    ]
    args = [qk_q, qk_q, p_q, p_q]
    n_lat_blocks = 0
    for seg in (lat_kv, ctx_kv):
        if seg is None:
            continue
        qk_s, vt_s, seq_s = seg
        in_specs.append(pl.BlockSpec((seq_s, LANES), lambda b, j, i: (b, ATT_H + j)))
        in_specs.append(pl.BlockSpec((VT_ROWS, seq_s), lambda b, j, i: (b * ATT_KV + j, 0)))
        args += [qk_s, vt_s]
    if lat_kv is not None:
        n_lat_blocks = lat_kv[2] // tk
    return pl.pallas_call(
        functools.partial(_flash_kernel, n_lat_blocks=n_lat_blocks, tk=tk,
                          has_ctx_kv=ctx_kv is not None),
        grid=(batch, ATT_KV, nq),
        in_specs=in_specs,
        out_specs=pl.BlockSpec((tq, 2 * LANES), lambda b, j, i: (b * nq + i, j)),
        out_shape=jax.ShapeDtypeStruct((batch * seq_q, BRANCH_W), BF16),
        compiler_params=_cparams(("parallel", "parallel", "arbitrary")),
        name="flash_lat" if lat_kv is not None else "flash_ctx",
    )(*args)


def _chunk_rows(c):
    return pl.ds(pl.multiple_of(c * CHUNK, CHUNK), CHUNK)


def _gla_chunk(seg, c, wup, bias, consts, recv_state, send_state):
    q_ref, k_ref, v_refs, wd_ref = seg
    tri, incl2, halves = consts
    rows = _chunk_rows(c)
    q = q_ref[rows, :].astype(F32) * (GLA_DK ** -0.5)
    k = k_ref[rows, :].astype(F32)
    vs = [v_ref[rows, :] for v_ref in v_refs]
    z = _dot(wd_ref[rows, :], wup) + bias
    yield
    g = (jnp.minimum(z, 0.0) - jnp.log1p(jnp.exp(-jnp.abs(z)))) * (1.0 / GLA_TAU)
    gc = _tri_cumsum(tri, g)
    yield
    tot = jnp.sum(g, axis=0, keepdims=True)
    q_dec = q * jnp.exp(gc)
    k_inv = (k * jnp.exp(-gc)).astype(BF16)
    k_tail = k * jnp.exp(tot - gc)
    q_heads = [(q_dec * hm).astype(BF16) for hm in halves]
    a = (_dot_nt(jnp.concatenate(q_heads, axis=0), k_inv) * incl2).astype(BF16)
    upds = [_dot_tn(v, (k_tail * hm).astype(BF16)) for v, hm in zip(vs, halves)]
    yield
    sts = recv_state()
    while sts is None:
        yield
        sts = recv_state()
    decay = jnp.exp(tot)
    send_state(tuple(st * decay + upd for st, upd in zip(sts, upds)))
    return tuple(_dot(a[h * CHUNK:(h + 1) * CHUNK], vs[h]) + _dot_nt(q_heads[h], sts[h].astype(BF16))
                 for h in range(2))


_GLA_SEG_REFS = 7


def _gla_kernel(*refs, need_ctx):
    mt_ref, tri_ref = refs[:2]
    lat = refs[2:2 + _GLA_SEG_REFS]
    ctx = refs[2 + _GLA_SEG_REFS:2 + 2 * _GLA_SEG_REFS]
    wupf_ref, wupb_ref, bf_ref, bb_ref, gn_ref = refs[2 + 2 * _GLA_SEG_REFS:7 + 2 * _GLA_SEG_REFS]
    rest = refs[7 + 2 * _GLA_SEG_REFS:]
    if need_ctx:
        y_l, y_c = rest[:2]
        scr = rest[2:]
    else:
        y_l, y_c = rest[0], None
        scr = rest[1:]
    scr_l, scr_c = scr[:4], scr[4:]
    nl = lat[0].shape[0] // CHUNK
    nc = ctx[0].shape[0] // CHUNK
    wupf, wupb = wupf_ref[0], wupb_ref[0]
    bias_f, bias_b = bf_ref[0], bb_ref[0]
    row0 = _MT_HALVES * CHUNK
    halves = (mt_ref[row0:row0 + 1, :], mt_ref[row0 + 1:row0 + 2, :])

    def consts(d):
        incl = mt_ref[(_MT_INCL_WIDE + d) * CHUNK:(_MT_INCL_WIDE + d + 1) * CHUNK, :CHUNK]
        return tri_ref[d], jnp.concatenate([incl, incl], axis=0), halves
    consts_f, consts_b = consts(0), consts(1)

    def scan(seg, scr, n, carry):
        unroll = _unroll_factor(n, GLA_UNROLL)
        seg = (seg[0], seg[1], seg[2:4], seg[6])

        def body(i, carry):
            states = {("f", -1): carry[0], ("b", -1): carry[1]}
            gens = []
            for u in range(unroll):
                c = i * unroll + u
                for d, cc, wup, bias, cst in (("f", c, wupf, bias_f, consts_f),
                                              ("b", n - 1 - c, wupb, bias_b, consts_b)):
                    gens.append(_gla_chunk(
                        seg, cc, wup, bias, cst,
                        functools.partial(states.get, (d, u - 1)),
                        functools.partial(states.__setitem__, (d, u))))
            outs = _run_interleaved(gens)
            for u in range(unroll):
                c = i * unroll + u
                for h in range(2):
                    scr[h][_chunk_rows(c), :] = outs[2 * u][h]
                    scr[2 + h][_chunk_rows(n - 1 - c), :] = outs[2 * u + 1][h]
            return states[("f", unroll - 1)], states[("b", unroll - 1)]
        return lax.fori_loop(0, n // unroll, body, carry)

    zero = (jnp.zeros((GLA_DV, LANES), F32),) * 2
    carry = scan(ctx, scr_c, nc, (zero, zero))
    scan(lat, scr_l, nl, carry)

    def finish(seg, scr, y, n):
        def body(i, _):
            rows = _chunk_rows(i)
            for h in range(2):
                o = scr[h][rows, :] + scr[2 + h][rows, :]
                ms = jnp.mean(o * o, axis=-1, keepdims=True)
                o = o * lax.rsqrt(ms + NORM_EPS) * gn_ref[h]
                gate = seg[4 + h][rows, :].astype(F32)
                y[rows, h * LANES:(h + 1) * LANES] = (o * _silu(gate)).astype(y.dtype)
            return 0
        lax.fori_loop(0, n, body, 0, unroll=_unroll_factor(n, FINISH_UNROLL))

    finish(lat, scr_l, y_l, nl)
    if need_ctx:
        finish(ctx, scr_c, y_c, nc)


def _gla(p_lat, p_ctx, batch, seq, ctx_len, wupf, wupb, bf, bb, gnorm, need_ctx):
    def seg_specs(n):
        return [pl.BlockSpec((n, LANES), lambda b, g: (b, CB_GLA_Q + g)),
                pl.BlockSpec((n, LANES), lambda b, g: (b, CB_GLA_K + g)),
                pl.BlockSpec((n, LANES), lambda b, g: (b, CB_GLA_V + 2 * g)),
                pl.BlockSpec((n, LANES), lambda b, g: (b, CB_GLA_V + 2 * g + 1)),
                pl.BlockSpec((n, LANES), lambda b, g: (b, CB_GLA_GATE + 2 * g)),
                pl.BlockSpec((n, LANES), lambda b, g: (b, CB_GLA_GATE + 2 * g + 1)),
                pl.BlockSpec((n, LANES), lambda b, g: (b, CB_GLA_WD))]
    table, tri = _rwkv_mask_tables()
    w_spec = pl.BlockSpec((1, LANES, LANES), lambda b, g: (g, 0, 0))
    b_spec = pl.BlockSpec((1, 1, LANES), lambda b, g: (g, 0, 0))
    g_spec = pl.BlockSpec((2, 1, GLA_DV), lambda b, g: (g, 0, 0))
    out_specs = [pl.BlockSpec((seq, 2 * LANES), lambda b, g: (b, g))]
    out_shape = [jax.ShapeDtypeStruct((batch * seq, BRANCH_W), BF16)]
    if need_ctx:
        out_specs.append(pl.BlockSpec((ctx_len, 2 * LANES), lambda b, g: (b, g)))
        out_shape.append(jax.ShapeDtypeStruct((batch * ctx_len, BRANCH_W), BF16))
    outs = pl.pallas_call(
        functools.partial(_gla_kernel, need_ctx=need_ctx),
        grid=(batch, GLA_H // 2),
        in_specs=([pl.BlockSpec(table.shape, lambda b, g: (0, 0)),
                   pl.BlockSpec(tri.shape, lambda b, g: (0, 0, 0))]
                  + seg_specs(seq) + seg_specs(ctx_len) + [w_spec, w_spec, b_spec, b_spec, g_spec]),
        out_specs=out_specs,
        out_shape=out_shape,
        scratch_shapes=[pltpu.VMEM((seq, LANES), F32)] * 4 + [pltpu.VMEM((ctx_len, LANES), F32)] * 4,
        compiler_params=_cparams(("parallel", "parallel")),
        name="gla",
    )(jnp.asarray(table), jnp.asarray(tri, BF16), *([p_lat] * _GLA_SEG_REFS + [p_ctx] * _GLA_SEG_REFS),
      wupf, wupb, bf, bb, gnorm)
    return (outs[0], outs[1]) if need_ctx else (outs[0], None)


def _seg_sum(x, lo):
    s0 = jnp.sum(jnp.where(lo, x, 0.0), axis=-1, keepdims=True)
    s1 = jnp.sum(jnp.where(lo, 0.0, x), axis=-1, keepdims=True)
    return jnp.where(lo, s0, s1)


def _run_interleaved(gens):
    gens = list(gens)
    out = [None] * len(gens)
    live = list(range(len(gens)))
    while live:
        still = []
        for j in live:
            try:
                next(gens[j])
                still.append(j)
            except StopIteration as stop:
                out[j] = stop.value
        live = still
    return out


def _unit_tri_inverse_offdiag(mats, blockdiag):
    def mm(x, y):
        return _dot(x.astype(BF16), y.astype(BF16))

    def mm2(x1, x2, y):
        r = mm(jnp.concatenate([x1, x2], axis=0), y)
        return r[:CHUNK], r[CHUNK:]

    nd = [-(a * blockdiag) for a in mats]
    no = [-a - d for a, d in zip(mats, nd)]
    n2 = [mm(d, d) for d in nd]
    yield
    r = [mm2(s, d, s) for d, s in zip(nd, n2)]
    n4 = [ri[0] for ri in r]
    x = [d + s + ri[1] for d, s, ri in zip(nd, n2, r)]
    yield
    r = [mm2(s, xi, s) for xi, s in zip(x, n4)]
    n8 = [ri[0] for ri in r]
    x = [xi + s + ri[1] for xi, s, ri in zip(x, n4, r)]
    yield
    xd = [xi + s + mm(xi, s) for xi, s in zip(x, n8)]
    yield
    m = [o + mm(xi, o) for xi, o in zip(xd, no)]
    yield
    m2 = [mm(mi, mi) for mi in m]
    yield
    xq = [mi + s + mm(mi, s) for mi, s in zip(m, m2)]
    yield
    return [q + d + mm(q, d) for q, d in zip(xq, xd)]


SHIFT_COLS = RW_BLOCKS * LANES


def _rwkv_shift_kernel(cur_ref, prev_ref, next_ref, mu_ref, o_ref, *, tiles_per_seq):
    t = pl.program_id(0) % tiles_per_seq
    x = cur_ref[...].astype(F32)
    tm = x.shape[0]
    prev_row = jnp.where(t > 0, prev_ref[...].astype(F32)[BF16_ROWS - 1:BF16_ROWS, :], 0.0)
    next_row = jnp.where(t < tiles_per_seq - 1, next_ref[...].astype(F32)[0:1, :], 0.0)
    row8 = lax.broadcasted_iota(jnp.int32, (8, x.shape[1]), 0)
    up = pltpu.roll(x, 1, 0)
    up = jnp.concatenate([jnp.where(row8 == 0, prev_row, up[:8]), up[8:]], axis=0)
    dn = pltpu.roll(x, tm - 1, 0)
    dn = jnp.concatenate([dn[:tm - 8], jnp.where(row8 == 7, next_row, dn[tm - 8:])], axis=0)
    mu = mu_ref[...]
    y = x * (1.0 - mu) + (up + dn) * (0.5 * mu)
    g0, w0 = RW_GATE * LANES, RW_WD * LANES
    o_ref[:, :g0] = y[:, :g0].astype(o_ref.dtype)
    o_ref[:, g0:w0] = _silu(y[:, g0:w0]).astype(o_ref.dtype)
    o_ref[:, w0:w0 + LANES] = jnp.tanh(y[:, w0:w0 + LANES]).astype(o_ref.dtype)
    o_ref[:, w0 + LANES:] = y[:, w0 + LANES:].astype(o_ref.dtype)


def _rwkv_shift(p, mu, seq, tm):
    r = p.shape[0]
    tiles_per_seq = seq // tm
    sub = tm // BF16_ROWS
    last = r // BF16_ROWS - 1
    cb = CB_RW * LANES // SHIFT_COLS
    return pl.pallas_call(
        functools.partial(_rwkv_shift_kernel, tiles_per_seq=tiles_per_seq),
        grid=(r // tm,),
        in_specs=[pl.BlockSpec((tm, SHIFT_COLS), lambda i: (i, cb)),
                  pl.BlockSpec((BF16_ROWS, SHIFT_COLS), lambda i: (jnp.maximum(i * sub - 1, 0), cb)),
                  pl.BlockSpec((BF16_ROWS, SHIFT_COLS), lambda i: (jnp.minimum((i + 1) * sub, last), cb)),
                  pl.BlockSpec((1, SHIFT_COLS), lambda i: (0, 0))],
        out_specs=pl.BlockSpec((tm, SHIFT_COLS), lambda i: (i, 0)),
        out_shape=jax.ShapeDtypeStruct((r, RW_BLOCKS * LANES), BF16),
        compiler_params=_cparams(("parallel",)),
        name="rwkv_shift",
    )(p, p, p, mu.reshape(1, RW_BLOCKS * LANES))


def _rwkv_chunk(seg, c, prm, consts, recv_state, send_state):
    (r_ref, k_ref, v_ref, _, wd_ref, ad_ref) = seg
    (w0, a0, kkw, kaw, rkw, wup, aup) = prm
    (lo_f, hi_f, lo, blockdiag16, headdiag, tri, strict, strict_hi, incl_wide) = consts
    rows = _chunk_rows(c)

    r = r_ref[rows, :].astype(F32)
    k = k_ref[rows, :].astype(F32)
    v16 = v_ref[rows, :]
    v = v16.astype(F32)
    z_w = _dot(wd_ref[rows, :], wup)
    z_a = _dot(ad_ref[rows, :], aup)
    yield
    lw = -RWKV_DECAY_SCALE * _sigmoid(w0 + z_w)
    a = _sigmoid(a0 + z_a)
    kd = k * (1.0 + (a - 1.0) * kaw)
    kk = k * kkw
    kk = kk * lax.rsqrt(_seg_sum(kk * kk, lo) + L2_EPS)
    kka = kk * a
    bonus = _seg_sum(r * kd * rkw, lo) * v

    cw = _tri_cumsum(tri, lw)
    yield
    tot = jnp.sum(lw, axis=0, keepdims=True)
    e_neg = jnp.exp(-cw)
    e_tail = jnp.exp(tot - cw)
    kap = kk * jnp.exp(cw - lw)
    rt = r * jnp.exp(cw)
    bt = (kka * e_neg).astype(BF16)
    kt = (kd * e_neg).astype(BF16)
    tails = jnp.concatenate([kka * e_tail, kd * e_tail], axis=0).astype(BF16)
    decay = jnp.exp(tot)
    kap_rt = jnp.concatenate([kap, rt], axis=0)

    yt = jnp.concatenate([bt, kt], axis=0)
    gram = _dot_nt(jnp.concatenate([kap_rt * lo_f, kap_rt * hi_f], axis=0).astype(BF16), yt)
    grams = (gram[:LANES], gram[LANES:])
    kap_rt = kap_rt.astype(BF16)
    yield
    vv = jnp.concatenate([v16, v16], axis=0)
    a_b = [g[:CHUNK, :CHUNK] * strict for g in grams]
    akv = _dot(jnp.concatenate([g[:CHUNK] * strict_hi for g in grams], axis=0).astype(BF16), vv)
    akv = jnp.where(lo, akv[:CHUNK], akv[CHUNK:])
    mats = jnp.concatenate([g[CHUNK:] * incl_wide for g in grams], axis=0).astype(BF16)
    x_inv = yield from _unit_tri_inverse_offdiag(a_b, blockdiag16)
    x_inv = jnp.concatenate(x_inv, axis=0).astype(BF16)
    yield

    s2 = recv_state()
    while s2 is None:
        yield
        s2 = recv_state()
    ks_rs = _dot_nt(kap_rt, s2.astype(BF16))
    ks, rs = ks_rs[:CHUNK], ks_rs[CHUNK:]
    yield
    rhs = ks + akv
    xr = _dot(x_inv, rhs.astype(BF16))
    e = rhs + jnp.where(lo, xr[:CHUNK], xr[CHUNK:])
    yield
    ev = jnp.concatenate([-e, v], axis=0).astype(BF16)
    upd = _dot_tn(ev, tails)
    send_state(s2 * decay + upd * headdiag)
    my = _dot(mats, ev)
    y = rs + jnp.where(lo, my[:CHUNK], my[CHUNK:])
    return y, bonus


RWKV_PAIRS_PER_STEP = 2
_RWKV_IN_REFS = 25
_RWKV_SCRATCH = 4
_MT_STRICT, _MT_STRICT_HI, _MT_INCL_WIDE = 0, 2, 4
_MT_BLOCKDIAG, _MT_HEADDIAG, _MT_HALVES, _MT_BANDS = 6, 7, 9, 10


def _rwkv_mask_tables():
    import numpy as np
    t = np.arange(CHUNK)[:, None]
    s = np.arange(CHUNK)[None, :]
    lane = np.arange(LANES)[None, :]
    table = np.zeros((_MT_BANDS, CHUNK, LANES), np.float32)
    tri = np.zeros((2, CHUNK, CHUNK), np.float32)
    for d, (strict, incl) in enumerate(((s < t, s <= t), (s > t, s >= t))):
        table[_MT_STRICT + d, :, :CHUNK] = strict
        table[_MT_STRICT_HI + d, :, CHUNK:] = strict
        table[_MT_INCL_WIDE + d] = np.concatenate([incl, incl], axis=1)
        tri[d] = incl
    table[_MT_BLOCKDIAG, :, :CHUNK] = (t // 16) == (s // 16)
    v = np.arange(LANES)[:, None]
    table[_MT_HEADDIAG:_MT_HEADDIAG + 2] = ((v < RWKV_HD) == (lane < RWKV_HD)).reshape(2, CHUNK, LANES)
    table[_MT_HALVES, 0] = lane[0] < RWKV_HD
    table[_MT_HALVES, 1] = lane[0] >= RWKV_HD
    return table.reshape(_MT_BANDS * CHUNK, LANES), tri


def _rwkv_kernel(*refs, need_ctx):
    n_pairs = RWKV_PAIRS_PER_STEP
    mt_ref, tri_ref = refs[:2]
    refs = refs[2:]
    n_in = n_pairs * _RWKV_IN_REFS
    rest = refs[n_in:]
    if need_ctx:
        y_l, y_c = rest[:2]
        scr = rest[2:]
    else:
        y_l, y_c = rest[0], None
        scr = rest[1:]
    nl = refs[0].shape[0] // CHUNK
    nc = refs[6].shape[0] // CHUNK

    def band(j, rows=CHUNK, lanes=LANES):
        return mt_ref[j * CHUNK:j * CHUNK + rows, :lanes]

    lo = lax.broadcasted_iota(jnp.int32, (1, LANES), 1) < RWKV_HD
    lo_f, hi_f = band(_MT_HALVES, 1), mt_ref[_MT_HALVES * CHUNK + 1:_MT_HALVES * CHUNK + 2, :]
    blockdiag16 = band(_MT_BLOCKDIAG, lanes=CHUNK)
    headdiag = band(_MT_HEADDIAG, rows=LANES)

    def consts(d):
        return (lo_f, hi_f, lo, blockdiag16, headdiag, tri_ref[d], band(_MT_STRICT + d, lanes=CHUNK),
                band(_MT_STRICT_HI + d), band(_MT_INCL_WIDE + d))
    consts_f, consts_b = consts(0), consts(1)

    pairs = []
    for g in range(n_pairs):
        r = refs[g * _RWKV_IN_REFS:(g + 1) * _RWKV_IN_REFS]
        (w0f, w0b, a0f, a0b, kkw, kaw, rkw, lng, lnb, wupf, wupb, aupf, aupb) = [x[0] for x in r[12:]]
        s = scr[g * 2 * _RWKV_SCRATCH:(g + 1) * 2 * _RWKV_SCRATCH]
        pairs.append(dict(
            lat=tuple(r[0:6]), ctx=tuple(r[6:12]),
            prm_f=(w0f, a0f, kkw, kaw, rkw, wupf, aupf),
            prm_b=(w0b, a0b, kkw, kaw, rkw, wupb, aupb),
            scr_lat=s[:_RWKV_SCRATCH], scr_ctx=s[_RWKV_SCRATCH:], lng=lng, lnb=lnb))

    def scan(which, n, carry):
        unroll = _unroll_factor(n, RWKV_UNROLL)

        def body(i, carry):
            states = {}
            gens = []
            for u in range(unroll):
                c = i * unroll + u
                for g, pr in enumerate(pairs):
                    states[(g, "f", -1)], states[(g, "b", -1)] = carry[2 * g], carry[2 * g + 1]
                    for d, cc, prm, consts in (("f", c, pr["prm_f"], consts_f),
                                               ("b", n - 1 - c, pr["prm_b"], consts_b)):
                        gens.append(_rwkv_chunk(
                            pr[which], cc, prm, consts,
                            functools.partial(states.get, (g, d, u - 1)),
                            functools.partial(states.__setitem__, (g, d, u))))
            outs = iter(_run_interleaved(gens))
            for u in range(unroll):
                c = i * unroll + u
                rows_f, rows_b = _chunk_rows(c), _chunk_rows(n - 1 - c)
                for pr in pairs:
                    yf, yb, bf, bb = pr["scr_" + which]
                    (y_f, bon_f), (y_b, bon_b) = next(outs), next(outs)
                    yf[rows_f, :] = y_f
                    bf[rows_f, :] = bon_f.astype(bf.dtype)
                    yb[rows_b, :] = y_b
                    bb[rows_b, :] = bon_b.astype(bb.dtype)
            return tuple(states[(g, d, unroll - 1)] for g in range(n_pairs) for d in ("f", "b"))
        return lax.fori_loop(0, n // unroll, body, carry)

    zero = jnp.zeros((LANES, LANES), F32)
    carry = scan("ctx", nc, (zero,) * (2 * n_pairs))
    scan("lat", nl, carry)

    head_mean = (headdiag * (1.0 / RWKV_HD)).astype(BF16)

    def seg_mean(x):
        hi = x.astype(BF16)
        lo_part = (x - hi.astype(F32)).astype(BF16)
        r = _dot(jnp.concatenate([hi, lo_part], axis=0), head_mean)
        return r[:CHUNK] + r[CHUNK:]

    def finish(which, y_out, n):
        def body(i, _):
            rows = _chunk_rows(i)
            for g, pr in enumerate(pairs):
                yf, yb, bf, bb = pr["scr_" + which]
                gs = pr[which][3]
                y = yf[rows, :] + yb[rows, :]
                d = y - seg_mean(y)
                var = seg_mean(d * d)
                yn = d * lax.rsqrt(var + GN_EPS) * pr["lng"] + pr["lnb"]
                bonus = bf[rows, :].astype(F32) + bb[rows, :].astype(F32)
                y_out[rows, g * LANES:(g + 1) * LANES] = (
                    (yn + bonus) * gs[rows, :].astype(F32)).astype(y_out.dtype)
            return 0
        lax.fori_loop(0, n, body, 0, unroll=_unroll_factor(n, FINISH_UNROLL))

    finish("lat", y_l, nl)
    if need_ctx:
        finish("ctx", y_c, nc)


def _rwkv(ps_lat, ps_ctx, batch, seq, ctx_len, prm, need_ctx):
    n_pairs = RWKV_PAIRS_PER_STEP

    def pair_specs(g):
        hp = lambda s: n_pairs * s + g

        def seg_specs(n):
            return [pl.BlockSpec((n, LANES), lambda b, s: (b, RW_R + hp(s))),
                    pl.BlockSpec((n, LANES), lambda b, s: (b, RW_K + hp(s))),
                    pl.BlockSpec((n, LANES), lambda b, s: (b, RW_V + hp(s))),
                    pl.BlockSpec((n, LANES), lambda b, s: (b, RW_GATE + hp(s))),
                    pl.BlockSpec((n, LANES), lambda b, s: (b, RW_WD)),
                    pl.BlockSpec((n, LANES), lambda b, s: (b, RW_AD))]
        vec_h = pl.BlockSpec((1, 1, LANES), lambda b, s: (hp(s), 0, 0))
        mat_h = pl.BlockSpec((1, LANES, LANES), lambda b, s: (hp(s), 0, 0))
        return seg_specs(seq) + seg_specs(ctx_len) + [vec_h] * 9 + [mat_h] * 4

    table, tri = _rwkv_mask_tables()
    in_specs = [pl.BlockSpec(table.shape, lambda b, s: (0, 0)),
                pl.BlockSpec(tri.shape, lambda b, s: (0, 0, 0))]
    args = [jnp.asarray(table), jnp.asarray(tri, BF16)]
    for g in range(n_pairs):
        in_specs += pair_specs(g)
        args += [ps_lat] * 6 + [ps_ctx] * 6 + list(prm)
    width = n_pairs * LANES
    out_specs = [pl.BlockSpec((seq, width), lambda b, s: (b, s))]
    out_shape = [jax.ShapeDtypeStruct((batch * seq, BRANCH_W), BF16)]
    if need_ctx:
        out_specs.append(pl.BlockSpec((ctx_len, width), lambda b, s: (b, s)))
        out_shape.append(jax.ShapeDtypeStruct((batch * ctx_len, BRANCH_W), BF16))
    def seg_scratch(n):
        return [pltpu.VMEM((n, LANES), F32)] * 2 + [pltpu.VMEM((n, LANES), BF16)] * 2
    scratch = (seg_scratch(seq) + seg_scratch(ctx_len)) * n_pairs
    outs = pl.pallas_call(
        functools.partial(_rwkv_kernel, need_ctx=need_ctx),
        grid=(batch, RWKV_H // 2 // n_pairs),
        in_specs=in_specs,
        out_specs=out_specs,
        out_shape=out_shape,
        scratch_shapes=scratch,
        compiler_params=_cparams(("parallel", "parallel"), RWKV_VMEM_LIMIT),
        name="rwkv",
    )(*args)
    return (outs[0], outs[1]) if need_ctx else (outs[0], None)


def _merge_kernel(x_ref, y1_ref, y2_ref, y3_ref, g1_ref, g2_ref, g3_ref, mod_ref,
                  wo1_ref, wo2_ref, wo3_ref, wout_ref, gp_ref, o_ref):
    m = (_sigmoid(g1_ref[...].astype(F32)) * _dot(y1_ref[...], wo1_ref[...])
         + _sigmoid(g2_ref[...].astype(F32)) * _dot(y2_ref[...], wo2_ref[...])
         + _sigmoid(g3_ref[...].astype(F32)) * _dot(y3_ref[...], wo3_ref[...]))
    mo = _dot(m.astype(BF16), wout_ref[...])
    ms = jnp.mean(mo * mo, axis=-1, keepdims=True)
    out = mo * lax.rsqrt(ms + NORM_EPS) * gp_ref[...]
    o_ref[...] = x_ref[...] + mod_ref[0][2:3] * out


def _merge(x2d, y1, y2, y3, p, mod, mod_index, wo1, wo2, wo3, wout, g_post, tm):
    r, d = x2d.shape
    row = lambda i: (i, 0)
    const = lambda i: (0, 0)
    return pl.pallas_call(
        _merge_kernel,
        grid=(r // tm,),
        in_specs=[pl.BlockSpec((tm, d), row),
                  pl.BlockSpec((tm, BRANCH_W), row),
                  pl.BlockSpec((tm, BRANCH_W), row),
                  pl.BlockSpec((tm, BRANCH_W), row),
                  pl.BlockSpec((tm, d), lambda i: (i, 0)),
                  pl.BlockSpec((tm, d), lambda i: (i, 1)),
                  pl.BlockSpec((tm, d), lambda i: (i, 2)),
                  pl.BlockSpec((1, 3, d), lambda i: (mod_index(i), 0, 0)),
                  pl.BlockSpec((BRANCH_W, d), const),
                  pl.BlockSpec((BRANCH_W, d), const),
                  pl.BlockSpec((BRANCH_W, d), const),
                  pl.BlockSpec((d, d), const),
                  pl.BlockSpec((1, d), const)],
        out_specs=pl.BlockSpec((tm, d), row),
        out_shape=jax.ShapeDtypeStruct((r, d), F32),
        compiler_params=_cparams(("parallel",)),
        name="merge",
    )(x2d, y1, y2, y3, p, p, p, mod, wo1, wo2, wo3, wout, g_post.reshape(1, d))


def _pack_w_in(w_in):
    d = w_in.shape[0]
    o = 0
    gq = w_in[:, o:o + 256]; o += 256
    gk = w_in[:, o:o + 256]; o += 256
    gv = w_in[:, o:o + 512]; o += 512
    gg = w_in[:, o:o + 512]; o += 512
    gwd = w_in[:, o:o + 32]; o += 32
    att = w_in[:, o:o + 1536]; o += 1536
    rw = w_in[:, o:o + 2304]; o += 2304
    mg = w_in[:, o:o + 3072]
    n_qk = (ATT_H + ATT_KV) * ATT_HD
    att_qk = att[:, :n_qk].reshape(d, ATT_H + ATT_KV, ATT_HD)[:, :, _rope_lane_order()].reshape(d, n_qk)
    att = jnp.concatenate([att_qk, att[:, n_qk:]], axis=1)
    pad_wd = jnp.zeros((d, LANES - 32), w_in.dtype)
    pad_end = jnp.zeros((d, LANES), w_in.dtype)
    return jnp.concatenate([mg, att, rw, gq, gk, gv, gg, gwd, pad_wd, pad_end], axis=1).astype(BF16)


def _rope_tables(seq):
    quarter = ATT_HD // 4
    inv = ROPE_THETA ** (-jnp.arange(quarter, dtype=F32) / quarter)
    t = jnp.arange(seq)
    row_pos = (t // GRID_W).astype(F32)
    col_pos = (t % GRID_W).astype(F32)
    ar = row_pos[:, None] * inv[None, :]
    ac = col_pos[:, None] * inv[None, :]
    cos = jnp.concatenate([jnp.cos(ar), jnp.cos(ac), jnp.cos(ar), jnp.cos(ac)], axis=1)
    sin = jnp.concatenate([-jnp.sin(ar), -jnp.sin(ac), jnp.sin(ar), jnp.sin(ac)], axis=1)
    return cos, sin


def _rope_lane_order():
    quarter = ATT_HD // 4
    i = jnp.arange(quarter)
    return jnp.concatenate([i, 2 * quarter + i, quarter + i, 3 * quarter + i])


def _pad_rows(w, lo_half):
    z = jnp.zeros((LANES - w.shape[0], w.shape[1]), w.dtype)
    if lo_half:
        return jnp.concatenate([w, z], axis=0)
    half = LANES // 2
    return jnp.concatenate([z[:half], w, z[half:]], axis=0)


def _pick(n, target):
    t = min(n, target)
    while n % t:
        t //= 2
    return t


def kernel(x, c, ctx, c_ctx, w_mod, b_mod, g_pre, w_in, gla_wup_f, gla_b_f, gla_wup_b, gla_b_b, gla_norm, att_qnorm, att_knorm, rwkv_mu, rwkv_w0_f, rwkv_wup_f, rwkv_w0_b, rwkv_wup_b, rwkv_a0_f, rwkv_aup_f, rwkv_a0_b, rwkv_aup_b, rwkv_kk, rwkv_ka, rwkv_rk, rwkv_ln_g, rwkv_ln_b, w_o_gla, w_o_att, w_o_rwkv, w_out, g_post):
    batch, seq, d = x.shape
    ctx_len = ctx.shape[1]
    depth = w_in.shape[0]
    assert seq % CHUNK == 0 and ctx_len % CHUNK == 0 and seq % GRID_W == 0

    nb = -(-(batch + 1) // SUBLANES) * SUBLANES
    cond = jnp.concatenate([c, c_ctx[None, :], jnp.zeros((nb - batch - 1, d), F32)], axis=0)
    tables = _rope_tables(seq)

    tm_lat = _pick(seq, MATMUL_ROWS)
    tm_ctx = _pick(batch * ctx_len, MATMUL_ROWS)
    tm_merge = _pick(seq, MATMUL_ROWS)
    tm_merge_c = _pick(batch * ctx_len, MATMUL_ROWS)
    tm_prep = _pick(seq, PREP_ROWS)
    tm_prep_c = _pick(ctx_len, PREP_ROWS)
    tq_lat, tk = _pick(seq, FLASH_ROWS), _pick(seq, FLASH_ROWS)
    tq_ctx = _pick(ctx_len, FLASH_ROWS)

    def lat_mod(tm):
        return lambda i: i // (seq // tm)

    def ctx_mod(tm):
        return lambda i: batch

    x2 = x.reshape(batch * seq, d)
    xc2 = ctx.reshape(batch * ctx_len, d)

    def hp(v):
        return v.reshape(RWKV_H // 2, 1, LANES)

    for l in range(depth):
        need_ctx = l < depth - 1
        mod = _modulation(cond, w_mod[l], b_mod[l]).reshape(nb, 3, d)
        w_cat = _pack_w_in(w_in[l])
        p_lat = _inproj(x2, mod, lat_mod(tm_lat), g_pre[l], w_cat, tm_lat)
        p_ctx = _inproj(xc2, mod, ctx_mod(tm_ctx), g_pre[l], w_cat, tm_ctx)

        gains = jnp.stack([att_qnorm[l] * (ATT_HD ** -0.5 * LOG2_E), att_knorm[l]])[:, _rope_lane_order()]
        att_lat = tuple(_qkvprep(p_lat, gains, tables, batch, seq, tm_prep)) + (seq,)
        att_ctx = tuple(_qkvprep(p_ctx, gains, None, batch, ctx_len, tm_prep_c)) + (ctx_len,)
        ps_lat = _rwkv_shift(p_lat, rwkv_mu[l], seq, tm_prep)
        ps_ctx = _rwkv_shift(p_ctx, rwkv_mu[l], ctx_len, tm_prep_c)
        y2 = _flash(att_lat[0], p_lat, seq, att_lat, att_ctx, batch, tq_lat, tk)
        y2c = None
        if need_ctx:
            y2c = _flash(att_ctx[0], p_ctx, ctx_len, None, att_ctx, batch, tq_ctx, tk)

        def gla_w(w, lo_rows):
            w = w.reshape(GLA_LR, GLA_H // 2, LANES).transpose(1, 0, 2)
            z = jnp.zeros((GLA_H // 2, LANES, LANES), w.dtype)
            start = 0 if lo_rows else GLA_LR
            return lax.dynamic_update_slice(z, w, (0, start, 0)).astype(BF16)
        y1, y1c = _gla(p_lat, p_ctx, batch, seq, ctx_len,
                       gla_w(gla_wup_f[l], True), gla_w(gla_wup_b[l], False),
                       gla_b_f[l].reshape(GLA_H // 2, 1, LANES), gla_b_b[l].reshape(GLA_H // 2, 1, LANES),
                       gla_norm[l].reshape(GLA_H, 1, GLA_DV), need_ctx)

        def rw_w(w, lo_rows):
            w = w.reshape(RWKV_LR, RWKV_H // 2, LANES).transpose(1, 0, 2)
            return jnp.stack([_pad_rows(w[i], lo_rows) for i in range(RWKV_H // 2)]).astype(BF16)
        prm = (hp(rwkv_w0_f[l]), hp(rwkv_w0_b[l]), hp(rwkv_a0_f[l]), hp(rwkv_a0_b[l]),
               hp(rwkv_kk[l]), hp(rwkv_ka[l]), hp(rwkv_rk[l]), hp(rwkv_ln_g[l]), hp(rwkv_ln_b[l]),
               rw_w(rwkv_wup_f[l], True), rw_w(rwkv_wup_b[l], False),
               rw_w(rwkv_aup_f[l], True), rw_w(rwkv_aup_b[l], False))
        y3, y3c = _rwkv(ps_lat, ps_ctx, batch, seq, ctx_len, prm, need_ctx)

        wo1, wo2, wo3 = (w_o_gla[l].astype(BF16), w_o_att[l].astype(BF16),
                         w_o_rwkv[l].astype(BF16))
        wout = w_out[l].astype(BF16)
        x2 = _merge(x2, y1, y2, y3, p_lat, mod, lat_mod(tm_merge), wo1, wo2, wo3, wout, g_post[l], tm_merge)
        if need_ctx:
            xc2 = _merge(xc2, y1c, y2c, y3c, p_ctx, mod, ctx_mod(tm_merge_c), wo1, wo2, wo3, wout,
                         g_post[l], tm_merge_c)
    return x2.reshape(batch, seq, d)
```

```python
import functools

import jax
import jax.numpy as jnp
from jax import lax
from jax.experimental import pallas as pl
from jax.experimental.pallas import tpu as pltpu

F32 = jnp.float32
BF16 = jnp.bfloat16

GRID_W = 64
BRANCH_W = 512
GLA_H, GLA_DK, GLA_DV, GLA_LR = 4, 64, 128, 16
GLA_TAU = 16.0
ATT_H, ATT_KV, ATT_HD = 4, 2, 128
ROPE_THETA = 10000.0
RWKV_H, RWKV_HD, RWKV_LR = 8, 64, 64
RWKV_DECAY_SCALE = 0.6065306597
NORM_EPS = 1e-6
GN_EPS = 64e-5
L2_EPS = 1e-12
LOG2_E = 1.4426950408889634

CHUNK = 64
LANES = 128
SUBLANES = 8
BF16_ROWS = 16
MATMUL_ROWS = 1024
PREP_ROWS = 512
FLASH_ROWS = 512
MOD_COLS = 768
VMEM_CAPACITY = 64 * 1024 * 1024
VMEM_LIMIT = VMEM_CAPACITY * 7 // 8
RWKV_VMEM_LIMIT = VMEM_CAPACITY - 2 * 1024 * 1024

CB_MG = 0
CB_ATT_Q = 24
CB_ATT_V = 30
CB_ATT_GATE = 32
CB_RW = 36
RW_BLOCKS = 18
CB_GLA_Q = 54
CB_GLA_K = 56
CB_GLA_V = 58
CB_GLA_GATE = 62
CB_GLA_WD = 66
NP_BLOCKS = 68
RW_R, RW_K, RW_V, RW_GATE, RW_WD, RW_AD = 0, 4, 8, 12, 16, 17
NP = NP_BLOCKS * LANES
TN_PROJ = NP // 2


def _cparams(sem, vmem_limit=VMEM_LIMIT):
    return pltpu.CompilerParams(dimension_semantics=sem, vmem_limit_bytes=vmem_limit)


def _dot(a, b):
    return jnp.dot(a, b, preferred_element_type=F32)


def _dot_nt(a, b):
    return lax.dot_general(a, b, (((1,), (1,)), ((), ())), preferred_element_type=F32)


def _dot_tn(a, b):
    return lax.dot_general(a, b, (((0,), (0,)), ((), ())), preferred_element_type=F32)


def _sigmoid(x):
    return 0.5 + 0.5 * jnp.tanh(0.5 * x)


def _silu(x):
    h = 0.5 * x
    return h + h * jnp.tanh(h)


def _tri_cumsum(tri, x):
    hi = x.astype(BF16)
    lo = (x - hi.astype(F32)).astype(BF16)
    n = x.shape[1]
    if n % LANES:
        return _dot(tri, hi) + _dot(tri, lo)
    r = _dot(tri, jnp.concatenate([hi, lo], axis=1))
    return r[:, :n] + r[:, n:]


GLA_UNROLL = 16
RWKV_UNROLL = 4
FINISH_UNROLL = 8


def _unroll_factor(n_chunks, target):
    u = target
    while n_chunks % u:
        u //= 2
    return u


def _mod_kernel(c_ref, w_ref, b_ref, o_ref):
    c = c_ref[...]
    s = _silu(c)
    o_ref[...] = jnp.dot(s, w_ref[...], preferred_element_type=F32,
                         precision=lax.Precision.HIGHEST) + b_ref[...]


def _modulation(cond, w_mod, b_mod):
    nb, d = cond.shape
    n = w_mod.shape[1]
    tn = _pick(n, MOD_COLS)
    return pl.pallas_call(
        _mod_kernel,
        grid=(n // tn,),
        in_specs=[pl.BlockSpec((nb, d), lambda j: (0, 0)),
                  pl.BlockSpec((d, tn), lambda j: (0, j)),
                  pl.BlockSpec((1, tn), lambda j: (0, j))],
        out_specs=pl.BlockSpec((nb, tn), lambda j: (0, j)),
        out_shape=jax.ShapeDtypeStruct((nb, n), F32),
        compiler_params=_cparams(("parallel",)),
        name="modulation",
    )(cond, w_mod, b_mod.reshape(1, n))


def _inproj_kernel(x_ref, mod_ref, g_ref, w_ref, o_ref, h_ref):
    first = pl.program_id(1) == 0

    @pl.when(first)
    def _():
        x = x_ref[...]
        ms = jnp.mean(x * x, axis=-1, keepdims=True)
        y = x * lax.rsqrt(ms + NORM_EPS) * g_ref[...]
        mod = mod_ref[0]
        h = (y * (1.0 + mod[1:2]) + mod[0:1]).astype(BF16)
        h_ref[...] = h
        o_ref[...] = _dot(h, w_ref[...]).astype(o_ref.dtype)

    @pl.when(jnp.logical_not(first))
    def _():
        o_ref[...] = _dot(h_ref[...], w_ref[...]).astype(o_ref.dtype)


def _inproj(x2d, mod, mod_index, g_pre, w_cat, tm):
    r, d = x2d.shape
    return pl.pallas_call(
        _inproj_kernel,
        grid=(r // tm, NP // TN_PROJ),
        in_specs=[pl.BlockSpec((tm, d), lambda i, j: (i, 0)),
                  pl.BlockSpec((1, 3, d), lambda i, j: (mod_index(i), 0, 0)),
                  pl.BlockSpec((1, d), lambda i, j: (0, 0)),
                  pl.BlockSpec((d, TN_PROJ), lambda i, j: (0, j))],
        out_specs=pl.BlockSpec((tm, TN_PROJ), lambda i, j: (i, j)),
        out_shape=jax.ShapeDtypeStruct((r, NP), BF16),
        scratch_shapes=[pltpu.VMEM((tm, d), BF16)],
        compiler_params=_cparams(("parallel", "arbitrary")),
        name="inproj",
    )(x2d, mod, g_pre.reshape(1, d), w_cat)


def _norm_rope(x, gain, tables):
    x = x.astype(F32)
    ms = jnp.mean(x * x, axis=-1, keepdims=True)
    y = x * lax.rsqrt(ms + NORM_EPS) * gain
    if tables is not None:
        cos, sin = tables
        y = y * cos + pltpu.roll(y, ATT_HD // 2, 1) * sin
    return y.astype(BF16)


ATT_QKV_BLOCKS = ATT_H + 2 * ATT_KV


def _qkvprep_kernel(*refs, rope):
    if rope:
        x_ref, g_ref, cos_ref, sin_ref, qk_ref, vt_ref = refs
        tables = (cos_ref[...], sin_ref[...])
    else:
        x_ref, g_ref, qk_ref, vt_ref = refs
        tables = None
    for h in range(ATT_H + ATT_KV):
        cols = slice(h * LANES, (h + 1) * LANES)
        gain = g_ref[0:1, :] if h < ATT_H else g_ref[1:2, :]
        qk_ref[:, cols] = _norm_rope(x_ref[:, cols], gain, tables)
    for j in range(ATT_KV):
        cols = slice((ATT_H + ATT_KV + j) * LANES, (ATT_H + ATT_KV + j + 1) * LANES)
        base = j * VT_ROWS
        vt_ref[base:base + ATT_HD, :] = x_ref[:, cols].astype(F32).T.astype(BF16)
        vt_ref[base + ATT_HD:base + VT_ROWS, :] = jnp.ones((VT_ROWS - ATT_HD, vt_ref.shape[1]), BF16)


VT_ROWS = ATT_HD + BF16_ROWS


def _qkvprep(p, gains, tables, batch, seq, tm):
    nt = seq // tm
    width = ATT_QKV_BLOCKS * LANES
    cb = CB_ATT_Q * LANES // width
    in_specs = [pl.BlockSpec((tm, width), lambda i: (i, cb)),
                pl.BlockSpec((2, LANES), lambda i: (0, 0))]
    args = [p, gains]
    if tables is not None:
        for tab in tables:
            in_specs.append(pl.BlockSpec((tm, LANES), lambda i: (i % nt, 0)))
            args.append(tab)
    qk_width = (ATT_H + ATT_KV) * LANES
    return pl.pallas_call(
        functools.partial(_qkvprep_kernel, rope=tables is not None),
        grid=(batch * nt,),
        in_specs=in_specs,
        out_specs=[pl.BlockSpec((tm, qk_width), lambda i: (i, 0)),
                   pl.BlockSpec((ATT_KV * VT_ROWS, tm), lambda i: (i // nt, i % nt))],
        out_shape=[jax.ShapeDtypeStruct((batch * seq, qk_width), BF16),
                   jax.ShapeDtypeStruct((batch * ATT_KV * VT_ROWS, seq), BF16)],
        compiler_params=_cparams(("parallel",)),
        name="qkvprep_rope" if tables is not None else "qkvprep",
    )(*args)


FLASH_KV_INTERLEAVE = 8


def _flash_kernel(*refs, n_lat_blocks, tk, has_ctx_kv):
    refs = list(refs)
    q0_ref, q1_ref, g0_ref, g1_ref = refs[:4]
    pos = 4
    if n_lat_blocks:
        kl_ref, vtl_ref = refs[pos:pos + 2]
        pos += 2
    if has_ctx_kv:
        kc_ref, vtc_ref = refs[pos:pos + 2]
        pos += 2
    o_ref = refs[pos]
    tq = q0_ref.shape[0]
    q = jnp.concatenate([q0_ref[...], q1_ref[...]], axis=0)

    def blocks(kvs, carry):
        scores = [_dot_nt(k, q) for k, _ in kvs]
        m, acc = carry
        for s, (_, vt) in zip(scores, kvs):
            m_new = jnp.maximum(m, jnp.max(s, axis=0, keepdims=True))
            p = jnp.exp2(s - m_new)
            acc = jnp.exp2(m - m_new) * acc + _dot(vt, p.astype(BF16))
            m = m_new
        return m, acc

    carry = (jnp.full((1, 2 * tq), -jnp.inf, F32), jnp.zeros((VT_ROWS, 2 * tq), F32))
    if n_lat_blocks:
        nb = _unroll_factor(n_lat_blocks, FLASH_KV_INTERLEAVE)

        def body(i, c):
            kvs = []
            for u in range(nb):
                start = pl.multiple_of((i * nb + u) * tk, tk)
                kvs.append((kl_ref[pl.ds(start, tk), :], vtl_ref[:, pl.ds(start, tk)]))
            return blocks(kvs, c)
        carry = lax.fori_loop(0, n_lat_blocks // nb, body, carry)
    if has_ctx_kv:
        carry = blocks([(kc_ref[...], vtc_ref[...])], carry)
    _, acc = carry
    o = (acc[:ATT_HD] / acc[ATT_HD:ATT_HD + 1]).T
    o_ref[:, :LANES] = (o[:tq] * _silu(g0_ref[...].astype(F32))).astype(o_ref.dtype)
    o_ref[:, LANES:] = (o[tq:] * _silu(g1_ref[...].astype(F32))).astype(o_ref.dtype)


def _flash(qk_q, p_q, seq_q, lat_kv, ctx_kv, batch, tq, tk):
    nq = seq_q // tq
    in_specs = [
        pl.BlockSpec((tq, LANES), lambda b, j, i: (b * nq + i, 2 * j)),
        pl.BlockSpec((tq, LANES), lambda b, j, i: (b * nq + i, 2 * j + 1)),
        pl.BlockSpec((tq, LANES), lambda b, j, i: (b * nq + i, CB_ATT_GATE + 2 * j)),
        pl.BlockSpec((tq, LANES), lambda b, j, i: (b * nq + i, CB_ATT_GATE + 2 * j + 1)),
    ]
    args = [qk_q, qk_q, p_q, p_q]
    n_lat_blocks = 0
    for seg in (lat_kv, ctx_kv):
        if seg is None:
            continue
        qk_s, vt_s, seq_s = seg
        in_specs.append(pl.BlockSpec((seq_s, LANES), lambda b, j, i: (b, ATT_H + j)))
        in_specs.append(pl.BlockSpec((VT_ROWS, seq_s), lambda b, j, i: (b * ATT_KV + j, 0)))
        args += [qk_s, vt_s]
    if lat_kv is not None:
        n_lat_blocks = lat_kv[2] // tk
    return pl.pallas_call(
        functools.partial(_flash_kernel, n_lat_blocks=n_lat_blocks, tk=tk,
                          has_ctx_kv=ctx_kv is not None),
        grid=(batch, ATT_KV, nq),
        in_specs=in_specs,
        out_specs=pl.BlockSpec((tq, 2 * LANES), lambda b, j, i: (b * nq + i, j)),
        out_shape=jax.ShapeDtypeStruct((batch * seq_q, BRANCH_W), BF16),
        compiler_params=_cparams(("parallel", "parallel", "arbitrary")),
        name="flash_lat" if lat_kv is not None else "flash_ctx",
    )(*args)


def _chunk_rows(c):
    return pl.ds(pl.multiple_of(c * CHUNK, CHUNK), CHUNK)


def _gla_chunk(seg, c, wup, bias, consts, recv_state, send_state):
    q_ref, k_ref, v_refs, wd_ref = seg
    tri, incl2, halves = consts
    rows = _chunk_rows(c)
    q = q_ref[rows, :].astype(F32) * (GLA_DK ** -0.5)
    k = k_ref[rows, :].astype(F32)
    vs = [v_ref[rows, :] for v_ref in v_refs]
    z = _dot(wd_ref[rows, :], wup) + bias
    yield
    g = (jnp.minimum(z, 0.0) - jnp.log1p(jnp.exp(-jnp.abs(z)))) * (1.0 / GLA_TAU)
    gc = _tri_cumsum(tri, g)
    yield
    tot = jnp.sum(g, axis=0, keepdims=True)
    q_dec = q * jnp.exp(gc)
    k_inv = (k * jnp.exp(-gc)).astype(BF16)
    k_tail = k * jnp.exp(tot - gc)
    q_heads = [(q_dec * hm).astype(BF16) for hm in halves]
    a = (_dot_nt(jnp.concatenate(q_heads, axis=0), k_inv) * incl2).astype(BF16)
    upds = [_dot_tn(v, (k_tail * hm).astype(BF16)) for v, hm in zip(vs, halves)]
    yield
    sts = recv_state()
    while sts is None:
        yield
        sts = recv_state()
    decay = jnp.exp(tot)
    send_state(tuple(st * decay + upd for st, upd in zip(sts, upds)))
    return tuple(_dot(a[h * CHUNK:(h + 1) * CHUNK], vs[h]) + _dot_nt(q_heads[h], sts[h].astype(BF16))
                 for h in range(2))


_GLA_SEG_REFS = 7


def _gla_kernel(*refs, need_ctx):
    mt_ref, tri_ref = refs[:2]
    lat = refs[2:2 + _GLA_SEG_REFS]
    ctx = refs[2 + _GLA_SEG_REFS:2 + 2 * _GLA_SEG_REFS]
    wupf_ref, wupb_ref, bf_ref, bb_ref, gn_ref = refs[2 + 2 * _GLA_SEG_REFS:7 + 2 * _GLA_SEG_REFS]
    rest = refs[7 + 2 * _GLA_SEG_REFS:]
    if need_ctx:
        y_l, y_c = rest[:2]
        scr = rest[2:]
    else:
        y_l, y_c = rest[0], None
        scr = rest[1:]
    scr_l, scr_c = scr[:4], scr[4:]
    nl = lat[0].shape[0] // CHUNK
    nc = ctx[0].shape[0] // CHUNK
    wupf, wupb = wupf_ref[0], wupb_ref[0]
    bias_f, bias_b = bf_ref[0], bb_ref[0]
    row0 = _MT_HALVES * CHUNK
    halves = (mt_ref[row0:row0 + 1, :], mt_ref[row0 + 1:row0 + 2, :])

    def consts(d):
        incl = mt_ref[(_MT_INCL_WIDE + d) * CHUNK:(_MT_INCL_WIDE + d + 1) * CHUNK, :CHUNK]
        return tri_ref[d], jnp.concatenate([incl, incl], axis=0), halves
    consts_f, consts_b = consts(0), consts(1)

    def scan(seg, scr, n, carry):
        unroll = _unroll_factor(n, GLA_UNROLL)
        seg = (seg[0], seg[1], seg[2:4], seg[6])

        def body(i, carry):
            states = {("f", -1): carry[0], ("b", -1): carry[1]}
            gens = []
            for u in range(unroll):
                c = i * unroll + u
                for d, cc, wup, bias, cst in (("f", c, wupf, bias_f, consts_f),
                                              ("b", n - 1 - c, wupb, bias_b, consts_b)):
                    gens.append(_gla_chunk(
                        seg, cc, wup, bias, cst,
                        functools.partial(states.get, (d, u - 1)),
                        functools.partial(states.__setitem__, (d, u))))
            outs = _run_interleaved(gens)
            for u in range(unroll):
                c = i * unroll + u
                for h in range(2):
                    scr[h][_chunk_rows(c), :] = outs[2 * u][h]
                    scr[2 + h][_chunk_rows(n - 1 - c), :] = outs[2 * u + 1][h]
            return states[("f", unroll - 1)], states[("b", unroll - 1)]
        return lax.fori_loop(0, n // unroll, body, carry)

    zero = (jnp.zeros((GLA_DV, LANES), F32),) * 2
    carry = scan(ctx, scr_c, nc, (zero, zero))
    scan(lat, scr_l, nl, carry)

    def finish(seg, scr, y, n):
        def body(i, _):
            rows = _chunk_rows(i)
            for h in range(2):
                o = scr[h][rows, :] + scr[2 + h][rows, :]
                ms = jnp.mean(o * o, axis=-1, keepdims=True)
                o = o * lax.rsqrt(ms + NORM_EPS) * gn_ref[h]
                gate = seg[4 + h][rows, :].astype(F32)
                y[rows, h * LANES:(h + 1) * LANES] = (o * _silu(gate)).astype(y.dtype)
            return 0
        lax.fori_loop(0, n, body, 0, unroll=_unroll_factor(n, FINISH_UNROLL))

    finish(lat, scr_l, y_l, nl)
    if need_ctx:
        finish(ctx, scr_c, y_c, nc)


def _gla(p_lat, p_ctx, batch, seq, ctx_len, wupf, wupb, bf, bb, gnorm, need_ctx):
    def seg_specs(n):
        return [pl.BlockSpec((n, LANES), lambda b, g: (b, CB_GLA_Q + g)),
                pl.BlockSpec((n, LANES), lambda b, g: (b, CB_GLA_K + g)),
                pl.BlockSpec((n, LANES), lambda b, g: (b, CB_GLA_V + 2 * g)),
                pl.BlockSpec((n, LANES), lambda b, g: (b, CB_GLA_V + 2 * g + 1)),
                pl.BlockSpec((n, LANES), lambda b, g: (b, CB_GLA_GATE + 2 * g)),
                pl.BlockSpec((n, LANES), lambda b, g: (b, CB_GLA_GATE + 2 * g + 1)),
                pl.BlockSpec((n, LANES), lambda b, g: (b, CB_GLA_WD))]
    table, tri = _rwkv_mask_tables()
    w_spec = pl.BlockSpec((1, LANES, LANES), lambda b, g: (g, 0, 0))
    b_spec = pl.BlockSpec((1, 1, LANES), lambda b, g: (g, 0, 0))
    g_spec = pl.BlockSpec((2, 1, GLA_DV), lambda b, g: (g, 0, 0))
    out_specs = [pl.BlockSpec((seq, 2 * LANES), lambda b, g: (b, g))]
    out_shape = [jax.ShapeDtypeStruct((batch * seq, BRANCH_W), BF16)]
    if need_ctx:
        out_specs.append(pl.BlockSpec((ctx_len, 2 * LANES), lambda b, g: (b, g)))
        out_shape.append(jax.ShapeDtypeStruct((batch * ctx_len, BRANCH_W), BF16))
    outs = pl.pallas_call(
        functools.partial(_gla_kernel, need_ctx=need_ctx),
        grid=(batch, GLA_H // 2),
        in_specs=([pl.BlockSpec(table.shape, lambda b, g: (0, 0)),
                   pl.BlockSpec(tri.shape, lambda b, g: (0, 0, 0))]
                  + seg_specs(seq) + seg_specs(ctx_len) + [w_spec, w_spec, b_spec, b_spec, g_spec]),
        out_specs=out_specs,
        out_shape=out_shape,
        scratch_shapes=[pltpu.VMEM((seq, LANES), F32)] * 4 + [pltpu.VMEM((ctx_len, LANES), F32)] * 4,
        compiler_params=_cparams(("parallel", "parallel")),
        name="gla",
    )(jnp.asarray(table), jnp.asarray(tri, BF16), *([p_lat] * _GLA_SEG_REFS + [p_ctx] * _GLA_SEG_REFS),
      wupf, wupb, bf, bb, gnorm)
    return (outs[0], outs[1]) if need_ctx else (outs[0], None)


def _seg_sum(x, lo):
    s0 = jnp.sum(jnp.where(lo, x, 0.0), axis=-1, keepdims=True)
    s1 = jnp.sum(jnp.where(lo, 0.0, x), axis=-1, keepdims=True)
    return jnp.where(lo, s0, s1)


def _run_interleaved(gens):
    gens = list(gens)
    out = [None] * len(gens)
    live = list(range(len(gens)))
    while live:
        still = []
        for j in live:
            try:
                next(gens[j])
                still.append(j)
            except StopIteration as stop:
                out[j] = stop.value
        live = still
    return out


def _unit_tri_inverse_offdiag(mats, blockdiag):
    def mm(x, y):
        return _dot(x.astype(BF16), y.astype(BF16))

    def mm2(x1, x2, y):
        r = mm(jnp.concatenate([x1, x2], axis=0), y)
        return r[:CHUNK], r[CHUNK:]

    nd = [-(a * blockdiag) for a in mats]
    no = [-a - d for a, d in zip(mats, nd)]
    n2 = [mm(d, d) for d in nd]
    yield
    r = [mm2(s, d, s) for d, s in zip(nd, n2)]
    n4 = [ri[0] for ri in r]
    x = [d + s + ri[1] for d, s, ri in zip(nd, n2, r)]
    yield
    r = [mm2(s, xi, s) for xi, s in zip(x, n4)]
    n8 = [ri[0] for ri in r]
    x = [xi + s + ri[1] for xi, s, ri in zip(x, n4, r)]
    yield
    xd = [xi + s + mm(xi, s) for xi, s in zip(x, n8)]
    yield
    m = [o + mm(xi, o) for xi, o in zip(xd, no)]
    yield
    m2 = [mm(mi, mi) for mi in m]
    yield
    xq = [mi + s + mm(mi, s) for mi, s in zip(m, m2)]
    yield
    return [q + d + mm(q, d) for q, d in zip(xq, xd)]


SHIFT_COLS = RW_BLOCKS * LANES


def _rwkv_shift_kernel(cur_ref, prev_ref, next_ref, mu_ref, o_ref, *, tiles_per_seq):
    t = pl.program_id(0) % tiles_per_seq
    x = cur_ref[...].astype(F32)
    tm = x.shape[0]
    prev_row = jnp.where(t > 0, prev_ref[...].astype(F32)[BF16_ROWS - 1:BF16_ROWS, :], 0.0)
    next_row = jnp.where(t < tiles_per_seq - 1, next_ref[...].astype(F32)[0:1, :], 0.0)
    row8 = lax.broadcasted_iota(jnp.int32, (8, x.shape[1]), 0)
    up = pltpu.roll(x, 1, 0)
    up = jnp.concatenate([jnp.where(row8 == 0, prev_row, up[:8]), up[8:]], axis=0)
    dn = pltpu.roll(x, tm - 1, 0)
    dn = jnp.concatenate([dn[:tm - 8], jnp.where(row8 == 7, next_row, dn[tm - 8:])], axis=0)
    mu = mu_ref[...]
    y = x * (1.0 - mu) + (up + dn) * (0.5 * mu)
    g0, w0 = RW_GATE * LANES, RW_WD * LANES
    o_ref[:, :g0] = y[:, :g0].astype(o_ref.dtype)
    o_ref[:, g0:w0] = _silu(y[:, g0:w0]).astype(o_ref.dtype)
    o_ref[:, w0:w0 + LANES] = jnp.tanh(y[:, w0:w0 + LANES]).astype(o_ref.dtype)
    o_ref[:, w0 + LANES:] = y[:, w0 + LANES:].astype(o_ref.dtype)


def _rwkv_shift(p, mu, seq, tm):
    r = p.shape[0]
    tiles_per_seq = seq // tm
    sub = tm // BF16_ROWS
    last = r // BF16_ROWS - 1
    cb = CB_RW * LANES // SHIFT_COLS
    return pl.pallas_call(
        functools.partial(_rwkv_shift_kernel, tiles_per_seq=tiles_per_seq),
        grid=(r // tm,),
        in_specs=[pl.BlockSpec((tm, SHIFT_COLS), lambda i: (i, cb)),
                  pl.BlockSpec((BF16_ROWS, SHIFT_COLS), lambda i: (jnp.maximum(i * sub - 1, 0), cb)),
                  pl.BlockSpec((BF16_ROWS, SHIFT_COLS), lambda i: (jnp.minimum((i + 1) * sub, last), cb)),
                  pl.BlockSpec((1, SHIFT_COLS), lambda i: (0, 0))],
        out_specs=pl.BlockSpec((tm, SHIFT_COLS), lambda i: (i, 0)),
        out_shape=jax.ShapeDtypeStruct((r, RW_BLOCKS * LANES), BF16),
        compiler_params=_cparams(("parallel",)),
        name="rwkv_shift",
    )(p, p, p, mu.reshape(1, RW_BLOCKS * LANES))


def _rwkv_chunk(seg, c, prm, consts, recv_state, send_state):
    (r_ref, k_ref, v_ref, _, wd_ref, ad_ref) = seg
    (w0, a0, kkw, kaw, rkw, wup, aup) = prm
    (lo_f, hi_f, lo, blockdiag16, headdiag, tri, strict, strict_hi, incl_wide) = consts
    rows = _chunk_rows(c)

    r = r_ref[rows, :].astype(F32)
    k = k_ref[rows, :].astype(F32)
    v16 = v_ref[rows, :]
    v = v16.astype(F32)
    z_w = _dot(wd_ref[rows, :], wup)
    z_a = _dot(ad_ref[rows, :], aup)
    yield
    lw = -RWKV_DECAY_SCALE * _sigmoid(w0 + z_w)
    a = _sigmoid(a0 + z_a)
    kd = k * (1.0 + (a - 1.0) * kaw)
    kk = k * kkw
    kk = kk * lax.rsqrt(_seg_sum(kk * kk, lo) + L2_EPS)
    kka = kk * a
    bonus = _seg_sum(r * kd * rkw, lo) * v

    cw = _tri_cumsum(tri, lw)
    yield
    tot = jnp.sum(lw, axis=0, keepdims=True)
    e_neg = jnp.exp(-cw)
    e_tail = jnp.exp(tot - cw)
    kap = kk * jnp.exp(cw - lw)
    rt = r * jnp.exp(cw)
    bt = (kka * e_neg).astype(BF16)
    kt = (kd * e_neg).astype(BF16)
    tails = jnp.concatenate([kka * e_tail, kd * e_tail], axis=0).astype(BF16)
    decay = jnp.exp(tot)
    kap_rt = jnp.concatenate([kap, rt], axis=0)

    yt = jnp.concatenate([bt, kt], axis=0)
    gram = _dot_nt(jnp.concatenate([kap_rt * lo_f, kap_rt * hi_f], axis=0).astype(BF16), yt)
    grams = (gram[:LANES], gram[LANES:])
    kap_rt = kap_rt.astype(BF16)
    yield
    vv = jnp.concatenate([v16, v16], axis=0)
    a_b = [g[:CHUNK, :CHUNK] * strict for g in grams]
    akv = _dot(jnp.concatenate([g[:CHUNK] * strict_hi for g in grams], axis=0).astype(BF16), vv)
    akv = jnp.where(lo, akv[:CHUNK], akv[CHUNK:])
    mats = jnp.concatenate([g[CHUNK:] * incl_wide for g in grams], axis=0).astype(BF16)
    x_inv = yield from _unit_tri_inverse_offdiag(a_b, blockdiag16)
    x_inv = jnp.concatenate(x_inv, axis=0).astype(BF16)
    yield
    xk = _dot(x_inv, jnp.concatenate([kap, akv], axis=1).astype(BF16))
    xk = jnp.where(jnp.concatenate([lo, lo], axis=1), xk[:CHUNK], xk[CHUNK:])
    kap_rt = jnp.concatenate([kap + xk[:, :LANES], rt], axis=0).astype(BF16)
    akv = akv + xk[:, LANES:]
    yield

    s2 = recv_state()
    while s2 is None:
        yield
        s2 = recv_state()
    ks_rs = _dot_nt(kap_rt, s2.astype(BF16))
    e, rs = ks_rs[:CHUNK] + akv, ks_rs[CHUNK:]
    yield
    ev = jnp.concatenate([-e, v], axis=0).astype(BF16)
    upd = _dot_tn(ev, tails)
    send_state(s2 * decay + upd * headdiag)
    my = _dot(mats, ev)
    y = rs + jnp.where(lo, my[:CHUNK], my[CHUNK:])
    return y, bonus


RWKV_PAIRS_PER_STEP = 2
_RWKV_IN_REFS = 25
_RWKV_SCRATCH = 4
_MT_STRICT, _MT_STRICT_HI, _MT_INCL_WIDE = 0, 2, 4
_MT_BLOCKDIAG, _MT_HEADDIAG, _MT_HALVES, _MT_BANDS = 6, 7, 9, 10


def _rwkv_mask_tables():
    import numpy as np
    t = np.arange(CHUNK)[:, None]
    s = np.arange(CHUNK)[None, :]
    lane = np.arange(LANES)[None, :]
    table = np.zeros((_MT_BANDS, CHUNK, LANES), np.float32)
    tri = np.zeros((2, CHUNK, CHUNK), np.float32)
    for d, (strict, incl) in enumerate(((s < t, s <= t), (s > t, s >= t))):
        table[_MT_STRICT + d, :, :CHUNK] = strict
        table[_MT_STRICT_HI + d, :, CHUNK:] = strict
        table[_MT_INCL_WIDE + d] = np.concatenate([incl, incl], axis=1)
        tri[d] = incl
    table[_MT_BLOCKDIAG, :, :CHUNK] = (t // 16) == (s // 16)
    v = np.arange(LANES)[:, None]
    table[_MT_HEADDIAG:_MT_HEADDIAG + 2] = ((v < RWKV_HD) == (lane < RWKV_HD)).reshape(2, CHUNK, LANES)
    table[_MT_HALVES, 0] = lane[0] < RWKV_HD
    table[_MT_HALVES, 1] = lane[0] >= RWKV_HD
    return table.reshape(_MT_BANDS * CHUNK, LANES), tri


def _rwkv_kernel(*refs, need_ctx):
    n_pairs = RWKV_PAIRS_PER_STEP
    mt_ref, tri_ref = refs[:2]
    refs = refs[2:]
    n_in = n_pairs * _RWKV_IN_REFS
    rest = refs[n_in:]
    if need_ctx:
        y_l, y_c = rest[:2]
        scr = rest[2:]
    else:
        y_l, y_c = rest[0], None
        scr = rest[1:]
    nl = refs[0].shape[0] // CHUNK
    nc = refs[6].shape[0] // CHUNK

    def band(j, rows=CHUNK, lanes=LANES):
        return mt_ref[j * CHUNK:j * CHUNK + rows, :lanes]

    lo = lax.broadcasted_iota(jnp.int32, (1, LANES), 1) < RWKV_HD
    lo_f, hi_f = band(_MT_HALVES, 1), mt_ref[_MT_HALVES * CHUNK + 1:_MT_HALVES * CHUNK + 2, :]
    blockdiag16 = band(_MT_BLOCKDIAG, lanes=CHUNK)
    headdiag = band(_MT_HEADDIAG, rows=LANES)

    def consts(d):
        return (lo_f, hi_f, lo, blockdiag16, headdiag, tri_ref[d], band(_MT_STRICT + d, lanes=CHUNK),
                band(_MT_STRICT_HI + d), band(_MT_INCL_WIDE + d))
    consts_f, consts_b = consts(0), consts(1)

    pairs = []
    for g in range(n_pairs):
        r = refs[g * _RWKV_IN_REFS:(g + 1) * _RWKV_IN_REFS]
        (w0f, w0b, a0f, a0b, kkw, kaw, rkw, lng, lnb, wupf, wupb, aupf, aupb) = [x[0] for x in r[12:]]
        s = scr[g * 2 * _RWKV_SCRATCH:(g + 1) * 2 * _RWKV_SCRATCH]
        pairs.append(dict(
            lat=tuple(r[0:6]), ctx=tuple(r[6:12]),
            prm_f=(w0f, a0f, kkw, kaw, rkw, wupf, aupf),
            prm_b=(w0b, a0b, kkw, kaw, rkw, wupb, aupb),
            scr_lat=s[:_RWKV_SCRATCH], scr_ctx=s[_RWKV_SCRATCH:], lng=lng, lnb=lnb))

    def scan(which, n, carry):
        unroll = _unroll_factor(n, RWKV_UNROLL)

        def body(i, carry):
            states = {}
            gens = []
            for u in range(unroll):
                c = i * unroll + u
                for g, pr in enumerate(pairs):
                    states[(g, "f", -1)], states[(g, "b", -1)] = carry[2 * g], carry[2 * g + 1]
                    for d, cc, prm, consts in (("f", c, pr["prm_f"], consts_f),
                                               ("b", n - 1 - c, pr["prm_b"], consts_b)):
                        gens.append(_rwkv_chunk(
                            pr[which], cc, prm, consts,
                            functools.partial(states.get, (g, d, u - 1)),
                            functools.partial(states.__setitem__, (g, d, u))))
            outs = iter(_run_interleaved(gens))
            for u in range(unroll):
                c = i * unroll + u
                rows_f, rows_b = _chunk_rows(c), _chunk_rows(n - 1 - c)
                for pr in pairs:
                    yf, yb, bf, bb = pr["scr_" + which]
                    (y_f, bon_f), (y_b, bon_b) = next(outs), next(outs)
                    yf[rows_f, :] = y_f
                    bf[rows_f, :] = bon_f.astype(bf.dtype)
                    yb[rows_b, :] = y_b
                    bb[rows_b, :] = bon_b.astype(bb.dtype)
            return tuple(states[(g, d, unroll - 1)] for g in range(n_pairs) for d in ("f", "b"))
        return lax.fori_loop(0, n // unroll, body, carry)

    zero = jnp.zeros((LANES, LANES), F32)
    carry = scan("ctx", nc, (zero,) * (2 * n_pairs))
    scan("lat", nl, carry)

    head_mean = (headdiag * (1.0 / RWKV_HD)).astype(BF16)

    def seg_mean(x):
        hi = x.astype(BF16)
        lo_part = (x - hi.astype(F32)).astype(BF16)
        r = _dot(jnp.concatenate([hi, lo_part], axis=0), head_mean)
        return r[:CHUNK] + r[CHUNK:]

    def finish(which, y_out, n):
        def body(i, _):
            rows = _chunk_rows(i)
            for g, pr in enumerate(pairs):
                yf, yb, bf, bb = pr["scr_" + which]
                gs = pr[which][3]
                y = yf[rows, :] + yb[rows, :]
                d = y - seg_mean(y)
                var = seg_mean(d * d)
                yn = d * lax.rsqrt(var + GN_EPS) * pr["lng"] + pr["lnb"]
                bonus = bf[rows, :].astype(F32) + bb[rows, :].astype(F32)
                y_out[rows, g * LANES:(g + 1) * LANES] = (
                    (yn + bonus) * gs[rows, :].astype(F32)).astype(y_out.dtype)
            return 0
        lax.fori_loop(0, n, body, 0, unroll=_unroll_factor(n, FINISH_UNROLL))

    finish("lat", y_l, nl)
    if need_ctx:
        finish("ctx", y_c, nc)


def _rwkv(ps_lat, ps_ctx, batch, seq, ctx_len, prm, need_ctx):
    n_pairs = RWKV_PAIRS_PER_STEP

    def pair_specs(g):
        hp = lambda s: n_pairs * s + g

        def seg_specs(n):
            return [pl.BlockSpec((n, LANES), lambda b, s: (b, RW_R + hp(s))),
                    pl.BlockSpec((n, LANES), lambda b, s: (b, RW_K + hp(s))),
                    pl.BlockSpec((n, LANES), lambda b, s: (b, RW_V + hp(s))),
                    pl.BlockSpec((n, LANES), lambda b, s: (b, RW_GATE + hp(s))),
                    pl.BlockSpec((n, LANES), lambda b, s: (b, RW_WD)),
                    pl.BlockSpec((n, LANES), lambda b, s: (b, RW_AD))]
        vec_h = pl.BlockSpec((1, 1, LANES), lambda b, s: (hp(s), 0, 0))
        mat_h = pl.BlockSpec((1, LANES, LANES), lambda b, s: (hp(s), 0, 0))
        return seg_specs(seq) + seg_specs(ctx_len) + [vec_h] * 9 + [mat_h] * 4

    table, tri = _rwkv_mask_tables()
    in_specs = [pl.BlockSpec(table.shape, lambda b, s: (0, 0)),
                pl.BlockSpec(tri.shape, lambda b, s: (0, 0, 0))]
    args = [jnp.asarray(table), jnp.asarray(tri, BF16)]
    for g in range(n_pairs):
        in_specs += pair_specs(g)
        args += [ps_lat] * 6 + [ps_ctx] * 6 + list(prm)
    width = n_pairs * LANES
    out_specs = [pl.BlockSpec((seq, width), lambda b, s: (b, s))]
    out_shape = [jax.ShapeDtypeStruct((batch * seq, BRANCH_W), BF16)]
    if need_ctx:
        out_specs.append(pl.BlockSpec((ctx_len, width), lambda b, s: (b, s)))
        out_shape.append(jax.ShapeDtypeStruct((batch * ctx_len, BRANCH_W), BF16))
    def seg_scratch(n):
        return [pltpu.VMEM((n, LANES), F32)] * 2 + [pltpu.VMEM((n, LANES), BF16)] * 2
    scratch = (seg_scratch(seq) + seg_scratch(ctx_len)) * n_pairs
    outs = pl.pallas_call(
        functools.partial(_rwkv_kernel, need_ctx=need_ctx),
        grid=(batch, RWKV_H // 2 // n_pairs),
        in_specs=in_specs,
        out_specs=out_specs,
        out_shape=out_shape,
        scratch_shapes=scratch,
        compiler_params=_cparams(("parallel", "parallel"), RWKV_VMEM_LIMIT),
        name="rwkv",
    )(*args)
    return (outs[0], outs[1]) if need_ctx else (outs[0], None)


def _merge_kernel(x_ref, y1_ref, y2_ref, y3_ref, g1_ref, g2_ref, g3_ref, mod_ref,
                  wo1_ref, wo2_ref, wo3_ref, wout_ref, gp_ref, o_ref):
    m = (_sigmoid(g1_ref[...].astype(F32)) * _dot(y1_ref[...], wo1_ref[...])
         + _sigmoid(g2_ref[...].astype(F32)) * _dot(y2_ref[...], wo2_ref[...])
         + _sigmoid(g3_ref[...].astype(F32)) * _dot(y3_ref[...], wo3_ref[...]))
    mo = _dot(m.astype(BF16), wout_ref[...])
    ms = jnp.mean(mo * mo, axis=-1, keepdims=True)
    out = mo * lax.rsqrt(ms + NORM_EPS) * gp_ref[...]
    o_ref[...] = x_ref[...] + mod_ref[0][2:3] * out


def _merge(x2d, y1, y2, y3, p, mod, mod_index, wo1, wo2, wo3, wout, g_post, tm):
    r, d = x2d.shape
    row = lambda i: (i, 0)
    const = lambda i: (0, 0)
    return pl.pallas_call(
        _merge_kernel,
        grid=(r // tm,),
        in_specs=[pl.BlockSpec((tm, d), row),
                  pl.BlockSpec((tm, BRANCH_W), row),
                  pl.BlockSpec((tm, BRANCH_W), row),
                  pl.BlockSpec((tm, BRANCH_W), row),
                  pl.BlockSpec((tm, d), lambda i: (i, 0)),
                  pl.BlockSpec((tm, d), lambda i: (i, 1)),
                  pl.BlockSpec((tm, d), lambda i: (i, 2)),
                  pl.BlockSpec((1, 3, d), lambda i: (mod_index(i), 0, 0)),
                  pl.BlockSpec((BRANCH_W, d), const),
                  pl.BlockSpec((BRANCH_W, d), const),
                  pl.BlockSpec((BRANCH_W, d), const),
                  pl.BlockSpec((d, d), const),
                  pl.BlockSpec((1, d), const)],
        out_specs=pl.BlockSpec((tm, d), row),
        out_shape=jax.ShapeDtypeStruct((r, d), F32),
        compiler_params=_cparams(("parallel",)),
        name="merge",
    )(x2d, y1, y2, y3, p, p, p, mod, wo1, wo2, wo3, wout, g_post.reshape(1, d))


def _pack_w_in(w_in):
    d = w_in.shape[0]
    o = 0
    gq = w_in[:, o:o + 256]; o += 256
    gk = w_in[:, o:o + 256]; o += 256
    gv = w_in[:, o:o + 512]; o += 512
    gg = w_in[:, o:o + 512]; o += 512
    gwd = w_in[:, o:o + 32]; o += 32
    att = w_in[:, o:o + 1536]; o += 1536
    rw = w_in[:, o:o + 2304]; o += 2304
    mg = w_in[:, o:o + 3072]
    n_qk = (ATT_H + ATT_KV) * ATT_HD
    att_qk = att[:, :n_qk].reshape(d, ATT_H + ATT_KV, ATT_HD)[:, :, _rope_lane_order()].reshape(d, n_qk)
    att = jnp.concatenate([att_qk, att[:, n_qk:]], axis=1)
    pad_wd = jnp.zeros((d, LANES - 32), w_in.dtype)
    pad_end = jnp.zeros((d, LANES), w_in.dtype)
    return jnp.concatenate([mg, att, rw, gq, gk, gv, gg, gwd, pad_wd, pad_end], axis=1).astype(BF16)


def _rope_tables(seq):
    quarter = ATT_HD // 4
    inv = ROPE_THETA ** (-jnp.arange(quarter, dtype=F32) / quarter)
    t = jnp.arange(seq)
    row_pos = (t // GRID_W).astype(F32)
    col_pos = (t % GRID_W).astype(F32)
    ar = row_pos[:, None] * inv[None, :]
    ac = col_pos[:, None] * inv[None, :]
    cos = jnp.concatenate([jnp.cos(ar), jnp.cos(ac), jnp.cos(ar), jnp.cos(ac)], axis=1)
    sin = jnp.concatenate([-jnp.sin(ar), -jnp.sin(ac), jnp.sin(ar), jnp.sin(ac)], axis=1)
    return cos, sin


def _rope_lane_order():
    quarter = ATT_HD // 4
    i = jnp.arange(quarter)
    return jnp.concatenate([i, 2 * quarter + i, quarter + i, 3 * quarter + i])


def _pad_rows(w, lo_half):
    z = jnp.zeros((LANES - w.shape[0], w.shape[1]), w.dtype)
    if lo_half:
        return jnp.concatenate([w, z], axis=0)
    half = LANES // 2
    return jnp.concatenate([z[:half], w, z[half:]], axis=0)


def _pick(n, target):
    t = min(n, target)
    while n % t:
        t //= 2
    return t


def kernel(x, c, ctx, c_ctx, w_mod, b_mod, g_pre, w_in, gla_wup_f, gla_b_f, gla_wup_b, gla_b_b, gla_norm, att_qnorm, att_knorm, rwkv_mu, rwkv_w0_f, rwkv_wup_f, rwkv_w0_b, rwkv_wup_b, rwkv_a0_f, rwkv_aup_f, rwkv_a0_b, rwkv_aup_b, rwkv_kk, rwkv_ka, rwkv_rk, rwkv_ln_g, rwkv_ln_b, w_o_gla, w_o_att, w_o_rwkv, w_out, g_post):
    batch, seq, d = x.shape
    ctx_len = ctx.shape[1]
    depth = w_in.shape[0]
    assert seq % CHUNK == 0 and ctx_len % CHUNK == 0 and seq % GRID_W == 0

    nb = -(-(batch + 1) // SUBLANES) * SUBLANES
    cond = jnp.concatenate([c, c_ctx[None, :], jnp.zeros((nb - batch - 1, d), F32)], axis=0)
    tables = _rope_tables(seq)

    tm_lat = _pick(seq, MATMUL_ROWS)
    tm_ctx = _pick(batch * ctx_len, MATMUL_ROWS)
    tm_merge = _pick(seq, MATMUL_ROWS)
    tm_merge_c = _pick(batch * ctx_len, MATMUL_ROWS)
    tm_prep = _pick(seq, PREP_ROWS)
    tm_prep_c = _pick(ctx_len, PREP_ROWS)
    tq_lat, tk = _pick(seq, FLASH_ROWS), _pick(seq, FLASH_ROWS)
    tq_ctx = _pick(ctx_len, FLASH_ROWS)

    def lat_mod(tm):
        return lambda i: i // (seq // tm)

    def ctx_mod(tm):
        return lambda i: batch

    x2 = x.reshape(batch * seq, d)
    xc2 = ctx.reshape(batch * ctx_len, d)

    def hp(v):
        return v.reshape(RWKV_H // 2, 1, LANES)

    for l in range(depth):
        need_ctx = l < depth - 1
        mod = _modulation(cond, w_mod[l], b_mod[l]).reshape(nb, 3, d)
        w_cat = _pack_w_in(w_in[l])
        p_lat = _inproj(x2, mod, lat_mod(tm_lat), g_pre[l], w_cat, tm_lat)
        p_ctx = _inproj(xc2, mod, ctx_mod(tm_ctx), g_pre[l], w_cat, tm_ctx)

        gains = jnp.stack([att_qnorm[l] * (ATT_HD ** -0.5 * LOG2_E), att_knorm[l]])[:, _rope_lane_order()]
        att_lat = tuple(_qkvprep(p_lat, gains, tables, batch, seq, tm_prep)) + (seq,)
        att_ctx = tuple(_qkvprep(p_ctx, gains, None, batch, ctx_len, tm_prep_c)) + (ctx_len,)
        ps_lat = _rwkv_shift(p_lat, rwkv_mu[l], seq, tm_prep)
        ps_ctx = _rwkv_shift(p_ctx, rwkv_mu[l], ctx_len, tm_prep_c)
        y2 = _flash(att_lat[0], p_lat, seq, att_lat, att_ctx, batch, tq_lat, tk)
        y2c = None
        if need_ctx:
            y2c = _flash(att_ctx[0], p_ctx, ctx_len, None, att_ctx, batch, tq_ctx, tk)

        def gla_w(w, lo_rows):
            w = w.reshape(GLA_LR, GLA_H // 2, LANES).transpose(1, 0, 2)
            z = jnp.zeros((GLA_H // 2, LANES, LANES), w.dtype)
            start = 0 if lo_rows else GLA_LR
            return lax.dynamic_update_slice(z, w, (0, start, 0)).astype(BF16)
        y1, y1c = _gla(p_lat, p_ctx, batch, seq, ctx_len,
                       gla_w(gla_wup_f[l], True), gla_w(gla_wup_b[l], False),
                       gla_b_f[l].reshape(GLA_H // 2, 1, LANES), gla_b_b[l].reshape(GLA_H // 2, 1, LANES),
                       gla_norm[l].reshape(GLA_H, 1, GLA_DV), need_ctx)

        def rw_w(w, lo_rows):
            w = w.reshape(RWKV_LR, RWKV_H // 2, LANES).transpose(1, 0, 2)
            return jnp.stack([_pad_rows(w[i], lo_rows) for i in range(RWKV_H // 2)]).astype(BF16)
        prm = (hp(rwkv_w0_f[l]), hp(rwkv_w0_b[l]), hp(rwkv_a0_f[l]), hp(rwkv_a0_b[l]),
               hp(rwkv_kk[l]), hp(rwkv_ka[l]), hp(rwkv_rk[l]), hp(rwkv_ln_g[l]), hp(rwkv_ln_b[l]),
               rw_w(rwkv_wup_f[l], True), rw_w(rwkv_wup_b[l], False),
               rw_w(rwkv_aup_f[l], True), rw_w(rwkv_aup_b[l], False))
        y3, y3c = _rwkv(ps_lat, ps_ctx, batch, seq, ctx_len, prm, need_ctx)

        wo1, wo2, wo3 = (w_o_gla[l].astype(BF16), w_o_att[l].astype(BF16),
                         w_o_rwkv[l].astype(BF16))
        wout = w_out[l].astype(BF16)
        x2 = _merge(x2, y1, y2, y3, p_lat, mod, lat_mod(tm_merge), wo1, wo2, wo3, wout, g_post[l], tm_merge)
        if need_ctx:
            xc2 = _merge(xc2, y1c, y2c, y3c, p_ctx, mod, ctx_mod(tm_merge_c), wo1, wo2, wo3, wout,
                         g_post[l], tm_merge_c)
    return x2.reshape(batch, seq, d)
```
